```python
import math
import jax
import jax.numpy as jnp
from jax import lax
import numpy as np


D_MODEL = 2048
BATCH = 4
SEQ = 4096
DEPTH = 2

GRID_W = 64
CTX_LEN = 256
N_EVEN = (DEPTH + 1) // 2
N_ODD = DEPTH // 2
N_MOD = 6
ALPHA = (2.0 * DEPTH) ** 0.25
BETA = (8.0 * DEPTH) ** -0.25
LN_EPS = 1e-5
NORM_EPS = 1e-6

NA_HEADS = 8
NA_DH = 128
NA_KR = 8
NA_KC = 16
NA_QC = 16
NA_SPAN = NA_QC + NA_KC
NA_W = NA_HEADS * NA_DH

GLA_HEADS = 4
GLA_DK = 128
GLA_DV = 256
GLA_RANK = 16
GLA_TAU = 16.0
GLA_CHUNK = 64
ROPE_BASE = 10000.0
GLA_WK = GLA_HEADS * GLA_DK
GLA_WV = GLA_HEADS * GLA_DV

EV_SPLITS = (NA_W, NA_W, NA_W, GLA_WK, GLA_WK, GLA_WV, GLA_WV, GLA_RANK, GLA_RANK)
EV_OFFSETS = tuple(int(v) for v in np.cumsum(EV_SPLITS)[:-1])
EV_IN = sum(EV_SPLITS)
EV_MIX = NA_W + GLA_WV

S5_W = D_MODEL // 2
S5_CH = 16
S5_G = S5_W // S5_CH
S5_P = 64

N_EXPERTS = 16
N_GROUPS = 4
TOP_K = 2
D_EXPERT = 1024

kernel_name = 'hybrid_na_gla_s5_moe_dit'


def _flip(a):
    return a[:, ::-1]


def split_heads(a, n):
    return a.reshape(a.shape[0], a.shape[1], n, -1)


def layer_norm(x, g, b):
    xf = x.astype(jnp.float32)
    mu = jnp.mean(xf, -1, keepdims=True)
    var = jnp.mean(jnp.square(xf - mu), -1, keepdims=True)
    return ((xf - mu) * lax.rsqrt(var + LN_EPS) * g + b).astype(x.dtype)


def axial_rope(x):
    L, dk = x.shape[1], x.shape[-1]
    half = dk // 2
    nf = half // 2
    t = jnp.arange(L)
    inv = ROPE_BASE ** (-jnp.arange(nf, dtype=jnp.float32) / nf)
    xf = x.astype(jnp.float32)

    def rot(xp, pos):
        ang = pos.astype(jnp.float32)[:, None] * inv
        cos, sin = jnp.cos(ang)[None, :, None, :], jnp.sin(ang)[None, :, None, :]
        x1, x2 = xp[..., :nf], xp[..., nf:]
        return jnp.concatenate([x1 * cos - x2 * sin, x1 * sin + x2 * cos], -1)

    out = jnp.concatenate([rot(xf[..., :half], t // GRID_W), rot(xf[..., half:], t % GRID_W)], -1)
    return out.astype(x.dtype)


def dense_attn(q, k, v):
    s = jnp.einsum('bqhd,bkhd->bhqk', q * q.shape[-1] ** -0.5, k).astype(jnp.float32)
    p = jax.nn.softmax(s, axis=-1).astype(v.dtype)
    return jnp.einsum('bhqk,bkhd->bqhd', p, v)


def na_latent(q, k, v, kc, vc, rpb):
    B, L, H, d = q.shape
    rows = L // GRID_W
    kr = min(NA_KR, rows)
    n_cb = GRID_W // NA_QC
    scale = d ** -0.5
    blk0 = np.clip(np.arange(n_cb) * NA_QC - NA_KC // 2, 0, GRID_W - NA_SPAN)
    key_cols = blk0[:, None] + np.arange(NA_SPAN)[None, :]
    q_cols = np.arange(n_cb)[:, None] * NA_QC + np.arange(NA_QC)[None, :]
    win0 = np.clip(q_cols - NA_KC // 2, 0, GRID_W - NA_KC)
    kcol = key_cols[:, None, :]
    col_ok = (kcol >= win0[:, :, None]) & (kcol < win0[:, :, None] + NA_KC)
    dc_idx = np.clip(kcol - q_cols[:, :, None] + NA_KC - 1, 0, 2 * NA_KC - 2)
    mask = np.broadcast_to(col_ok[:, :, None, :], (n_cb, NA_QC, kr, NA_SPAN)).reshape(n_cb, NA_QC, kr * NA_SPAN)

    def grid(a):
        return a.reshape(B, rows, GRID_W, H, d).transpose(0, 3, 1, 2, 4)

    qg, kg, vg = grid(q * scale), grid(k), grid(v)
    kct, vct = kc.transpose(0, 2, 1, 3), vc.transpose(0, 2, 1, 3)
    rpb32 = rpb.astype(jnp.float32)
    n_loc = kr * NA_SPAN

    def row_block(r):
        rs = jnp.clip(r - kr // 2, 0, rows - kr)
        q_blk = lax.dynamic_index_in_dim(qg, r, axis=2, keepdims=False).reshape(B, H, n_cb, NA_QC, d)

        def band(a):
            a = lax.dynamic_slice_in_dim(a, rs, kr, axis=2)[:, :, :, key_cols]
            return a.transpose(0, 1, 3, 2, 4, 5).reshape(B, H, n_cb, n_loc, d)

        k_blk, v_blk = band(kg), band(vg)
        dr_idx = rs + jnp.arange(kr) - r + NA_KR - 1
        bias = rpb32[:, dr_idx[:, None, None, None], dc_idx[None]]
        bias = bias.transpose(0, 2, 3, 1, 4).reshape(H, n_cb, NA_QC, n_loc)
        s_loc = jnp.einsum('bhnqd,bhnkd->bhnqk', q_blk, k_blk).astype(jnp.float32) + bias
        s_loc = jnp.where(mask, s_loc, -jnp.inf)
        s_ctx = jnp.einsum('bhnqd,bhkd->bhnqk', q_blk, kct).astype(jnp.float32)
        p = jax.nn.softmax(jnp.concatenate([s_loc, s_ctx], -1), axis=-1).astype(v.dtype)
        o = (jnp.einsum('bhnqk,bhnkd->bhnqd', p[..., :n_loc], v_blk)
             + jnp.einsum('bhnqk,bhkd->bhnqd', p[..., n_loc:], vct))
        return o.reshape(B, H, GRID_W, d)

    out = lax.map(row_block, jnp.arange(rows))
    return out.transpose(1, 0, 3, 2, 4).reshape(B, L, H, d)


def gla_chunked(q, k, v, g, s0):
    f32 = jnp.float32
    B, T, H, _ = q.shape
    n = T // GLA_CHUNK

    def to_chunks(a):
        return a.astype(f32).reshape(B, n, GLA_CHUNK, H, a.shape[-1]).transpose(1, 0, 3, 2, 4)

    causal = jnp.tril(jnp.ones((GLA_CHUNK, GLA_CHUNK), dtype=bool))

    def step(s, inp):
        qc, kc, vc, gc = inp
        b = jnp.cumsum(gc, axis=2)
        b_last = b[:, :, -1:, :]
        q_dec = qc * jnp.exp(b)
        att = jnp.where(causal, jnp.einsum('bhqd,bhkd->bhqk', q_dec, kc * jnp.exp(-b)), 0.0)
        o = jnp.einsum('bhqk,bhkv->bhqv', att, vc) + jnp.einsum('bhqd,bhdv->bhqv', q_dec, s)
        s_new = jnp.exp(b_last[:, :, 0, :])[..., None] * s + jnp.einsum('bhkd,bhkv->bhdv', kc * jnp.exp(b_last - b), vc)
        return s_new, o

    s_fin, o = lax.scan(step, s0.astype(f32), (to_chunks(q), to_chunks(k), to_chunks(v), to_chunks(g)))
    return o.transpose(1, 0, 3, 2, 4).reshape(B, T, H, -1), s_fin


def gla_gate_norm(o, r, g):
    of = o.astype(jnp.float32)
    of = of * lax.rsqrt(jnp.mean(jnp.square(of), -1, keepdims=True) + NORM_EPS) * g
    return (of.reshape(o.shape[0], o.shape[1], -1) * jax.nn.silu(r.astype(jnp.float32))).astype(r.dtype)


def even_mixer(h, hc, w_in, gate_w2, gate_b, rpb, norm_g, w_out, need_ctx):
    B, L, _ = h.shape
    q_a, k_a, v_a, q_b, k_b, v_b, r_b, lr_f, lr_b = jnp.split(h @ w_in, EV_OFFSETS, axis=-1)
    cq_a, ck_a, cv_a, cq_b, ck_b, cv_b, cr_b, clr_f, clr_b = jnp.split(hc @ w_in, EV_OFFSETS, axis=-1)
    ck_h, cv_h = split_heads(ck_a, NA_HEADS), split_heads(cv_a, NA_HEADS)
    a_lat = na_latent(split_heads(q_a, NA_HEADS), split_heads(k_a, NA_HEADS), split_heads(v_a, NA_HEADS), ck_h, cv_h, rpb)

    def log_gate(lr, dirn):
        z = (lr @ gate_w2[dirn] + gate_b[dirn]).astype(jnp.float32)
        return split_heads(jax.nn.log_sigmoid(z) / GLA_TAU, GLA_HEADS)

    gscale = GLA_DK ** -0.5
    zero = jnp.zeros((B, GLA_HEADS, GLA_DK, GLA_DV), jnp.float32)
    cq = split_heads(cq_b, GLA_HEADS) * gscale
    ck = split_heads(ck_b, GLA_HEADS)
    cv = split_heads(cv_b, GLA_HEADS)
    oc_f, st_f = gla_chunked(cq, ck, cv, log_gate(clr_f, 0), zero)
    oc_b, st_b = gla_chunked(_flip(cq), _flip(ck), _flip(cv), _flip(log_gate(clr_b, 1)), zero)
    q = axial_rope(split_heads(q_b, GLA_HEADS)) * gscale
    k = axial_rope(split_heads(k_b, GLA_HEADS))
    v = split_heads(v_b, GLA_HEADS)
    o_f, _ = gla_chunked(q, k, v, log_gate(lr_f, 0), st_f)
    o_b, _ = gla_chunked(_flip(q), _flip(k), _flip(v), _flip(log_gate(lr_b, 1)), st_b)
    b_lat = gla_gate_norm(o_f + _flip(o_b), r_b, norm_g)
    out = jnp.concatenate([a_lat.reshape(B, L, -1), b_lat], -1) @ w_out
    if not need_ctx:
        return out, None
    a_ctx = dense_attn(split_heads(cq_a, NA_HEADS), ck_h, cv_h)
    b_ctx = gla_gate_norm(oc_f + _flip(oc_b), cr_b, norm_g)
    out_c = jnp.concatenate([a_ctx.reshape(B, hc.shape[1], -1), b_ctx], -1) @ w_out
    return out, out_c


def zoh(lam_re, lam_im, log_dt, b_re, b_im):
    f32 = jnp.float32
    lr, li = lam_re.astype(f32), lam_im.astype(f32)
    dt = jnp.exp(log_dt.astype(f32))[:, None]
    mag = jnp.exp(lr * dt)
    lb_re, lb_im = mag * jnp.cos(li * dt), mag * jnp.sin(li * dt)
    den = lr * lr + li * li
    fr = ((lb_re - 1.0) * lr + lb_im * li) / den
    fi = (lb_im * lr - (lb_re - 1.0) * li) / den
    br, bi = b_re.astype(f32), b_im.astype(f32)
    return lb_re, lb_im, fr[..., None] * br - fi[..., None] * bi, fr[..., None] * bi + fi[..., None] * br


def diag_scan(lb_re, lb_im, bu_re, bu_im, h0, reverse):
    if h0 is not None:
        h_re, h_im = h0
        end = -1 if reverse else 0
        bu_re = bu_re.at[:, end].add(lb_re * h_re - lb_im * h_im)
        bu_im = bu_im.at[:, end].add(lb_re * h_im + lb_im * h_re)
    a_re = jnp.broadcast_to(lb_re, bu_re.shape)
    a_im = jnp.broadcast_to(lb_im, bu_im.shape)

    def combine(e1, e2):
        a1r, a1i, b1r, b1i = e1
        a2r, a2i, b2r, b2i = e2
        return (a2r * a1r - a2i * a1i, a2r * a1i + a2i * a1r,
                a2r * b1r - a2i * b1i + b2r, a2r * b1i + a2i * b1r + b2i)

    _, _, x_re, x_im = lax.associative_scan(combine, (a_re, a_im, bu_re, bu_im), reverse=reverse, axis=1)
    return x_re, x_im


def s5_bidir(u, uc, lam_re, lam_im, log_dt, b_re, b_im, c_re, c_im, d_skip, need_ctx):
    f32 = jnp.float32
    B, L, W = u.shape

    def grp(a):
        return a.astype(f32).reshape(a.shape[0], a.shape[1], S5_G, S5_CH)

    ug, ucg = grp(u), grp(uc)
    dsk = d_skip.astype(f32)
    y = dsk * ug
    yc = dsk * ucg if need_ctx else None
    for dirn in range(2):
        rev = dirn == 1
        lb_re, lb_im, bb_re, bb_im = zoh(lam_re[dirn], lam_im[dirn], log_dt[dirn], b_re[dirn], b_im[dirn])
        cr, ci = c_re[dirn].astype(f32), c_im[dirn].astype(f32)
        xc_re, xc_im = diag_scan(lb_re, lb_im, jnp.einsum('btgc,gpc->btgp', ucg, bb_re),
                                 jnp.einsum('btgc,gpc->btgp', ucg, bb_im), None, rev)
        end = 0 if rev else -1
        x_re, x_im = diag_scan(lb_re, lb_im, jnp.einsum('btgc,gpc->btgp', ug, bb_re),
                               jnp.einsum('btgc,gpc->btgp', ug, bb_im), (xc_re[:, end], xc_im[:, end]), rev)
        y = y + jnp.einsum('btgp,gcp->btgc', x_re, cr) - jnp.einsum('btgp,gcp->btgc', x_im, ci)
        if need_ctx:
            yc = yc + jnp.einsum('btgp,gcp->btgc', xc_re, cr) - jnp.einsum('btgp,gcp->btgc', xc_im, ci)
    y = y.reshape(B, L, W).astype(u.dtype)
    if need_ctx:
        yc = yc.reshape(B, uc.shape[1], W).astype(u.dtype)
    return y, yc


def odd_mixer(h, hc, w_in, lam_re, lam_im, log_dt, b_re, b_im, c_re, c_im, d_skip, w_glu, b_glu, w_out, need_ctx):
    y, yc = s5_bidir(h @ w_in, hc @ w_in, lam_re, lam_im, log_dt, b_re, b_im, c_re, c_im, d_skip, need_ctx)

    def glu_out(t):
        g = jax.nn.gelu(t)
        return (g * jax.nn.sigmoid(g @ w_glu + b_glu)) @ w_out

    return glu_out(y), (glu_out(yc) if need_ctx else None)


def moe(t, router_w, router_b, w_gate, w_up, w_down):
    n = t.shape[0]
    eg = N_EXPERTS // N_GROUPS
    aff = jax.nn.sigmoid((t @ router_w).astype(jnp.float32))
    sel = aff + router_b.astype(jnp.float32)
    grp_score = lax.top_k(sel.reshape(n, N_GROUPS, eg), TOP_K)[0].sum(-1)
    grp = jnp.argmax(grp_score, axis=-1)
    in_grp = (jnp.arange(N_EXPERTS) // eg)[None, :] == grp[:, None]
    _, idx = lax.top_k(jnp.where(in_grp, sel, -jnp.inf), TOP_K)
    w = jnp.take_along_axis(aff, idx, axis=-1)
    w = w / jnp.sum(w, -1, keepdims=True)
    comb = jnp.sum(jax.nn.one_hot(idx, N_EXPERTS, dtype=jnp.float32) * w[..., None], axis=1).astype(t.dtype)
    y = jnp.zeros_like(t)
    for e in range(N_EXPERTS):
        hid = jax.nn.silu(t @ w_gate[e]) * (t @ w_up[e])
        y = y + comb[:, e:e + 1] * (hid @ w_down[e])
    return y


def setup_inputs(seed: int = 0) -> dict:
    key = jax.random.key(seed)
    ks = iter(jax.random.split(key, 48))

    def nrm(shape, std):
        return jax.random.normal(next(ks), shape, jnp.float32) * std

    D = D_MODEL
    p_idx = jnp.arange(S5_P, dtype=jnp.float32)
    return {
        'x': nrm((BATCH, SEQ, D), 1.0),
        'c': nrm((BATCH, D), 1.0),
        'ctx': nrm((BATCH, CTX_LEN, D), 1.0),
        'c_ctx': nrm((D,), 1.0),
        'ada_w': nrm((DEPTH, D, N_MOD * D), 0.5 * D ** -0.5),
        'ada_b': nrm((DEPTH, N_MOD * D), 0.02),
        'ln_mix_g': 1.0 + nrm((DEPTH, D), 0.02),
        'ln_mix_b': nrm((DEPTH, D), 0.02),
        'ln_ffn_g': 1.0 + nrm((DEPTH, D), 0.02),
        'ln_ffn_b': nrm((DEPTH, D), 0.02),
        'ev_w_in': nrm((N_EVEN, D, EV_IN), D ** -0.5),
        'ev_gate_w2': nrm((N_EVEN, 2, GLA_RANK, GLA_WK), GLA_RANK ** -0.5),
        'ev_gate_b': nrm((N_EVEN, 2, GLA_WK), 0.02),
        'ev_rpb': nrm((N_EVEN, NA_HEADS, 2 * NA_KR - 1, 2 * NA_KC - 1), 0.2),
        'ev_norm_g': 1.0 + nrm((N_EVEN, GLA_DV), 0.02),
        'ev_w_out': nrm((N_EVEN, EV_MIX, D), BETA * EV_MIX ** -0.5),
        'od_w_in': nrm((N_ODD, D, S5_W), D ** -0.5),
        'od_lam_re': -0.5 + nrm((N_ODD, 2, S5_G, S5_P), 0.01),
        'od_lam_im': math.pi * p_idx + nrm((N_ODD, 2, S5_G, S5_P), 0.01),
        'od_log_dt': jax.random.uniform(next(ks), (N_ODD, 2, S5_G), jnp.float32, math.log(1e-3), math.log(1e-1)),
        'od_b_re': nrm((N_ODD, 2, S5_G, S5_P, S5_CH), (2 * S5_CH) ** -0.5),
        'od_b_im': nrm((N_ODD, 2, S5_G, S5_P, S5_CH), (2 * S5_CH) ** -0.5),
        'od_c_re': nrm((N_ODD, 2, S5_G, S5_CH, S5_P), 0.5),
        'od_c_im': nrm((N_ODD, 2, S5_G, S5_CH, S5_P), 0.5),
        'od_d': nrm((N_ODD, S5_G, S5_CH), 0.5),
        'od_w_glu': nrm((N_ODD, S5_W, S5_W), S5_W ** -0.5),
        'od_b_glu': nrm((N_ODD, S5_W), 0.02),
        'od_w_out': nrm((N_ODD, S5_W, D), BETA * S5_W ** -0.5),
        'router_w': nrm((D, N_EXPERTS), D ** -0.5),
        'router_b': nrm((N_EXPERTS,), 0.01),
        'moe_w_gate': nrm((DEPTH, N_EXPERTS, D, D_EXPERT), D ** -0.5),
        'moe_w_up': nrm((DEPTH, N_EXPERTS, D, D_EXPERT), D ** -0.5),
        'moe_w_down': nrm((DEPTH, N_EXPERTS, D_EXPERT, D), BETA * D_EXPERT ** -0.5),
    }


def reference(x, c, ctx, c_ctx, ada_w, ada_b, ln_mix_g, ln_mix_b, ln_ffn_g, ln_ffn_b,
              ev_w_in, ev_gate_w2, ev_gate_b, ev_rpb, ev_norm_g, ev_w_out,
              od_w_in, od_lam_re, od_lam_im, od_log_dt, od_b_re, od_b_im, od_c_re, od_c_im, od_d,
              od_w_glu, od_b_glu, od_w_out, router_w, router_b, moe_w_gate, moe_w_up, moe_w_down):
    B, L, D = x.shape
    xc = ctx
    sc = jax.nn.silu(c)
    scc = jax.nn.silu(c_ctx)
    for layer in range(DEPTH):
        last = layer == DEPTH - 1
        m = jnp.split(sc @ ada_w[layer] + ada_b[layer], N_MOD, axis=-1)
        mc = jnp.split(scc @ ada_w[layer] + ada_b[layer], N_MOD, axis=-1)
        h = x * (1.0 + m[1][:, None]) + m[0][:, None]
        hc = xc * (1.0 + mc[1]) + mc[0]
        if layer % 2 == 0:
            i = layer // 2
            out, out_c = even_mixer(h, hc, ev_w_in[i], ev_gate_w2[i], ev_gate_b[i], ev_rpb[i],
                                    ev_norm_g[i], ev_w_out[i], not last)
        else:
            i = layer // 2
            out, out_c = odd_mixer(h, hc, od_w_in[i], od_lam_re[i], od_lam_im[i], od_log_dt[i],
                                   od_b_re[i], od_b_im[i], od_c_re[i], od_c_im[i], od_d[i],
                                   od_w_glu[i], od_b_glu[i], od_w_out[i], not last)
        x = layer_norm(ALPHA * x + m[2][:, None] * out, ln_mix_g[layer], ln_mix_b[layer])
        h = x * (1.0 + m[4][:, None]) + m[3][:, None]
        if last:
            y = moe(h.reshape(-1, D), router_w, router_b, moe_w_gate[layer], moe_w_up[layer],
                    moe_w_down[layer]).reshape(B, L, D)
        else:
            xc = layer_norm(ALPHA * xc + mc[2] * out_c, ln_mix_g[layer], ln_mix_b[layer])
            hc = xc * (1.0 + mc[4]) + mc[3]
            tokens = jnp.concatenate([h.reshape(-1, D), hc.reshape(-1, D)], axis=0)
            y_all = moe(tokens, router_w, router_b, moe_w_gate[layer], moe_w_up[layer], moe_w_down[layer])
            y = y_all[:B * L].reshape(B, L, D)
            y_c = y_all[B * L:].reshape(xc.shape)
            xc = layer_norm(ALPHA * xc + mc[5] * y_c, ln_ffn_g[layer], ln_ffn_b[layer])
        x = layer_norm(ALPHA * x + m[5][:, None] * y, ln_ffn_g[layer], ln_ffn_b[layer])
    return x
```

```python
import functools
import math

import numpy as np
import jax
import jax.numpy as jnp
from jax import lax
from jax.experimental import pallas as pl
from jax.experimental.pallas import tpu as pltpu

F32 = jnp.float32
BF16 = jnp.bfloat16
HIGHEST = lax.Precision.HIGHEST

N_MOD = 6
LN_EPS = 1e-5
NORM_EPS = 1e-6

GRID_W = 64
NA_HEADS = 8
NA_DH = 128
NA_KR = 8
NA_KC = 16

GLA_HEADS = 4
GLA_DK = 128
GLA_DV = 256
GLA_RANK = 16
GLA_TAU = 16.0
GLA_CHUNK = 64
ROPE_BASE = 10000.0

S5_CH = 16
S5_P = 64
S5_TC = 16

N_EXPERTS = 16
N_GROUPS = 4
TOP_K = 2

VMEM_LIMIT = 56 * 1024 * 1024
NEG_BIG = -1e30


def _cparams(*sem):
    return pltpu.CompilerParams(dimension_semantics=sem, vmem_limit_bytes=VMEM_LIMIT)


def _dot(a, b, precision=None):
    return jnp.dot(a, b, preferred_element_type=F32, precision=precision)


def _dot_nt(a, b, precision=None):
    return lax.dot_general(a, b, (((1,), (1,)), ((), ())), preferred_element_type=F32, precision=precision)


def _dot_tn(a, b):
    return lax.dot_general(a, b, (((0,), (0,)), ((), ())), preferred_element_type=F32)


def _mods_kernel(s_ref, w_ref, b_ref, o_ref):
    s = s_ref[...]
    s = s * jax.nn.sigmoid(s)
    o_ref[0] = _dot(s, w_ref[0], HIGHEST) + b_ref[0]


def compute_mods(cvec, ada_w, ada_b, tn=1024):
    n_layer, d, n = ada_w.shape
    tn = math.gcd(tn, n)
    return pl.pallas_call(
        _mods_kernel,
        grid=(n_layer, n // tn),
        in_specs=[pl.BlockSpec((8, d), lambda l, j: (0, 0)),
                  pl.BlockSpec((1, d, tn), lambda l, j: (l, 0, j)),
                  pl.BlockSpec((1, 1, tn), lambda l, j: (l, 0, j))],
        out_specs=pl.BlockSpec((1, 8, tn), lambda l, j: (l, 0, j)),
        out_shape=jax.ShapeDtypeStruct((n_layer, 8, n), F32),
        compiler_params=_cparams("arbitrary", "arbitrary"),
        name="ada_mods",
    )(cvec, ada_w, ada_b.reshape(n_layer, 1, n))


def _mod_spec(d, layer, which, seg_of_tile):
    return pl.BlockSpec((None, None, 1, d), lambda i, *_: (layer, seg_of_tile(i), 0, which))


def _seg_fn(tm, seg_rows, n_batch):
    return lambda i: jnp.minimum((i * tm) // seg_rows, n_batch)


def _modmm_kernel(x_ref, s1_ref, s0_ref, w_ref, o_ref):
    h = x_ref[...] * (1.0 + s1_ref[...]) + s0_ref[...]
    o_ref[...] = _dot(h.astype(BF16), w_ref[...])


def mod_matmul(x, mods4, layer, w_bf16, seg_rows, n_batch, tm=256, tn=None):
    t, d = x.shape
    n = w_bf16.shape[1]
    tn = n if tn is None else tn
    seg = _seg_fn(tm, seg_rows, n_batch)
    return pl.pallas_call(
        _modmm_kernel,
        grid=(n // tn, t // tm),
        in_specs=[pl.BlockSpec((tm, d), lambda j, i: (i, 0)),
                  pl.BlockSpec((None, None, 1, d), lambda j, i: (layer, seg(i), 0, 1)),
                  pl.BlockSpec((None, None, 1, d), lambda j, i: (layer, seg(i), 0, 0)),
                  pl.BlockSpec((d, tn), lambda j, i: (0, j))],
        out_specs=pl.BlockSpec((tm, tn), lambda j, i: (i, j)),
        out_shape=jax.ShapeDtypeStruct((t, n), F32),
        compiler_params=_cparams("arbitrary", "arbitrary"),
        name="mod_matmul",
    )(x, mods4, mods4, w_bf16)


def na_bias_table(rpb):
    w = GRID_W
    q = np.arange(w)
    kc = np.arange(w)
    win0 = np.clip(q - NA_KC // 2, 0, w - NA_KC)
    ok = (kc[None, :] >= win0[:, None]) & (kc[None, :] < win0[:, None] + NA_KC)
    dc = np.clip(kc[None, :] - q[:, None] + NA_KC - 1, 0, 2 * NA_KC - 2)
    dr = np.arange(NA_KR)[:, None] + np.arange(NA_KR)[None, :]
    tab = rpb.astype(F32)[:, dr[:, :, None, None], dc[None, None, :, :]]
    tab = jnp.where(ok[None, None, None], tab, NEG_BIG)
    return tab.transpose(0, 1, 3, 2, 4).reshape(rpb.shape[0], NA_KR, w, NA_KR * w)


def _na_kernel(q_ref, k_ref, v_ref, qc_ref, kc_ref, vc_ref, bias_ref, o_ref, oc_ref, kbf, vbf, *, rows):
    w = GRID_W
    scale = NA_DH ** -0.5
    kbf[...] = k_ref[...].astype(BF16)
    vbf[...] = v_ref[...].astype(BF16)
    kc = kc_ref[...].astype(BF16)
    vc = vc_ref[...].astype(BF16)

    def body(r, carry):
        rs = jnp.clip(r - NA_KR // 2, 0, rows - NA_KR)
        var = rs - r + (NA_KR - 1)
        q = (q_ref[pl.ds(pl.multiple_of(r * w, w), w), :] * scale).astype(BF16)
        k0 = pl.multiple_of(rs * w, w)
        kb = kbf[pl.ds(k0, NA_KR * w), :]
        vb = vbf[pl.ds(k0, NA_KR * w), :]
        s_loc = _dot_nt(q, kb) + bias_ref[var]
        s_ctx = _dot_nt(q, kc)
        m = jnp.maximum(jnp.max(s_loc, -1, keepdims=True), jnp.max(s_ctx, -1, keepdims=True))
        p_loc = jnp.exp(s_loc - m)
        p_ctx = jnp.exp(s_ctx - m)
        den = jnp.sum(p_loc, -1, keepdims=True) + jnp.sum(p_ctx, -1, keepdims=True)
        o = _dot(p_loc.astype(BF16), vb) + _dot(p_ctx.astype(BF16), vc)
        o_ref[pl.ds(pl.multiple_of(r * w, w), w), :] = o / den
        return carry

    lax.fori_loop(0, rows, body, 0)

    qc = (qc_ref[...] * scale).astype(BF16)
    s = _dot_nt(qc, kc)
    p = jnp.exp(s - jnp.max(s, -1, keepdims=True))
    oc_ref[...] = _dot(p.astype(BF16), vc) / jnp.sum(p, -1, keepdims=True)


def na_attention(proj, bias_tab, n_batch, l_lat, l_ctx):
    h = NA_HEADS
    dh = NA_DH
    rows = l_lat // GRID_W
    ctx0 = (n_batch * l_lat) // l_ctx
    return pl.pallas_call(
        functools.partial(_na_kernel, rows=rows),
        grid=(n_batch, h),
        in_specs=[pl.BlockSpec((l_lat, dh), lambda b, hh: (b, hh)),
                  pl.BlockSpec((l_lat, dh), lambda b, hh: (b, h + hh)),
                  pl.BlockSpec((l_lat, dh), lambda b, hh: (b, 2 * h + hh)),
                  pl.BlockSpec((l_ctx, dh), lambda b, hh: (ctx0 + b, hh)),
                  pl.BlockSpec((l_ctx, dh), lambda b, hh: (ctx0 + b, h + hh)),
                  pl.BlockSpec((l_ctx, dh), lambda b, hh: (ctx0 + b, 2 * h + hh)),
                  pl.BlockSpec((None, NA_KR, GRID_W, NA_KR * GRID_W), lambda b, hh: (hh, 0, 0, 0))],
        out_specs=[pl.BlockSpec((l_lat, dh), lambda b, hh: (b, hh)),
                   pl.BlockSpec((l_ctx, dh), lambda b, hh: (b, hh))],
        out_shape=[jax.ShapeDtypeStruct((n_batch * l_lat, h * dh), F32),
                   jax.ShapeDtypeStruct((n_batch * l_ctx, h * dh), F32)],
        scratch_shapes=[pltpu.VMEM((l_lat, dh), BF16), pltpu.VMEM((l_lat, dh), BF16)],
        compiler_params=_cparams("arbitrary", "arbitrary"),
        name="na_attention",
    )(proj, proj, proj, proj, proj, proj, bias_tab)


def rope_tables(l_lat, l_ctx):
    half = GLA_DK // 2
    nf = half // 2
    inv = ROPE_BASE ** (-np.arange(nf, dtype=np.float64) / nf)
    t = np.arange(l_lat)
    lane = np.arange(GLA_DK)
    pos = np.where(lane[None, :] < half, (t // GRID_W)[:, None], (t % GRID_W)[:, None]).astype(np.float64)
    ang = pos * inv[lane % nf][None, :]
    first = (lane % half) < nf
    cos = np.cos(ang)
    sin_a = np.where(first[None, :], -np.sin(ang), 0.0)
    sin_b = np.where(first[None, :], 0.0, np.sin(ang))
    one = np.ones((l_ctx, GLA_DK))
    zero = np.zeros((l_ctx, GLA_DK))
    cat = lambda ident, a: jnp.asarray(np.concatenate([ident, a, ident], 0), F32)
    return cat(one, cos), cat(zero, sin_a), cat(zero, sin_b)


def _gla_direction(q_ref, k_ref, v_ref, lr_ref, cos_ref, sa_ref, sb_ref, g2_ref, gb_ref, st_ref, o_ref, reverse):
    c = GLA_CHUNK
    nf = GLA_DK // 4
    gscale = GLA_DK ** -0.5
    row = lax.broadcasted_iota(jnp.int32, (c, c), 0)
    col = lax.broadcasted_iota(jnp.int32, (c, c), 1)
    seen = (col >= row) if reverse else (col <= row)
    tri = seen.astype(F32)

    z = _dot(lr_ref[...].astype(BF16), g2_ref[...]) + gb_ref[...]
    g = (jnp.minimum(z, 0.0) - jnp.log1p(jnp.exp(-jnp.abs(z)))) * (1.0 / GLA_TAU)
    b = _dot(tri, g, HIGHEST)
    b_last = b[0:1, :] if reverse else b[c - 1:c, :]
    e_b = jnp.exp(b)
    e_nb = jnp.exp(-b)
    e_rem = jnp.exp(b_last - b)
    e_last = jnp.exp(b_last)
    cos = cos_ref[...]
    sa = sa_ref[...]
    sb = sb_ref[...]

    def rope(x):
        return x * cos + pltpu.roll(x, GLA_DK - nf, 1) * sa + pltpu.roll(x, nf, 1) * sb

    for h in range(GLA_HEADS):
        ks = slice(h * GLA_DK, (h + 1) * GLA_DK)
        vs = slice(h * GLA_DV, (h + 1) * GLA_DV)
        qh = rope(q_ref[:, ks]) * gscale
        kh = rope(k_ref[:, ks])
        q_dec = (qh * e_b[:, ks]).astype(BF16)
        k_dec = (kh * e_nb[:, ks]).astype(BF16)
        k_rem = (kh * e_rem[:, ks]).astype(BF16)
        vh = v_ref[:, vs].astype(BF16)
        att = jnp.where(seen, _dot_nt(q_dec, k_dec), 0.0)
        st = st_ref[h]
        o_ref[:, vs] = _dot(att.astype(BF16), vh) + _dot_nt(q_dec, st.astype(BF16))
        st_ref[h] = st * e_last[:, ks] + _dot_tn(vh, k_rem)


def _gla_kernel(qf, kf, vf, lrf, cf, saf, sbf, qb, kb, vb, lrb, cb, sab, sbb, g2_ref, gb_ref,
                of_ref, ob_ref, st_ref):
    @pl.when(pl.program_id(1) == 0)
    def _():
        st_ref[...] = jnp.zeros_like(st_ref)

    _gla_direction(qf, kf, vf, lrf, cf, saf, sbf, g2_ref.at[0], gb_ref.at[0], st_ref.at[0], of_ref, False)
    _gla_direction(qb, kb, vb, lrb, cb, sab, sbb, g2_ref.at[1], gb_ref.at[1], st_ref.at[1], ob_ref, True)


def gla_bidir(proj, g2, gb, tables, n_batch, l_lat, l_ctx, col_q, col_k, col_v, col_lr):
    c = GLA_CHUNK
    nc = l_ctx // c
    nl = l_lat // c
    nz = nl + 2 * nc
    steps = nl + nc
    wk = GLA_HEADS * GLA_DK
    wv = GLA_HEADS * GLA_DV
    t_rows = n_batch * (l_lat + l_ctx)

    def zblk(b, j):
        lat = b * nl + (j - nc)
        ctx = n_batch * nl + b * nc + jnp.where(j < nc, j, j - nc - nl)
        return jnp.where((j >= nc) & (j < nc + nl), lat, ctx)

    fwd = lambda b, i: zblk(b, i)
    bwd = lambda b, i: zblk(b, nz - 1 - i)

    def dir_specs(blk, tab):
        return [pl.BlockSpec((c, wk), lambda b, i: (blk(b, i), col_q)),
                pl.BlockSpec((c, wk), lambda b, i: (blk(b, i), col_k)),
                pl.BlockSpec((c, wv), lambda b, i: (blk(b, i), col_v)),
                pl.BlockSpec((c, 128), lambda b, i: (blk(b, i), col_lr)),
                pl.BlockSpec((c, GLA_DK), lambda b, i: (tab(i), 0)),
                pl.BlockSpec((c, GLA_DK), lambda b, i: (tab(i), 0)),
                pl.BlockSpec((c, GLA_DK), lambda b, i: (tab(i), 0))]

    cos, sa, sb = tables
    return pl.pallas_call(
        _gla_kernel,
        grid=(n_batch, steps),
        in_specs=(dir_specs(fwd, lambda i: i) + dir_specs(bwd, lambda i: nz - 1 - i)
                  + [pl.BlockSpec((2, 128, wk), lambda b, i: (0, 0, 0)),
                     pl.BlockSpec((2, 1, wk), lambda b, i: (0, 0, 0))]),
        out_specs=[pl.BlockSpec((c, wv), lambda b, i: (fwd(b, i), 0)),
                   pl.BlockSpec((c, wv), lambda b, i: (bwd(b, i), 0))],
        out_shape=[jax.ShapeDtypeStruct((t_rows, wv), F32), jax.ShapeDtypeStruct((t_rows, wv), F32)],
        scratch_shapes=[pltpu.VMEM((2, GLA_HEADS, GLA_DV, GLA_DK), F32)],
        compiler_params=_cparams("arbitrary", "arbitrary"),
        name="gla_bidir",
    )(proj, proj, proj, proj, cos, sa, sb, proj, proj, proj, proj, cos, sa, sb, g2, gb)


def _post_mix(out, x_ref, m2_ref, m3_ref, m4_ref, lg_ref, lb_ref, wr_ref, alpha, x1_ref, h2_ref, lt_ref):
    y = alpha * x_ref[...] + m2_ref[...] * out
    mu = jnp.mean(y, -1, keepdims=True)
    yc = y - mu
    var = jnp.mean(yc * yc, -1, keepdims=True)
    x1 = yc * lax.rsqrt(var + LN_EPS) * lg_ref[...] + lb_ref[...]
    h2 = x1 * (1.0 + m4_ref[...]) + m3_ref[...]
    x1_ref[...] = x1
    h2_ref[...] = h2.astype(BF16)
    lt_ref[...] = _dot_nt(wr_ref[...], h2, HIGHEST)


def _even_out_kernel(a_ref, of_ref, ob_ref, r_ref, x_ref, m2_ref, m3_ref, m4_ref, ng_ref, wo_ref,
                     lg_ref, lb_ref, wr_ref, x1_ref, h2_ref, lt_ref, *, alpha):
    o = of_ref[...] + ob_ref[...]
    r = r_ref[...]
    gate = r * jax.nn.sigmoid(r)
    acc = _dot(a_ref[...].astype(BF16), wo_ref[0:a_ref.shape[1], :])
    na = a_ref.shape[1]
    for h in range(GLA_HEADS):
        vs = slice(h * GLA_DV, (h + 1) * GLA_DV)
        oh = o[:, vs]
        nrm = oh * lax.rsqrt(jnp.mean(oh * oh, -1, keepdims=True) + NORM_EPS) * ng_ref[...]
        bh = (nrm * gate[:, vs]).astype(BF16)
        acc = acc + _dot(bh, wo_ref[na + h * GLA_DV:na + (h + 1) * GLA_DV, :])
    _post_mix(acc, x_ref, m2_ref, m3_ref, m4_ref, lg_ref, lb_ref, wr_ref, alpha, x1_ref, h2_ref, lt_ref)


def _post_specs(d, layer, seg, tm, n_exp):
    ins = [pl.BlockSpec((tm, d), lambda i: (i, 0)),
           _mod_spec(d, layer, 2, seg), _mod_spec(d, layer, 3, seg), _mod_spec(d, layer, 4, seg)]
    tail = [pl.BlockSpec((1, d), lambda i: (0, 0)), pl.BlockSpec((1, d), lambda i: (0, 0)),
            pl.BlockSpec((n_exp, d), lambda i: (0, 0))]
    outs = [pl.BlockSpec((tm, d), lambda i: (i, 0)), pl.BlockSpec((tm, d), lambda i: (i, 0)),
            pl.BlockSpec((n_exp, tm), lambda i: (0, i))]
    return ins, tail, outs


def _post_shapes(t, d, n_exp):
    return [jax.ShapeDtypeStruct((t, d), F32), jax.ShapeDtypeStruct((t, d), BF16),
            jax.ShapeDtypeStruct((n_exp, t), F32)]


def even_out(a, o_f, o_b, proj, col_r, x, mods4, layer, norm_g, w_out_bf16, ln_g, ln_b, router_wt,
             alpha, seg_rows, n_batch, tm=256):
    t, d = x.shape
    na = a.shape[1]
    wv = o_f.shape[1]
    n_exp = router_wt.shape[0]
    seg = _seg_fn(tm, seg_rows, n_batch)
    ins, tail, outs = _post_specs(d, layer, seg, tm, n_exp)
    return pl.pallas_call(
        functools.partial(_even_out_kernel, alpha=alpha),
        grid=(t // tm,),
        in_specs=([pl.BlockSpec((tm, na), lambda i: (i, 0)),
                   pl.BlockSpec((tm, wv), lambda i: (i, 0)),
                   pl.BlockSpec((tm, wv), lambda i: (i, 0)),
                   pl.BlockSpec((tm, wv), lambda i: (i, col_r))] + ins
                  + [pl.BlockSpec((1, GLA_DV), lambda i: (0, 0)),
                     pl.BlockSpec((na + wv, d), lambda i: (0, 0))] + tail),
        out_specs=outs,
        out_shape=_post_shapes(t, d, n_exp),
        compiler_params=_cparams("arbitrary"),
        name="even_out",
    )(a, o_f, o_b, proj, x, mods4, mods4, mods4, norm_g.reshape(1, -1), w_out_bf16,
      ln_g.reshape(1, -1), ln_b.reshape(1, -1), router_wt)


def _odd_out_kernel(y_ref, x_ref, m2_ref, m3_ref, m4_ref, wg_ref, bg_ref, wo_ref,
                    lg_ref, lb_ref, wr_ref, x1_ref, h2_ref, lt_ref, *, alpha):
    g = jax.nn.gelu(y_ref[...], approximate=True)
    z = _dot(g.astype(BF16), wg_ref[...]) + bg_ref[...]
    v = g * jax.nn.sigmoid(z)
    out = _dot(v.astype(BF16), wo_ref[...])
    _post_mix(out, x_ref, m2_ref, m3_ref, m4_ref, lg_ref, lb_ref, wr_ref, alpha, x1_ref, h2_ref, lt_ref)


def odd_out(y, x, mods4, layer, w_glu_bf16, b_glu, w_out_bf16, ln_g, ln_b, router_wt,
            alpha, seg_rows, n_batch, tm=256):
    t, w5 = y.shape
    d = x.shape[1]
    n_exp = router_wt.shape[0]
    seg = _seg_fn(tm, seg_rows, n_batch)
    ins, tail, outs = _post_specs(d, layer, seg, tm, n_exp)
    return pl.pallas_call(
        functools.partial(_odd_out_kernel, alpha=alpha),
        grid=(t // tm,),
        in_specs=([pl.BlockSpec((tm, w5), lambda i: (i, 0))] + ins
                  + [pl.BlockSpec((w5, w5), lambda i: (0, 0)),
                     pl.BlockSpec((1, w5), lambda i: (0, 0)),
                     pl.BlockSpec((w5, d), lambda i: (0, 0))] + tail),
        out_specs=outs,
        out_shape=_post_shapes(t, d, n_exp),
        compiler_params=_cparams("arbitrary"),
        name="odd_out",
    )(y, x, mods4, mods4, mods4, w_glu_bf16, b_glu.reshape(1, -1), w_out_bf16,
      ln_g.reshape(1, -1), ln_b.reshape(1, -1), router_wt)


def _route_kernel(lt_ref, rb_ref, idx_ref, w_ref):
    eg = N_EXPERTS // N_GROUPS
    logits = lt_ref[...]
    aff = jax.nn.sigmoid(logits)
    sel = aff + rb_ref[...]
    s = [sel[e:e + 1, :] for e in range(N_EXPERTS)]
    a = [aff[e:e + 1, :] for e in range(N_EXPERTS)]

    def top2_sum(v):
        hi1, lo1 = jnp.maximum(v[0], v[1]), jnp.minimum(v[0], v[1])
        hi2, lo2 = jnp.maximum(v[2], v[3]), jnp.minimum(v[2], v[3])
        return jnp.maximum(hi1, hi2) + jnp.maximum(jnp.minimum(hi1, hi2), jnp.maximum(lo1, lo2))

    best = top2_sum(s[0:eg])
    grp = jnp.zeros_like(best, dtype=jnp.int32)
    for g in range(1, N_GROUPS):
        sc = top2_sum(s[g * eg:(g + 1) * eg])
        better = sc > best
        best = jnp.where(better, sc, best)
        grp = jnp.where(better, g, grp)

    def pick(vals, j):
        out = vals[j]
        for g in range(1, N_GROUPS):
            out = jnp.where(grp == g, vals[g * eg + j], out)
        return out

    sv = [pick(s, j) for j in range(eg)]
    av = [pick(a, j) for j in range(eg)]

    def argmax_first(vals, exclude):
        bi = jnp.zeros_like(grp)
        bv = jnp.where(exclude == 0, -jnp.inf, vals[0]) if exclude is not None else vals[0]
        for j in range(1, eg):
            vj = jnp.where(exclude == j, -jnp.inf, vals[j]) if exclude is not None else vals[j]
            better = vj > bv
            bv = jnp.where(better, vj, bv)
            bi = jnp.where(better, j, bi)
        return bi

    i1 = argmax_first(sv, None)
    i2 = argmax_first(sv, i1)

    def take(vals, i):
        out = vals[0]
        for j in range(1, eg):
            out = jnp.where(i == j, vals[j], out)
        return out

    w1 = take(av, i1)
    w2 = take(av, i2)
    tot = w1 + w2
    idx_ref[0:1, :] = grp * eg + i1
    idx_ref[1:2, :] = grp * eg + i2
    w_ref[0:1, :] = w1 / tot
    w_ref[1:2, :] = w2 / tot


def route(logits_t, router_b, tile=1024):
    n_exp, t = logits_t.shape
    tile = math.gcd(tile, t)
    return pl.pallas_call(
        _route_kernel,
        grid=(t // tile,),
        in_specs=[pl.BlockSpec((n_exp, tile), lambda i: (0, i)),
                  pl.BlockSpec((n_exp, 1), lambda i: (0, 0))],
        out_specs=[pl.BlockSpec((TOP_K, tile), lambda i: (0, i)),
                   pl.BlockSpec((TOP_K, tile), lambda i: (0, i))],
        out_shape=[jax.ShapeDtypeStruct((TOP_K, t), jnp.int32), jax.ShapeDtypeStruct((TOP_K, t), F32)],
        compiler_params=_cparams("arbitrary"),
        name="moe_route",
    )(logits_t, router_b.reshape(n_exp, 1).astype(F32))


def moe_plan(idx, tm):
    t = idx.shape[1]
    n_pair = TOP_K * t
    n_tiles = (n_pair + N_EXPERTS * (tm - 1)) // tm
    e_flat = idx.reshape(-1)
    order = jnp.argsort(e_flat, stable=True).astype(jnp.int32)
    counts = jnp.sum((e_flat[:, None] == jnp.arange(N_EXPERTS)[None, :]).astype(jnp.int32), 0)
    tiles_per = (counts + tm - 1) // tm
    tile_end = jnp.cumsum(tiles_per)
    n_used = tile_end[-1]
    pstart = (tile_end - tiles_per) * tm
    ustart = jnp.cumsum(counts) - counts
    tile_expert = jnp.minimum(jnp.searchsorted(tile_end, jnp.arange(n_tiles), side="right"),
                              N_EXPERTS - 1).astype(jnp.int32)
    p = jnp.arange(n_tiles * tm)
    e_p = tile_expert[p // tm]
    rank = p - pstart[e_p]
    valid = rank < counts[e_p]
    src = jnp.clip(ustart[e_p] + rank, 0, n_pair - 1)
    gidx = jnp.where(valid, order[src] % t, 0).astype(jnp.int32)
    e_sorted = e_flat[order]
    dest_sorted = pstart[e_sorted] + jnp.arange(n_pair) - ustart[e_sorted]
    pos = jnp.zeros((n_pair,), jnp.int32).at[order].set(dest_sorted.astype(jnp.int32))
    return gidx, tile_expert, n_used.reshape(1).astype(jnp.int32), pos.reshape(TOP_K, t)


def _ffn_kernel(te_ref, nu_ref, xs_ref, wg_ref, wu_ref, wd_ref, o_ref):
    used = pl.program_id(0) < nu_ref[0]

    @pl.when(used)
    def _():
        xs = xs_ref[...]
        g = _dot(xs, wg_ref[0])
        u = _dot(xs, wu_ref[0])
        hid = (g * jax.nn.sigmoid(g)) * u
        o_ref[...] = _dot(hid.astype(BF16), wd_ref[0])

    @pl.when(jnp.logical_not(used))
    def _():
        o_ref[...] = jnp.zeros_like(o_ref)


def grouped_ffn(xs, tile_expert, n_used, w_gate, w_up, w_down, tm):
    p, d = xs.shape
    de = w_gate.shape[2]
    n_tiles = p // tm
    return pl.pallas_call(
        _ffn_kernel,
        grid_spec=pltpu.PrefetchScalarGridSpec(
            num_scalar_prefetch=2,
            grid=(n_tiles,),
            in_specs=[pl.BlockSpec((tm, d), lambda i, te, nu: (i, 0)),
                      pl.BlockSpec((1, d, de), lambda i, te, nu: (te[i], 0, 0)),
                      pl.BlockSpec((1, d, de), lambda i, te, nu: (te[i], 0, 0)),
                      pl.BlockSpec((1, de, d), lambda i, te, nu: (te[i], 0, 0))],
            out_specs=pl.BlockSpec((tm, d), lambda i, te, nu: (i, 0))),
        out_shape=jax.ShapeDtypeStruct((p, d), F32),
        compiler_params=_cparams("arbitrary"),
        name="moe_ffn",
    )(tile_expert, n_used, xs, w_gate, w_up, w_down)


def _final_kernel(x_ref, y0_ref, y1_ref, w_ref, m5_ref, lg_ref, lb_ref, o_ref, *, alpha):
    w = w_ref[...]
    y = w[:, 0:1] * y0_ref[...] + w[:, 1:2] * y1_ref[...]
    z = alpha * x_ref[...] + m5_ref[...] * y
    mu = jnp.mean(z, -1, keepdims=True)
    zc = z - mu
    var = jnp.mean(zc * zc, -1, keepdims=True)
    o_ref[...] = zc * lax.rsqrt(var + LN_EPS) * lg_ref[...] + lb_ref[...]


def final_norm(x1, y0, y1, wts, mods4, layer, ln_g, ln_b, alpha, seg_rows, n_batch, tm=256):
    t, d = y0.shape
    seg = _seg_fn(tm, seg_rows, n_batch)
    row = pl.BlockSpec((tm, d), lambda i: (i, 0))
    vec = pl.BlockSpec((1, d), lambda i: (0, 0))
    return pl.pallas_call(
        functools.partial(_final_kernel, alpha=alpha),
        grid=(t // tm,),
        in_specs=[row, row, row, pl.BlockSpec((tm, TOP_K), lambda i: (i, 0)),
                  _mod_spec(d, layer, 5, seg), vec, vec],
        out_specs=row,
        out_shape=jax.ShapeDtypeStruct((t, d), F32),
        compiler_params=_cparams("arbitrary"),
        name="final_norm",
    )(x1, y0, y1, wts, mods4, ln_g.reshape(1, -1), ln_b.reshape(1, -1))


def moe_block(x1, h2, logits_t, router_b, w_gate, w_up, w_down, mods4, layer, ln_g, ln_b,
              alpha, seg_rows, n_batch, tm=256):
    t = h2.shape[0]
    idx, wts = route(logits_t, router_b)
    gidx, tile_expert, n_used, pos = moe_plan(idx, tm)
    xs = jnp.take(h2, gidx, axis=0)
    ys = grouped_ffn(xs, tile_expert, n_used, w_gate, w_up, w_down, tm)
    y0 = jnp.take(ys, pos[0], axis=0)
    y1 = jnp.take(ys, pos[1], axis=0)
    return final_norm(x1, y0, y1, wts.T, mods4, layer, ln_g, ln_b, alpha, seg_rows, n_batch)


def s5_matrices(lam_re, lam_im, log_dt, b_re, b_im, c_re, c_im, d_skip):
    f32 = F32
    tc = S5_TC
    n_g, n_p = lam_re.shape[1], lam_re.shape[2]
    n_c = b_re.shape[-1]
    n_q = n_g // 2
    lr, li = lam_re.astype(f32), lam_im.astype(f32)
    dt = jnp.exp(log_dt.astype(f32))[..., None]
    j = jnp.arange(tc + 1, dtype=f32)[:, None, None, None]
    mag = jnp.exp(lr * dt * j)
    pw_re, pw_im = mag * jnp.cos(li * dt * j), mag * jnp.sin(li * dt * j)
    lb_re, lb_im = pw_re[1], pw_im[1]
    den = lr * lr + li * li
    fr = ((lb_re - 1.0) * lr + lb_im * li) / den
    fi = (lb_im * lr - (lb_re - 1.0) * li) / den
    br, bi = b_re.astype(f32), b_im.astype(f32)
    bb_re = fr[..., None] * br - fi[..., None] * bi
    bb_im = fr[..., None] * bi + fi[..., None] * br
    cr, ci = c_re.astype(f32), c_im.astype(f32)
    e_re = pw_re[..., None] * bb_re[None] - pw_im[..., None] * bb_im[None]
    e_im = pw_re[..., None] * bb_im[None] + pw_im[..., None] * bb_re[None]
    kmat = (jnp.einsum("dgcp,jdgpe->jdgce", cr, e_re, precision=HIGHEST)
            - jnp.einsum("dgcp,jdgpe->jdgce", ci, e_im, precision=HIGHEST))
    s_idx = np.arange(tc)[:, None]
    t_idx = np.arange(tc)[None, :]

    def toeplitz(k_dir, lag, ok):
        m = k_dir[np.clip(lag, 0, tc)]
        m = jnp.where(ok[:, :, None, None, None], m, 0.0)
        return m.transpose(2, 0, 4, 1, 3)

    mt = toeplitz(kmat[:, 0], t_idx - s_idx, t_idx >= s_idx) + toeplitz(kmat[:, 1], s_idx - t_idx, s_idx >= t_idx)
    eye_t = jnp.eye(tc, dtype=f32)
    eye_c = jnp.eye(n_c, dtype=f32)
    mt = mt + (eye_t[None, :, None, :, None] * eye_c[None, None, :, None, :]
               * d_skip.astype(f32)[:, None, None, None, :])
    mt = mt.reshape(n_g, tc * n_c, tc * n_c)

    def pair_diag(m):
        r, c = m.shape[1], m.shape[2]
        m = m.reshape(n_q, 2, r, c)
        z = jnp.zeros((n_q, r, c), f32)
        top = jnp.concatenate([m[:, 0], z], -1)
        bot = jnp.concatenate([z, m[:, 1]], -1)
        return jnp.concatenate([top, bot], 1)

    mt_q = pair_diag(mt)
    s_ar = np.arange(tc)
    def w_of(e, d, powers):
        m = e[powers, d]
        return m.transpose(1, 0, 3, 2).reshape(n_g, tc * n_c, n_p)
    w_q = jnp.concatenate([pair_diag(w_of(e_re, 0, tc - 1 - s_ar)), pair_diag(w_of(e_im, 0, tc - 1 - s_ar)),
                           pair_diag(w_of(e_re, 1, s_ar)), pair_diag(w_of(e_im, 1, s_ar))], -1)
    def v_of(d, powers):
        p_re, p_im = pw_re[powers, d], pw_im[powers, d]
        f_re = cr[d][None] * p_re[:, :, None, :] - ci[d][None] * p_im[:, :, None, :]
        f_im = cr[d][None] * p_im[:, :, None, :] + ci[d][None] * p_re[:, :, None, :]
        shape = lambda m: m.transpose(1, 3, 0, 2).reshape(n_g, n_p, tc * n_c)
        return pair_diag(shape(f_re)), pair_diag(shape(-f_im))
    vf_re, vf_im = v_of(0, s_ar + 1)
    vb_re, vb_im = v_of(1, tc - s_ar)
    v_q = jnp.stack([vf_re, vf_im, vb_re, vb_im], 1)
    dec = lambda m: m.reshape(1, n_g * n_p)
    decay = jnp.concatenate([dec(pw_re[tc, 0]), dec(pw_im[tc, 0]), dec(pw_re[tc, 1]), dec(pw_im[tc, 1])], 0)
    return mt_q.astype(BF16), w_q.astype(BF16), v_q.astype(BF16), decay


def _s5_in_kernel(u_ref, w_ref, fr_ref, fi_ref, br_ref, bi_ref):
    w = _dot(u_ref[...].astype(BF16), w_ref[...])
    n = fr_ref.shape[1]
    fr_ref[...] = w[:, 0:n]
    fi_ref[...] = w[:, n:2 * n]
    br_ref[...] = w[:, 2 * n:3 * n]
    bi_ref[...] = w[:, 3 * n:4 * n]


def _s5_scan_kernel(wfr, wfi, wbr, wbi, dec_ref, xfr, xfi, xbr, xbi, *, n_tiles):
    lanes = wfr.shape[1]
    low = lax.broadcasted_iota(jnp.int32, (8, lanes), 0) < 4
    a_fr, a_fi = dec_ref[0:1, :], dec_ref[1:2, :]
    a_br, a_bi = dec_ref[2:3, :], dec_ref[3:4, :]

    def half_step(s_re, s_im, a_re, a_im, w_re, w_im):
        return a_re * s_re - a_im * s_im + w_re, a_re * s_im + a_im * s_re + w_im

    def one_dir(w_re_ref, w_im_ref, x_re_ref, x_im_ref, row0, s_re, s_im, a_re, a_im, first_low):
        first = low if first_low else jnp.logical_not(low)
        wt_re, wt_im = w_re_ref[pl.ds(row0, 8), :], w_im_ref[pl.ds(row0, 8), :]
        wr_re, wr_im = pltpu.roll(wt_re, 4, 0), pltpu.roll(wt_im, 4, 0)
        mid_re, mid_im = half_step(s_re, s_im, a_re, a_im, wr_re, wr_im)
        x_re_ref[pl.ds(row0, 8), :] = jnp.where(first, s_re, mid_re)
        x_im_ref[pl.ds(row0, 8), :] = jnp.where(first, s_im, mid_im)
        m_re = jnp.where(first, pltpu.roll(mid_re, 4, 0), mid_re)
        m_im = jnp.where(first, pltpu.roll(mid_im, 4, 0), mid_im)
        w2_re = jnp.where(first, wr_re, wt_re)
        w2_im = jnp.where(first, wr_im, wt_im)
        return half_step(m_re, m_im, a_re, a_im, w2_re, w2_im)

    def body(i, carry):
        f_re, f_im, b_re, b_im = carry
        rf = pl.multiple_of(i * 8, 8)
        rb = pl.multiple_of((n_tiles - 1 - i) * 8, 8)
        f_re, f_im = one_dir(wfr, wfi, xfr, xfi, rf, f_re, f_im, a_fr, a_fi, True)
        b_re, b_im = one_dir(wbr, wbi, xbr, xbi, rb, b_re, b_im, a_br, a_bi, False)
        return f_re, f_im, b_re, b_im

    z = jnp.zeros((8, lanes), F32)
    lax.fori_loop(0, n_tiles, body, (z, z, z, z))


def _s5_out_kernel(u_ref, fr_ref, fi_ref, br_ref, bi_ref, mt_ref, v_ref, y_ref):
    y = _dot(u_ref[...].astype(BF16), mt_ref[...])
    y = y + _dot(fr_ref[...].astype(BF16), v_ref[0])
    y = y + _dot(fi_ref[...].astype(BF16), v_ref[1])
    y = y + _dot(br_ref[...].astype(BF16), v_ref[2])
    y = y + _dot(bi_ref[...].astype(BF16), v_ref[3])
    y_ref[...] = y


def s5_bidir(u, mats, n_batch, l_lat, l_ctx):
    assert n_batch == 4, "the chunk scan packs two chunks of 4 batch rows per 8-sublane tile"
    mt_q, w_q, v_q, decay = mats
    tc = S5_TC
    wd = u.shape[1]
    n_q = wd // (2 * S5_CH)
    lane_q = 2 * S5_CH * tc
    st_q = 2 * S5_P
    u_lat = u[:n_batch * l_lat].reshape(n_batch, l_lat, wd)
    u_ctx = u[n_batch * l_lat:].reshape(n_batch, l_ctx, wd)
    z = jnp.concatenate([u_ctx, u_lat, u_ctx], 1)
    nk = z.shape[1] // tc
    assert nk % 2 == 0
    rows = nk * n_batch
    uq = z.reshape(n_batch, nk, tc, n_q, 2, S5_CH).transpose(3, 1, 0, 4, 2, 5).reshape(n_q, rows, lane_q)

    plane = jax.ShapeDtypeStruct((rows, n_q * st_q), F32)
    plane_spec = pl.BlockSpec((rows, st_q), lambda q: (0, q))
    u_spec = pl.BlockSpec((None, rows, lane_q), lambda q: (q, 0, 0))
    w_planes = pl.pallas_call(
        _s5_in_kernel,
        grid=(n_q,),
        in_specs=[u_spec, pl.BlockSpec((None, lane_q, 4 * st_q), lambda q: (q, 0, 0))],
        out_specs=[plane_spec] * 4,
        out_shape=[plane] * 4,
        compiler_params=_cparams("arbitrary"),
        name="s5_chunk_in",
    )(uq, w_q)

    lb = min(512, n_q * st_q)
    blk = pl.BlockSpec((rows, lb), lambda j: (0, j))
    x_planes = pl.pallas_call(
        functools.partial(_s5_scan_kernel, n_tiles=rows // 8),
        grid=(n_q * st_q // lb,),
        in_specs=[blk] * 4 + [pl.BlockSpec((4, lb), lambda j: (0, j))],
        out_specs=[blk] * 4,
        out_shape=[plane] * 4,
        compiler_params=_cparams("arbitrary"),
        name="s5_chunk_scan",
    )(*w_planes, decay)

    yq = pl.pallas_call(
        _s5_out_kernel,
        grid=(n_q,),
        in_specs=[u_spec] + [plane_spec] * 4
                 + [pl.BlockSpec((None, lane_q, lane_q), lambda q: (q, 0, 0)),
                    pl.BlockSpec((None, 4, st_q, lane_q), lambda q: (q, 0, 0, 0))],
        out_specs=pl.BlockSpec((None, rows, lane_q), lambda q: (q, 0, 0)),
        out_shape=jax.ShapeDtypeStruct((n_q, rows, lane_q), F32),
        compiler_params=_cparams("arbitrary"),
        name="s5_chunk_out",
    )(uq, *x_planes, mt_q, v_q)

    y = yq.reshape(n_q, nk, n_batch, 2, tc, S5_CH).transpose(2, 1, 4, 0, 3, 5).reshape(n_batch, nk * tc, wd)
    return y[:, l_ctx:l_ctx + l_lat].reshape(n_batch * l_lat, wd)


def kernel(x, c, ctx, c_ctx, ada_w, ada_b, ln_mix_g, ln_mix_b, ln_ffn_g, ln_ffn_b, ev_w_in, ev_gate_w2,
           ev_gate_b, ev_rpb, ev_norm_g, ev_w_out, od_w_in, od_lam_re, od_lam_im, od_log_dt, od_b_re,
           od_b_im, od_c_re, od_c_im, od_d, od_w_glu, od_b_glu, od_w_out, router_w, router_b,
           moe_w_gate, moe_w_up, moe_w_down):
    n_batch, l_lat, d = x.shape
    l_ctx = ctx.shape[1]
    depth = ada_w.shape[0]
    assert depth == 2, "one even (NA + GLA) layer followed by one odd (S5) layer"
    alpha = (2.0 * depth) ** 0.25
    n_lat = n_batch * l_lat

    cvec = jnp.concatenate([c, c_ctx[None], jnp.zeros((8 - n_batch - 1, d), F32)], 0)
    mods = compute_mods(cvec, ada_w, ada_b)
    mods4 = mods.reshape(depth, 8, 1, N_MOD * d)
    rows = jnp.concatenate([x.reshape(n_lat, d), ctx.reshape(n_batch * l_ctx, d)], 0)
    router_wt = router_w.T.astype(F32)

    na_w = NA_HEADS * NA_DH
    wk = GLA_HEADS * GLA_DK
    wv = GLA_HEADS * GLA_DV
    ev_in = ev_w_in.shape[2]
    pad = (-ev_in) % 256
    w_in = jnp.pad(ev_w_in[0], ((0, 0), (0, pad))).astype(BF16)
    proj = mod_matmul(rows, mods4, 0, w_in, l_lat, n_batch, tm=512, tn=(ev_in + pad) // 2)
    a_lat, a_ctx = na_attention(proj, na_bias_table(ev_rpb[0]), n_batch, l_lat, l_ctx)
    col_lr = (3 * na_w + 2 * wk + 2 * wv) // 128
    g2 = jnp.zeros((2, 128, wk), F32)
    g2 = g2.at[0, 0:GLA_RANK].set(ev_gate_w2[0, 0]).at[1, GLA_RANK:2 * GLA_RANK].set(ev_gate_w2[0, 1])
    o_f, o_b = gla_bidir(proj, g2.astype(BF16), ev_gate_b[0].reshape(2, 1, wk), rope_tables(l_lat, l_ctx),
                         n_batch, l_lat, l_ctx,
                         col_q=3 * na_w // wk, col_k=(3 * na_w + wk) // wk,
                         col_v=(3 * na_w + 2 * wk) // wv, col_lr=col_lr)
    a_all = jnp.concatenate([a_lat, a_ctx], 0)
    x1, h2, logits_t = even_out(a_all, o_f, o_b, proj, (3 * na_w + 2 * wk + wv) // wv, rows, mods4, 0,
                                ev_norm_g[0], ev_w_out[0].astype(BF16), ln_mix_g[0], ln_mix_b[0],
                                router_wt, alpha, l_lat, n_batch)
    rows = moe_block(x1, h2, logits_t, router_b, moe_w_gate[0].astype(BF16), moe_w_up[0].astype(BF16),
                     moe_w_down[0].astype(BF16), mods4, 0, ln_ffn_g[0], ln_ffn_b[0], alpha, l_lat, n_batch)

    u = mod_matmul(rows, mods4, 1, od_w_in[0].astype(BF16), l_lat, n_batch)
    mats = s5_matrices(od_lam_re[0], od_lam_im[0], od_log_dt[0], od_b_re[0], od_b_im[0],
                       od_c_re[0], od_c_im[0], od_d[0])
    y5 = s5_bidir(u, mats, n_batch, l_lat, l_ctx)
    x1, h2, logits_t = odd_out(y5, rows[:n_lat], mods4, 1, od_w_glu[0].astype(BF16), od_b_glu[0],
                               od_w_out[0].astype(BF16), ln_mix_g[1], ln_mix_b[1], router_wt,
                               alpha, l_lat, n_batch)
    out = moe_block(x1, h2, logits_t, router_b, moe_w_gate[1].astype(BF16), moe_w_up[1].astype(BF16),
                    moe_w_down[1].astype(BF16), mods4, 1, ln_ffn_g[1], ln_ffn_b[1], alpha, l_lat, n_batch)
    return out.reshape(n_batch, l_lat, d)
```

```python
import functools
import math

import numpy as np
import jax
import jax.numpy as jnp
from jax import lax
from jax.experimental import pallas as pl
from jax.experimental.pallas import tpu as pltpu

F32 = jnp.float32
BF16 = jnp.bfloat16
HIGHEST = lax.Precision.HIGHEST

N_MOD = 6
LN_EPS = 1e-5
NORM_EPS = 1e-6

GRID_W = 64
NA_HEADS = 8
NA_DH = 128
NA_KR = 8
NA_KC = 16

GLA_HEADS = 4
GLA_DK = 128
GLA_DV = 256
GLA_RANK = 16
GLA_TAU = 16.0
GLA_CHUNK = 64
ROPE_BASE = 10000.0

S5_CH = 16
S5_P = 64
S5_TC = 16

N_EXPERTS = 16
N_GROUPS = 4
TOP_K = 2

VMEM_LIMIT = 56 * 1024 * 1024
NEG_BIG = -1e30


def _cparams(*sem):
    return pltpu.CompilerParams(dimension_semantics=sem, vmem_limit_bytes=VMEM_LIMIT)


def _dot(a, b, precision=None):
    return jnp.dot(a, b, preferred_element_type=F32, precision=precision)


def _dot_nt(a, b, precision=None):
    return lax.dot_general(a, b, (((1,), (1,)), ((), ())), preferred_element_type=F32, precision=precision)


def _dot_tn(a, b):
    return lax.dot_general(a, b, (((0,), (0,)), ((), ())), preferred_element_type=F32)


def _mods_kernel(s_ref, w_ref, b_ref, o_ref):
    s = s_ref[...]
    s = s * jax.nn.sigmoid(s)
    o_ref[0] = _dot(s, w_ref[0], HIGHEST) + b_ref[0]


def compute_mods(cvec, ada_w, ada_b, tn=1024):
    n_layer, d, n = ada_w.shape
    tn = math.gcd(tn, n)
    return pl.pallas_call(
        _mods_kernel,
        grid=(n_layer, n // tn),
        in_specs=[pl.BlockSpec((8, d), lambda l, j: (0, 0)),
                  pl.BlockSpec((1, d, tn), lambda l, j: (l, 0, j)),
                  pl.BlockSpec((1, 1, tn), lambda l, j: (l, 0, j))],
        out_specs=pl.BlockSpec((1, 8, tn), lambda l, j: (l, 0, j)),
        out_shape=jax.ShapeDtypeStruct((n_layer, 8, n), F32),
        compiler_params=_cparams("arbitrary", "arbitrary"),
        name="ada_mods",
    )(cvec, ada_w, ada_b.reshape(n_layer, 1, n))


def _mod_spec(d, layer, which, seg_of_tile):
    return pl.BlockSpec((None, None, 1, d), lambda i, *_: (layer, seg_of_tile(i), 0, which))


def _seg_fn(tm, seg_rows, n_batch):
    return lambda i: jnp.minimum((i * tm) // seg_rows, n_batch)


def _modmm_kernel(x_ref, s1_ref, s0_ref, w_ref, o_ref):
    h = x_ref[...] * (1.0 + s1_ref[...]) + s0_ref[...]
    o_ref[...] = _dot(h.astype(BF16), w_ref[...])


def mod_matmul(x, mods4, layer, w_bf16, seg_rows, n_batch, tm=256, tn=None):
    t, d = x.shape
    n = w_bf16.shape[1]
    tn = n if tn is None else tn
    seg = _seg_fn(tm, seg_rows, n_batch)
    return pl.pallas_call(
        _modmm_kernel,
        grid=(n // tn, t // tm),
        in_specs=[pl.BlockSpec((tm, d), lambda j, i: (i, 0)),
                  pl.BlockSpec((None, None, 1, d), lambda j, i: (layer, seg(i), 0, 1)),
                  pl.BlockSpec((None, None, 1, d), lambda j, i: (layer, seg(i), 0, 0)),
                  pl.BlockSpec((d, tn), lambda j, i: (0, j))],
        out_specs=pl.BlockSpec((tm, tn), lambda j, i: (i, j)),
        out_shape=jax.ShapeDtypeStruct((t, n), F32),
        compiler_params=_cparams("arbitrary", "arbitrary"),
        name="mod_matmul",
    )(x, mods4, mods4, w_bf16)


def na_bias_table(rpb):
    w = GRID_W
    q = np.arange(w)
    kc = np.arange(w)
    win0 = np.clip(q - NA_KC // 2, 0, w - NA_KC)
    ok = (kc[None, :] >= win0[:, None]) & (kc[None, :] < win0[:, None] + NA_KC)
    dc = np.clip(kc[None, :] - q[:, None] + NA_KC - 1, 0, 2 * NA_KC - 2)
    pick = ((dc[None] == np.arange(2 * NA_KC - 1)[:, None, None]) & ok[None]).astype(np.float32)
    colb = jnp.einsum("hrd,dqk->hrqk", rpb.astype(F32), jnp.asarray(pick), precision=HIGHEST)
    colb = jnp.where(ok[None, None], colb, NEG_BIG)
    tab = jnp.stack([colb[:, v:v + NA_KR] for v in range(NA_KR)], 1)
    return tab.transpose(0, 1, 3, 2, 4).reshape(rpb.shape[0], NA_KR, w, NA_KR * w)


def _na_kernel(q_ref, k_ref, v_ref, qc_ref, kc_ref, vc_ref, bias_ref, o_ref, oc_ref, kbf, vbf, *, rows):
    w = GRID_W
    scale = NA_DH ** -0.5
    kbf[...] = k_ref[...].astype(BF16)
    vbf[...] = v_ref[...].astype(BF16)
    kc = kc_ref[...].astype(BF16)
    vc = vc_ref[...].astype(BF16)

    def body(r, carry):
        rs = jnp.clip(r - NA_KR // 2, 0, rows - NA_KR)
        var = rs - r + (NA_KR - 1)
        q = (q_ref[pl.ds(pl.multiple_of(r * w, w), w), :] * scale).astype(BF16)
        k0 = pl.multiple_of(rs * w, w)
        kb = kbf[pl.ds(k0, NA_KR * w), :]
        vb = vbf[pl.ds(k0, NA_KR * w), :]
        s_loc = _dot_nt(q, kb) + bias_ref[var]
        s_ctx = _dot_nt(q, kc)
        m = jnp.maximum(jnp.max(s_loc, -1, keepdims=True), jnp.max(s_ctx, -1, keepdims=True))
        p_loc = jnp.exp(s_loc - m)
        p_ctx = jnp.exp(s_ctx - m)
        den = jnp.sum(p_loc, -1, keepdims=True) + jnp.sum(p_ctx, -1, keepdims=True)
        o = _dot(p_loc.astype(BF16), vb) + _dot(p_ctx.astype(BF16), vc)
        o_ref[pl.ds(pl.multiple_of(r * w, w), w), :] = o / den
        return carry

    lax.fori_loop(0, rows, body, 0)

    qc = (qc_ref[...] * scale).astype(BF16)
    s = _dot_nt(qc, kc)
    p = jnp.exp(s - jnp.max(s, -1, keepdims=True))
    oc_ref[...] = _dot(p.astype(BF16), vc) / jnp.sum(p, -1, keepdims=True)


def na_attention(proj, bias_tab, n_batch, l_lat, l_ctx):
    h = NA_HEADS
    dh = NA_DH
    rows = l_lat // GRID_W
    ctx0 = (n_batch * l_lat) // l_ctx
    return pl.pallas_call(
        functools.partial(_na_kernel, rows=rows),
        grid=(n_batch, h),
        in_specs=[pl.BlockSpec((l_lat, dh), lambda b, hh: (b, hh)),
                  pl.BlockSpec((l_lat, dh), lambda b, hh: (b, h + hh)),
                  pl.BlockSpec((l_lat, dh), lambda b, hh: (b, 2 * h + hh)),
                  pl.BlockSpec((l_ctx, dh), lambda b, hh: (ctx0 + b, hh)),
                  pl.BlockSpec((l_ctx, dh), lambda b, hh: (ctx0 + b, h + hh)),
                  pl.BlockSpec((l_ctx, dh), lambda b, hh: (ctx0 + b, 2 * h + hh)),
                  pl.BlockSpec((None, NA_KR, GRID_W, NA_KR * GRID_W), lambda b, hh: (hh, 0, 0, 0))],
        out_specs=[pl.BlockSpec((l_lat, dh), lambda b, hh: (b, hh)),
                   pl.BlockSpec((l_ctx, dh), lambda b, hh: (b, hh))],
        out_shape=[jax.ShapeDtypeStruct((n_batch * l_lat, h * dh), F32),
                   jax.ShapeDtypeStruct((n_batch * l_ctx, h * dh), F32)],
        scratch_shapes=[pltpu.VMEM((l_lat, dh), BF16), pltpu.VMEM((l_lat, dh), BF16)],
        compiler_params=_cparams("arbitrary", "arbitrary"),
        name="na_attention",
    )(proj, proj, proj, proj, proj, proj, bias_tab)


def rope_tables(l_lat, l_ctx):
    half = GLA_DK // 2
    nf = half // 2
    inv = ROPE_BASE ** (-np.arange(nf, dtype=np.float64) / nf)
    t = np.arange(l_lat)
    lane = np.arange(GLA_DK)
    pos = np.where(lane[None, :] < half, (t // GRID_W)[:, None], (t % GRID_W)[:, None]).astype(np.float64)
    ang = pos * inv[lane % nf][None, :]
    first = (lane % half) < nf
    cos = np.cos(ang)
    sin_a = np.where(first[None, :], -np.sin(ang), 0.0)
    sin_b = np.where(first[None, :], 0.0, np.sin(ang))
    one = np.ones((l_ctx, GLA_DK))
    zero = np.zeros((l_ctx, GLA_DK))
    cat = lambda ident, a: jnp.asarray(np.concatenate([ident, a, ident], 0), F32)
    return cat(one, cos), cat(zero, sin_a), cat(zero, sin_b)


def _gla_direction(q_ref, k_ref, v_ref, lr_ref, cos_ref, sa_ref, sb_ref, g2_ref, gb_ref, st_ref, o_ref, reverse):
    c = GLA_CHUNK
    nf = GLA_DK // 4
    gscale = GLA_DK ** -0.5
    row = lax.broadcasted_iota(jnp.int32, (c, c), 0)
    col = lax.broadcasted_iota(jnp.int32, (c, c), 1)
    seen = (col >= row) if reverse else (col <= row)
    tri = seen.astype(F32)

    z = _dot(lr_ref[...].astype(BF16), g2_ref[...]) + gb_ref[...]
    g = (jnp.minimum(z, 0.0) - jnp.log1p(jnp.exp(-jnp.abs(z)))) * (1.0 / GLA_TAU)
    b = _dot(tri, g, HIGHEST)
    b_last = b[0:1, :] if reverse else b[c - 1:c, :]
    e_b = jnp.exp(b)
    e_nb = jnp.exp(-b)
    e_rem = jnp.exp(b_last - b)
    e_last = jnp.exp(b_last)
    cos = cos_ref[...]
    sa = sa_ref[...]
    sb = sb_ref[...]

    def rope(x):
        return x * cos + pltpu.roll(x, GLA_DK - nf, 1) * sa + pltpu.roll(x, nf, 1) * sb

    for h in range(GLA_HEADS):
        ks = slice(h * GLA_DK, (h + 1) * GLA_DK)
        vs = slice(h * GLA_DV, (h + 1) * GLA_DV)
        qh = rope(q_ref[:, ks]) * gscale
        kh = rope(k_ref[:, ks])
        q_dec = (qh * e_b[:, ks]).astype(BF16)
        k_dec = (kh * e_nb[:, ks]).astype(BF16)
        k_rem = (kh * e_rem[:, ks]).astype(BF16)
        vh = v_ref[:, vs].astype(BF16)
        att = jnp.where(seen, _dot_nt(q_dec, k_dec), 0.0)
        st = st_ref[h]
        o_ref[:, vs] = _dot(att.astype(BF16), vh) + _dot_nt(q_dec, st.astype(BF16))
        st_ref[h] = st * e_last[:, ks] + _dot_tn(vh, k_rem)


def _gla_kernel(qf, kf, vf, lrf, cf, saf, sbf, qb, kb, vb, lrb, cb, sab, sbb, g2_ref, gb_ref,
                of_ref, ob_ref, st_ref):
    @pl.when(pl.program_id(1) == 0)
    def _():
        st_ref[...] = jnp.zeros_like(st_ref)

    _gla_direction(qf, kf, vf, lrf, cf, saf, sbf, g2_ref.at[0], gb_ref.at[0], st_ref.at[0], of_ref, False)
    _gla_direction(qb, kb, vb, lrb, cb, sab, sbb, g2_ref.at[1], gb_ref.at[1], st_ref.at[1], ob_ref, True)


def gla_bidir(proj, g2, gb, tables, n_batch, l_lat, l_ctx, col_q, col_k, col_v, col_lr):
    c = GLA_CHUNK
    nc = l_ctx // c
    nl = l_lat // c
    nz = nl + 2 * nc
    steps = nl + nc
    wk = GLA_HEADS * GLA_DK
    wv = GLA_HEADS * GLA_DV
    t_rows = n_batch * (l_lat + l_ctx)

    def zblk(b, j):
        lat = b * nl + (j - nc)
        ctx = n_batch * nl + b * nc + jnp.where(j < nc, j, j - nc - nl)
        return jnp.where((j >= nc) & (j < nc + nl), lat, ctx)

    fwd = lambda b, i: zblk(b, i)
    bwd = lambda b, i: zblk(b, nz - 1 - i)

    def dir_specs(blk, tab):
        return [pl.BlockSpec((c, wk), lambda b, i: (blk(b, i), col_q)),
                pl.BlockSpec((c, wk), lambda b, i: (blk(b, i), col_k)),
                pl.BlockSpec((c, wv), lambda b, i: (blk(b, i), col_v)),
                pl.BlockSpec((c, 128), lambda b, i: (blk(b, i), col_lr)),
                pl.BlockSpec((c, GLA_DK), lambda b, i: (tab(i), 0)),
                pl.BlockSpec((c, GLA_DK), lambda b, i: (tab(i), 0)),
                pl.BlockSpec((c, GLA_DK), lambda b, i: (tab(i), 0))]

    cos, sa, sb = tables
    return pl.pallas_call(
        _gla_kernel,
        grid=(n_batch, steps),
        in_specs=(dir_specs(fwd, lambda i: i) + dir_specs(bwd, lambda i: nz - 1 - i)
                  + [pl.BlockSpec((2, 128, wk), lambda b, i: (0, 0, 0)),
                     pl.BlockSpec((2, 1, wk), lambda b, i: (0, 0, 0))]),
        out_specs=[pl.BlockSpec((c, wv), lambda b, i: (fwd(b, i), 0)),
                   pl.BlockSpec((c, wv), lambda b, i: (bwd(b, i), 0))],
        out_shape=[jax.ShapeDtypeStruct((t_rows, wv), F32), jax.ShapeDtypeStruct((t_rows, wv), F32)],
        scratch_shapes=[pltpu.VMEM((2, GLA_HEADS, GLA_DV, GLA_DK), F32)],
        compiler_params=_cparams("arbitrary", "arbitrary"),
        name="gla_bidir",
    )(proj, proj, proj, proj, cos, sa, sb, proj, proj, proj, proj, cos, sa, sb, g2, gb)


def _post_mix(out, x_ref, m2_ref, m3_ref, m4_ref, lg_ref, lb_ref, wr_ref, alpha, x1_ref, h2_ref, lt_ref):
    y = alpha * x_ref[...] + m2_ref[...] * out
    mu = jnp.mean(y, -1, keepdims=True)
    yc = y - mu
    var = jnp.mean(yc * yc, -1, keepdims=True)
    x1 = yc * lax.rsqrt(var + LN_EPS) * lg_ref[...] + lb_ref[...]
    h2 = x1 * (1.0 + m4_ref[...]) + m3_ref[...]
    x1_ref[...] = x1
    h2_ref[...] = h2
    lt_ref[...] = _dot_nt(wr_ref[...], h2, HIGHEST)


def _even_out_kernel(a_ref, of_ref, ob_ref, r_ref, x_ref, m2_ref, m3_ref, m4_ref, ng_ref, wo_ref,
                     lg_ref, lb_ref, wr_ref, x1_ref, h2_ref, lt_ref, *, alpha):
    o = of_ref[...] + ob_ref[...]
    r = r_ref[...]
    gate = r * jax.nn.sigmoid(r)
    acc = _dot(a_ref[...].astype(BF16), wo_ref[0:a_ref.shape[1], :])
    na = a_ref.shape[1]
    for h in range(GLA_HEADS):
        vs = slice(h * GLA_DV, (h + 1) * GLA_DV)
        oh = o[:, vs]
        nrm = oh * lax.rsqrt(jnp.mean(oh * oh, -1, keepdims=True) + NORM_EPS) * ng_ref[...]
        bh = (nrm * gate[:, vs]).astype(BF16)
        acc = acc + _dot(bh, wo_ref[na + h * GLA_DV:na + (h + 1) * GLA_DV, :])
    _post_mix(acc, x_ref, m2_ref, m3_ref, m4_ref, lg_ref, lb_ref, wr_ref, alpha, x1_ref, h2_ref, lt_ref)


def _post_specs(d, layer, seg, tm, n_exp):
    ins = [pl.BlockSpec((tm, d), lambda i: (i, 0)),
           _mod_spec(d, layer, 2, seg), _mod_spec(d, layer, 3, seg), _mod_spec(d, layer, 4, seg)]
    tail = [pl.BlockSpec((1, d), lambda i: (0, 0)), pl.BlockSpec((1, d), lambda i: (0, 0)),
            pl.BlockSpec((n_exp, d), lambda i: (0, 0))]
    outs = [pl.BlockSpec((tm, d), lambda i: (i, 0)), pl.BlockSpec((tm, d), lambda i: (i, 0)),
            pl.BlockSpec((n_exp, tm), lambda i: (0, i))]
    return ins, tail, outs


def _post_shapes(t, d, n_exp):
    return [jax.ShapeDtypeStruct((t, d), F32), jax.ShapeDtypeStruct((t, d), F32),
            jax.ShapeDtypeStruct((n_exp, t), F32)]


def even_out(a, o_f, o_b, proj, col_r, x, mods4, layer, norm_g, w_out_bf16, ln_g, ln_b, router_wt,
             alpha, seg_rows, n_batch, tm=256):
    t, d = x.shape
    na = a.shape[1]
    wv = o_f.shape[1]
    n_exp = router_wt.shape[0]
    seg = _seg_fn(tm, seg_rows, n_batch)
    ins, tail, outs = _post_specs(d, layer, seg, tm, n_exp)
    return pl.pallas_call(
        functools.partial(_even_out_kernel, alpha=alpha),
        grid=(t // tm,),
        in_specs=([pl.BlockSpec((tm, na), lambda i: (i, 0)),
                   pl.BlockSpec((tm, wv), lambda i: (i, 0)),
                   pl.BlockSpec((tm, wv), lambda i: (i, 0)),
                   pl.BlockSpec((tm, wv), lambda i: (i, col_r))] + ins
                  + [pl.BlockSpec((1, GLA_DV), lambda i: (0, 0)),
                     pl.BlockSpec((na + wv, d), lambda i: (0, 0))] + tail),
        out_specs=outs,
        out_shape=_post_shapes(t, d, n_exp),
        compiler_params=_cparams("arbitrary"),
        name="even_out",
    )(a, o_f, o_b, proj, x, mods4, mods4, mods4, norm_g.reshape(1, -1), w_out_bf16,
      ln_g.reshape(1, -1), ln_b.reshape(1, -1), router_wt)


def _odd_out_kernel(y_ref, x_ref, m2_ref, m3_ref, m4_ref, wg_ref, bg_ref, wo_ref,
                    lg_ref, lb_ref, wr_ref, x1_ref, h2_ref, lt_ref, *, alpha):
    g = jax.nn.gelu(y_ref[...], approximate=True)
    z = _dot(g.astype(BF16), wg_ref[...]) + bg_ref[...]
    v = g * jax.nn.sigmoid(z)
    out = _dot(v.astype(BF16), wo_ref[...])
    _post_mix(out, x_ref, m2_ref, m3_ref, m4_ref, lg_ref, lb_ref, wr_ref, alpha, x1_ref, h2_ref, lt_ref)


def odd_out(y, x, mods4, layer, w_glu_bf16, b_glu, w_out_bf16, ln_g, ln_b, router_wt,
            alpha, seg_rows, n_batch, tm=256):
    t, w5 = y.shape
    d = x.shape[1]
    n_exp = router_wt.shape[0]
    seg = _seg_fn(tm, seg_rows, n_batch)
    ins, tail, outs = _post_specs(d, layer, seg, tm, n_exp)
    return pl.pallas_call(
        functools.partial(_odd_out_kernel, alpha=alpha),
        grid=(t // tm,),
        in_specs=([pl.BlockSpec((tm, w5), lambda i: (i, 0))] + ins
                  + [pl.BlockSpec((w5, w5), lambda i: (0, 0)),
                     pl.BlockSpec((1, w5), lambda i: (0, 0)),
                     pl.BlockSpec((w5, d), lambda i: (0, 0))] + tail),
        out_specs=outs,
        out_shape=_post_shapes(t, d, n_exp),
        compiler_params=_cparams("arbitrary"),
        name="odd_out",
    )(y, x, mods4, mods4, mods4, w_glu_bf16, b_glu.reshape(1, -1), w_out_bf16,
      ln_g.reshape(1, -1), ln_b.reshape(1, -1), router_wt)


def _route_kernel(lt_ref, rb_ref, idx_ref, w_ref):
    eg = N_EXPERTS // N_GROUPS
    logits = lt_ref[...]
    aff = jax.nn.sigmoid(logits)
    sel = aff + rb_ref[...]
    s = [sel[e:e + 1, :] for e in range(N_EXPERTS)]
    a = [aff[e:e + 1, :] for e in range(N_EXPERTS)]

    def top2_sum(v):
        hi1, lo1 = jnp.maximum(v[0], v[1]), jnp.minimum(v[0], v[1])
        hi2, lo2 = jnp.maximum(v[2], v[3]), jnp.minimum(v[2], v[3])
        return jnp.maximum(hi1, hi2) + jnp.maximum(jnp.minimum(hi1, hi2), jnp.maximum(lo1, lo2))

    best = top2_sum(s[0:eg])
    grp = jnp.zeros_like(best, dtype=jnp.int32)
    for g in range(1, N_GROUPS):
        sc = top2_sum(s[g * eg:(g + 1) * eg])
        better = sc > best
        best = jnp.where(better, sc, best)
        grp = jnp.where(better, g, grp)

    def pick(vals, j):
        out = vals[j]
        for g in range(1, N_GROUPS):
            out = jnp.where(grp == g, vals[g * eg + j], out)
        return out

    sv = [pick(s, j) for j in range(eg)]
    av = [pick(a, j) for j in range(eg)]

    def argmax_first(vals, exclude):
        bi = jnp.zeros_like(grp)
        bv = jnp.where(exclude == 0, -jnp.inf, vals[0]) if exclude is not None else vals[0]
        for j in range(1, eg):
            vj = jnp.where(exclude == j, -jnp.inf, vals[j]) if exclude is not None else vals[j]
            better = vj > bv
            bv = jnp.where(better, vj, bv)
            bi = jnp.where(better, j, bi)
        return bi

    i1 = argmax_first(sv, None)
    i2 = argmax_first(sv, i1)

    def take(vals, i):
        out = vals[0]
        for j in range(1, eg):
            out = jnp.where(i == j, vals[j], out)
        return out

    w1 = take(av, i1)
    w2 = take(av, i2)
    tot = w1 + w2
    idx_ref[0:1, :] = grp * eg + i1
    idx_ref[1:2, :] = grp * eg + i2
    w_ref[0:1, :] = w1 / tot
    w_ref[1:2, :] = w2 / tot


def route(logits_t, router_b, tile=1024):
    n_exp, t = logits_t.shape
    tile = math.gcd(tile, t)
    return pl.pallas_call(
        _route_kernel,
        grid=(t // tile,),
        in_specs=[pl.BlockSpec((n_exp, tile), lambda i: (0, i)),
                  pl.BlockSpec((n_exp, 1), lambda i: (0, 0))],
        out_specs=[pl.BlockSpec((TOP_K, tile), lambda i: (0, i)),
                   pl.BlockSpec((TOP_K, tile), lambda i: (0, i))],
        out_shape=[jax.ShapeDtypeStruct((TOP_K, t), jnp.int32), jax.ShapeDtypeStruct((TOP_K, t), F32)],
        compiler_params=_cparams("arbitrary"),
        name="moe_route",
    )(logits_t, router_b.reshape(n_exp, 1).astype(F32))


def moe_plan(idx, tm):
    t = idx.shape[1]
    n_pair = TOP_K * t
    n_tiles = (n_pair + N_EXPERTS * (tm - 1)) // tm
    e_flat = idx.reshape(-1)
    onehot = (e_flat[:, None] == jnp.arange(N_EXPERTS)[None, :]).astype(jnp.int32)
    running = jnp.cumsum(onehot, axis=0)
    counts = running[-1]
    rank = jnp.sum(onehot * running, 1) - 1
    tiles_per = (counts + tm - 1) // tm
    tile_end = jnp.cumsum(tiles_per)
    n_used = tile_end[-1]
    pstart = (tile_end - tiles_per) * tm
    pos = jnp.sum(onehot * pstart[None, :], 1) + rank
    tile_expert = jnp.minimum(jnp.sum((tile_end[None, :] <= jnp.arange(n_tiles)[:, None]).astype(jnp.int32), 1),
                              N_EXPERTS - 1).astype(jnp.int32)
    j = jnp.arange(tm - 1)[None, :]
    n_fill = (tiles_per * tm - counts)[:, None]
    fill_dst = jnp.where(j < n_fill, (pstart + counts)[:, None] + j, 0).reshape(-1)
    fill_src = jnp.where(j < n_fill, 0, -1).reshape(-1)
    n_fill_pad = (-(n_pair + fill_dst.shape[0])) % ROW_COPY_BLOCK
    src_rows =jnp.concatenate([jnp.arange(n_pair, dtype=jnp.int32) % t, fill_src.astype(jnp.int32),
                                jnp.full((n_fill_pad,), -1, jnp.int32)])
    dst_rows = jnp.concatenate([pos.astype(jnp.int32), fill_dst.astype(jnp.int32),
                                jnp.zeros((n_fill_pad,), jnp.int32)])
    return (src_rows, dst_rows, tile_expert, n_used.reshape(1).astype(jnp.int32),
            pos.astype(jnp.int32))


ROW_COPY_BLOCK = 2048


def _row_copy_kernel(sidx_ref, didx_ref, src_ref, dst_ref, sem, *, n):
    def row_dma(s, d):
        return pltpu.make_async_copy(src_ref.at[pl.ds(s, 1)], dst_ref.at[pl.ds(d, 1)], sem)

    def issue(j, carry):
        s = sidx_ref[j]

        @pl.when(s >= 0)
        def _():
            row_dma(s, didx_ref[j]).start()
        return carry

    def drain(j, carry):
        @pl.when(sidx_ref[j] >= 0)
        def _():
            row_dma(0, 0).wait()
        return carry

    lax.fori_loop(0, n, issue, 0)
    lax.fori_loop(0, n, drain, 0)


def row_copy(src, sidx, didx, n_dst):
    total = sidx.shape[0]
    n = math.gcd(ROW_COPY_BLOCK, total)
    return pl.pallas_call(
        functools.partial(_row_copy_kernel, n=n),
        grid=(total // n,),
        in_specs=[pl.BlockSpec((n,), lambda i: (i,), memory_space=pltpu.SMEM),
                  pl.BlockSpec((n,), lambda i: (i,), memory_space=pltpu.SMEM),
                  pl.BlockSpec(memory_space=pl.ANY)],
        out_specs=pl.BlockSpec(memory_space=pl.ANY),
        out_shape=jax.ShapeDtypeStruct((n_dst, src.shape[1]), src.dtype),
        scratch_shapes=[pltpu.SemaphoreType.DMA(())],
        compiler_params=pltpu.CompilerParams(dimension_semantics=("arbitrary",)),
        name="row_copy",
    )(sidx, didx, src)


def _ffn_kernel(te_ref, nu_ref, xs_ref, wg_ref, wu_ref, wd_ref, o_ref):
    used = pl.program_id(0) < nu_ref[0]

    @pl.when(used)
    def _():
        xs = xs_ref[...].astype(BF16)
        g = _dot(xs, wg_ref[0])
        u = _dot(xs, wu_ref[0])
        hid = (g * jax.nn.sigmoid(g)) * u
        o_ref[...] = _dot(hid.astype(BF16), wd_ref[0])

    @pl.when(jnp.logical_not(used))
    def _():
        o_ref[...] = jnp.zeros_like(o_ref)


def grouped_ffn(xs, tile_expert, n_used, w_gate, w_up, w_down, layer, tm):
    p, d = xs.shape
    de = w_gate.shape[3]
    n_tiles = p // tm
    wmap = lambda i, te, nu: (layer, te[i], 0, 0)
    return pl.pallas_call(
        _ffn_kernel,
        grid_spec=pltpu.PrefetchScalarGridSpec(
            num_scalar_prefetch=2,
            grid=(n_tiles,),
            in_specs=[pl.BlockSpec((tm, d), lambda i, te, nu: (jnp.minimum(i, nu[0] - 1), 0)),
                      pl.BlockSpec((None, 1, d, de), wmap),
                      pl.BlockSpec((None, 1, d, de), wmap),
                      pl.BlockSpec((None, 1, de, d), wmap)],
            out_specs=pl.BlockSpec((tm, d), lambda i, te, nu: (i, 0))),
        out_shape=jax.ShapeDtypeStruct((p, d), F32),
        compiler_params=_cparams("arbitrary"),
        name="moe_ffn",
    )(tile_expert, n_used, xs, w_gate, w_up, w_down)


def _final_kernel(x_ref, y0_ref, y1_ref, w_ref, m5_ref, lg_ref, lb_ref, o_ref, *, alpha):
    w = w_ref[...]
    y = w[:, 0:1] * y0_ref[...] + w[:, 1:2] * y1_ref[...]
    z = alpha * x_ref[...] + m5_ref[...] * y
    mu = jnp.mean(z, -1, keepdims=True)
    zc = z - mu
    var = jnp.mean(zc * zc, -1, keepdims=True)
    o_ref[...] = zc * lax.rsqrt(var + LN_EPS) * lg_ref[...] + lb_ref[...]


def final_norm(x1, yg, wts, mods4, layer, ln_g, ln_b, alpha, seg_rows, n_batch, tm=256):
    t, d = x1.shape
    seg = _seg_fn(tm, seg_rows, n_batch)
    row = pl.BlockSpec((tm, d), lambda i: (i, 0))
    vec = pl.BlockSpec((1, d), lambda i: (0, 0))
    return pl.pallas_call(
        functools.partial(_final_kernel, alpha=alpha),
        grid=(t // tm,),
        in_specs=[row, row, pl.BlockSpec((tm, d), lambda i: (i + t // tm, 0)),
                  pl.BlockSpec((tm, TOP_K), lambda i: (i, 0)),
                  _mod_spec(d, layer, 5, seg), vec, vec],
        out_specs=row,
        out_shape=jax.ShapeDtypeStruct((t, d), F32),
        compiler_params=_cparams("arbitrary"),
        name="final_norm",
    )(x1, yg, yg, wts, mods4, ln_g.reshape(1, -1), ln_b.reshape(1, -1))


def moe_block(x1, h2, logits_t, router_b, w_gate, w_up, w_down, mods4, layer, ln_g, ln_b,
              alpha, seg_rows, n_batch, tm=256):
    t = h2.shape[0]
    n_pair = TOP_K * t
    idx, wts = route(logits_t, router_b)
    src_rows, dst_rows, tile_expert, n_used, pos = moe_plan(idx, tm)
    n_rows = ((n_pair + N_EXPERTS * (tm - 1)) // tm) * tm
    xs = row_copy(h2, src_rows, dst_rows, n_rows)
    ys = grouped_ffn(xs, tile_expert, n_used, w_gate, w_up, w_down, layer, tm)
    yg = row_copy(ys, pos, jnp.arange(n_pair, dtype=jnp.int32), n_pair)
    return final_norm(x1, yg, wts.T, mods4, layer, ln_g, ln_b, alpha, seg_rows, n_batch)


def s5_matrices(lam_re, lam_im, log_dt, b_re, b_im, c_re, c_im, d_skip):
    f32 = F32
    tc = S5_TC
    n_g, n_p = lam_re.shape[1], lam_re.shape[2]
    n_c = b_re.shape[-1]
    nb = 128 // n_c
    n_q = n_g // nb
    lr, li = lam_re.astype(f32), lam_im.astype(f32)
    dt = jnp.exp(log_dt.astype(f32))[..., None]

    def powers(jvals):
        j = jnp.asarray(np.asarray(jvals, np.float32))[:, None, None, None]
        mag = jnp.exp(lr * dt * j)
        return mag * jnp.cos(li * dt * j), mag * jnp.sin(li * dt * j)

    up = np.arange(tc)
    pw_re, pw_im = powers(np.arange(tc + 1))
    lb_re, lb_im = pw_re[1], pw_im[1]
    den = lr * lr + li * li
    fr = ((lb_re - 1.0) * lr + lb_im * li) / den
    fi = (lb_im * lr - (lb_re - 1.0) * li) / den
    br, bi = b_re.astype(f32), b_im.astype(f32)
    bb_re = fr[..., None] * br - fi[..., None] * bi
    bb_im = fr[..., None] * bi + fi[..., None] * br
    cr, ci = c_re.astype(f32), c_im.astype(f32)

    def times_b(p_re, p_im):
        return (p_re[..., None] * bb_re[None] - p_im[..., None] * bb_im[None],
                p_re[..., None] * bb_im[None] + p_im[..., None] * bb_re[None])

    e_re, e_im = times_b(pw_re, pw_im)
    kmat = (jnp.einsum("dgcp,jdgpe->jdgce", cr, e_re, precision=HIGHEST)
            - jnp.einsum("dgcp,jdgpe->jdgce", ci, e_im, precision=HIGHEST))
    s_idx = np.arange(tc)[:, None]
    t_idx = np.arange(tc)[None, :]
    lags = np.arange(tc + 1)[:, None, None]
    sel_f = jnp.asarray((lags == (t_idx - s_idx)[None]).astype(np.float32))
    sel_b = jnp.asarray((lags == (s_idx - t_idx)[None]).astype(np.float32))
    mt = (jnp.einsum("jst,jgce->gsetc", sel_f, kmat[:, 0], precision=HIGHEST)
          + jnp.einsum("jst,jgce->gsetc", sel_b, kmat[:, 1], precision=HIGHEST))
    eye_t = jnp.eye(tc, dtype=f32)
    eye_c = jnp.eye(n_c, dtype=f32)
    mt = mt + (eye_t[None, :, None, :, None] * eye_c[None, None, :, None, :]
               * d_skip.astype(f32)[:, None, None, None, :])
    eye_b = jnp.eye(nb, dtype=f32)

    a = mt.reshape(n_q, nb, tc, n_c, tc, n_c).transpose(0, 2, 1, 3, 4, 5)
    mt_q = (a[:, :, :, :, :, None, :] * eye_b[None, None, :, None, None, :, None]).astype(BF16)
    mt_q = mt_q.reshape(n_q, tc * nb * n_c, tc * nb * n_c)

    def w_of(e):
        m = e.transpose(1, 0, 3, 2).reshape(n_q, nb, tc, n_c, n_p).transpose(0, 2, 1, 3, 4)
        m = (m[:, :, :, :, None, :] * eye_b[None, None, :, None, :, None]).astype(BF16)
        return m.reshape(n_q, tc * nb * n_c, nb * n_p)

    ef_re, ef_im = times_b(*powers(tc - 1 - up))
    w_q = jnp.concatenate([w_of(ef_re[:, 0]), w_of(ef_im[:, 0]),
                           w_of(e_re[:tc, 1]), w_of(e_im[:tc, 1])], -1)

    def v_of(d, p_re, p_im):
        f_re = cr[d][None] * p_re[:, :, None, :] - ci[d][None] * p_im[:, :, None, :]
        f_im = cr[d][None] * p_im[:, :, None, :] + ci[d][None] * p_re[:, :, None, :]

        def shape(m):
            m = m.transpose(1, 3, 0, 2).reshape(n_q, nb, n_p, tc, n_c)
            m = (m[:, :, :, :, None, :] * eye_b[None, :, None, None, :, None]).astype(BF16)
            return m.reshape(n_q, nb * n_p, tc * nb * n_c)
        return shape(f_re), shape(-f_im)

    vf_re, vf_im = v_of(0, pw_re[1:, 0], pw_im[1:, 0])
    pb_re, pb_im = powers(tc - up)
    vb_re, vb_im = v_of(1, pb_re[:, 1], pb_im[:, 1])
    v_q = jnp.stack([vf_re, vf_im, vb_re, vb_im], 1)
    dec = lambda m: m.reshape(1, n_g * n_p // 128, 1, 128)
    decay = jnp.concatenate([dec(pw_re[tc, 0]), dec(pw_im[tc, 0]), dec(pw_re[tc, 1]), dec(pw_im[tc, 1])], 0)
    return mt_q, w_q, v_q, decay


def _s5_chunk_rows(ref, n):
    return jnp.concatenate([ref[pl.ds(s, n, stride=S5_TC), :] for s in range(S5_TC)], axis=1).astype(BF16)


def _s5_in_kernel(ul_ref, uc_ref, w_ref, fr_ref, fi_ref, br_ref, bi_ref, *, n_lat, n_ctx, n_batch):
    b = pl.program_id(1)
    w_lat = _dot(_s5_chunk_rows(ul_ref, n_lat), w_ref[...])
    w_ctx = _dot(_s5_chunk_rows(uc_ref, n_ctx), w_ref[...])
    nv = fr_ref.shape[0]
    for i, ref in enumerate((fr_ref, fi_ref, br_ref, bi_ref)):
        for c in range(nv):
            lanes = slice((i * nv + c) * 128, (i * nv + c + 1) * 128)
            ref[c, pl.ds(b, n_ctx, stride=n_batch), :] = w_ctx[:, lanes]
            ref[c, pl.ds(n_ctx * n_batch + b, n_lat, stride=n_batch), :] = w_lat[:, lanes]
            ref[c, pl.ds((n_ctx + n_lat) * n_batch + b, n_ctx, stride=n_batch), :] = w_ctx[:, lanes]


def _s5_scan_kernel(wfr, wfi, wbr, wbi, dec_ref, xfr, xfi, xbr, xbi, *, n_tiles):
    nv = wfr.shape[0]
    low = lax.broadcasted_iota(jnp.int32, (nv, 8, 128), 1) < 4
    a_fr, a_fi, a_br, a_bi = dec_ref[0], dec_ref[1], dec_ref[2], dec_ref[3]

    def half_step(s_re, s_im, a_re, a_im, w_re, w_im):
        return a_re * s_re - a_im * s_im + w_re, a_re * s_im + a_im * s_re + w_im

    def one_dir(w_re_ref, w_im_ref, x_re_ref, x_im_ref, row0, s_re, s_im, a_re, a_im, first_low):
        first = low if first_low else jnp.logical_not(low)
        wt_re, wt_im = w_re_ref[:, pl.ds(row0, 8), :], w_im_ref[:, pl.ds(row0, 8), :]
        wr_re, wr_im = pltpu.roll(wt_re, 4, 1), pltpu.roll(wt_im, 4, 1)
        mid_re, mid_im = half_step(s_re, s_im, a_re, a_im, wr_re, wr_im)
        x_re_ref[:, pl.ds(row0, 8), :] = jnp.where(first, s_re, mid_re)
        x_im_ref[:, pl.ds(row0, 8), :] = jnp.where(first, s_im, mid_im)
        m_re = jnp.where(first, pltpu.roll(mid_re, 4, 1), mid_re)
        m_im = jnp.where(first, pltpu.roll(mid_im, 4, 1), mid_im)
        w2_re = jnp.where(first, wr_re, wt_re)
        w2_im = jnp.where(first, wr_im, wt_im)
        return half_step(m_re, m_im, a_re, a_im, w2_re, w2_im)

    def body(i, carry):
        f_re, f_im, b_re, b_im = carry
        rf = pl.multiple_of(i * 8, 8)
        rb = pl.multiple_of((n_tiles - 1 - i) * 8, 8)
        f_re, f_im = one_dir(wfr, wfi, xfr, xfi, rf, f_re, f_im, a_fr, a_fi, True)
        b_re, b_im = one_dir(wbr, wbi, xbr, xbi, rb, b_re, b_im, a_br, a_bi, False)
        return f_re, f_im, b_re, b_im

    z = jnp.zeros((nv, 8, 128), F32)
    lax.fori_loop(0, n_tiles, body, (z, z, z, z))


def _s5_out_kernel(ul_ref, fr_ref, fi_ref, br_ref, bi_ref, mt_ref, v_ref, y_ref, *, n_lat, n_ctx, n_batch):
    b = pl.program_id(1)
    y = _dot(_s5_chunk_rows(ul_ref, n_lat), mt_ref[...])
    row0 = n_ctx * n_batch + b
    for i, ref in enumerate((fr_ref, fi_ref, br_ref, bi_ref)):
        xs = jnp.concatenate([ref[c, pl.ds(row0, n_lat, stride=n_batch), :] for c in range(ref.shape[0])], 1)
        y = y + _dot(xs.astype(BF16), v_ref[i])
    for s in range(S5_TC):
        y_ref[pl.ds(s, n_lat, stride=S5_TC), :] = y[:, s * 128:(s + 1) * 128]


def s5_bidir(u, mats, n_batch, l_lat, l_ctx):
    assert n_batch == 4, "the chunk scan packs two chunks of 4 batch rows per 8-sublane tile"
    mt_q, w_q, v_q, decay = mats
    tc = S5_TC
    wd = u.shape[1]
    n_q = wd // 128
    lane_q = tc * 128
    st_q = (128 // S5_CH) * S5_P
    n_lat, n_ctx = l_lat // tc, l_ctx // tc
    nk = n_lat + 2 * n_ctx
    assert nk % 2 == 0
    rows = nk * n_batch
    ctx0 = (n_batch * l_lat) // l_ctx
    dims = dict(n_lat=n_lat, n_ctx=n_ctx, n_batch=n_batch)

    nv = st_q // 128
    plane = jax.ShapeDtypeStruct((n_q * nv, rows, 128), F32)
    plane_spec = pl.BlockSpec((nv, rows, 128), lambda q, b: (q, 0, 0))
    ul_spec = pl.BlockSpec((l_lat, 128), lambda q, b: (b, q))
    uc_spec = pl.BlockSpec((l_ctx, 128), lambda q, b: (ctx0 + b, q))
    w_planes = pl.pallas_call(
        functools.partial(_s5_in_kernel, **dims),
        grid=(n_q, n_batch),
        in_specs=[ul_spec, uc_spec, pl.BlockSpec((None, lane_q, 4 * st_q), lambda q, b: (q, 0, 0))],
        out_specs=[plane_spec] * 4,
        out_shape=[plane] * 4,
        compiler_params=_cparams("arbitrary", "arbitrary"),
        name="s5_chunk_in",
    )(u, u, w_q)

    blk = pl.BlockSpec((nv, rows, 128), lambda j: (j, 0, 0))
    x_planes = pl.pallas_call(
        functools.partial(_s5_scan_kernel, n_tiles=rows // 8),
        grid=(n_q,),
        in_specs=[blk] * 4 + [pl.BlockSpec((4, nv, 1, 128), lambda j: (0, j, 0, 0))],
        out_specs=[blk] * 4,
        out_shape=[plane] * 4,
        compiler_params=_cparams("arbitrary"),
        name="s5_chunk_scan",
    )(*w_planes, decay)

    once = pl.Buffered(1)
    return pl.pallas_call(
        functools.partial(_s5_out_kernel, **dims),
        grid=(n_q, n_batch),
        in_specs=[ul_spec] + [plane_spec] * 4
                 + [pl.BlockSpec((None, lane_q, lane_q), lambda q, b: (q, 0, 0), pipeline_mode=once),
                    pl.BlockSpec((None, 4, st_q, lane_q), lambda q, b: (q, 0, 0, 0), pipeline_mode=once)],
        out_specs=pl.BlockSpec((l_lat, 128), lambda q, b: (b, q)),
        out_shape=jax.ShapeDtypeStruct((n_batch * l_lat, wd), F32),
        compiler_params=_cparams("arbitrary", "arbitrary"),
        name="s5_chunk_out",
    )(u, *x_planes, mt_q, v_q)


def kernel(x, c, ctx, c_ctx, ada_w, ada_b, ln_mix_g, ln_mix_b, ln_ffn_g, ln_ffn_b, ev_w_in, ev_gate_w2,
           ev_gate_b, ev_rpb, ev_norm_g, ev_w_out, od_w_in, od_lam_re, od_lam_im, od_log_dt, od_b_re,
           od_b_im, od_c_re, od_c_im, od_d, od_w_glu, od_b_glu, od_w_out, router_w, router_b,
           moe_w_gate, moe_w_up, moe_w_down):
    n_batch, l_lat, d = x.shape
    l_ctx = ctx.shape[1]
    depth = ada_w.shape[0]
    assert depth == 2, "one even (NA + GLA) layer followed by one odd (S5) layer"
    alpha = (2.0 * depth) ** 0.25
    n_lat = n_batch * l_lat

    cvec = jnp.concatenate([c, c_ctx[None], jnp.zeros((8 - n_batch - 1, d), F32)], 0)
    mods = compute_mods(cvec, ada_w, ada_b)
    mods4 = mods.reshape(depth, 8, 1, N_MOD * d)
    rows = jnp.concatenate([x.reshape(n_lat, d), ctx.reshape(n_batch * l_ctx, d)], 0)
    router_wt = router_w.T.astype(F32)

    na_w = NA_HEADS * NA_DH
    wk = GLA_HEADS * GLA_DK
    wv = GLA_HEADS * GLA_DV
    ev_in = ev_w_in.shape[2]
    pad = (-ev_in) % 256
    w_in = jnp.pad(ev_w_in[0], ((0, 0), (0, pad))).astype(BF16)
    proj = mod_matmul(rows, mods4, 0, w_in, l_lat, n_batch, tm=512, tn=(ev_in + pad) // 2)
    a_lat, a_ctx = na_attention(proj, na_bias_table(ev_rpb[0]), n_batch, l_lat, l_ctx)
    col_lr = (3 * na_w + 2 * wk + 2 * wv) // 128
    g2 = jnp.zeros((2, 128, wk), F32)
    g2 = g2.at[0, 0:GLA_RANK].set(ev_gate_w2[0, 0]).at[1, GLA_RANK:2 * GLA_RANK].set(ev_gate_w2[0, 1])
    o_f, o_b = gla_bidir(proj, g2.astype(BF16), ev_gate_b[0].reshape(2, 1, wk), rope_tables(l_lat, l_ctx),
                         n_batch, l_lat, l_ctx,
                         col_q=3 * na_w // wk, col_k=(3 * na_w + wk) // wk,
                         col_v=(3 * na_w + 2 * wk) // wv, col_lr=col_lr)
    a_all = jnp.concatenate([a_lat, a_ctx], 0)
    x1, h2, logits_t = even_out(a_all, o_f, o_b, proj, (3 * na_w + 2 * wk + wv) // wv, rows, mods4, 0,
                                ev_norm_g[0], ev_w_out[0].astype(BF16), ln_mix_g[0], ln_mix_b[0],
                                router_wt, alpha, l_lat, n_batch)
    w_gate, w_up, w_down = moe_w_gate.astype(BF16), moe_w_up.astype(BF16), moe_w_down.astype(BF16)
    rows = moe_block(x1, h2, logits_t, router_b, w_gate, w_up, w_down, mods4, 0,
                     ln_ffn_g[0], ln_ffn_b[0], alpha, l_lat, n_batch)

    u = mod_matmul(rows, mods4, 1, od_w_in[0].astype(BF16), l_lat, n_batch)
    mats = s5_matrices(od_lam_re[0], od_lam_im[0], od_log_dt[0], od_b_re[0], od_b_im[0],
                       od_c_re[0], od_c_im[0], od_d[0])
    y5 = s5_bidir(u, mats, n_batch, l_lat, l_ctx)
    x1, h2, logits_t = odd_out(y5, rows, mods4, 1, od_w_glu[0].astype(BF16), od_b_glu[0],
                               od_w_out[0].astype(BF16), ln_mix_g[1], ln_mix_b[1], router_wt,
                               alpha, l_lat, n_batch)
    out = moe_block(x1, h2, logits_t, router_b, w_gate, w_up, w_down, mods4, 1,
                    ln_ffn_g[1], ln_ffn_b[1], alpha, l_lat, n_batch)
    return out.reshape(n_batch, l_lat, d)
```

```python
import functools
import math

import numpy as np
import jax
import jax.numpy as jnp
from jax import lax
from jax.experimental import pallas as pl
from jax.experimental.pallas import tpu as pltpu

F32 = jnp.float32
BF16 = jnp.bfloat16
HIGHEST = lax.Precision.HIGHEST

N_MOD = 6
LN_EPS = 1e-5
NORM_EPS = 1e-6

GRID_W = 64
NA_HEADS = 8
NA_DH = 128
NA_KR = 8
NA_KC = 16

GLA_HEADS = 4
GLA_DK = 128
GLA_DV = 256
GLA_RANK = 16
GLA_TAU = 16.0
GLA_CHUNK = 64
ROPE_BASE = 10000.0

S5_CH = 16
S5_P = 64
S5_TC = 16

N_EXPERTS = 16
N_GROUPS = 4
TOP_K = 2

VMEM_LIMIT = 56 * 1024 * 1024
NEG_BIG = -1e30


def _cparams(*sem):
    return pltpu.CompilerParams(dimension_semantics=sem, vmem_limit_bytes=VMEM_LIMIT)


def _dot(a, b, precision=None):
    return jnp.dot(a, b, preferred_element_type=F32, precision=precision)


def _dot_nt(a, b, precision=None):
    return lax.dot_general(a, b, (((1,), (1,)), ((), ())), preferred_element_type=F32, precision=precision)


def _dot_tn(a, b):
    return lax.dot_general(a, b, (((0,), (0,)), ((), ())), preferred_element_type=F32)


def _mods_kernel(s_ref, w_ref, b_ref, o_ref):
    s = s_ref[...]
    s = s * jax.nn.sigmoid(s)
    o_ref[0] = _dot(s, w_ref[0], HIGHEST) + b_ref[0]


def compute_mods(cvec, ada_w, ada_b, tn=1024):
    n_layer, d, n = ada_w.shape
    tn = math.gcd(tn, n)
    return pl.pallas_call(
        _mods_kernel,
        grid=(n_layer, n // tn),
        in_specs=[pl.BlockSpec((8, d), lambda l, j: (0, 0)),
                  pl.BlockSpec((1, d, tn), lambda l, j: (l, 0, j)),
                  pl.BlockSpec((1, 1, tn), lambda l, j: (l, 0, j))],
        out_specs=pl.BlockSpec((1, 8, tn), lambda l, j: (l, 0, j)),
        out_shape=jax.ShapeDtypeStruct((n_layer, 8, n), F32),
        compiler_params=_cparams("arbitrary", "arbitrary"),
        name="ada_mods",
    )(cvec, ada_w, ada_b.reshape(n_layer, 1, n))


def _mod_spec(d, layer, which, seg_of_tile):
    return pl.BlockSpec((None, None, 1, d), lambda i, *_: (layer, seg_of_tile(i), 0, which))


def _seg_fn(tm, seg_rows, n_batch):
    return lambda i: jnp.minimum((i * tm) // seg_rows, n_batch)


def _modmm_kernel(x_ref, s1_ref, s0_ref, w_ref, o_ref):
    h = x_ref[...] * (1.0 + s1_ref[...]) + s0_ref[...]
    o_ref[...] = _dot(h.astype(BF16), w_ref[...])


def mod_matmul(x, mods4, layer, w_bf16, seg_rows, n_batch, tm=256, tn=None):
    t, d = x.shape
    n = w_bf16.shape[1]
    tn = n if tn is None else tn
    seg = _seg_fn(tm, seg_rows, n_batch)
    return pl.pallas_call(
        _modmm_kernel,
        grid=(n // tn, t // tm),
        in_specs=[pl.BlockSpec((tm, d), lambda j, i: (i, 0)),
                  pl.BlockSpec((None, None, 1, d), lambda j, i: (layer, seg(i), 0, 1)),
                  pl.BlockSpec((None, None, 1, d), lambda j, i: (layer, seg(i), 0, 0)),
                  pl.BlockSpec((d, tn), lambda j, i: (0, j))],
        out_specs=pl.BlockSpec((tm, tn), lambda j, i: (i, j)),
        out_shape=jax.ShapeDtypeStruct((t, n), F32),
        compiler_params=_cparams("arbitrary", "arbitrary"),
        name="mod_matmul",
    )(x, mods4, mods4, w_bf16)


NA_RB = 4
NA_BAND = NA_RB + NA_KR - 1


def _na_row_start(r, rows):
    return min(max(r - NA_KR // 2, 0), rows - NA_KR)


def na_bias_table(rpb, rows):
    w = GRID_W
    q = np.arange(w)
    kc = np.arange(w)
    win0 = np.clip(q - NA_KC // 2, 0, w - NA_KC)
    ok = (kc[None, :] >= win0[:, None]) & (kc[None, :] < win0[:, None] + NA_KC)
    dc = np.clip(kc[None, :] - q[:, None] + NA_KC - 1, 0, 2 * NA_KC - 2)
    pick = ((dc[None] == np.arange(2 * NA_KC - 1)[:, None, None]) & ok[None]).astype(np.float32)
    colb = jnp.einsum("hrd,dqk->hrqk", rpb.astype(F32), jnp.asarray(pick), precision=HIGHEST)
    colb = jnp.where(ok[None, None], colb, NEG_BIG)
    neg = jnp.full((rpb.shape[0], w, w), NEG_BIG, F32)

    def block(r0):
        band0 = min(max(r0 - NA_KR // 2, 0), rows - NA_BAND)
        out = []
        for r in range(r0, r0 + NA_RB):
            rs = _na_row_start(r, rows)
            first = rs - r + NA_KR - 1
            cols = [neg] * (rs - band0) + [colb[:, first + j] for j in range(NA_KR)]
            cols += [neg] * (NA_BAND - len(cols))
            out.append(jnp.concatenate(cols, -1))
        return jnp.concatenate(out, 1)

    return jnp.stack([block(0), block(NA_RB), block(rows - NA_RB)], 1)


def _na_kernel(q_ref, k_ref, v_ref, qc_ref, kc_ref, vc_ref, bias_ref, o_ref, oc_ref, kbf, vbf, *, rows):
    w = GRID_W
    n_blk = rows // NA_RB
    scale = NA_DH ** -0.5
    kbf[...] = k_ref[...].astype(BF16)
    vbf[...] = v_ref[...].astype(BF16)
    kc = kc_ref[...].astype(BF16)
    vc = vc_ref[...].astype(BF16)

    def body(i, carry):
        r0 = i * NA_RB
        band0 = jnp.clip(r0 - NA_KR // 2, 0, rows - NA_BAND)
        variant = jnp.where(i == 0, 0, jnp.where(i == n_blk - 1, 2, 1))
        q0 = pl.multiple_of(r0 * w, NA_RB * w)
        k0 = pl.multiple_of(band0 * w, w)
        q = (q_ref[pl.ds(q0, NA_RB * w), :] * scale).astype(BF16)
        kb = kbf[pl.ds(k0, NA_BAND * w), :]
        vb = vbf[pl.ds(k0, NA_BAND * w), :]
        s_loc = _dot_nt(q, kb) + bias_ref[variant]
        s_ctx = _dot_nt(q, kc)
        m = jnp.maximum(jnp.max(s_loc, -1, keepdims=True), jnp.max(s_ctx, -1, keepdims=True))
        p_loc = jnp.exp(s_loc - m)
        p_ctx = jnp.exp(s_ctx - m)
        den = jnp.sum(p_loc, -1, keepdims=True) + jnp.sum(p_ctx, -1, keepdims=True)
        o = _dot(p_loc.astype(BF16), vb) + _dot(p_ctx.astype(BF16), vc)
        o_ref[pl.ds(q0, NA_RB * w), :] = o / den
        return carry

    lax.fori_loop(0, n_blk, body, 0)

    qc = (qc_ref[...] * scale).astype(BF16)
    s = _dot_nt(qc, kc)
    p = jnp.exp(s - jnp.max(s, -1, keepdims=True))
    oc_ref[...] = _dot(p.astype(BF16), vc) / jnp.sum(p, -1, keepdims=True)


def na_attention(proj, bias_tab, n_batch, l_lat, l_ctx):
    h = NA_HEADS
    dh = NA_DH
    rows = l_lat // GRID_W
    ctx0 = (n_batch * l_lat) // l_ctx
    return pl.pallas_call(
        functools.partial(_na_kernel, rows=rows),
        grid=(n_batch, h),
        in_specs=[pl.BlockSpec((l_lat, dh), lambda b, hh: (b, hh)),
                  pl.BlockSpec((l_lat, dh), lambda b, hh: (b, h + hh)),
                  pl.BlockSpec((l_lat, dh), lambda b, hh: (b, 2 * h + hh)),
                  pl.BlockSpec((l_ctx, dh), lambda b, hh: (ctx0 + b, hh)),
                  pl.BlockSpec((l_ctx, dh), lambda b, hh: (ctx0 + b, h + hh)),
                  pl.BlockSpec((l_ctx, dh), lambda b, hh: (ctx0 + b, 2 * h + hh)),
                  pl.BlockSpec((None,) + bias_tab.shape[1:], lambda b, hh: (hh, 0, 0, 0))],
        out_specs=[pl.BlockSpec((l_lat, dh), lambda b, hh: (b, hh)),
                   pl.BlockSpec((l_ctx, dh), lambda b, hh: (b, hh))],
        out_shape=[jax.ShapeDtypeStruct((n_batch * l_lat, h * dh), F32),
                   jax.ShapeDtypeStruct((n_batch * l_ctx, h * dh), F32)],
        scratch_shapes=[pltpu.VMEM((l_lat, dh), BF16), pltpu.VMEM((l_lat, dh), BF16)],
        compiler_params=_cparams("arbitrary", "arbitrary"),
        name="na_attention",
    )(proj, proj, proj, proj, proj, proj, bias_tab)


def rope_tables(l_lat, l_ctx):
    half = GLA_DK // 2
    nf = half // 2
    inv = ROPE_BASE ** (-np.arange(nf, dtype=np.float64) / nf)
    t = np.arange(l_lat)
    lane = np.arange(GLA_DK)
    pos = np.where(lane[None, :] < half, (t // GRID_W)[:, None], (t % GRID_W)[:, None]).astype(np.float64)
    ang = pos * inv[lane % nf][None, :]
    first = (lane % half) < nf
    cos = np.cos(ang)
    sin_a = np.where(first[None, :], -np.sin(ang), 0.0)
    sin_b = np.where(first[None, :], 0.0, np.sin(ang))
    one = np.ones((l_ctx, GLA_DK))
    zero = np.zeros((l_ctx, GLA_DK))
    cat = lambda ident, a: jnp.asarray(np.concatenate([ident, a, ident], 0), F32)
    return cat(one, cos), cat(zero, sin_a), cat(zero, sin_b)


def _gla_direction(q_ref, k_ref, v_ref, lr_ref, cos_ref, sa_ref, sb_ref, g2_ref, gb_ref, st_ref, o_ref, reverse):
    c = GLA_CHUNK
    nf = GLA_DK // 4
    gscale = GLA_DK ** -0.5
    row = lax.broadcasted_iota(jnp.int32, (c, c), 0)
    col = lax.broadcasted_iota(jnp.int32, (c, c), 1)
    seen = (col >= row) if reverse else (col <= row)
    tri = seen.astype(F32)

    z = _dot(lr_ref[...].astype(BF16), g2_ref[...]) + gb_ref[...]
    g = (jnp.minimum(z, 0.0) - jnp.log1p(jnp.exp(-jnp.abs(z)))) * (1.0 / GLA_TAU)
    b = _dot(tri, g, HIGHEST)
    b_last = b[0:1, :] if reverse else b[c - 1:c, :]
    e_b = jnp.exp(b)
    e_nb = jnp.exp(-b)
    e_rem = jnp.exp(b_last - b)
    e_last = jnp.exp(b_last)
    cos = cos_ref[...]
    sa = sa_ref[...]
    sb = sb_ref[...]

    def rope(x):
        return x * cos + pltpu.roll(x, GLA_DK - nf, 1) * sa + pltpu.roll(x, nf, 1) * sb

    for h in range(GLA_HEADS):
        ks = slice(h * GLA_DK, (h + 1) * GLA_DK)
        vs = slice(h * GLA_DV, (h + 1) * GLA_DV)
        qh = rope(q_ref[:, ks]) * gscale
        kh = rope(k_ref[:, ks])
        q_dec = (qh * e_b[:, ks]).astype(BF16)
        k_dec = (kh * e_nb[:, ks]).astype(BF16)
        k_rem = (kh * e_rem[:, ks]).astype(BF16)
        vh = v_ref[:, vs].astype(BF16)
        att = jnp.where(seen, _dot_nt(q_dec, k_dec), 0.0)
        st = st_ref[h]
        o_ref[:, vs] = _dot(att.astype(BF16), vh) + _dot_nt(q_dec, st.astype(BF16))
        st_ref[h] = st * e_last[:, ks] + _dot_tn(vh, k_rem)


def _gla_kernel(qf, kf, vf, lrf, cf, saf, sbf, qb, kb, vb, lrb, cb, sab, sbb, g2_ref, gb_ref,
                of_ref, ob_ref, st_ref):
    @pl.when(pl.program_id(1) == 0)
    def _():
        st_ref[...] = jnp.zeros_like(st_ref)

    _gla_direction(qf, kf, vf, lrf, cf, saf, sbf, g2_ref.at[0], gb_ref.at[0], st_ref.at[0], of_ref, False)
    _gla_direction(qb, kb, vb, lrb, cb, sab, sbb, g2_ref.at[1], gb_ref.at[1], st_ref.at[1], ob_ref, True)


def gla_bidir(proj, g2, gb, tables, n_batch, l_lat, l_ctx, col_q, col_k, col_v, col_lr):
    c = GLA_CHUNK
    nc = l_ctx // c
    nl = l_lat // c
    nz = nl + 2 * nc
    steps = nl + nc
    wk = GLA_HEADS * GLA_DK
    wv = GLA_HEADS * GLA_DV
    t_rows = n_batch * (l_lat + l_ctx)

    def zblk(b, j):
        lat = b * nl + (j - nc)
        ctx = n_batch * nl + b * nc + jnp.where(j < nc, j, j - nc - nl)
        return jnp.where((j >= nc) & (j < nc + nl), lat, ctx)

    fwd = lambda b, i: zblk(b, i)
    bwd = lambda b, i: zblk(b, nz - 1 - i)

    def dir_specs(blk, tab):
        return [pl.BlockSpec((c, wk), lambda b, i: (blk(b, i), col_q)),
                pl.BlockSpec((c, wk), lambda b, i: (blk(b, i), col_k)),
                pl.BlockSpec((c, wv), lambda b, i: (blk(b, i), col_v)),
                pl.BlockSpec((c, 128), lambda b, i: (blk(b, i), col_lr)),
                pl.BlockSpec((c, GLA_DK), lambda b, i: (tab(i), 0)),
                pl.BlockSpec((c, GLA_DK), lambda b, i: (tab(i), 0)),
                pl.BlockSpec((c, GLA_DK), lambda b, i: (tab(i), 0))]

    cos, sa, sb = tables
    return pl.pallas_call(
        _gla_kernel,
        grid=(n_batch, steps),
        in_specs=(dir_specs(fwd, lambda i: i) + dir_specs(bwd, lambda i: nz - 1 - i)
                  + [pl.BlockSpec((2, 128, wk), lambda b, i: (0, 0, 0)),
                     pl.BlockSpec((2, 1, wk), lambda b, i: (0, 0, 0))]),
        out_specs=[pl.BlockSpec((c, wv), lambda b, i: (fwd(b, i), 0)),
                   pl.BlockSpec((c, wv), lambda b, i: (bwd(b, i), 0))],
        out_shape=[jax.ShapeDtypeStruct((t_rows, wv), F32), jax.ShapeDtypeStruct((t_rows, wv), F32)],
        scratch_shapes=[pltpu.VMEM((2, GLA_HEADS, GLA_DV, GLA_DK), F32)],
        compiler_params=_cparams("arbitrary", "arbitrary"),
        name="gla_bidir",
    )(proj, proj, proj, proj, cos, sa, sb, proj, proj, proj, proj, cos, sa, sb, g2, gb)


def _post_mix(out, x_ref, m2_ref, m3_ref, m4_ref, lg_ref, lb_ref, wr_ref, alpha, x1_ref, h2_ref, lt_ref):
    y = alpha * x_ref[...] + m2_ref[...] * out
    mu = jnp.mean(y, -1, keepdims=True)
    yc = y - mu
    var = jnp.mean(yc * yc, -1, keepdims=True)
    x1 = yc * lax.rsqrt(var + LN_EPS) * lg_ref[...] + lb_ref[...]
    h2 = x1 * (1.0 + m4_ref[...]) + m3_ref[...]
    x1_ref[...] = x1
    h2_ref[...] = h2.astype(BF16)
    lt_ref[...] = _dot_nt(wr_ref[...], h2, HIGHEST)


def _even_out_kernel(a_ref, of_ref, ob_ref, r_ref, x_ref, m2_ref, m3_ref, m4_ref, ng_ref, wo_ref,
                     lg_ref, lb_ref, wr_ref, x1_ref, h2_ref, lt_ref, *, alpha):
    o = of_ref[...] + ob_ref[...]
    r = r_ref[...]
    gate = r * jax.nn.sigmoid(r)
    acc = _dot(a_ref[...].astype(BF16), wo_ref[0:a_ref.shape[1], :])
    na = a_ref.shape[1]
    for h in range(GLA_HEADS):
        vs = slice(h * GLA_DV, (h + 1) * GLA_DV)
        oh = o[:, vs]
        nrm = oh * lax.rsqrt(jnp.mean(oh * oh, -1, keepdims=True) + NORM_EPS) * ng_ref[...]
        bh = (nrm * gate[:, vs]).astype(BF16)
        acc = acc + _dot(bh, wo_ref[na + h * GLA_DV:na + (h + 1) * GLA_DV, :])
    _post_mix(acc, x_ref, m2_ref, m3_ref, m4_ref, lg_ref, lb_ref, wr_ref, alpha, x1_ref, h2_ref, lt_ref)


def _post_specs(d, layer, seg, tm, n_exp):
    ins = [pl.BlockSpec((tm, d), lambda i: (i, 0)),
           _mod_spec(d, layer, 2, seg), _mod_spec(d, layer, 3, seg), _mod_spec(d, layer, 4, seg)]
    tail = [pl.BlockSpec((1, d), lambda i: (0, 0)), pl.BlockSpec((1, d), lambda i: (0, 0)),
            pl.BlockSpec((n_exp, d), lambda i: (0, 0))]
    outs = [pl.BlockSpec((tm, d), lambda i: (i, 0)), pl.BlockSpec((tm, d), lambda i: (i, 0)),
            pl.BlockSpec((n_exp, tm), lambda i: (0, i))]
    return ins, tail, outs


def _post_shapes(t, d, n_exp):
    return [jax.ShapeDtypeStruct((t, d), F32), jax.ShapeDtypeStruct((t, d), BF16),
            jax.ShapeDtypeStruct((n_exp, t), F32)]


def even_out(a, o_f, o_b, proj, col_r, x, mods4, layer, norm_g, w_out_bf16, ln_g, ln_b, router_wt,
             alpha, seg_rows, n_batch, tm=256):
    t, d = x.shape
    na = a.shape[1]
    wv = o_f.shape[1]
    n_exp = router_wt.shape[0]
    seg = _seg_fn(tm, seg_rows, n_batch)
    ins, tail, outs = _post_specs(d, layer, seg, tm, n_exp)
    return pl.pallas_call(
        functools.partial(_even_out_kernel, alpha=alpha),
        grid=(t // tm,),
        in_specs=([pl.BlockSpec((tm, na), lambda i: (i, 0)),
                   pl.BlockSpec((tm, wv), lambda i: (i, 0)),
                   pl.BlockSpec((tm, wv), lambda i: (i, 0)),
                   pl.BlockSpec((tm, wv), lambda i: (i, col_r))] + ins
                  + [pl.BlockSpec((1, GLA_DV), lambda i: (0, 0)),
                     pl.BlockSpec((na + wv, d), lambda i: (0, 0))] + tail),
        out_specs=outs,
        out_shape=_post_shapes(t, d, n_exp),
        compiler_params=_cparams("arbitrary"),
        name="even_out",
    )(a, o_f, o_b, proj, x, mods4, mods4, mods4, norm_g.reshape(1, -1), w_out_bf16,
      ln_g.reshape(1, -1), ln_b.reshape(1, -1), router_wt)


def _odd_out_kernel(y_ref, x_ref, m2_ref, m3_ref, m4_ref, wg_ref, bg_ref, wo_ref,
                    lg_ref, lb_ref, wr_ref, x1_ref, h2_ref, lt_ref, *, alpha):
    g = jax.nn.gelu(y_ref[...], approximate=True)
    z = _dot(g.astype(BF16), wg_ref[...]) + bg_ref[...]
    v = g * jax.nn.sigmoid(z)
    out = _dot(v.astype(BF16), wo_ref[...])
    _post_mix(out, x_ref, m2_ref, m3_ref, m4_ref, lg_ref, lb_ref, wr_ref, alpha, x1_ref, h2_ref, lt_ref)


def odd_out(y, x, mods4, layer, w_glu_bf16, b_glu, w_out_bf16, ln_g, ln_b, router_wt,
            alpha, seg_rows, n_batch, tm=256):
    t, w5 = y.shape
    d = x.shape[1]
    n_exp = router_wt.shape[0]
    seg = _seg_fn(tm, seg_rows, n_batch)
    ins, tail, outs = _post_specs(d, layer, seg, tm, n_exp)
    return pl.pallas_call(
        functools.partial(_odd_out_kernel, alpha=alpha),
        grid=(t // tm,),
        in_specs=([pl.BlockSpec((tm, w5), lambda i: (i, 0))] + ins
                  + [pl.BlockSpec((w5, w5), lambda i: (0, 0)),
                     pl.BlockSpec((1, w5), lambda i: (0, 0)),
                     pl.BlockSpec((w5, d), lambda i: (0, 0))] + tail),
        out_specs=outs,
        out_shape=_post_shapes(t, d, n_exp),
        compiler_params=_cparams("arbitrary"),
        name="odd_out",
    )(y, x, mods4, mods4, mods4, w_glu_bf16, b_glu.reshape(1, -1), w_out_bf16,
      ln_g.reshape(1, -1), ln_b.reshape(1, -1), router_wt)


def _route_kernel(lt_ref, rb_ref, idx_ref, w_ref):
    eg = N_EXPERTS // N_GROUPS
    logits = lt_ref[...]
    aff = jax.nn.sigmoid(logits)
    sel = aff + rb_ref[...]
    s = [sel[e:e + 1, :] for e in range(N_EXPERTS)]
    a = [aff[e:e + 1, :] for e in range(N_EXPERTS)]

    def top2_sum(v):
        hi1, lo1 = jnp.maximum(v[0], v[1]), jnp.minimum(v[0], v[1])
        hi2, lo2 = jnp.maximum(v[2], v[3]), jnp.minimum(v[2], v[3])
        return jnp.maximum(hi1, hi2) + jnp.maximum(jnp.minimum(hi1, hi2), jnp.maximum(lo1, lo2))

    best = top2_sum(s[0:eg])
    grp = jnp.zeros_like(best, dtype=jnp.int32)
    for g in range(1, N_GROUPS):
        sc = top2_sum(s[g * eg:(g + 1) * eg])
        better = sc > best
        best = jnp.where(better, sc, best)
        grp = jnp.where(better, g, grp)

    def pick(vals, j):
        out = vals[j]
        for g in range(1, N_GROUPS):
            out = jnp.where(grp == g, vals[g * eg + j], out)
        return out

    sv = [pick(s, j) for j in range(eg)]
    av = [pick(a, j) for j in range(eg)]

    def argmax_first(vals, exclude):
        bi = jnp.zeros_like(grp)
        bv = jnp.where(exclude == 0, -jnp.inf, vals[0]) if exclude is not None else vals[0]
        for j in range(1, eg):
            vj = jnp.where(exclude == j, -jnp.inf, vals[j]) if exclude is not None else vals[j]
            better = vj > bv
            bv = jnp.where(better, vj, bv)
            bi = jnp.where(better, j, bi)
        return bi

    i1 = argmax_first(sv, None)
    i2 = argmax_first(sv, i1)

    def take(vals, i):
        out = vals[0]
        for j in range(1, eg):
            out = jnp.where(i == j, vals[j], out)
        return out

    w1 = take(av, i1)
    w2 = take(av, i2)
    tot = w1 + w2
    idx_ref[0:1, :] = grp * eg + i1
    idx_ref[1:2, :] = grp * eg + i2
    w_ref[0:1, :] = w1 / tot
    w_ref[1:2, :] = w2 / tot


def route(logits_t, router_b, tile=1024):
    n_exp, t = logits_t.shape
    tile = math.gcd(tile, t)
    return pl.pallas_call(
        _route_kernel,
        grid=(t // tile,),
        in_specs=[pl.BlockSpec((n_exp, tile), lambda i: (0, i)),
                  pl.BlockSpec((n_exp, 1), lambda i: (0, 0))],
        out_specs=[pl.BlockSpec((TOP_K, tile), lambda i: (0, i)),
                   pl.BlockSpec((TOP_K, tile), lambda i: (0, i))],
        out_shape=[jax.ShapeDtypeStruct((TOP_K, t), jnp.int32), jax.ShapeDtypeStruct((TOP_K, t), F32)],
        compiler_params=_cparams("arbitrary"),
        name="moe_route",
    )(logits_t, router_b.reshape(n_exp, 1).astype(F32))


def moe_plan(idx, tm):
    t = idx.shape[1]
    n_pair = TOP_K * t
    n_tiles = (n_pair + N_EXPERTS * (tm - 1)) // tm
    e_flat = idx.reshape(-1)
    onehot = (e_flat[:, None] == jnp.arange(N_EXPERTS)[None, :]).astype(jnp.int32)
    running = jnp.cumsum(onehot, axis=0)
    counts = running[-1]
    rank = jnp.sum(onehot * running, 1) - 1
    tiles_per = (counts + tm - 1) // tm
    tile_end = jnp.cumsum(tiles_per)
    n_used = tile_end[-1]
    pstart = (tile_end - tiles_per) * tm
    pos = jnp.sum(onehot * pstart[None, :], 1) + rank
    tile_expert = jnp.minimum(jnp.sum((tile_end[None, :] <= jnp.arange(n_tiles)[:, None]).astype(jnp.int32), 1),
                              N_EXPERTS - 1).astype(jnp.int32)
    gidx = jnp.zeros((n_tiles * tm,), jnp.int32).at[pos].set(
        jnp.arange(n_pair, dtype=jnp.int32) % t, mode="promise_in_bounds", unique_indices=True)
    return gidx, tile_expert, n_used.reshape(1).astype(jnp.int32), pos.astype(jnp.int32)


def _ffn_kernel(te_ref, nu_ref, xs_ref, wg_ref, wu_ref, wd_ref, o_ref):
    used = pl.program_id(0) < nu_ref[0]

    @pl.when(used)
    def _():
        xs = xs_ref[...]
        g = _dot(xs, wg_ref[0])
        u = _dot(xs, wu_ref[0])
        hid = (g * jax.nn.sigmoid(g)) * u
        o_ref[...] = _dot(hid.astype(BF16), wd_ref[0]).astype(o_ref.dtype)

    @pl.when(jnp.logical_not(used))
    def _():
        o_ref[...] = jnp.zeros_like(o_ref)


def grouped_ffn(xs, tile_expert, n_used, w_gate, w_up, w_down, layer, tm):
    p, d = xs.shape
    de = w_gate.shape[3]
    n_tiles = p // tm
    wmap = lambda i, te, nu: (layer, te[i], 0, 0)
    return pl.pallas_call(
        _ffn_kernel,
        grid_spec=pltpu.PrefetchScalarGridSpec(
            num_scalar_prefetch=2,
            grid=(n_tiles,),
            in_specs=[pl.BlockSpec((tm, d), lambda i, te, nu: (jnp.minimum(i, nu[0] - 1), 0)),
                      pl.BlockSpec((None, 1, d, de), wmap),
                      pl.BlockSpec((None, 1, d, de), wmap),
                      pl.BlockSpec((None, 1, de, d), wmap)],
            out_specs=pl.BlockSpec((tm, d), lambda i, te, nu: (i, 0))),
        out_shape=jax.ShapeDtypeStruct((p, d), BF16),
        compiler_params=_cparams("arbitrary"),
        name="moe_ffn",
    )(tile_expert, n_used, xs, w_gate, w_up, w_down)


def _final_kernel(x_ref, y0_ref, y1_ref, w_ref, m5_ref, lg_ref, lb_ref, o_ref, *, alpha):
    w = w_ref[...]
    y = w[:, 0:1] * y0_ref[...].astype(F32) + w[:, 1:2] * y1_ref[...].astype(F32)
    z = alpha * x_ref[...] + m5_ref[...] * y
    mu = jnp.mean(z, -1, keepdims=True)
    zc = z - mu
    var = jnp.mean(zc * zc, -1, keepdims=True)
    o_ref[...] = zc * lax.rsqrt(var + LN_EPS) * lg_ref[...] + lb_ref[...]


def final_norm(x1, yg, wts, mods4, layer, ln_g, ln_b, alpha, seg_rows, n_batch, tm=256):
    t, d = x1.shape
    seg = _seg_fn(tm, seg_rows, n_batch)
    row = pl.BlockSpec((tm, d), lambda i: (i, 0))
    vec = pl.BlockSpec((1, d), lambda i: (0, 0))
    return pl.pallas_call(
        functools.partial(_final_kernel, alpha=alpha),
        grid=(t // tm,),
        in_specs=[row, row, pl.BlockSpec((tm, d), lambda i: (i + t // tm, 0)),
                  pl.BlockSpec((tm, TOP_K), lambda i: (i, 0)),
                  _mod_spec(d, layer, 5, seg), vec, vec],
        out_specs=row,
        out_shape=jax.ShapeDtypeStruct((t, d), F32),
        compiler_params=_cparams("arbitrary"),
        name="final_norm",
    )(x1, yg, yg, wts, mods4, ln_g.reshape(1, -1), ln_b.reshape(1, -1))


def moe_block(x1, h2, logits_t, router_b, w_gate, w_up, w_down, mods4, layer, ln_g, ln_b,
              alpha, seg_rows, n_batch, tm=256):
    idx, wts = route(logits_t, router_b)
    gidx, tile_expert, n_used, pos = moe_plan(idx, tm)
    xs = h2.at[gidx].get(mode="promise_in_bounds")
    ys = grouped_ffn(xs, tile_expert, n_used, w_gate, w_up, w_down, layer, tm)
    yg = ys.at[pos].get(mode="promise_in_bounds")
    return final_norm(x1, yg, wts.T, mods4, layer, ln_g, ln_b, alpha, seg_rows, n_batch)


def s5_matrices(lam_re, lam_im, log_dt, b_re, b_im, c_re, c_im, d_skip):
    f32 = F32
    tc = S5_TC
    n_g, n_p = lam_re.shape[1], lam_re.shape[2]
    n_c = b_re.shape[-1]
    nb = 128 // n_c
    n_q = n_g // nb
    lr, li = lam_re.astype(f32), lam_im.astype(f32)
    dt = jnp.exp(log_dt.astype(f32))[..., None]

    def powers(jvals):
        j = jnp.asarray(np.asarray(jvals, np.float32))[:, None, None, None]
        mag = jnp.exp(lr * dt * j)
        return mag * jnp.cos(li * dt * j), mag * jnp.sin(li * dt * j)

    up = np.arange(tc)
    pw_re, pw_im = powers(np.arange(tc + 1))
    lb_re, lb_im = pw_re[1], pw_im[1]
    den = lr * lr + li * li
    fr = ((lb_re - 1.0) * lr + lb_im * li) / den
    fi = (lb_im * lr - (lb_re - 1.0) * li) / den
    br, bi = b_re.astype(f32), b_im.astype(f32)
    bb_re = fr[..., None] * br - fi[..., None] * bi
    bb_im = fr[..., None] * bi + fi[..., None] * br
    cr, ci = c_re.astype(f32), c_im.astype(f32)

    def times_b(p_re, p_im):
        return (p_re[..., None] * bb_re[None] - p_im[..., None] * bb_im[None],
                p_re[..., None] * bb_im[None] + p_im[..., None] * bb_re[None])

    e_re, e_im = times_b(pw_re, pw_im)
    kmat = (jnp.einsum("dgcp,jdgpe->jdgce", cr, e_re, precision=HIGHEST)
            - jnp.einsum("dgcp,jdgpe->jdgce", ci, e_im, precision=HIGHEST))
    def lag_slab(k_dir):
        return k_dir.reshape(tc, n_q, nb, n_c, n_c).transpose(1, 0, 4, 2, 3).reshape(n_q, tc, n_c, nb * n_c)
    skip = (jnp.eye(n_c, dtype=f32)[None, None, :, None, :]
            * d_skip.astype(f32).reshape(n_q, nb, n_c)[:, None, None, :, :]).reshape(n_q, 1, n_c, nb * n_c)
    k_c = jnp.concatenate([lag_slab(kmat[:tc, 0]), lag_slab(kmat[:tc, 1]), skip], 1)

    def w_slab(e):
        return e.reshape(tc, n_q, nb, n_p, n_c).transpose(1, 0, 4, 2, 3).reshape(n_q, tc, n_c, nb * n_p)
    ef_re, ef_im = times_b(*powers(tc - 1 - up))
    w_c = jnp.stack([w_slab(ef_re[:, 0]), w_slab(ef_im[:, 0]),
                     w_slab(e_re[:tc, 1]), w_slab(e_im[:tc, 1])], 2)

    def v_slabs(d, p_re, p_im):
        f_re = cr[d][None] * p_re[:, :, None, :] - ci[d][None] * p_im[:, :, None, :]
        f_im = cr[d][None] * p_im[:, :, None, :] + ci[d][None] * p_re[:, :, None, :]
        slab = lambda m: m.reshape(tc, n_q, nb, n_c, n_p).transpose(1, 0, 3, 2, 4).reshape(n_q, tc, n_c, nb * n_p)
        return slab(f_re), slab(-f_im)
    pb_re, pb_im = powers(tc - up)
    vt_c = jnp.stack(v_slabs(0, pw_re[1:, 0], pw_im[1:, 0]) + v_slabs(1, pb_re[:, 1], pb_im[:, 1]), 2)
    dec = lambda m: m.reshape(1, n_g * n_p // 128, 1, 128)
    decay = jnp.concatenate([dec(pw_re[tc, 0]), dec(pw_im[tc, 0]), dec(pw_re[tc, 1]), dec(pw_im[tc, 1])], 0)
    return k_c, w_c, vt_c, decay


def _s5_chunk_rows(ref, n):
    return jnp.concatenate([ref[pl.ds(s, n, stride=S5_TC), :] for s in range(S5_TC)], axis=1).astype(BF16)


def _s5_expand(slab, group_lanes):
    rows = 128
    tiled = jnp.concatenate([slab] * (rows // slab.shape[0]), axis=0)
    r = lax.broadcasted_iota(jnp.int32, tiled.shape, 0) // S5_CH
    l = lax.broadcasted_iota(jnp.int32, tiled.shape, 1) // group_lanes
    return jnp.where(r == l, tiled, 0.0).astype(BF16)


def _s5_in_kernel(ul_ref, uc_ref, wc_ref, fr_ref, fi_ref, br_ref, bi_ref, w_ref, *, n_lat, n_ctx, n_batch):
    b = pl.program_id(1)
    n_plane = wc_ref.shape[1]
    st = wc_ref.shape[3]

    @pl.when(b == 0)
    def _():
        for s in range(S5_TC):
            for i in range(n_plane):
                w_ref[s * 128:(s + 1) * 128, i * st:(i + 1) * st] = _s5_expand(wc_ref[s, i], S5_P)

    w_lat = _dot(_s5_chunk_rows(ul_ref, n_lat), w_ref[...])
    w_ctx = _dot(_s5_chunk_rows(uc_ref, n_ctx), w_ref[...])
    nv = fr_ref.shape[0]
    for i, ref in enumerate((fr_ref, fi_ref, br_ref, bi_ref)):
        for c in range(nv):
            lanes = slice((i * nv + c) * 128, (i * nv + c + 1) * 128)
            ref[c, pl.ds(b, n_ctx, stride=n_batch), :] = w_ctx[:, lanes]
            ref[c, pl.ds(n_ctx * n_batch + b, n_lat, stride=n_batch), :] = w_lat[:, lanes]
            ref[c, pl.ds((n_ctx + n_lat) * n_batch + b, n_ctx, stride=n_batch), :] = w_ctx[:, lanes]


def _s5_scan_kernel(wfr, wfi, wbr, wbi, dec_ref, xfr, xfi, xbr, xbi, *, n_tiles):
    nv = wfr.shape[0]
    low = lax.broadcasted_iota(jnp.int32, (nv, 8, 128), 1) < 4
    a_fr, a_fi, a_br, a_bi = dec_ref[0], dec_ref[1], dec_ref[2], dec_ref[3]

    def half_step(s_re, s_im, a_re, a_im, w_re, w_im):
        return a_re * s_re - a_im * s_im + w_re, a_re * s_im + a_im * s_re + w_im

    def one_dir(w_re_ref, w_im_ref, x_re_ref, x_im_ref, row0, s_re, s_im, a_re, a_im, first_low):
        first = low if first_low else jnp.logical_not(low)
        wt_re, wt_im = w_re_ref[:, pl.ds(row0, 8), :], w_im_ref[:, pl.ds(row0, 8), :]
        wr_re, wr_im = pltpu.roll(wt_re, 4, 1), pltpu.roll(wt_im, 4, 1)
        mid_re, mid_im = half_step(s_re, s_im, a_re, a_im, wr_re, wr_im)
        x_re_ref[:, pl.ds(row0, 8), :] = jnp.where(first, s_re, mid_re)
        x_im_ref[:, pl.ds(row0, 8), :] = jnp.where(first, s_im, mid_im)
        m_re = jnp.where(first, pltpu.roll(mid_re, 4, 1), mid_re)
        m_im = jnp.where(first, pltpu.roll(mid_im, 4, 1), mid_im)
        w2_re = jnp.where(first, wr_re, wt_re)
        w2_im = jnp.where(first, wr_im, wt_im)
        return half_step(m_re, m_im, a_re, a_im, w2_re, w2_im)

    def body(i, carry):
        f_re, f_im, b_re, b_im = carry
        rf = pl.multiple_of(i * 8, 8)
        rb = pl.multiple_of((n_tiles - 1 - i) * 8, 8)
        f_re, f_im = one_dir(wfr, wfi, xfr, xfi, rf, f_re, f_im, a_fr, a_fi, True)
        b_re, b_im = one_dir(wbr, wbi, xbr, xbi, rb, b_re, b_im, a_br, a_bi, False)
        return f_re, f_im, b_re, b_im

    z = jnp.zeros((nv, 8, 128), F32)
    lax.fori_loop(0, n_tiles, body, (z, z, z, z))


def _s5_out_kernel(ul_ref, fr_ref, fi_ref, br_ref, bi_ref, kc_ref, vc_ref, y_ref, mt_ref, vt_ref,
                   *, n_lat, n_ctx, n_batch):
    b = pl.program_id(1)
    tc = S5_TC

    @pl.when(b == 0)
    def _():
        lag = [_s5_expand(kc_ref[j], S5_CH) for j in range(2 * tc)]
        diag = _s5_expand(kc_ref[0] + kc_ref[tc] + kc_ref[2 * tc], S5_CH)
        for s in range(tc):
            for t in range(tc):
                blk = diag if s == t else (lag[t - s] if t > s else lag[tc + s - t])
                mt_ref[s * 128:(s + 1) * 128, t * 128:(t + 1) * 128] = blk
        for t in range(tc):
            for i in range(vt_ref.shape[0]):
                vt_ref[i, t * 128:(t + 1) * 128, :] = _s5_expand(vc_ref[t, i], S5_P)

    y = _dot(_s5_chunk_rows(ul_ref, n_lat), mt_ref[...])
    row0 = n_ctx * n_batch + b
    for i, ref in enumerate((fr_ref, fi_ref, br_ref, bi_ref)):
        xs = jnp.concatenate([ref[c, pl.ds(row0, n_lat, stride=n_batch), :] for c in range(ref.shape[0])], 1)
        y = y + _dot_nt(xs.astype(BF16), vt_ref[i])
    for s in range(S5_TC):
        y_ref[pl.ds(s, n_lat, stride=S5_TC), :] = y[:, s * 128:(s + 1) * 128]


def s5_bidir(u, mats, n_batch, l_lat, l_ctx):
    assert n_batch == 4, "the chunk scan packs two chunks of 4 batch rows per 8-sublane tile"
    k_c, w_c, vt_c, decay = mats
    tc = S5_TC
    wd = u.shape[1]
    n_q = wd // 128
    lane_q = tc * 128
    st_q = (128 // S5_CH) * S5_P
    n_lat, n_ctx = l_lat // tc, l_ctx // tc
    nk = n_lat + 2 * n_ctx
    assert nk % 2 == 0
    rows = nk * n_batch
    ctx0 = (n_batch * l_lat) // l_ctx
    dims = dict(n_lat=n_lat, n_ctx=n_ctx, n_batch=n_batch)

    nv = st_q // 128
    plane = jax.ShapeDtypeStruct((n_q * nv, rows, 128), F32)
    plane_spec = pl.BlockSpec((nv, rows, 128), lambda q, b: (q, 0, 0))
    ul_spec = pl.BlockSpec((l_lat, 128), lambda q, b: (b, q))
    uc_spec = pl.BlockSpec((l_ctx, 128), lambda q, b: (ctx0 + b, q))
    w_planes = pl.pallas_call(
        functools.partial(_s5_in_kernel, **dims),
        grid=(n_q, n_batch),
        in_specs=[ul_spec, uc_spec, pl.BlockSpec((None,) + w_c.shape[1:], lambda q, b: (q, 0, 0, 0, 0))],
        out_specs=[plane_spec] * 4,
        out_shape=[plane] * 4,
        scratch_shapes=[pltpu.VMEM((lane_q, 4 * st_q), BF16)],
        compiler_params=_cparams("arbitrary", "arbitrary"),
        name="s5_chunk_in",
    )(u, u, w_c)

    blk = pl.BlockSpec((nv, rows, 128), lambda j: (j, 0, 0))
    x_planes = pl.pallas_call(
        functools.partial(_s5_scan_kernel, n_tiles=rows // 8),
        grid=(n_q,),
        in_specs=[blk] * 4 + [pl.BlockSpec((4, nv, 1, 128), lambda j: (0, j, 0, 0))],
        out_specs=[blk] * 4,
        out_shape=[plane] * 4,
        compiler_params=_cparams("arbitrary"),
        name="s5_chunk_scan",
    )(*w_planes, decay)

    return pl.pallas_call(
        functools.partial(_s5_out_kernel, **dims),
        grid=(n_q, n_batch),
        in_specs=[ul_spec] + [plane_spec] * 4
                 + [pl.BlockSpec((None,) + k_c.shape[1:], lambda q, b: (q, 0, 0, 0)),
                    pl.BlockSpec((None,) + vt_c.shape[1:], lambda q, b: (q, 0, 0, 0, 0))],
        out_specs=pl.BlockSpec((l_lat, 128), lambda q, b: (b, q)),
        out_shape=jax.ShapeDtypeStruct((n_batch * l_lat, wd), F32),
        scratch_shapes=[pltpu.VMEM((lane_q, lane_q), BF16), pltpu.VMEM((4, lane_q, st_q), BF16)],
        compiler_params=_cparams("arbitrary", "arbitrary"),
        name="s5_chunk_out",
    )(u, *x_planes, k_c, vt_c)


def kernel(x, c, ctx, c_ctx, ada_w, ada_b, ln_mix_g, ln_mix_b, ln_ffn_g, ln_ffn_b, ev_w_in, ev_gate_w2,
           ev_gate_b, ev_rpb, ev_norm_g, ev_w_out, od_w_in, od_lam_re, od_lam_im, od_log_dt, od_b_re,
           od_b_im, od_c_re, od_c_im, od_d, od_w_glu, od_b_glu, od_w_out, router_w, router_b,
           moe_w_gate, moe_w_up, moe_w_down):
    n_batch, l_lat, d = x.shape
    l_ctx = ctx.shape[1]
    depth = ada_w.shape[0]
    assert depth == 2, "one even (NA + GLA) layer followed by one odd (S5) layer"
    alpha = (2.0 * depth) ** 0.25
    n_lat = n_batch * l_lat

    cvec = jnp.concatenate([c, c_ctx[None], jnp.zeros((8 - n_batch - 1, d), F32)], 0)
    mods = compute_mods(cvec, ada_w, ada_b)
    mods4 = mods.reshape(depth, 8, 1, N_MOD * d)
    rows = jnp.concatenate([x.reshape(n_lat, d), ctx.reshape(n_batch * l_ctx, d)], 0)
    router_wt = router_w.T.astype(F32)

    na_w = NA_HEADS * NA_DH
    wk = GLA_HEADS * GLA_DK
    wv = GLA_HEADS * GLA_DV
    ev_in = ev_w_in.shape[2]
    pad = (-ev_in) % 256
    w_in = jnp.pad(ev_w_in[0], ((0, 0), (0, pad))).astype(BF16)
    proj = mod_matmul(rows, mods4, 0, w_in, l_lat, n_batch, tm=512, tn=(ev_in + pad) // 2)
    a_lat, a_ctx = na_attention(proj, na_bias_table(ev_rpb[0], l_lat // GRID_W), n_batch, l_lat, l_ctx)
    col_lr = (3 * na_w + 2 * wk + 2 * wv) // 128
    g2 = jnp.zeros((2, 128, wk), F32)
    g2 = g2.at[0, 0:GLA_RANK].set(ev_gate_w2[0, 0]).at[1, GLA_RANK:2 * GLA_RANK].set(ev_gate_w2[0, 1])
    o_f, o_b = gla_bidir(proj, g2.astype(BF16), ev_gate_b[0].reshape(2, 1, wk), rope_tables(l_lat, l_ctx),
                         n_batch, l_lat, l_ctx,
                         col_q=3 * na_w // wk, col_k=(3 * na_w + wk) // wk,
                         col_v=(3 * na_w + 2 * wk) // wv, col_lr=col_lr)
    a_all = jnp.concatenate([a_lat, a_ctx], 0)
    x1, h2, logits_t = even_out(a_all, o_f, o_b, proj, (3 * na_w + 2 * wk + wv) // wv, rows, mods4, 0,
                                ev_norm_g[0], ev_w_out[0].astype(BF16), ln_mix_g[0], ln_mix_b[0],
                                router_wt, alpha, l_lat, n_batch)
    w_gate, w_up, w_down = moe_w_gate.astype(BF16), moe_w_up.astype(BF16), moe_w_down.astype(BF16)
    rows = moe_block(x1, h2, logits_t, router_b, w_gate, w_up, w_down, mods4, 0,
                     ln_ffn_g[0], ln_ffn_b[0], alpha, l_lat, n_batch)

    u = mod_matmul(rows, mods4, 1, od_w_in[0].astype(BF16), l_lat, n_batch)
    mats = s5_matrices(od_lam_re[0], od_lam_im[0], od_log_dt[0], od_b_re[0], od_b_im[0],
                       od_c_re[0], od_c_im[0], od_d[0])
    y5 = s5_bidir(u, mats, n_batch, l_lat, l_ctx)
    x1, h2, logits_t = odd_out(y5, rows, mods4, 1, od_w_glu[0].astype(BF16), od_b_glu[0],
                               od_w_out[0].astype(BF16), ln_mix_g[1], ln_mix_b[1], router_wt,
                               alpha, l_lat, n_batch)
    out = moe_block(x1, h2, logits_t, router_b, w_gate, w_up, w_down, mods4, 1,
                    ln_ffn_g[1], ln_ffn_b[1], alpha, l_lat, n_batch)
    return out.reshape(n_batch, l_lat, d)
```

```python
import functools
import math

import numpy as np
import jax
import jax.numpy as jnp
from jax import lax
from jax.experimental import pallas as pl
from jax.experimental.pallas import tpu as pltpu

F32 = jnp.float32
BF16 = jnp.bfloat16
HIGHEST = lax.Precision.HIGHEST

N_MOD = 6
LN_EPS = 1e-5
NORM_EPS = 1e-6

GRID_W = 64
NA_HEADS = 8
NA_DH = 128
NA_KR = 8
NA_KC = 16

GLA_HEADS = 4
GLA_DK = 128
GLA_DV = 256
GLA_RANK = 16
GLA_TAU = 16.0
GLA_CHUNK = 64
ROPE_BASE = 10000.0

S5_CH = 16
S5_P = 64
S5_TC = 16

N_EXPERTS = 16
N_GROUPS = 4
TOP_K = 2

VMEM_LIMIT = 56 * 1024 * 1024
NEG_BIG = -1e30


def _cparams(*sem):
    return pltpu.CompilerParams(dimension_semantics=sem, vmem_limit_bytes=VMEM_LIMIT)


def _dot(a, b, precision=None):
    return jnp.dot(a, b, preferred_element_type=F32, precision=precision)


def _dot_nt(a, b, precision=None):
    return lax.dot_general(a, b, (((1,), (1,)), ((), ())), preferred_element_type=F32, precision=precision)


def _dot_tn(a, b):
    return lax.dot_general(a, b, (((0,), (0,)), ((), ())), preferred_element_type=F32)


def _mods_kernel(s_ref, w_ref, b_ref, o_ref):
    s = s_ref[...]
    s = s * jax.nn.sigmoid(s)
    o_ref[0] = _dot(s, w_ref[0], HIGHEST) + b_ref[0]


def compute_mods(cvec, ada_w, ada_b, tn=1024):
    n_layer, d, n = ada_w.shape
    tn = math.gcd(tn, n)
    return pl.pallas_call(
        _mods_kernel,
        grid=(n_layer, n // tn),
        in_specs=[pl.BlockSpec((8, d), lambda l, j: (0, 0)),
                  pl.BlockSpec((1, d, tn), lambda l, j: (l, 0, j)),
                  pl.BlockSpec((1, 1, tn), lambda l, j: (l, 0, j))],
        out_specs=pl.BlockSpec((1, 8, tn), lambda l, j: (l, 0, j)),
        out_shape=jax.ShapeDtypeStruct((n_layer, 8, n), F32),
        compiler_params=_cparams("arbitrary", "arbitrary"),
        name="ada_mods",
    )(cvec, ada_w, ada_b.reshape(n_layer, 1, n))


def _mod_spec(d, layer, which, seg_of_tile):
    return pl.BlockSpec((None, None, 1, d), lambda i, *_: (layer, seg_of_tile(i), 0, which))


def _seg_fn(tm, seg_rows, n_batch):
    return lambda i: jnp.minimum((i * tm) // seg_rows, n_batch)


def _modmm_kernel(x_ref, s1_ref, s0_ref, w_ref, o_ref):
    h = x_ref[...] * (1.0 + s1_ref[...]) + s0_ref[...]
    o_ref[...] = _dot(h.astype(BF16), w_ref[...])


def mod_matmul(x, mods4, layer, w_bf16, seg_rows, n_batch, tm=256, tn=None):
    t, d = x.shape
    n = w_bf16.shape[1]
    tn = n if tn is None else tn
    seg = _seg_fn(tm, seg_rows, n_batch)
    return pl.pallas_call(
        _modmm_kernel,
        grid=(n // tn, t // tm),
        in_specs=[pl.BlockSpec((tm, d), lambda j, i: (i, 0)),
                  pl.BlockSpec((None, None, 1, d), lambda j, i: (layer, seg(i), 0, 1)),
                  pl.BlockSpec((None, None, 1, d), lambda j, i: (layer, seg(i), 0, 0)),
                  pl.BlockSpec((d, tn), lambda j, i: (0, j))],
        out_specs=pl.BlockSpec((tm, tn), lambda j, i: (i, j)),
        out_shape=jax.ShapeDtypeStruct((t, n), F32),
        compiler_params=_cparams("arbitrary", "arbitrary"),
        name="mod_matmul",
    )(x, mods4, mods4, w_bf16)


NA_RB = 4
NA_BAND = NA_RB + NA_KR - 1


def _na_row_start(r, rows):
    return min(max(r - NA_KR // 2, 0), rows - NA_KR)


def na_bias_table(rpb, rows):
    w = GRID_W
    q = np.arange(w)
    kc = np.arange(w)
    win0 = np.clip(q - NA_KC // 2, 0, w - NA_KC)
    ok = (kc[None, :] >= win0[:, None]) & (kc[None, :] < win0[:, None] + NA_KC)
    dc = np.clip(kc[None, :] - q[:, None] + NA_KC - 1, 0, 2 * NA_KC - 2)
    pick = ((dc[None] == np.arange(2 * NA_KC - 1)[:, None, None]) & ok[None]).astype(np.float32)
    colb = jnp.einsum("hrd,dqk->hrqk", rpb.astype(F32), jnp.asarray(pick), precision=HIGHEST)
    colb = jnp.where(ok[None, None], colb, NEG_BIG)
    neg = jnp.full((rpb.shape[0], w, w), NEG_BIG, F32)

    def block(r0):
        band0 = min(max(r0 - NA_KR // 2, 0), rows - NA_BAND)
        out = []
        for r in range(r0, r0 + NA_RB):
            rs = _na_row_start(r, rows)
            first = rs - r + NA_KR - 1
            cols = [neg] * (rs - band0) + [colb[:, first + j] for j in range(NA_KR)]
            cols += [neg] * (NA_BAND - len(cols))
            out.append(jnp.concatenate(cols, -1))
        return jnp.concatenate(out, 1)

    return jnp.stack([block(0), block(NA_RB), block(rows - NA_RB)], 1)


def _na_kernel(q_ref, k_ref, v_ref, qc_ref, kc_ref, vc_ref, bias_ref, o_ref, oc_ref, kbf, vbf, *, rows):
    w = GRID_W
    n_blk = rows // NA_RB
    scale = NA_DH ** -0.5
    kbf[...] = k_ref[...].astype(BF16)
    vbf[...] = v_ref[...].astype(BF16)
    kc = kc_ref[...].astype(BF16)
    vc = vc_ref[...].astype(BF16)

    def body(i, carry):
        r0 = i * NA_RB
        band0 = jnp.clip(r0 - NA_KR // 2, 0, rows - NA_BAND)
        variant = jnp.where(i == 0, 0, jnp.where(i == n_blk - 1, 2, 1))
        q0 = pl.multiple_of(r0 * w, NA_RB * w)
        k0 = pl.multiple_of(band0 * w, w)
        q = (q_ref[pl.ds(q0, NA_RB * w), :] * scale).astype(BF16)
        kb = kbf[pl.ds(k0, NA_BAND * w), :]
        vb = vbf[pl.ds(k0, NA_BAND * w), :]
        s_loc = _dot_nt(q, kb) + bias_ref[variant]
        s_ctx = _dot_nt(q, kc)
        m = jnp.maximum(jnp.max(s_loc, -1, keepdims=True), jnp.max(s_ctx, -1, keepdims=True))
        p_loc = jnp.exp(s_loc - m)
        p_ctx = jnp.exp(s_ctx - m)
        den = jnp.sum(p_loc, -1, keepdims=True) + jnp.sum(p_ctx, -1, keepdims=True)
        o = _dot(p_loc.astype(BF16), vb) + _dot(p_ctx.astype(BF16), vc)
        o_ref[pl.ds(q0, NA_RB * w), :] = o / den
        return carry

    lax.fori_loop(0, n_blk, body, 0)

    qc = (qc_ref[...] * scale).astype(BF16)
    s = _dot_nt(qc, kc)
    p = jnp.exp(s - jnp.max(s, -1, keepdims=True))
    oc_ref[...] = _dot(p.astype(BF16), vc) / jnp.sum(p, -1, keepdims=True)


def na_attention(proj, bias_tab, n_batch, l_lat, l_ctx):
    h = NA_HEADS
    dh = NA_DH
    rows = l_lat // GRID_W
    ctx0 = (n_batch * l_lat) // l_ctx
    return pl.pallas_call(
        functools.partial(_na_kernel, rows=rows),
        grid=(n_batch, h),
        in_specs=[pl.BlockSpec((l_lat, dh), lambda b, hh: (b, hh)),
                  pl.BlockSpec((l_lat, dh), lambda b, hh: (b, h + hh)),
                  pl.BlockSpec((l_lat, dh), lambda b, hh: (b, 2 * h + hh)),
                  pl.BlockSpec((l_ctx, dh), lambda b, hh: (ctx0 + b, hh)),
                  pl.BlockSpec((l_ctx, dh), lambda b, hh: (ctx0 + b, h + hh)),
                  pl.BlockSpec((l_ctx, dh), lambda b, hh: (ctx0 + b, 2 * h + hh)),
                  pl.BlockSpec((None,) + bias_tab.shape[1:], lambda b, hh: (hh, 0, 0, 0))],
        out_specs=[pl.BlockSpec((l_lat, dh), lambda b, hh: (b, hh)),
                   pl.BlockSpec((l_ctx, dh), lambda b, hh: (b, hh))],
        out_shape=[jax.ShapeDtypeStruct((n_batch * l_lat, h * dh), F32),
                   jax.ShapeDtypeStruct((n_batch * l_ctx, h * dh), F32)],
        scratch_shapes=[pltpu.VMEM((l_lat, dh), BF16), pltpu.VMEM((l_lat, dh), BF16)],
        compiler_params=_cparams("arbitrary", "arbitrary"),
        name="na_attention",
    )(proj, proj, proj, proj, proj, proj, bias_tab)


def rope_tables(l_lat, l_ctx):
    half = GLA_DK // 2
    nf = half // 2
    inv = ROPE_BASE ** (-np.arange(nf, dtype=np.float64) / nf)
    t = np.arange(l_lat)
    lane = np.arange(GLA_DK)
    pos = np.where(lane[None, :] < half, (t // GRID_W)[:, None], (t % GRID_W)[:, None]).astype(np.float64)
    ang = pos * inv[lane % nf][None, :]
    first = (lane % half) < nf
    cos = np.cos(ang)
    sin_a = np.where(first[None, :], -np.sin(ang), 0.0)
    sin_b = np.where(first[None, :], 0.0, np.sin(ang))
    one = np.ones((l_ctx, GLA_DK))
    zero = np.zeros((l_ctx, GLA_DK))
    cat = lambda ident, a: jnp.asarray(np.concatenate([ident, a, ident], 0), F32)
    return cat(one, cos), cat(zero, sin_a), cat(zero, sin_b)


def _gla_direction(q_ref, k_ref, v_ref, lr_ref, cos_ref, sa_ref, sb_ref, g2_ref, gb_ref, st_ref, o_ref, reverse):
    c = GLA_CHUNK
    nf = GLA_DK // 4
    gscale = GLA_DK ** -0.5
    row = lax.broadcasted_iota(jnp.int32, (c, c), 0)
    col = lax.broadcasted_iota(jnp.int32, (c, c), 1)
    seen = (col >= row) if reverse else (col <= row)
    tri = seen.astype(F32)

    z = _dot(lr_ref[...].astype(BF16), g2_ref[...]) + gb_ref[...]
    g = (jnp.minimum(z, 0.0) - jnp.log1p(jnp.exp(-jnp.abs(z)))) * (1.0 / GLA_TAU)
    b = _dot(tri, g, HIGHEST)
    b_last = b[0:1, :] if reverse else b[c - 1:c, :]
    e_b = jnp.exp(b)
    e_nb = jnp.exp(-b)
    e_rem = jnp.exp(b_last - b)
    e_last = jnp.exp(b_last)
    cos = cos_ref[...]
    sa = sa_ref[...]
    sb = sb_ref[...]

    def rope(x):
        return x * cos + pltpu.roll(x, GLA_DK - nf, 1) * sa + pltpu.roll(x, nf, 1) * sb

    for h in range(GLA_HEADS):
        ks = slice(h * GLA_DK, (h + 1) * GLA_DK)
        vs = slice(h * GLA_DV, (h + 1) * GLA_DV)
        qh = rope(q_ref[:, ks]) * gscale
        kh = rope(k_ref[:, ks])
        q_dec = (qh * e_b[:, ks]).astype(BF16)
        k_dec = (kh * e_nb[:, ks]).astype(BF16)
        k_rem = (kh * e_rem[:, ks]).astype(BF16)
        vh = v_ref[:, vs].astype(BF16)
        att = jnp.where(seen, _dot_nt(q_dec, k_dec), 0.0)
        st = st_ref[h]
        o_ref[:, vs] = _dot(att.astype(BF16), vh) + _dot_nt(q_dec, st.astype(BF16))
        st_ref[h] = st * e_last[:, ks] + _dot_tn(vh, k_rem)


def _gla_kernel(qf, kf, vf, lrf, cf, saf, sbf, qb, kb, vb, lrb, cb, sab, sbb, g2_ref, gb_ref,
                of_ref, ob_ref, st_ref):
    @pl.when(pl.program_id(1) == 0)
    def _():
        st_ref[...] = jnp.zeros_like(st_ref)

    _gla_direction(qf, kf, vf, lrf, cf, saf, sbf, g2_ref.at[0], gb_ref.at[0], st_ref.at[0], of_ref, False)
    _gla_direction(qb, kb, vb, lrb, cb, sab, sbb, g2_ref.at[1], gb_ref.at[1], st_ref.at[1], ob_ref, True)


def gla_bidir(proj, g2, gb, tables, n_batch, l_lat, l_ctx, col_q, col_k, col_v, col_lr):
    c = GLA_CHUNK
    nc = l_ctx // c
    nl = l_lat // c
    nz = nl + 2 * nc
    steps = nl + nc
    wk = GLA_HEADS * GLA_DK
    wv = GLA_HEADS * GLA_DV
    t_rows = n_batch * (l_lat + l_ctx)

    def zblk(b, j):
        lat = b * nl + (j - nc)
        ctx = n_batch * nl + b * nc + jnp.where(j < nc, j, j - nc - nl)
        return jnp.where((j >= nc) & (j < nc + nl), lat, ctx)

    fwd = lambda b, i: zblk(b, i)
    bwd = lambda b, i: zblk(b, nz - 1 - i)

    def dir_specs(blk, tab):
        return [pl.BlockSpec((c, wk), lambda b, i: (blk(b, i), col_q)),
                pl.BlockSpec((c, wk), lambda b, i: (blk(b, i), col_k)),
                pl.BlockSpec((c, wv), lambda b, i: (blk(b, i), col_v)),
                pl.BlockSpec((c, 128), lambda b, i: (blk(b, i), col_lr)),
                pl.BlockSpec((c, GLA_DK), lambda b, i: (tab(i), 0)),
                pl.BlockSpec((c, GLA_DK), lambda b, i: (tab(i), 0)),
                pl.BlockSpec((c, GLA_DK), lambda b, i: (tab(i), 0))]

    cos, sa, sb = tables
    return pl.pallas_call(
        _gla_kernel,
        grid=(n_batch, steps),
        in_specs=(dir_specs(fwd, lambda i: i) + dir_specs(bwd, lambda i: nz - 1 - i)
                  + [pl.BlockSpec((2, 128, wk), lambda b, i: (0, 0, 0)),
                     pl.BlockSpec((2, 1, wk), lambda b, i: (0, 0, 0))]),
        out_specs=[pl.BlockSpec((c, wv), lambda b, i: (fwd(b, i), 0)),
                   pl.BlockSpec((c, wv), lambda b, i: (bwd(b, i), 0))],
        out_shape=[jax.ShapeDtypeStruct((t_rows, wv), F32), jax.ShapeDtypeStruct((t_rows, wv), F32)],
        scratch_shapes=[pltpu.VMEM((2, GLA_HEADS, GLA_DV, GLA_DK), F32)],
        compiler_params=_cparams("arbitrary", "arbitrary"),
        name="gla_bidir",
    )(proj, proj, proj, proj, cos, sa, sb, proj, proj, proj, proj, cos, sa, sb, g2, gb)


def _post_mix(out, x_ref, m2_ref, m3_ref, m4_ref, lg_ref, lb_ref, wr_ref, alpha, x1_ref, h2_ref, lt_ref):
    y = alpha * x_ref[...] + m2_ref[...] * out
    mu = jnp.mean(y, -1, keepdims=True)
    yc = y - mu
    var = jnp.mean(yc * yc, -1, keepdims=True)
    x1 = yc * lax.rsqrt(var + LN_EPS) * lg_ref[...] + lb_ref[...]
    h2 = x1 * (1.0 + m4_ref[...]) + m3_ref[...]
    x1_ref[...] = x1
    n_exp = lt_ref.shape[0]
    h2_hi = h2.astype(BF16)
    h2_lo = (h2 - h2_hi.astype(F32)).astype(BF16)
    h2_ref[...] = h2_hi
    wr = wr_ref[...]
    wr_hi = wr.astype(BF16)
    wr_lo = (wr - wr_hi.astype(F32)).astype(BF16)
    both = _dot_nt(jnp.concatenate([wr_hi, wr_lo], 0), h2_hi)
    lt_ref[...] = both[:n_exp] + both[n_exp:] + _dot_nt(wr_hi, h2_lo)


def _even_out_kernel(a_ref, of_ref, ob_ref, r_ref, x_ref, m2_ref, m3_ref, m4_ref, ng_ref, wo_ref,
                     lg_ref, lb_ref, wr_ref, x1_ref, h2_ref, lt_ref, *, alpha):
    o = of_ref[...] + ob_ref[...]
    r = r_ref[...]
    gate = r * jax.nn.sigmoid(r)
    acc = _dot(a_ref[...].astype(BF16), wo_ref[0:a_ref.shape[1], :])
    na = a_ref.shape[1]
    for h in range(GLA_HEADS):
        vs = slice(h * GLA_DV, (h + 1) * GLA_DV)
        oh = o[:, vs]
        nrm = oh * lax.rsqrt(jnp.mean(oh * oh, -1, keepdims=True) + NORM_EPS) * ng_ref[...]
        bh = (nrm * gate[:, vs]).astype(BF16)
        acc = acc + _dot(bh, wo_ref[na + h * GLA_DV:na + (h + 1) * GLA_DV, :])
    _post_mix(acc, x_ref, m2_ref, m3_ref, m4_ref, lg_ref, lb_ref, wr_ref, alpha, x1_ref, h2_ref, lt_ref)


def _post_specs(d, layer, seg, tm, n_exp):
    ins = [pl.BlockSpec((tm, d), lambda i: (i, 0)),
           _mod_spec(d, layer, 2, seg), _mod_spec(d, layer, 3, seg), _mod_spec(d, layer, 4, seg)]
    tail = [pl.BlockSpec((1, d), lambda i: (0, 0)), pl.BlockSpec((1, d), lambda i: (0, 0)),
            pl.BlockSpec((n_exp, d), lambda i: (0, 0))]
    outs =[pl.BlockSpec((tm, d), lambda i: (i, 0)), pl.BlockSpec((tm, d), lambda i: (i, 0)),
            pl.BlockSpec((n_exp, tm), lambda i: (0, i))]
    return ins, tail, outs


def _post_shapes(t, d, n_exp):
    return [jax.ShapeDtypeStruct((t, d), F32), jax.ShapeDtypeStruct((t, d), BF16),
            jax.ShapeDtypeStruct((n_exp, t), F32)]


def even_out(a, o_f, o_b, proj, col_r, x, mods4, layer, norm_g, w_out_bf16, ln_g, ln_b, router_wt,
             alpha, seg_rows, n_batch, tm=256):
    t, d = x.shape
    na = a.shape[1]
    wv = o_f.shape[1]
    n_exp = router_wt.shape[0]
    seg = _seg_fn(tm, seg_rows, n_batch)
    ins, tail, outs = _post_specs(d, layer, seg, tm, n_exp)
    return pl.pallas_call(
        functools.partial(_even_out_kernel, alpha=alpha),
        grid=(t // tm,),
        in_specs=([pl.BlockSpec((tm, na), lambda i: (i, 0)),
                   pl.BlockSpec((tm, wv), lambda i: (i, 0)),
                   pl.BlockSpec((tm, wv), lambda i: (i, 0)),
                   pl.BlockSpec((tm, wv), lambda i: (i, col_r))] + ins
                  + [pl.BlockSpec((1, GLA_DV), lambda i: (0, 0)),
                     pl.BlockSpec((na + wv, d), lambda i: (0, 0))] + tail),
        out_specs=outs,
        out_shape=_post_shapes(t, d, n_exp),
        compiler_params=_cparams("arbitrary"),
        name="even_out",
    )(a, o_f, o_b, proj, x, mods4, mods4, mods4, norm_g.reshape(1, -1), w_out_bf16,
      ln_g.reshape(1, -1), ln_b.reshape(1, -1), router_wt)


def _odd_out_kernel(y_ref, x_ref, m2_ref, m3_ref, m4_ref, wg_ref, bg_ref, wo_ref,
                    lg_ref, lb_ref, wr_ref, x1_ref, h2_ref, lt_ref, *, alpha):
    g = jax.nn.gelu(y_ref[...], approximate=True)
    z = _dot(g.astype(BF16), wg_ref[...]) + bg_ref[...]
    v = g * jax.nn.sigmoid(z)
    out = _dot(v.astype(BF16), wo_ref[...])
    _post_mix(out, x_ref, m2_ref, m3_ref, m4_ref, lg_ref, lb_ref, wr_ref, alpha, x1_ref, h2_ref, lt_ref)


def odd_out(y, x, mods4, layer, w_glu_bf16, b_glu, w_out_bf16, ln_g, ln_b, router_wt,
            alpha, seg_rows, n_batch, tm=256):
    t, w5 = y.shape
    d = x.shape[1]
    n_exp = router_wt.shape[0]
    seg = _seg_fn(tm, seg_rows, n_batch)
    ins, tail, outs = _post_specs(d, layer, seg, tm, n_exp)
    return pl.pallas_call(
        functools.partial(_odd_out_kernel, alpha=alpha),
        grid=(t // tm,),
        in_specs=([pl.BlockSpec((tm, w5), lambda i: (i, 0))] + ins
                  + [pl.BlockSpec((w5, w5), lambda i: (0, 0)),
                     pl.BlockSpec((1, w5), lambda i: (0, 0)),
                     pl.BlockSpec((w5, d), lambda i: (0, 0))] + tail),
        out_specs=outs,
        out_shape=_post_shapes(t, d, n_exp),
        compiler_params=_cparams("arbitrary"),
        name="odd_out",
    )(y, x, mods4, mods4, mods4, w_glu_bf16, b_glu.reshape(1, -1), w_out_bf16,
      ln_g.reshape(1, -1), ln_b.reshape(1, -1), router_wt)


def _route_kernel(lt_ref, rb_ref, idx_ref, w_ref):
    eg = N_EXPERTS // N_GROUPS
    logits = lt_ref[...]
    aff = jax.nn.sigmoid(logits)
    sel = aff + rb_ref[...]
    s = [sel[e:e + 1, :] for e in range(N_EXPERTS)]
    a = [aff[e:e + 1, :] for e in range(N_EXPERTS)]

    def top2_sum(v):
        hi1, lo1 = jnp.maximum(v[0], v[1]), jnp.minimum(v[0], v[1])
        hi2, lo2 = jnp.maximum(v[2], v[3]), jnp.minimum(v[2], v[3])
        return jnp.maximum(hi1, hi2) + jnp.maximum(jnp.minimum(hi1, hi2), jnp.maximum(lo1, lo2))

    best = top2_sum(s[0:eg])
    grp = jnp.zeros_like(best, dtype=jnp.int32)
    for g in range(1, N_GROUPS):
        sc = top2_sum(s[g * eg:(g + 1) * eg])
        better = sc > best
        best = jnp.where(better, sc, best)
        grp = jnp.where(better, g, grp)

    def pick(vals, j):
        out = vals[j]
        for g in range(1, N_GROUPS):
            out = jnp.where(grp == g, vals[g * eg + j], out)
        return out

    sv = [pick(s, j) for j in range(eg)]
    av = [pick(a, j) for j in range(eg)]

    def argmax_first(vals, exclude):
        bi = jnp.zeros_like(grp)
        bv = jnp.where(exclude == 0, -jnp.inf, vals[0]) if exclude is not None else vals[0]
        for j in range(1, eg):
            vj = jnp.where(exclude == j, -jnp.inf, vals[j]) if exclude is not None else vals[j]
            better = vj > bv
            bv = jnp.where(better, vj, bv)
            bi = jnp.where(better, j, bi)
        return bi

    i1 = argmax_first(sv, None)
    i2 = argmax_first(sv, i1)

    def take(vals, i):
        out = vals[0]
        for j in range(1, eg):
            out = jnp.where(i == j, vals[j], out)
        return out

    w1 = take(av, i1)
    w2 = take(av, i2)
    tot = w1 + w2
    idx_ref[0:1, :] = grp * eg + i1
    idx_ref[1:2, :] = grp * eg + i2
    w_ref[0:1, :] = w1 / tot
    w_ref[1:2, :] = w2 / tot


def route(logits_t, router_b, tile=1024):
    n_exp, t = logits_t.shape
    tile = math.gcd(tile, t)
    return pl.pallas_call(
        _route_kernel,
        grid=(t // tile,),
        in_specs=[pl.BlockSpec((n_exp, tile), lambda i: (0, i)),
                  pl.BlockSpec((n_exp, 1), lambda i: (0, 0))],
        out_specs=[pl.BlockSpec((TOP_K, tile), lambda i: (0, i)),
                   pl.BlockSpec((TOP_K, tile), lambda i: (0, i))],
        out_shape=[jax.ShapeDtypeStruct((TOP_K, t), jnp.int32), jax.ShapeDtypeStruct((TOP_K, t), F32)],
        compiler_params=_cparams("arbitrary"),
        name="moe_route",
    )(logits_t, router_b.reshape(n_exp, 1).astype(F32))


def moe_plan(idx, tm):
    t = idx.shape[1]
    n_pair = TOP_K * t
    n_tiles = (n_pair + N_EXPERTS * (tm - 1)) // tm
    e_flat = idx.reshape(-1)
    onehot = (e_flat[:, None] == jnp.arange(N_EXPERTS)[None, :]).astype(jnp.int32)
    running = jnp.cumsum(onehot, axis=0)
    counts = running[-1]
    rank = jnp.sum(onehot * running, 1) - 1
    tiles_per = (counts + tm - 1) // tm
    tile_end = jnp.cumsum(tiles_per)
    n_used = tile_end[-1]
    pstart = (tile_end - tiles_per) * tm
    pos = jnp.sum(onehot * pstart[None, :], 1) + rank
    tile_expert = jnp.minimum(jnp.sum((tile_end[None, :] <= jnp.arange(n_tiles)[:, None]).astype(jnp.int32), 1),
                              N_EXPERTS - 1).astype(jnp.int32)
    gidx = jnp.zeros((n_tiles * tm,), jnp.int32).at[pos].set(
        jnp.arange(n_pair, dtype=jnp.int32) % t, mode="promise_in_bounds", unique_indices=True)
    return gidx, tile_expert, n_used.reshape(1).astype(jnp.int32), pos.astype(jnp.int32)


def _expert_changed(te_ref):
    i = pl.program_id(0)
    return jnp.logical_or(i == 0, te_ref[i] != te_ref[jnp.maximum(i - 1, 0)])


def _ffn_up_kernel(te_ref, nu_ref, xs_ref, wg_ref, wu_ref, hid_ref, wg_bf, wu_bf):
    used = pl.program_id(0) < nu_ref[0]

    @pl.when(jnp.logical_and(used, _expert_changed(te_ref)))
    def _():
        wg_bf[...] = wg_ref[0].astype(BF16)
        wu_bf[...] = wu_ref[0].astype(BF16)

    @pl.when(used)
    def _():
        xs = xs_ref[...]
        g = _dot(xs, wg_bf[...])
        u = _dot(xs, wu_bf[...])
        hid_ref[...] = ((g * jax.nn.sigmoid(g)) * u).astype(BF16)

    @pl.when(jnp.logical_not(used))
    def _():
        hid_ref[...] = jnp.zeros_like(hid_ref)


def _ffn_down_kernel(te_ref, nu_ref, hid_ref, wd_ref, o_ref, wd_bf):
    used = pl.program_id(0) < nu_ref[0]

    @pl.when(jnp.logical_and(used, _expert_changed(te_ref)))
    def _():
        wd_bf[...] = wd_ref[0].astype(BF16)

    @pl.when(used)
    def _():
        o_ref[...] = _dot(hid_ref[...], wd_bf[...]).astype(o_ref.dtype)

    @pl.when(jnp.logical_not(used))
    def _():
        o_ref[...] = jnp.zeros_like(o_ref)


def grouped_ffn(xs, tile_expert, n_used, w_gate, w_up, w_down, layer, tm):
    p, d = xs.shape
    de = w_gate.shape[3]
    n_tiles = p // tm
    wmap = lambda i, te, nu: (layer, te[i], 0, 0)
    row_in = lambda i, te, nu: (jnp.minimum(i, nu[0] - 1), 0)
    row_out = lambda i, te, nu: (i, 0)
    hid = pl.pallas_call(
        _ffn_up_kernel,
        grid_spec=pltpu.PrefetchScalarGridSpec(
            num_scalar_prefetch=2,
            grid=(n_tiles,),
            in_specs=[pl.BlockSpec((tm, d), row_in),
                      pl.BlockSpec((None, 1, d, de), wmap),
                      pl.BlockSpec((None, 1, d, de), wmap)],
            out_specs=pl.BlockSpec((tm, de), row_out),
            scratch_shapes=[pltpu.VMEM((d, de), BF16), pltpu.VMEM((d, de), BF16)]),
        out_shape=jax.ShapeDtypeStruct((p, de), BF16),
        compiler_params=_cparams("arbitrary"),
        name="moe_ffn_up",
    )(tile_expert, n_used, xs, w_gate, w_up)
    return pl.pallas_call(
        _ffn_down_kernel,
        grid_spec=pltpu.PrefetchScalarGridSpec(
            num_scalar_prefetch=2,
            grid=(n_tiles,),
            in_specs=[pl.BlockSpec((tm, de), row_in),
                      pl.BlockSpec((None, 1, de, d), wmap)],
            out_specs=pl.BlockSpec((tm, d), row_out),
            scratch_shapes=[pltpu.VMEM((de, d), BF16)]),
        out_shape=jax.ShapeDtypeStruct((p, d), BF16),
        compiler_params=_cparams("arbitrary"),
        name="moe_ffn_down",
    )(tile_expert, n_used, hid, w_down)


def _final_kernel(x_ref, y0_ref, y1_ref, w_ref, m5_ref, lg_ref, lb_ref, o_ref, *, alpha):
    w = w_ref[...]
    y = w[:, 0:1] * y0_ref[...].astype(F32) + w[:, 1:2] * y1_ref[...].astype(F32)
    z = alpha * x_ref[...] + m5_ref[...] * y
    mu = jnp.mean(z, -1, keepdims=True)
    zc = z - mu
    var = jnp.mean(zc * zc, -1, keepdims=True)
    o_ref[...] = zc * lax.rsqrt(var + LN_EPS) * lg_ref[...] + lb_ref[...]


def final_norm(x1, yg, wts, mods4, layer, ln_g, ln_b, alpha, seg_rows, n_batch, tm=256):
    t, d = x1.shape
    seg = _seg_fn(tm, seg_rows, n_batch)
    row = pl.BlockSpec((tm, d), lambda i: (i, 0))
    vec = pl.BlockSpec((1, d), lambda i: (0, 0))
    return pl.pallas_call(
        functools.partial(_final_kernel, alpha=alpha),
        grid=(t // tm,),
        in_specs=[row, row, pl.BlockSpec((tm, d), lambda i: (i + t // tm, 0)),
                  pl.BlockSpec((tm, TOP_K), lambda i: (i, 0)),
                  _mod_spec(d, layer, 5, seg), vec, vec],
        out_specs=row,
        out_shape=jax.ShapeDtypeStruct((t, d), F32),
        compiler_params=_cparams("arbitrary"),
        name="final_norm",
    )(x1, yg, yg, wts, mods4, ln_g.reshape(1, -1), ln_b.reshape(1, -1))


def moe_block(x1, h2, logits_t, router_b, w_gate, w_up, w_down, mods4, layer, ln_g, ln_b,
              alpha, seg_rows, n_batch, tm=256):
    idx, wts = route(logits_t, router_b)
    gidx, tile_expert, n_used, pos = moe_plan(idx, tm)
    xs = h2.at[gidx].get(mode="promise_in_bounds")
    ys = grouped_ffn(xs, tile_expert, n_used, w_gate, w_up, w_down, layer, tm)
    yg = ys.at[pos].get(mode="promise_in_bounds")
    return final_norm(x1, yg, wts.T, mods4, layer, ln_g, ln_b, alpha, seg_rows, n_batch)


def s5_matrices(lam_re, lam_im, log_dt, b_re, b_im, c_re, c_im, d_skip):
    f32 = F32
    tc = S5_TC
    n_g, n_p = lam_re.shape[1], lam_re.shape[2]
    n_c = b_re.shape[-1]
    nb = 128 // n_c
    n_q = n_g // nb
    lr, li = lam_re.astype(f32), lam_im.astype(f32)
    dt = jnp.exp(log_dt.astype(f32))[..., None]

    def powers(jvals):
        j = jnp.asarray(np.asarray(jvals, np.float32))[:, None, None, None]
        mag = jnp.exp(lr * dt * j)
        return mag * jnp.cos(li * dt * j), mag * jnp.sin(li * dt * j)

    up = np.arange(tc)
    pw_re, pw_im = powers(np.arange(tc + 1))
    lb_re, lb_im = pw_re[1], pw_im[1]
    den = lr * lr + li * li
    fr = ((lb_re - 1.0) * lr + lb_im * li) / den
    fi = (lb_im * lr - (lb_re - 1.0) * li) / den
    br, bi = b_re.astype(f32), b_im.astype(f32)
    bb_re = fr[..., None] * br - fi[..., None] * bi
    bb_im = fr[..., None] * bi + fi[..., None] * br
    cr, ci = c_re.astype(f32), c_im.astype(f32)

    def times_b(p_re, p_im):
        return (p_re[..., None] * bb_re[None] - p_im[..., None] * bb_im[None],
                p_re[..., None] * bb_im[None] + p_im[..., None] * bb_re[None])

    e_re, e_im = times_b(pw_re, pw_im)
    kmat = jnp.sum(cr[None, :, :, :, :, None] * e_re[:, :, :, None, :, :]
                   - ci[None, :, :, :, :, None] * e_im[:, :, :, None, :, :], axis=4)
    def lag_slab(k_dir):
        return k_dir.reshape(tc, n_q, nb, n_c, n_c).transpose(1, 0, 4, 2, 3).reshape(n_q, tc, n_c, nb * n_c)
    skip = (jnp.eye(n_c, dtype=f32)[None, None, :, None, :]
            * d_skip.astype(f32).reshape(n_q, nb, n_c)[:, None, None, :, :]).reshape(n_q, 1, n_c, nb * n_c)
    k_c = jnp.concatenate([lag_slab(kmat[:tc, 0]), lag_slab(kmat[:tc, 1]), skip], 1)

    def w_slab(e):
        return e.reshape(tc, n_q, nb, n_p, n_c).transpose(1, 0, 4, 2, 3).reshape(n_q, tc, n_c, nb * n_p)
    ef_re, ef_im = times_b(*powers(tc - 1 - up))
    w_c = jnp.stack([w_slab(ef_re[:, 0]), w_slab(ef_im[:, 0]),
                     w_slab(e_re[:tc, 1]), w_slab(e_im[:tc, 1])], 2)

    def v_slabs(d, p_re, p_im):
        f_re = cr[d][None] * p_re[:, :, None, :] - ci[d][None] * p_im[:, :, None, :]
        f_im = cr[d][None] * p_im[:, :, None, :] + ci[d][None] * p_re[:, :, None, :]
        slab = lambda m: m.reshape(tc, n_q, nb, n_c, n_p).transpose(1, 0, 3, 2, 4).reshape(n_q, tc, n_c, nb * n_p)
        return slab(f_re), slab(-f_im)
    pb_re, pb_im = powers(tc - up)
    vt_c = jnp.stack(v_slabs(0, pw_re[1:, 0], pw_im[1:, 0]) + v_slabs(1, pb_re[:, 1], pb_im[:, 1]), 2)
    dec = lambda m: m.reshape(1, n_g * n_p // 128, 1, 128)
    decay = jnp.concatenate([dec(pw_re[tc, 0]), dec(pw_im[tc, 0]), dec(pw_re[tc, 1]), dec(pw_im[tc, 1])], 0)
    return k_c, w_c, vt_c, decay


def _s5_chunk_rows(ref, n):
    return jnp.concatenate([ref[pl.ds(s, n, stride=S5_TC), :] for s in range(S5_TC)], axis=1).astype(BF16)


def _s5_expand(slab, group_lanes):
    rows = 128
    tiled = jnp.concatenate([slab] * (rows // slab.shape[0]), axis=0)
    r = lax.broadcasted_iota(jnp.int32, tiled.shape, 0) // S5_CH
    l = lax.broadcasted_iota(jnp.int32, tiled.shape, 1) // group_lanes
    return jnp.where(r == l, tiled, 0.0).astype(BF16)


def _s5_in_kernel(ul_ref, uc_ref, wc_ref, fr_ref, fi_ref, br_ref, bi_ref, w_ref, *, n_lat, n_ctx, n_batch):
    b = pl.program_id(1)
    n_plane = wc_ref.shape[1]
    st = wc_ref.shape[3]

    @pl.when(b == 0)
    def _():
        for s in range(S5_TC):
            for i in range(n_plane):
                w_ref[s * 128:(s + 1) * 128, i * st:(i + 1) * st] = _s5_expand(wc_ref[s, i], S5_P)

    w_lat = _dot(_s5_chunk_rows(ul_ref, n_lat), w_ref[...])
    w_ctx = _dot(_s5_chunk_rows(uc_ref, n_ctx), w_ref[...])
    nv = fr_ref.shape[0]
    for i, ref in enumerate((fr_ref, fi_ref, br_ref, bi_ref)):
        for c in range(nv):
            lanes = slice((i * nv + c) * 128, (i * nv + c + 1) * 128)
            ref[c, pl.ds(b, n_ctx, stride=n_batch), :] = w_ctx[:, lanes]
            ref[c, pl.ds(n_ctx * n_batch + b, n_lat, stride=n_batch), :] = w_lat[:, lanes]
            ref[c, pl.ds((n_ctx + n_lat) * n_batch + b, n_ctx, stride=n_batch), :] = w_ctx[:, lanes]


def _s5_scan_kernel(wfr, wfi, wbr, wbi, dec_ref, xfr, xfi, xbr, xbi, *, n_tiles):
    nv = wfr.shape[0]
    low = lax.broadcasted_iota(jnp.int32, (nv, 8, 128), 1) < 4
    a_fr, a_fi, a_br, a_bi = dec_ref[0], dec_ref[1], dec_ref[2], dec_ref[3]

    def half_step(s_re, s_im, a_re, a_im, w_re, w_im):
        return a_re * s_re - a_im * s_im + w_re, a_re * s_im + a_im * s_re + w_im

    def one_dir(w_re_ref, w_im_ref, x_re_ref, x_im_ref, row0, s_re, s_im, a_re, a_im, first_low):
        first = low if first_low else jnp.logical_not(low)
        wt_re, wt_im = w_re_ref[:, pl.ds(row0, 8), :], w_im_ref[:, pl.ds(row0, 8), :]
        wr_re, wr_im = pltpu.roll(wt_re, 4, 1), pltpu.roll(wt_im, 4, 1)
        mid_re, mid_im = half_step(s_re, s_im, a_re, a_im, wr_re, wr_im)
        x_re_ref[:, pl.ds(row0, 8), :] = jnp.where(first, s_re, mid_re)
        x_im_ref[:, pl.ds(row0, 8), :] = jnp.where(first, s_im, mid_im)
        m_re = jnp.where(first, pltpu.roll(mid_re, 4, 1), mid_re)
        m_im = jnp.where(first, pltpu.roll(mid_im, 4, 1), mid_im)
        w2_re = jnp.where(first, wr_re, wt_re)
        w2_im = jnp.where(first, wr_im, wt_im)
        return half_step(m_re, m_im, a_re, a_im, w2_re, w2_im)

    def body(i, carry):
        f_re, f_im, b_re, b_im = carry
        rf = pl.multiple_of(i * 8, 8)
        rb = pl.multiple_of((n_tiles - 1 - i) * 8, 8)
        f_re, f_im = one_dir(wfr, wfi, xfr, xfi, rf, f_re, f_im, a_fr, a_fi, True)
        b_re, b_im = one_dir(wbr, wbi, xbr, xbi, rb, b_re, b_im, a_br, a_bi, False)
        return f_re, f_im, b_re, b_im

    z = jnp.zeros((nv, 8, 128), F32)
    lax.fori_loop(0, n_tiles, body, (z, z, z, z))


def _s5_out_kernel(ul_ref, fr_ref, fi_ref, br_ref, bi_ref, kc_ref, vc_ref, y_ref, mt_ref, vt_ref,
                   *, n_lat, n_ctx, n_batch):
    b = pl.program_id(1)
    tc = S5_TC

    @pl.when(b == 0)
    def _():
        lag = [_s5_expand(kc_ref[j], S5_CH) for j in range(2 * tc)]
        diag = _s5_expand(kc_ref[0] + kc_ref[tc] + kc_ref[2 * tc], S5_CH)
        for s in range(tc):
            for t in range(tc):
                blk = diag if s == t else (lag[t - s] if t > s else lag[tc + s - t])
                mt_ref[s * 128:(s + 1) * 128, t * 128:(t + 1) * 128] = blk
        for t in range(tc):
            for i in range(vt_ref.shape[0]):
                vt_ref[i, t * 128:(t + 1) * 128, :] = _s5_expand(vc_ref[t, i], S5_P)

    y = _dot(_s5_chunk_rows(ul_ref, n_lat), mt_ref[...])
    row0 = n_ctx * n_batch + b
    for i, ref in enumerate((fr_ref, fi_ref, br_ref, bi_ref)):
        xs = jnp.concatenate([ref[c, pl.ds(row0, n_lat, stride=n_batch), :] for c in range(ref.shape[0])], 1)
        y = y + _dot_nt(xs.astype(BF16), vt_ref[i])
    for s in range(S5_TC):
        y_ref[pl.ds(s, n_lat, stride=S5_TC), :] = y[:, s * 128:(s + 1) * 128]


def s5_bidir(u, mats, n_batch, l_lat, l_ctx):
    assert n_batch == 4, "the chunk scan packs two chunks of 4 batch rows per 8-sublane tile"
    k_c, w_c, vt_c, decay = mats
    tc = S5_TC
    wd = u.shape[1]
    n_q = wd // 128
    lane_q = tc * 128
    st_q = (128 // S5_CH) * S5_P
    n_lat, n_ctx = l_lat // tc, l_ctx // tc
    nk = n_lat + 2 * n_ctx
    assert nk % 2 == 0
    rows = nk * n_batch
    ctx0 = (n_batch * l_lat) // l_ctx
    dims = dict(n_lat=n_lat, n_ctx=n_ctx, n_batch=n_batch)

    nv = st_q // 128
    plane = jax.ShapeDtypeStruct((n_q * nv, rows, 128), F32)
    plane_spec = pl.BlockSpec((nv, rows, 128), lambda q, b: (q, 0, 0))
    ul_spec = pl.BlockSpec((l_lat, 128), lambda q, b: (b, q))
    uc_spec = pl.BlockSpec((l_ctx, 128), lambda q, b: (ctx0 + b, q))
    w_planes = pl.pallas_call(
        functools.partial(_s5_in_kernel, **dims),
        grid=(n_q, n_batch),
        in_specs=[ul_spec, uc_spec, pl.BlockSpec((None,) + w_c.shape[1:], lambda q, b: (q, 0, 0, 0, 0))],
        out_specs=[plane_spec] * 4,
        out_shape=[plane] * 4,
        scratch_shapes=[pltpu.VMEM((lane_q, 4 * st_q), BF16)],
        compiler_params=_cparams("arbitrary", "arbitrary"),
        name="s5_chunk_in",
    )(u, u, w_c)

    blk = pl.BlockSpec((nv, rows, 128), lambda j: (j, 0, 0))
    x_planes = pl.pallas_call(
        functools.partial(_s5_scan_kernel, n_tiles=rows // 8),
        grid=(n_q,),
        in_specs=[blk] * 4 + [pl.BlockSpec((4, nv, 1, 128), lambda j: (0, j, 0, 0))],
        out_specs=[blk] * 4,
        out_shape=[plane] * 4,
        compiler_params=_cparams("arbitrary"),
        name="s5_chunk_scan",
    )(*w_planes, decay)

    return pl.pallas_call(
        functools.partial(_s5_out_kernel, **dims),
        grid=(n_q, n_batch),
        in_specs=[ul_spec] + [plane_spec] * 4
                 + [pl.BlockSpec((None,) + k_c.shape[1:], lambda q, b: (q, 0, 0, 0)),
                    pl.BlockSpec((None,) + vt_c.shape[1:], lambda q, b: (q, 0, 0, 0, 0))],
        out_specs=pl.BlockSpec((l_lat, 128), lambda q, b: (b, q)),
        out_shape=jax.ShapeDtypeStruct((n_batch * l_lat, wd), F32),
        scratch_shapes=[pltpu.VMEM((lane_q, lane_q), BF16), pltpu.VMEM((4, lane_q, st_q), BF16)],
        compiler_params=_cparams("arbitrary", "arbitrary"),
        name="s5_chunk_out",
    )(u, *x_planes, k_c, vt_c)


def kernel(x, c, ctx, c_ctx, ada_w, ada_b, ln_mix_g, ln_mix_b, ln_ffn_g, ln_ffn_b, ev_w_in, ev_gate_w2,
           ev_gate_b, ev_rpb, ev_norm_g, ev_w_out, od_w_in, od_lam_re, od_lam_im, od_log_dt, od_b_re,
           od_b_im, od_c_re, od_c_im, od_d, od_w_glu, od_b_glu, od_w_out, router_w, router_b,
           moe_w_gate, moe_w_up, moe_w_down):
    n_batch, l_lat, d = x.shape
    l_ctx = ctx.shape[1]
    depth = ada_w.shape[0]
    assert depth == 2, "one even (NA + GLA) layer followed by one odd (S5) layer"
    alpha = (2.0 * depth) ** 0.25
    n_lat = n_batch * l_lat

    cvec = jnp.concatenate([c, c_ctx[None], jnp.zeros((8 - n_batch - 1, d), F32)], 0)
    mods = compute_mods(cvec, ada_w, ada_b)
    mods4 = mods.reshape(depth, 8, 1, N_MOD * d)
    rows = jnp.concatenate([x.reshape(n_lat, d), ctx.reshape(n_batch * l_ctx, d)], 0)
    router_wt = router_w.T.astype(F32)

    na_w = NA_HEADS * NA_DH
    wk = GLA_HEADS * GLA_DK
    wv = GLA_HEADS * GLA_DV
    ev_in = ev_w_in.shape[2]
    pad = (-ev_in) % 256
    w_in = jnp.pad(ev_w_in[0], ((0, 0), (0, pad))).astype(BF16)
    proj = mod_matmul(rows, mods4, 0, w_in, l_lat, n_batch, tm=512, tn=(ev_in + pad) // 2)
    a_lat, a_ctx = na_attention(proj, na_bias_table(ev_rpb[0], l_lat // GRID_W), n_batch, l_lat, l_ctx)
    col_lr = (3 * na_w + 2 * wk + 2 * wv) // 128
    g2 = jnp.zeros((2, 128, wk), F32)
    g2 = g2.at[0, 0:GLA_RANK].set(ev_gate_w2[0, 0]).at[1, GLA_RANK:2 * GLA_RANK].set(ev_gate_w2[0, 1])
    o_f, o_b = gla_bidir(proj, g2.astype(BF16), ev_gate_b[0].reshape(2, 1, wk), rope_tables(l_lat, l_ctx),
                         n_batch, l_lat, l_ctx,
                         col_q=3 * na_w // wk, col_k=(3 * na_w + wk) // wk,
                         col_v=(3 * na_w + 2 * wk) // wv, col_lr=col_lr)
    a_all = jnp.concatenate([a_lat, a_ctx], 0)
    x1, h2, logits_t = even_out(a_all, o_f, o_b, proj, (3 * na_w + 2 * wk + wv) // wv, rows, mods4, 0,
                                ev_norm_g[0], ev_w_out[0].astype(BF16), ln_mix_g[0], ln_mix_b[0],
                                router_wt, alpha, l_lat, n_batch)
    w_gate, w_up, w_down = moe_w_gate, moe_w_up, moe_w_down
    rows = moe_block(x1, h2, logits_t, router_b, w_gate, w_up, w_down, mods4, 0,
                     ln_ffn_g[0], ln_ffn_b[0], alpha, l_lat, n_batch)

    u = mod_matmul(rows, mods4, 1, od_w_in[0].astype(BF16), l_lat, n_batch)
    mats = s5_matrices(od_lam_re[0], od_lam_im[0], od_log_dt[0], od_b_re[0], od_b_im[0],
                       od_c_re[0], od_c_im[0], od_d[0])
    y5 = s5_bidir(u, mats, n_batch, l_lat, l_ctx)
    x1, h2, logits_t = odd_out(y5, rows, mods4, 1, od_w_glu[0].astype(BF16), od_b_glu[0],
                               od_w_out[0].astype(BF16), ln_mix_g[1], ln_mix_b[1], router_wt,
                               alpha, l_lat, n_batch)
    out = moe_block(x1, h2, logits_t, router_b, w_gate, w_up, w_down, mods4, 1,
                    ln_ffn_g[1], ln_ffn_b[1], alpha, l_lat, n_batch)
    return out.reshape(n_batch, l_lat, d)
```

```python
import functools
import math

import numpy as np
import jax
import jax.numpy as jnp
from jax import lax
from jax.experimental import pallas as pl
from jax.experimental.pallas import tpu as pltpu

F32 = jnp.float32
BF16 = jnp.bfloat16
HIGHEST = lax.Precision.HIGHEST

N_MOD = 6
LN_EPS = 1e-5
NORM_EPS = 1e-6

GRID_W = 64
NA_HEADS = 8
NA_DH = 128
NA_KR = 8
NA_KC = 16

GLA_HEADS = 4
GLA_DK = 128
GLA_DV = 256
GLA_RANK = 16
GLA_TAU = 16.0
GLA_CHUNK = 64
ROPE_BASE = 10000.0

S5_CH = 16
S5_P = 64
S5_TC = 16

N_EXPERTS = 16
N_GROUPS = 4
TOP_K = 2

VMEM_LIMIT = 56 * 1024 * 1024
NEG_BIG = -1e30


def _cparams(*sem):
    return pltpu.CompilerParams(dimension_semantics=sem, vmem_limit_bytes=VMEM_LIMIT)


def _dot(a, b, precision=None):
    return jnp.dot(a, b, preferred_element_type=F32, precision=precision)


def _dot_nt(a, b, precision=None):
    return lax.dot_general(a, b, (((1,), (1,)), ((), ())), preferred_element_type=F32, precision=precision)


def _dot_tn(a, b):
    return lax.dot_general(a, b, (((0,), (0,)), ((), ())), preferred_element_type=F32)


def _mods_kernel(s_ref, w_ref, b_ref, o_ref):
    s = s_ref[...]
    s = s * jax.nn.sigmoid(s)
    o_ref[0] = _dot(s, w_ref[0], HIGHEST) + b_ref[0]


def compute_mods(cvec, ada_w, ada_b, tn=1024):
    n_layer, d, n = ada_w.shape
    tn = math.gcd(tn, n)
    return pl.pallas_call(
        _mods_kernel,
        grid=(n_layer, n // tn),
        in_specs=[pl.BlockSpec((8, d), lambda l, j: (0, 0)),
                  pl.BlockSpec((1, d, tn), lambda l, j: (l, 0, j)),
                  pl.BlockSpec((1, 1, tn), lambda l, j: (l, 0, j))],
        out_specs=pl.BlockSpec((1, 8, tn), lambda l, j: (l, 0, j)),
        out_shape=jax.ShapeDtypeStruct((n_layer, 8, n), F32),
        compiler_params=_cparams("arbitrary", "arbitrary"),
        name="ada_mods",
    )(cvec, ada_w, ada_b.reshape(n_layer, 1, n))


def _mod_spec(d, layer, which, seg_of_tile):
    return pl.BlockSpec((None, None, 1, d), lambda i, *_: (layer, seg_of_tile(i), 0, which))


def _seg_fn(tm, seg_rows, n_batch):
    return lambda i: jnp.minimum((i * tm) // seg_rows, n_batch)


def _modmm_kernel(x_ref, s1_ref, s0_ref, w_ref, o_ref):
    h = x_ref[...] * (1.0 + s1_ref[...]) + s0_ref[...]
    o_ref[...] = _dot(h.astype(BF16), w_ref[...])


def mod_matmul(x, mods4, layer, w_bf16, seg_rows, n_batch, tm=256, tn=None):
    t, d = x.shape
    n = w_bf16.shape[1]
    tn = n if tn is None else tn
    seg = _seg_fn(tm, seg_rows, n_batch)
    return pl.pallas_call(
        _modmm_kernel,
        grid=(n // tn, t // tm),
        in_specs=[pl.BlockSpec((tm, d), lambda j, i: (i, 0)),
                  pl.BlockSpec((None, None, 1, d), lambda j, i: (layer, seg(i), 0, 1)),
                  pl.BlockSpec((None, None, 1, d), lambda j, i: (layer, seg(i), 0, 0)),
                  pl.BlockSpec((d, tn), lambda j, i: (0, j))],
        out_specs=pl.BlockSpec((tm, tn), lambda j, i: (i, j)),
        out_shape=jax.ShapeDtypeStruct((t, n), F32),
        compiler_params=_cparams("arbitrary", "arbitrary"),
        name="mod_matmul",
    )(x, mods4, mods4, w_bf16)


NA_RB = 4
NA_BAND = NA_RB + NA_KR - 1


def _na_row_start(r, rows):
    return min(max(r - NA_KR // 2, 0), rows - NA_KR)


def na_bias_table(rpb, rows):
    w = GRID_W
    q = np.arange(w)
    kc = np.arange(w)
    win0 = np.clip(q - NA_KC // 2, 0, w - NA_KC)
    ok = (kc[None, :] >= win0[:, None]) & (kc[None, :] < win0[:, None] + NA_KC)
    dc = np.clip(kc[None, :] - q[:, None] + NA_KC - 1, 0, 2 * NA_KC - 2)
    pick = ((dc[None] == np.arange(2 * NA_KC - 1)[:, None, None]) & ok[None]).astype(np.float32)
    colb = jnp.einsum("hrd,dqk->hrqk", rpb.astype(F32), jnp.asarray(pick), precision=HIGHEST)
    colb = jnp.where(ok[None, None], colb, NEG_BIG)
    neg = jnp.full((rpb.shape[0], w, w), NEG_BIG, F32)

    def block(r0):
        band0 = min(max(r0 - NA_KR // 2, 0), rows - NA_BAND)
        out = []
        for r in range(r0, r0 + NA_RB):
            rs = _na_row_start(r, rows)
            first = rs - r + NA_KR - 1
            cols = [neg] * (rs - band0) + [colb[:, first + j] for j in range(NA_KR)]
            cols += [neg] * (NA_BAND - len(cols))
            out.append(jnp.concatenate(cols, -1))
        return jnp.concatenate(out, 1)

    return jnp.stack([block(0), block(NA_RB), block(rows - NA_RB)], 1)


def _na_kernel(q_ref, k_ref, v_ref, qc_ref, kc_ref, vc_ref, bias_ref, o_ref, oc_ref, kbf, vbf, *, rows):
    w = GRID_W
    n_blk = rows // NA_RB
    scale = NA_DH ** -0.5
    kbf[...] = k_ref[...].astype(BF16)
    vbf[...] = v_ref[...].astype(BF16)
    kc = kc_ref[...].astype(BF16)
    vc = vc_ref[...].astype(BF16)

    def body(i, carry):
        r0 = i * NA_RB
        band0 = jnp.clip(r0 - NA_KR // 2, 0, rows - NA_BAND)
        variant = jnp.where(i == 0, 0, jnp.where(i == n_blk - 1, 2, 1))
        q0 = pl.multiple_of(r0 * w, NA_RB * w)
        k0 = pl.multiple_of(band0 * w, w)
        q = (q_ref[pl.ds(q0, NA_RB * w), :] * scale).astype(BF16)
        kb = kbf[pl.ds(k0, NA_BAND * w), :]
        vb = vbf[pl.ds(k0, NA_BAND * w), :]
        s_loc = _dot_nt(q, kb) + bias_ref[variant]
        s_ctx = _dot_nt(q, kc)
        m = jnp.maximum(jnp.max(s_loc, -1, keepdims=True), jnp.max(s_ctx, -1, keepdims=True))
        p_loc = jnp.exp(s_loc - m)
        p_ctx = jnp.exp(s_ctx - m)
        den = jnp.sum(p_loc, -1, keepdims=True) + jnp.sum(p_ctx, -1, keepdims=True)
        o = _dot(p_loc.astype(BF16), vb) + _dot(p_ctx.astype(BF16), vc)
        o_ref[pl.ds(q0, NA_RB * w), :] = o / den
        return carry

    lax.fori_loop(0, n_blk, body, 0)

    qc = (qc_ref[...] * scale).astype(BF16)
    s = _dot_nt(qc, kc)
    p = jnp.exp(s - jnp.max(s, -1, keepdims=True))
    oc_ref[...] = _dot(p.astype(BF16), vc) / jnp.sum(p, -1, keepdims=True)


def na_attention(proj, bias_tab, n_batch, l_lat, l_ctx):
    h = NA_HEADS
    dh = NA_DH
    rows = l_lat // GRID_W
    ctx0 = (n_batch * l_lat) // l_ctx
    return pl.pallas_call(
        functools.partial(_na_kernel, rows=rows),
        grid=(n_batch, h),
        in_specs=[pl.BlockSpec((l_lat, dh), lambda b, hh: (b, hh)),
                  pl.BlockSpec((l_lat, dh), lambda b, hh: (b, h + hh)),
                  pl.BlockSpec((l_lat, dh), lambda b, hh: (b, 2 * h + hh)),
                  pl.BlockSpec((l_ctx, dh), lambda b, hh: (ctx0 + b, hh)),
                  pl.BlockSpec((l_ctx, dh), lambda b, hh: (ctx0 + b, h + hh)),
                  pl.BlockSpec((l_ctx, dh), lambda b, hh: (ctx0 + b, 2 * h + hh)),
                  pl.BlockSpec((None,) + bias_tab.shape[1:], lambda b, hh: (hh, 0, 0, 0))],
        out_specs=[pl.BlockSpec((l_lat, dh), lambda b, hh: (b, hh)),
                   pl.BlockSpec((l_ctx, dh), lambda b, hh: (b, hh))],
        out_shape=[jax.ShapeDtypeStruct((n_batch * l_lat, h * dh), F32),
                   jax.ShapeDtypeStruct((n_batch * l_ctx, h * dh), F32)],
        scratch_shapes=[pltpu.VMEM((l_lat, dh), BF16), pltpu.VMEM((l_lat, dh), BF16)],
        compiler_params=_cparams("arbitrary", "arbitrary"),
        name="na_attention",
    )(proj, proj, proj, proj, proj, proj, bias_tab)


def rope_tables(l_lat, l_ctx):
    half = GLA_DK // 2
    nf = half // 2
    inv = ROPE_BASE ** (-np.arange(nf, dtype=np.float64) / nf)
    t = np.arange(l_lat)
    lane = np.arange(GLA_DK)
    pos = np.where(lane[None, :] < half, (t // GRID_W)[:, None], (t % GRID_W)[:, None]).astype(np.float64)
    ang = pos * inv[lane % nf][None, :]
    first = (lane % half) < nf
    cos = np.cos(ang)
    sin_a = np.where(first[None, :], -np.sin(ang), 0.0)
    sin_b = np.where(first[None, :], 0.0, np.sin(ang))
    one = np.ones((l_ctx, GLA_DK))
    zero = np.zeros((l_ctx, GLA_DK))
    cat = lambda ident, a: jnp.asarray(np.concatenate([ident, a, ident], 0), F32)
    return cat(one, cos), cat(zero, sin_a), cat(zero, sin_b)


def _gla_direction(q_ref, k_ref, v_ref, lr_ref, cos_ref, sa_ref, sb_ref, g2_ref, gb_ref, st_ref, o_ref, reverse):
    c = GLA_CHUNK
    nf = GLA_DK // 4
    gscale = GLA_DK ** -0.5
    row = lax.broadcasted_iota(jnp.int32, (c, c), 0)
    col = lax.broadcasted_iota(jnp.int32, (c, c), 1)
    seen = (col >= row) if reverse else (col <= row)
    tri = seen.astype(BF16)

    z = _dot(lr_ref[...].astype(BF16), g2_ref[...]) + gb_ref[...]
    g = (jnp.minimum(z, 0.0) - jnp.log1p(jnp.exp(-jnp.abs(z)))) * (1.0 / GLA_TAU)
    g_hi = g.astype(BF16)
    r1 = g - g_hi.astype(F32)
    g_mid = r1.astype(BF16)
    g_lo = (r1 - g_mid.astype(F32)).astype(BF16)
    b = _dot(tri, g_hi) + _dot(tri, g_mid) + _dot(tri, g_lo)
    b_last = b[0:1, :] if reverse else b[c - 1:c, :]
    cos = cos_ref[...]
    sa = sa_ref[...]
    sb = sb_ref[...]

    def rope(x):
        return x * cos + pltpu.roll(x, GLA_DK - nf, 1) * sa + pltpu.roll(x, nf, 1) * sb

    for h in range(GLA_HEADS):
        ks = slice(h * GLA_DK, (h + 1) * GLA_DK)
        vs = slice(h * GLA_DV, (h + 1) * GLA_DV)
        bh = b[:, ks]
        bl = b_last[:, ks]
        qh = rope(q_ref[:, ks]) * gscale
        kh = rope(k_ref[:, ks])
        q_dec = (qh * jnp.exp(bh)).astype(BF16)
        k_dec = (kh * jnp.exp(-bh)).astype(BF16)
        k_rem = (kh * jnp.exp(bl - bh)).astype(BF16)
        vh = v_ref[:, vs].astype(BF16)
        att = jnp.where(seen, _dot_nt(q_dec, k_dec), 0.0)
        st = st_ref[h]
        o_ref[:, vs] = _dot(att.astype(BF16), vh) + _dot_nt(q_dec, st.astype(BF16))
        st_ref[h] = st * jnp.exp(bl) + _dot_tn(vh, k_rem)


def _gla_kernel(qf, kf, vf, lrf, cf, saf, sbf, qb, kb, vb, lrb, cb, sab, sbb, g2_ref, gb_ref,
                of_ref, ob_ref, st_ref):
    @pl.when(pl.program_id(1) == 0)
    def _():
        st_ref[...] = jnp.zeros_like(st_ref)

    _gla_direction(qf, kf, vf, lrf, cf, saf, sbf, g2_ref.at[0], gb_ref.at[0], st_ref.at[0], of_ref, False)
    _gla_direction(qb, kb, vb, lrb, cb, sab, sbb, g2_ref.at[1], gb_ref.at[1], st_ref.at[1], ob_ref, True)


def gla_bidir(proj, g2, gb, tables, n_batch, l_lat, l_ctx, col_q, col_k, col_v, col_lr):
    c = GLA_CHUNK
    nc = l_ctx // c
    nl = l_lat // c
    nz = nl + 2 * nc
    steps = nl + nc
    wk = GLA_HEADS * GLA_DK
    wv = GLA_HEADS * GLA_DV
    t_rows = n_batch * (l_lat + l_ctx)

    def zblk(b, j):
        lat = b * nl + (j - nc)
        ctx = n_batch * nl + b * nc + jnp.where(j < nc, j, j - nc - nl)
        return jnp.where((j >= nc) & (j < nc + nl), lat, ctx)

    fwd = lambda b, i: zblk(b, i)
    bwd = lambda b, i: zblk(b, nz - 1 - i)

    def dir_specs(blk, tab):
        return [pl.BlockSpec((c, wk), lambda b, i: (blk(b, i), col_q)),
                pl.BlockSpec((c, wk), lambda b, i: (blk(b, i), col_k)),
                pl.BlockSpec((c, wv), lambda b, i: (blk(b, i), col_v)),
                pl.BlockSpec((c, 128), lambda b, i: (blk(b, i), col_lr)),
                pl.BlockSpec((c, GLA_DK), lambda b, i: (tab(i), 0)),
                pl.BlockSpec((c, GLA_DK), lambda b, i: (tab(i), 0)),
                pl.BlockSpec((c, GLA_DK), lambda b, i: (tab(i), 0))]

    cos, sa, sb = tables
    return pl.pallas_call(
        _gla_kernel,
        grid=(n_batch, steps),
        in_specs=(dir_specs(fwd, lambda i: i) + dir_specs(bwd, lambda i: nz - 1 - i)
                  + [pl.BlockSpec((2, 128, wk), lambda b, i: (0, 0, 0)),
                     pl.BlockSpec((2, 1, wk), lambda b, i: (0, 0, 0))]),
        out_specs=[pl.BlockSpec((c, wv), lambda b, i: (fwd(b, i), 0)),
                   pl.BlockSpec((c, wv), lambda b, i: (bwd(b, i), 0))],
        out_shape=[jax.ShapeDtypeStruct((t_rows, wv), F32), jax.ShapeDtypeStruct((t_rows, wv), F32)],
        scratch_shapes=[pltpu.VMEM((2, GLA_HEADS, GLA_DV, GLA_DK), F32)],
        compiler_params=_cparams("arbitrary", "arbitrary"),
        name="gla_bidir",
    )(proj, proj, proj, proj, cos, sa, sb, proj, proj, proj, proj, cos, sa, sb, g2, gb)


def _post_mix(out, x_ref, m2_ref, m3_ref, m4_ref, lg_ref, lb_ref, wr_ref, alpha, x1_ref, h2_ref, lt_ref):
    y = alpha * x_ref[...] + m2_ref[...] * out
    mu = jnp.mean(y, -1, keepdims=True)
    yc = y - mu
    var = jnp.mean(yc * yc, -1, keepdims=True)
    x1 = yc * lax.rsqrt(var + LN_EPS) * lg_ref[...] + lb_ref[...]
    h2 = x1 * (1.0 + m4_ref[...]) + m3_ref[...]
    x1_ref[...] = x1
    n_exp = lt_ref.shape[0]
    h2_hi = h2.astype(BF16)
    h2_lo = (h2 - h2_hi.astype(F32)).astype(BF16)
    h2_ref[...] = h2_hi
    wr = wr_ref[...]
    wr_hi = wr.astype(BF16)
    wr_lo = (wr - wr_hi.astype(F32)).astype(BF16)
    both = _dot_nt(jnp.concatenate([wr_hi, wr_lo], 0), h2_hi)
    lt_ref[...] = both[:n_exp] + both[n_exp:] + _dot_nt(wr_hi, h2_lo)


def _even_out_kernel(a_ref, of_ref, ob_ref, r_ref, x_ref, m2_ref, m3_ref, m4_ref, ng_ref, wo_ref,
                     lg_ref, lb_ref, wr_ref, x1_ref, h2_ref, lt_ref, *, alpha):
    o = of_ref[...] + ob_ref[...]
    r = r_ref[...]
    gate = r * jax.nn.sigmoid(r)
    acc = _dot(a_ref[...].astype(BF16), wo_ref[0:a_ref.shape[1], :])
    na = a_ref.shape[1]
    for h in range(GLA_HEADS):
        vs = slice(h * GLA_DV, (h + 1) * GLA_DV)
        oh = o[:, vs]
        nrm = oh * lax.rsqrt(jnp.mean(oh * oh, -1, keepdims=True) + NORM_EPS) * ng_ref[...]
        bh = (nrm * gate[:, vs]).astype(BF16)
        acc = acc + _dot(bh, wo_ref[na + h * GLA_DV:na + (h + 1) * GLA_DV, :])
    _post_mix(acc, x_ref, m2_ref, m3_ref, m4_ref, lg_ref, lb_ref, wr_ref, alpha, x1_ref, h2_ref, lt_ref)


def _post_specs(d, layer, seg, tm, n_exp):
    ins = [pl.BlockSpec((tm, d), lambda i: (i, 0)),
           _mod_spec(d, layer, 2, seg), _mod_spec(d, layer, 3, seg), _mod_spec(d, layer, 4, seg)]
    tail = [pl.BlockSpec((1, d), lambda i: (0, 0)), pl.BlockSpec((1, d), lambda i: (0, 0)),
            pl.BlockSpec((n_exp, d), lambda i: (0, 0))]
    outs =[pl.BlockSpec((tm, d), lambda i: (i, 0)), pl.BlockSpec((tm, d), lambda i: (i, 0)),
            pl.BlockSpec((n_exp, tm), lambda i: (0, i))]
    return ins, tail, outs


def _post_shapes(t, d, n_exp):
    return [jax.ShapeDtypeStruct((t, d), F32), jax.ShapeDtypeStruct((t, d), BF16),
            jax.ShapeDtypeStruct((n_exp, t), F32)]


def even_out(a, o_f, o_b, proj, col_r, x, mods4, layer, norm_g, w_out_bf16, ln_g, ln_b, router_wt,
             alpha, seg_rows, n_batch, tm=256):
    t, d = x.shape
    na = a.shape[1]
    wv = o_f.shape[1]
    n_exp = router_wt.shape[0]
    seg = _seg_fn(tm, seg_rows, n_batch)
    ins, tail, outs = _post_specs(d, layer, seg, tm, n_exp)
    return pl.pallas_call(
        functools.partial(_even_out_kernel, alpha=alpha),
        grid=(t // tm,),
        in_specs=([pl.BlockSpec((tm, na), lambda i: (i, 0)),
                   pl.BlockSpec((tm, wv), lambda i: (i, 0)),
                   pl.BlockSpec((tm, wv), lambda i: (i, 0)),
                   pl.BlockSpec((tm, wv), lambda i: (i, col_r))] + ins
                  + [pl.BlockSpec((1, GLA_DV), lambda i: (0, 0)),
                     pl.BlockSpec((na + wv, d), lambda i: (0, 0))] + tail),
        out_specs=outs,
        out_shape=_post_shapes(t, d, n_exp),
        compiler_params=_cparams("arbitrary"),
        name="even_out",
    )(a, o_f, o_b, proj, x, mods4, mods4, mods4, norm_g.reshape(1, -1), w_out_bf16,
      ln_g.reshape(1, -1), ln_b.reshape(1, -1), router_wt)


def _odd_out_kernel(y_ref, x_ref, m2_ref, m3_ref, m4_ref, wg_ref, bg_ref, wo_ref,
                    lg_ref, lb_ref, wr_ref, x1_ref, h2_ref, lt_ref, *, alpha):
    g = jax.nn.gelu(y_ref[...], approximate=True)
    z = _dot(g.astype(BF16), wg_ref[...]) + bg_ref[...]
    v = g * jax.nn.sigmoid(z)
    out = _dot(v.astype(BF16), wo_ref[...])
    _post_mix(out, x_ref, m2_ref, m3_ref, m4_ref, lg_ref, lb_ref, wr_ref, alpha, x1_ref, h2_ref, lt_ref)


def odd_out(y, x, mods4, layer, w_glu_bf16, b_glu, w_out_bf16, ln_g, ln_b, router_wt,
            alpha, seg_rows, n_batch, tm=256):
    t, w5 = y.shape
    d = x.shape[1]
    n_exp = router_wt.shape[0]
    seg = _seg_fn(tm, seg_rows, n_batch)
    ins, tail, outs = _post_specs(d, layer, seg, tm, n_exp)
    return pl.pallas_call(
        functools.partial(_odd_out_kernel, alpha=alpha),
        grid=(t // tm,),
        in_specs=([pl.BlockSpec((tm, w5), lambda i: (i, 0))] + ins
                  + [pl.BlockSpec((w5, w5), lambda i: (0, 0)),
                     pl.BlockSpec((1, w5), lambda i: (0, 0)),
                     pl.BlockSpec((w5, d), lambda i: (0, 0))] + tail),
        out_specs=outs,
        out_shape=_post_shapes(t, d, n_exp),
        compiler_params=_cparams("arbitrary"),
        name="odd_out",
    )(y, x, mods4, mods4, mods4, w_glu_bf16, b_glu.reshape(1, -1), w_out_bf16,
      ln_g.reshape(1, -1), ln_b.reshape(1, -1), router_wt)


def _route_kernel(lt_ref, rb_ref, idx_ref, w_ref):
    eg = N_EXPERTS // N_GROUPS
    logits = lt_ref[...]
    aff = jax.nn.sigmoid(logits)
    sel = aff + rb_ref[...]
    s = [sel[e:e + 1, :] for e in range(N_EXPERTS)]
    a = [aff[e:e + 1, :] for e in range(N_EXPERTS)]

    def top2_sum(v):
        hi1, lo1 = jnp.maximum(v[0], v[1]), jnp.minimum(v[0], v[1])
        hi2, lo2 = jnp.maximum(v[2], v[3]), jnp.minimum(v[2], v[3])
        return jnp.maximum(hi1, hi2) + jnp.maximum(jnp.minimum(hi1, hi2), jnp.maximum(lo1, lo2))

    best = top2_sum(s[0:eg])
    grp = jnp.zeros_like(best, dtype=jnp.int32)
    for g in range(1, N_GROUPS):
        sc = top2_sum(s[g * eg:(g + 1) * eg])
        better = sc > best
        best = jnp.where(better, sc, best)
        grp = jnp.where(better, g, grp)

    def pick(vals, j):
        out = vals[j]
        for g in range(1, N_GROUPS):
            out = jnp.where(grp == g, vals[g * eg + j], out)
        return out

    sv = [pick(s, j) for j in range(eg)]
    av = [pick(a, j) for j in range(eg)]

    def argmax_first(vals, exclude):
        bi = jnp.zeros_like(grp)
        bv = jnp.where(exclude == 0, -jnp.inf, vals[0]) if exclude is not None else vals[0]
        for j in range(1, eg):
            vj = jnp.where(exclude == j, -jnp.inf, vals[j]) if exclude is not None else vals[j]
            better = vj > bv
            bv = jnp.where(better, vj, bv)
            bi = jnp.where(better, j, bi)
        return bi

    i1 = argmax_first(sv, None)
    i2 = argmax_first(sv, i1)

    def take(vals, i):
        out = vals[0]
        for j in range(1, eg):
            out = jnp.where(i == j, vals[j], out)
        return out

    w1 = take(av, i1)
    w2 = take(av, i2)
    tot = w1 + w2
    idx_ref[0:1, :] = grp * eg + i1
    idx_ref[1:2, :] = grp * eg + i2
    w_ref[0:1, :] = w1 / tot
    w_ref[1:2, :] = w2 / tot


def route(logits_t, router_b, tile=1024):
    n_exp, t = logits_t.shape
    tile = math.gcd(tile, t)
    return pl.pallas_call(
        _route_kernel,
        grid=(t // tile,),
        in_specs=[pl.BlockSpec((n_exp, tile), lambda i: (0, i)),
                  pl.BlockSpec((n_exp, 1), lambda i: (0, 0))],
        out_specs=[pl.BlockSpec((TOP_K, tile), lambda i: (0, i)),
                   pl.BlockSpec((TOP_K, tile), lambda i: (0, i))],
        out_shape=[jax.ShapeDtypeStruct((TOP_K, t), jnp.int32), jax.ShapeDtypeStruct((TOP_K, t), F32)],
        compiler_params=_cparams("arbitrary"),
        name="moe_route",
    )(logits_t, router_b.reshape(n_exp, 1).astype(F32))


def moe_plan(idx, tm):
    t = idx.shape[1]
    n_pair = TOP_K * t
    n_tiles = (n_pair + N_EXPERTS * (tm - 1)) // tm
    e_flat = idx.reshape(-1)
    onehot = (e_flat[:, None] == jnp.arange(N_EXPERTS)[None, :]).astype(jnp.int32)
    running = jnp.cumsum(onehot, axis=0)
    counts = running[-1]
    rank = jnp.sum(onehot * running, 1) - 1
    tiles_per = (counts + tm - 1) // tm
    tile_end = jnp.cumsum(tiles_per)
    n_used = tile_end[-1]
    pstart = (tile_end - tiles_per) * tm
    pos = jnp.sum(onehot * pstart[None, :], 1) + rank
    tile_expert = jnp.minimum(jnp.sum((tile_end[None, :] <= jnp.arange(n_tiles)[:, None]).astype(jnp.int32), 1),
                              N_EXPERTS - 1).astype(jnp.int32)
    gidx = (jnp.arange(n_tiles * tm, dtype=jnp.int32) % t).at[pos].set(
        jnp.arange(n_pair, dtype=jnp.int32) % t, mode="promise_in_bounds", unique_indices=True)
    return gidx, tile_expert, n_used.reshape(1).astype(jnp.int32), pos.astype(jnp.int32)


def _expert_changed(te_ref):
    i = pl.program_id(0)
    return jnp.logical_or(i == 0, te_ref[i] != te_ref[jnp.maximum(i - 1, 0)])


def _ffn_up_kernel(te_ref, nu_ref, xs_ref, wg_ref, wu_ref, hid_ref, wg_bf, wu_bf):
    used = pl.program_id(0) < nu_ref[0]

    @pl.when(jnp.logical_and(used, _expert_changed(te_ref)))
    def _():
        wg_bf[...] = wg_ref[0].astype(BF16)
        wu_bf[...] = wu_ref[0].astype(BF16)

    @pl.when(used)
    def _():
        xs = xs_ref[...]
        g = _dot(xs, wg_bf[...])
        u = _dot(xs, wu_bf[...])
        hid_ref[...] = ((g * jax.nn.sigmoid(g)) * u).astype(BF16)

    @pl.when(jnp.logical_not(used))
    def _():
        hid_ref[...] = jnp.zeros_like(hid_ref)


def _ffn_down_kernel(te_ref, nu_ref, hid_ref, wd_ref, o_ref, wd_bf):
    used = pl.program_id(0) < nu_ref[0]

    @pl.when(jnp.logical_and(used, _expert_changed(te_ref)))
    def _():
        wd_bf[...] = wd_ref[0].astype(BF16)

    @pl.when(used)
    def _():
        o_ref[...] = _dot(hid_ref[...], wd_bf[...]).astype(o_ref.dtype)

    @pl.when(jnp.logical_not(used))
    def _():
        o_ref[...] = jnp.zeros_like(o_ref)


def grouped_ffn(xs, tile_expert, n_used, w_gate, w_up, w_down, layer, tm):
    p, d = xs.shape
    de = w_gate.shape[3]
    n_tiles = p // tm
    wmap = lambda i, te, nu: (layer, te[i], 0, 0)
    row_in = lambda i, te, nu: (jnp.minimum(i, nu[0] - 1), 0)
    row_out = lambda i, te, nu: (i, 0)
    hid = pl.pallas_call(
        _ffn_up_kernel,
        grid_spec=pltpu.PrefetchScalarGridSpec(
            num_scalar_prefetch=2,
            grid=(n_tiles,),
            in_specs=[pl.BlockSpec((tm, d), row_in),
                      pl.BlockSpec((None, 1, d, de), wmap),
                      pl.BlockSpec((None, 1, d, de), wmap)],
            out_specs=pl.BlockSpec((tm, de), row_out),
            scratch_shapes=[pltpu.VMEM((d, de), BF16), pltpu.VMEM((d, de), BF16)]),
        out_shape=jax.ShapeDtypeStruct((p, de), BF16),
        compiler_params=_cparams("arbitrary"),
        name="moe_ffn_up",
    )(tile_expert, n_used, xs, w_gate, w_up)
    return pl.pallas_call(
        _ffn_down_kernel,
        grid_spec=pltpu.PrefetchScalarGridSpec(
            num_scalar_prefetch=2,
            grid=(n_tiles,),
            in_specs=[pl.BlockSpec((tm, de), row_in),
                      pl.BlockSpec((None, 1, de, d), wmap)],
            out_specs=pl.BlockSpec((tm, d), row_out),
            scratch_shapes=[pltpu.VMEM((de, d), BF16)]),
        out_shape=jax.ShapeDtypeStruct((p, d), BF16),
        compiler_params=_cparams("arbitrary"),
        name="moe_ffn_down",
    )(tile_expert, n_used, hid, w_down)


def _final_kernel(x_ref, y0_ref, y1_ref, w_ref, m5_ref, lg_ref, lb_ref, o_ref, *, alpha):
    w = w_ref[...]
    y = w[:, 0:1] * y0_ref[...].astype(F32) + w[:, 1:2] * y1_ref[...].astype(F32)
    z = alpha * x_ref[...] + m5_ref[...] * y
    mu = jnp.mean(z, -1, keepdims=True)
    zc = z - mu
    var = jnp.mean(zc * zc, -1, keepdims=True)
    o_ref[...] = zc * lax.rsqrt(var + LN_EPS) * lg_ref[...] + lb_ref[...]


def final_norm(x1, yg, wts, mods4, layer, ln_g, ln_b, alpha, seg_rows, n_batch, tm=256):
    t, d = x1.shape
    seg = _seg_fn(tm, seg_rows, n_batch)
    row = pl.BlockSpec((tm, d), lambda i: (i, 0))
    vec = pl.BlockSpec((1, d), lambda i: (0, 0))
    return pl.pallas_call(
        functools.partial(_final_kernel, alpha=alpha),
        grid=(t // tm,),
        in_specs=[row, row, pl.BlockSpec((tm, d), lambda i: (i + t // tm, 0)),
                  pl.BlockSpec((tm, TOP_K), lambda i: (i, 0)),
                  _mod_spec(d, layer, 5, seg), vec, vec],
        out_specs=row,
        out_shape=jax.ShapeDtypeStruct((t, d), F32),
        compiler_params=_cparams("arbitrary"),
        name="final_norm",
    )(x1, yg, yg, wts, mods4, ln_g.reshape(1, -1), ln_b.reshape(1, -1))


def moe_block(x1, h2, logits_t, router_b, w_gate, w_up, w_down, mods4, layer, ln_g, ln_b,
              alpha, seg_rows, n_batch, tm=256):
    idx, wts = route(logits_t, router_b)
    gidx, tile_expert, n_used, pos = moe_plan(idx, tm)
    xs = h2.at[gidx].get(mode="promise_in_bounds")
    ys = grouped_ffn(xs, tile_expert, n_used, w_gate, w_up, w_down, layer, tm)
    yg = ys.at[pos].get(mode="promise_in_bounds")
    return final_norm(x1, yg, wts.T, mods4, layer, ln_g, ln_b, alpha, seg_rows, n_batch)


def s5_matrices(lam_re, lam_im, log_dt, b_re, b_im, c_re, c_im, d_skip):
    f32 = F32
    tc = S5_TC
    n_g, n_p = lam_re.shape[1], lam_re.shape[2]
    n_c = b_re.shape[-1]
    nb = 128 // n_c
    n_q = n_g // nb
    lr, li = lam_re.astype(f32), lam_im.astype(f32)
    dt = jnp.exp(log_dt.astype(f32))[..., None]

    def powers(jvals):
        j = jnp.asarray(np.asarray(jvals, np.float32))[:, None, None, None]
        mag = jnp.exp(lr * dt * j)
        return mag * jnp.cos(li * dt * j), mag * jnp.sin(li * dt * j)

    up = np.arange(tc)
    pw_re, pw_im = powers(np.arange(tc + 1))
    lb_re, lb_im = pw_re[1], pw_im[1]
    den = lr * lr + li * li
    fr = ((lb_re - 1.0) * lr + lb_im * li) / den
    fi = (lb_im * lr - (lb_re - 1.0) * li) / den
    br, bi = b_re.astype(f32), b_im.astype(f32)
    bb_re = fr[..., None] * br - fi[..., None] * bi
    bb_im = fr[..., None] * bi + fi[..., None] * br
    cr, ci = c_re.astype(f32), c_im.astype(f32)

    def times_b(p_re, p_im):
        return (p_re[..., None] * bb_re[None] - p_im[..., None] * bb_im[None],
                p_re[..., None] * bb_im[None] + p_im[..., None] * bb_re[None])

    e_re, e_im = times_b(pw_re, pw_im)
    kmat = jnp.sum(cr[None, :, :, :, :, None] * e_re[:, :, :, None, :, :]
                   - ci[None, :, :, :, :, None] * e_im[:, :, :, None, :, :], axis=4)
    def lag_slab(k_dir):
        return k_dir.reshape(tc, n_q, nb, n_c, n_c).transpose(1, 0, 4, 2, 3).reshape(n_q, tc, n_c, nb * n_c)
    skip = (jnp.eye(n_c, dtype=f32)[None, None, :, None, :]
            * d_skip.astype(f32).reshape(n_q, nb, n_c)[:, None, None, :, :]).reshape(n_q, 1, n_c, nb * n_c)
    k_c = jnp.concatenate([lag_slab(kmat[:tc, 0]), lag_slab(kmat[:tc, 1]), skip], 1)

    def w_slab(e):
        return e.reshape(tc, n_q, nb, n_p, n_c).transpose(1, 0, 4, 2, 3).reshape(n_q, tc, n_c, nb * n_p)
    ef_re, ef_im = times_b(*powers(tc - 1 - up))
    w_c = jnp.stack([w_slab(ef_re[:, 0]), w_slab(ef_im[:, 0]),
                     w_slab(e_re[:tc, 1]), w_slab(e_im[:tc, 1])], 2)

    def v_slabs(d, p_re, p_im):
        f_re = cr[d][None] * p_re[:, :, None, :] - ci[d][None] * p_im[:, :, None, :]
        f_im = cr[d][None] * p_im[:, :, None, :] + ci[d][None] * p_re[:, :, None, :]
        slab = lambda m: m.reshape(tc, n_q, nb, n_c, n_p).transpose(1, 0, 3, 2, 4).reshape(n_q, tc, n_c, nb * n_p)
        return slab(f_re), slab(-f_im)
    pb_re, pb_im = powers(tc - up)
    vt_c = jnp.stack(v_slabs(0, pw_re[1:, 0], pw_im[1:, 0]) + v_slabs(1, pb_re[:, 1], pb_im[:, 1]), 2)
    dec = lambda m: m.reshape(1, n_g * n_p // 128, 1, 128)
    decay = jnp.concatenate([dec(pw_re[tc, 0]), dec(pw_im[tc, 0]), dec(pw_re[tc, 1]), dec(pw_im[tc, 1])], 0)
    return k_c, w_c, vt_c, decay


def _s5_chunk_rows(ref, n):
    return jnp.concatenate([ref[pl.ds(s, n, stride=S5_TC), :] for s in range(S5_TC)], axis=1).astype(BF16)


def _s5_expand(slab, group_lanes):
    rows = 128
    tiled = jnp.concatenate([slab] * (rows // slab.shape[0]), axis=0)
    r = lax.broadcasted_iota(jnp.int32, tiled.shape, 0) // S5_CH
    l = lax.broadcasted_iota(jnp.int32, tiled.shape, 1) // group_lanes
    return jnp.where(r == l, tiled, 0.0).astype(BF16)


def _s5_in_kernel(ul_ref, uc_ref, wc_ref, fr_ref, fi_ref, br_ref, bi_ref, w_ref, *, n_lat, n_ctx, n_batch):
    b = pl.program_id(1)
    n_plane = wc_ref.shape[1]
    st = wc_ref.shape[3]

    @pl.when(b == 0)
    def _():
        for s in range(S5_TC):
            for i in range(n_plane):
                w_ref[s * 128:(s + 1) * 128, i * st:(i + 1) * st] = _s5_expand(wc_ref[s, i], S5_P)

    w_lat = _dot(_s5_chunk_rows(ul_ref, n_lat), w_ref[...])
    w_ctx = _dot(_s5_chunk_rows(uc_ref, n_ctx), w_ref[...])
    nv = fr_ref.shape[0]
    for i, ref in enumerate((fr_ref, fi_ref, br_ref, bi_ref)):
        for c in range(nv):
            lanes = slice((i * nv + c) * 128, (i * nv + c + 1) * 128)
            ref[c, pl.ds(b, n_ctx, stride=n_batch), :] = w_ctx[:, lanes]
            ref[c, pl.ds(n_ctx * n_batch + b, n_lat, stride=n_batch), :] = w_lat[:, lanes]
            ref[c, pl.ds((n_ctx + n_lat) * n_batch + b, n_ctx, stride=n_batch), :] = w_ctx[:, lanes]


def _s5_scan_kernel(wfr, wfi, wbr, wbi, dec_ref, xfr, xfi, xbr, xbi, *, n_tiles):
    nv = wfr.shape[0]
    low = lax.broadcasted_iota(jnp.int32, (nv, 8, 128), 1) < 4
    a_fr, a_fi, a_br, a_bi = dec_ref[0], dec_ref[1], dec_ref[2], dec_ref[3]

    def half_step(s_re, s_im, a_re, a_im, w_re, w_im):
        return a_re * s_re - a_im * s_im + w_re, a_re * s_im + a_im * s_re + w_im

    def one_dir(w_re_ref, w_im_ref, x_re_ref, x_im_ref, row0, s_re, s_im, a_re, a_im, first_low):
        first = low if first_low else jnp.logical_not(low)
        wt_re, wt_im = w_re_ref[:, pl.ds(row0, 8), :], w_im_ref[:, pl.ds(row0, 8), :]
        wr_re, wr_im = pltpu.roll(wt_re, 4, 1), pltpu.roll(wt_im, 4, 1)
        mid_re, mid_im = half_step(s_re, s_im, a_re, a_im, wr_re, wr_im)
        x_re_ref[:, pl.ds(row0, 8), :] = jnp.where(first, s_re, mid_re)
        x_im_ref[:, pl.ds(row0, 8), :] = jnp.where(first, s_im, mid_im)
        m_re = jnp.where(first, pltpu.roll(mid_re, 4, 1), mid_re)
        m_im = jnp.where(first, pltpu.roll(mid_im, 4, 1), mid_im)
        w2_re = jnp.where(first, wr_re, wt_re)
        w2_im = jnp.where(first, wr_im, wt_im)
        return half_step(m_re, m_im, a_re, a_im, w2_re, w2_im)

    def body(i, carry):
        f_re, f_im, b_re, b_im = carry
        rf = pl.multiple_of(i * 8, 8)
        rb = pl.multiple_of((n_tiles - 1 - i) * 8, 8)
        f_re, f_im = one_dir(wfr, wfi, xfr, xfi, rf, f_re, f_im, a_fr, a_fi, True)
        b_re, b_im = one_dir(wbr, wbi, xbr, xbi, rb, b_re, b_im, a_br, a_bi, False)
        return f_re, f_im, b_re, b_im

    z = jnp.zeros((nv, 8, 128), F32)
    lax.fori_loop(0, n_tiles, body, (z, z, z, z))


def _s5_out_kernel(ul_ref, fr_ref, fi_ref, br_ref, bi_ref, kc_ref, vc_ref, y_ref, mt_ref, vt_ref,
                   *, n_lat, n_ctx, n_batch):
    b = pl.program_id(1)
    tc = S5_TC

    @pl.when(b == 0)
    def _():
        lag = [_s5_expand(kc_ref[j], S5_CH) for j in range(2 * tc)]
        diag = _s5_expand(kc_ref[0] + kc_ref[tc] + kc_ref[2 * tc], S5_CH)
        for s in range(tc):
            for t in range(tc):
                blk = diag if s == t else (lag[t - s] if t > s else lag[tc + s - t])
                mt_ref[s * 128:(s + 1) * 128, t * 128:(t + 1) * 128] = blk
        for t in range(tc):
            for i in range(vt_ref.shape[0]):
                vt_ref[i, t * 128:(t + 1) * 128, :] = _s5_expand(vc_ref[t, i], S5_P)

    y = _dot(_s5_chunk_rows(ul_ref, n_lat), mt_ref[...])
    row0 = n_ctx * n_batch + b
    for i, ref in enumerate((fr_ref, fi_ref, br_ref, bi_ref)):
        xs = jnp.concatenate([ref[c, pl.ds(row0, n_lat, stride=n_batch), :] for c in range(ref.shape[0])], 1)
        y = y + _dot_nt(xs.astype(BF16), vt_ref[i])
    for s in range(S5_TC):
        y_ref[pl.ds(s, n_lat, stride=S5_TC), :] = y[:, s * 128:(s + 1) * 128]


def s5_bidir(u, mats, n_batch, l_lat, l_ctx):
    assert n_batch == 4, "the chunk scan packs two chunks of 4 batch rows per 8-sublane tile"
    k_c, w_c, vt_c, decay = mats
    tc = S5_TC
    wd = u.shape[1]
    n_q = wd // 128
    lane_q = tc * 128
    st_q = (128 // S5_CH) * S5_P
    n_lat, n_ctx = l_lat // tc, l_ctx // tc
    nk = n_lat + 2 * n_ctx
    assert nk % 2 == 0
    rows = nk * n_batch
    ctx0 = (n_batch * l_lat) // l_ctx
    dims = dict(n_lat=n_lat, n_ctx=n_ctx, n_batch=n_batch)

    nv = st_q // 128
    plane = jax.ShapeDtypeStruct((n_q * nv, rows, 128), F32)
    plane_spec = pl.BlockSpec((nv, rows, 128), lambda q, b: (q, 0, 0))
    ul_spec = pl.BlockSpec((l_lat, 128), lambda q, b: (b, q))
    uc_spec = pl.BlockSpec((l_ctx, 128), lambda q, b: (ctx0 + b, q))
    w_planes = pl.pallas_call(
        functools.partial(_s5_in_kernel, **dims),
        grid=(n_q, n_batch),
        in_specs=[ul_spec, uc_spec, pl.BlockSpec((None,) + w_c.shape[1:], lambda q, b: (q, 0, 0, 0, 0))],
        out_specs=[plane_spec] * 4,
        out_shape=[plane] * 4,
        scratch_shapes=[pltpu.VMEM((lane_q, 4 * st_q), BF16)],
        compiler_params=_cparams("arbitrary", "arbitrary"),
        name="s5_chunk_in",
    )(u, u, w_c)

    blk = pl.BlockSpec((nv, rows, 128), lambda j: (j, 0, 0))
    x_planes = pl.pallas_call(
        functools.partial(_s5_scan_kernel, n_tiles=rows // 8),
        grid=(n_q,),
        in_specs=[blk] * 4 + [pl.BlockSpec((4, nv, 1, 128), lambda j: (0, j, 0, 0))],
        out_specs=[blk] * 4,
        out_shape=[plane] * 4,
        compiler_params=_cparams("arbitrary"),
        name="s5_chunk_scan",
    )(*w_planes, decay)

    return pl.pallas_call(
        functools.partial(_s5_out_kernel, **dims),
        grid=(n_q, n_batch),
        in_specs=[ul_spec] + [plane_spec] * 4
                 + [pl.BlockSpec((None,) + k_c.shape[1:], lambda q, b: (q, 0, 0, 0)),
                    pl.BlockSpec((None,) + vt_c.shape[1:], lambda q, b: (q, 0, 0, 0, 0))],
        out_specs=pl.BlockSpec((l_lat, 128), lambda q, b: (b, q)),
        out_shape=jax.ShapeDtypeStruct((n_batch * l_lat, wd), F32),
        scratch_shapes=[pltpu.VMEM((lane_q, lane_q), BF16), pltpu.VMEM((4, lane_q, st_q), BF16)],
        compiler_params=_cparams("arbitrary", "arbitrary"),
        name="s5_chunk_out",
    )(u, *x_planes, k_c, vt_c)


def kernel(x, c, ctx, c_ctx, ada_w, ada_b, ln_mix_g, ln_mix_b, ln_ffn_g, ln_ffn_b, ev_w_in, ev_gate_w2,
           ev_gate_b, ev_rpb, ev_norm_g, ev_w_out, od_w_in, od_lam_re, od_lam_im, od_log_dt, od_b_re,
           od_b_im, od_c_re, od_c_im, od_d, od_w_glu, od_b_glu, od_w_out, router_w, router_b,
           moe_w_gate, moe_w_up, moe_w_down):
    n_batch, l_lat, d = x.shape
    l_ctx = ctx.shape[1]
    depth = ada_w.shape[0]
    assert depth == 2, "one even (NA + GLA) layer followed by one odd (S5) layer"
    alpha = (2.0 * depth) ** 0.25
    n_lat = n_batch * l_lat

    cvec = jnp.concatenate([c, c_ctx[None], jnp.zeros((8 - n_batch - 1, d), F32)], 0)
    mods = compute_mods(cvec, ada_w, ada_b)
    mods4 = mods.reshape(depth, 8, 1, N_MOD * d)
    rows = jnp.concatenate([x.reshape(n_lat, d), ctx.reshape(n_batch * l_ctx, d)], 0)
    router_wt = router_w.T.astype(F32)

    na_w = NA_HEADS * NA_DH
    wk = GLA_HEADS * GLA_DK
    wv = GLA_HEADS * GLA_DV
    ev_in = ev_w_in.shape[2]
    pad = (-ev_in) % 256
    w_in = jnp.pad(ev_w_in[0], ((0, 0), (0, pad))).astype(BF16)
    proj = mod_matmul(rows, mods4, 0, w_in, l_lat, n_batch, tm=512, tn=(ev_in + pad) // 2)
    a_lat, a_ctx = na_attention(proj, na_bias_table(ev_rpb[0], l_lat // GRID_W), n_batch, l_lat, l_ctx)
    col_lr = (3 * na_w + 2 * wk + 2 * wv) // 128
    g2 = jnp.zeros((2, 128, wk), F32)
    g2 = g2.at[0, 0:GLA_RANK].set(ev_gate_w2[0, 0]).at[1, GLA_RANK:2 * GLA_RANK].set(ev_gate_w2[0, 1])
    o_f, o_b = gla_bidir(proj, g2.astype(BF16), ev_gate_b[0].reshape(2, 1, wk), rope_tables(l_lat, l_ctx),
                         n_batch, l_lat, l_ctx,
                         col_q=3 * na_w // wk, col_k=(3 * na_w + wk) // wk,
                         col_v=(3 * na_w + 2 * wk) // wv, col_lr=col_lr)
    a_all = jnp.concatenate([a_lat, a_ctx], 0)
    x1, h2, logits_t = even_out(a_all, o_f, o_b, proj, (3 * na_w + 2 * wk + wv) // wv, rows, mods4, 0,
                                ev_norm_g[0], ev_w_out[0].astype(BF16), ln_mix_g[0], ln_mix_b[0],
                                router_wt, alpha, l_lat, n_batch)
    w_gate, w_up, w_down = moe_w_gate, moe_w_up, moe_w_down
    rows = moe_block(x1, h2, logits_t, router_b, w_gate, w_up, w_down, mods4, 0,
                     ln_ffn_g[0], ln_ffn_b[0], alpha, l_lat, n_batch)

    u = mod_matmul(rows, mods4, 1, od_w_in[0].astype(BF16), l_lat, n_batch)
    mats = s5_matrices(od_lam_re[0], od_lam_im[0], od_log_dt[0], od_b_re[0], od_b_im[0],
                       od_c_re[0], od_c_im[0], od_d[0])
    y5 = s5_bidir(u, mats, n_batch, l_lat, l_ctx)
    x1, h2, logits_t = odd_out(y5, rows, mods4, 1, od_w_glu[0].astype(BF16), od_b_glu[0],
                               od_w_out[0].astype(BF16), ln_mix_g[1], ln_mix_b[1], router_wt,
                               alpha, l_lat, n_batch)
    out = moe_block(x1, h2, logits_t, router_b, w_gate, w_up, w_down, mods4, 1,
                    ln_ffn_g[1], ln_ffn_b[1], alpha, l_lat, n_batch)
    return out.reshape(n_batch, l_lat, d)
```

```python
import functools
import math

import numpy as np
import jax
import jax.numpy as jnp
from jax import lax
from jax.experimental import pallas as pl
from jax.experimental.pallas import tpu as pltpu

F32 = jnp.float32
BF16 = jnp.bfloat16
HIGHEST = lax.Precision.HIGHEST

N_MOD = 6
LN_EPS = 1e-5
NORM_EPS = 1e-6

GRID_W = 64
NA_HEADS = 8
NA_DH = 128
NA_KR = 8
NA_KC = 16

GLA_HEADS = 4
GLA_DK = 128
GLA_DV = 256
GLA_RANK = 16
GLA_TAU = 16.0
GLA_CHUNK = 64
ROPE_BASE = 10000.0

S5_CH = 16
S5_P = 64
S5_TC = 16

N_EXPERTS = 16
N_GROUPS = 4
TOP_K = 2

VMEM_LIMIT = 56 * 1024 * 1024
NEG_BIG = -1e30


def _cparams(*sem):
    return pltpu.CompilerParams(dimension_semantics=sem, vmem_limit_bytes=VMEM_LIMIT)


def _dot(a, b, precision=None):
    return jnp.dot(a, b, preferred_element_type=F32, precision=precision)


def _dot_nt(a, b, precision=None):
    return lax.dot_general(a, b, (((1,), (1,)), ((), ())), preferred_element_type=F32, precision=precision)


def _dot_tn(a, b):
    return lax.dot_general(a, b, (((0,), (0,)), ((), ())), preferred_element_type=F32)


def _mods_kernel(s_ref, w_ref, b_ref, o_ref):
    s = s_ref[...]
    s = s * jax.nn.sigmoid(s)
    o_ref[0] = _dot(s, w_ref[0], HIGHEST) + b_ref[0]


def compute_mods(cvec, ada_w, ada_b, tn=1024):
    n_layer, d, n = ada_w.shape
    tn = math.gcd(tn, n)
    return pl.pallas_call(
        _mods_kernel,
        grid=(n_layer, n // tn),
        in_specs=[pl.BlockSpec((8, d), lambda l, j: (0, 0)),
                  pl.BlockSpec((1, d, tn), lambda l, j: (l, 0, j)),
                  pl.BlockSpec((1, 1, tn), lambda l, j: (l, 0, j))],
        out_specs=pl.BlockSpec((1, 8, tn), lambda l, j: (l, 0, j)),
        out_shape=jax.ShapeDtypeStruct((n_layer, 8, n), F32),
        compiler_params=_cparams("arbitrary", "arbitrary"),
        name="ada_mods",
    )(cvec, ada_w, ada_b.reshape(n_layer, 1, n))


def _mod_spec(d, layer, which, seg_of_tile):
    return pl.BlockSpec((None, None, 1, d), lambda i, *_: (layer, seg_of_tile(i), 0, which))


def _seg_fn(tm, seg_rows, n_batch):
    return lambda i: jnp.minimum((i * tm) // seg_rows, n_batch)


def _two_source_specs(lat, ctx, tm, tile_of):
    n_lat_tiles = lat.shape[0] // tm
    d = lat.shape[1]
    return [pl.BlockSpec((tm, d), lambda *g: (jnp.minimum(tile_of(*g), n_lat_tiles - 1), 0)),
            pl.BlockSpec((tm, d), lambda *g: (jnp.maximum(tile_of(*g) - n_lat_tiles, 0), 0))]


def _pick_rows(lat_ref, ctx_ref, tile, n_lat_tiles):
    return jnp.where(tile < n_lat_tiles, lat_ref[...], ctx_ref[...])


def _modmm_kernel(xl_ref, xc_ref, s1_ref, s0_ref, w_ref, o_ref, *, n_lat_tiles):
    x = _pick_rows(xl_ref, xc_ref, pl.program_id(1), n_lat_tiles)
    h = x * (1.0 + s1_ref[...]) + s0_ref[...]
    o_ref[...] = _dot(h.astype(BF16), w_ref[...])


def mod_matmul(x_lat, x_ctx, mods4, layer, w_bf16, seg_rows, n_batch, tm=256, tn=None):
    if x_ctx is None:
        t, x_ctx = x_lat.shape[0], x_lat
    else:
        t = x_lat.shape[0] + x_ctx.shape[0]
    d = x_lat.shape[1]
    n = w_bf16.shape[1]
    tn = n if tn is None else tn
    seg = _seg_fn(tm, seg_rows, n_batch)
    return pl.pallas_call(
        functools.partial(_modmm_kernel, n_lat_tiles=x_lat.shape[0] // tm),
        grid=(n // tn, t // tm),
        in_specs=_two_source_specs(x_lat, x_ctx, tm, lambda j, i: i) + [
            pl.BlockSpec((None, None, 1, d), lambda j, i: (layer, seg(i), 0, 1)),
            pl.BlockSpec((None, None, 1, d), lambda j, i: (layer, seg(i), 0, 0)),
            pl.BlockSpec((d, tn), lambda j, i: (0, j))],
        out_specs=pl.BlockSpec((tm, tn), lambda j, i: (i, j)),
        out_shape=jax.ShapeDtypeStruct((t, n), F32),
        compiler_params=_cparams("arbitrary", "arbitrary"),
        name="mod_matmul",
    )(x_lat, x_ctx, mods4, mods4, w_bf16)


NA_RB = 4
NA_BAND = NA_RB + NA_KR - 1


def _na_row_start(r, rows):
    return min(max(r - NA_KR // 2, 0), rows - NA_KR)


def na_bias_table(rpb, rows):
    w = GRID_W
    q = np.arange(w)
    kc = np.arange(w)
    win0 = np.clip(q - NA_KC // 2, 0, w - NA_KC)
    ok = (kc[None, :] >= win0[:, None]) & (kc[None, :] < win0[:, None] + NA_KC)
    dc = np.clip(kc[None, :] - q[:, None] + NA_KC - 1, 0, 2 * NA_KC - 2)
    pick = ((dc[None] == np.arange(2 * NA_KC - 1)[:, None, None]) & ok[None]).astype(np.float32)
    colb = jnp.einsum("hrd,dqk->hrqk", rpb.astype(F32), jnp.asarray(pick), precision=HIGHEST)
    colb = jnp.where(ok[None, None], colb, NEG_BIG)
    neg = jnp.full((rpb.shape[0], w, w), NEG_BIG, F32)

    def block(r0):
        band0 = min(max(r0 - NA_KR // 2, 0), rows - NA_BAND)
        out = []
        for r in range(r0, r0 + NA_RB):
            rs = _na_row_start(r, rows)
            first = rs - r + NA_KR - 1
            cols = [neg] * (rs - band0) + [colb[:, first + j] for j in range(NA_KR)]
            cols += [neg] * (NA_BAND - len(cols))
            out.append(jnp.concatenate(cols, -1))
        return jnp.concatenate(out, 1)

    return jnp.stack([block(0), block(NA_RB), block(rows - NA_RB)], 1)


def _na_kernel(q_ref, k_ref, v_ref, qc_ref, kc_ref, vc_ref, bias_ref, o_ref, oc_ref, kbf, vbf, *, rows):
    w = GRID_W
    n_blk = rows // NA_RB
    scale = NA_DH ** -0.5
    kbf[...] = k_ref[...].astype(BF16)
    vbf[...] = v_ref[...].astype(BF16)
    kc = kc_ref[...].astype(BF16)
    vc = vc_ref[...].astype(BF16)

    def body(i, carry):
        r0 = i * NA_RB
        band0 = jnp.clip(r0 - NA_KR // 2, 0, rows - NA_BAND)
        variant = jnp.where(i == 0, 0, jnp.where(i == n_blk - 1, 2, 1))
        q0 = pl.multiple_of(r0 * w, NA_RB * w)
        k0 = pl.multiple_of(band0 * w, w)
        q = (q_ref[pl.ds(q0, NA_RB * w), :] * scale).astype(BF16)
        kb = kbf[pl.ds(k0, NA_BAND * w), :]
        vb = vbf[pl.ds(k0, NA_BAND * w), :]
        s_loc = _dot_nt(q, kb) + bias_ref[variant]
        s_ctx = _dot_nt(q, kc)
        m = jnp.maximum(jnp.max(s_loc, -1, keepdims=True), jnp.max(s_ctx, -1, keepdims=True))
        p_loc = jnp.exp(s_loc - m)
        p_ctx = jnp.exp(s_ctx - m)
        den = jnp.sum(p_loc, -1, keepdims=True) + jnp.sum(p_ctx, -1, keepdims=True)
        o = _dot(p_loc.astype(BF16), vb) + _dot(p_ctx.astype(BF16), vc)
        o_ref[pl.ds(q0, NA_RB * w), :] = o / den
        return carry

    lax.fori_loop(0, n_blk, body, 0, unroll=2)

    qc = (qc_ref[...] * scale).astype(BF16)
    s = _dot_nt(qc, kc)
    p = jnp.exp(s - jnp.max(s, -1, keepdims=True))
    oc_ref[...] = _dot(p.astype(BF16), vc) / jnp.sum(p, -1, keepdims=True)


def na_attention(proj, bias_tab, n_batch, l_lat, l_ctx):
    h = NA_HEADS
    dh = NA_DH
    rows = l_lat // GRID_W
    ctx0 = (n_batch * l_lat) // l_ctx
    return pl.pallas_call(
        functools.partial(_na_kernel, rows=rows),
        grid=(n_batch, h),
        in_specs=[pl.BlockSpec((l_lat, dh), lambda b, hh: (b, hh)),
                  pl.BlockSpec((l_lat, dh), lambda b, hh: (b, h + hh)),
                  pl.BlockSpec((l_lat, dh), lambda b, hh: (b, 2 * h + hh)),
                  pl.BlockSpec((l_ctx, dh), lambda b, hh: (ctx0 + b, hh)),
                  pl.BlockSpec((l_ctx, dh), lambda b, hh: (ctx0 + b, h + hh)),
                  pl.BlockSpec((l_ctx, dh), lambda b, hh: (ctx0 + b, 2 * h + hh)),
                  pl.BlockSpec((None,) + bias_tab.shape[1:], lambda b, hh: (hh, 0, 0, 0))],
        out_specs=[pl.BlockSpec((l_lat, dh), lambda b, hh: (b, hh)),
                   pl.BlockSpec((l_ctx, dh), lambda b, hh: (b, hh))],
        out_shape=[jax.ShapeDtypeStruct((n_batch * l_lat, h * dh), F32),
                   jax.ShapeDtypeStruct((n_batch * l_ctx, h * dh), F32)],
        scratch_shapes=[pltpu.VMEM((l_lat, dh), BF16), pltpu.VMEM((l_lat, dh), BF16)],
        compiler_params=_cparams("arbitrary", "arbitrary"),
        name="na_attention",
    )(proj, proj, proj, proj, proj, proj, bias_tab)


def rope_tables(l_lat, l_ctx):
    half = GLA_DK // 2
    nf = half // 2
    inv = ROPE_BASE ** (-np.arange(nf, dtype=np.float64) / nf)
    t = np.arange(l_lat)
    lane = np.arange(GLA_DK)
    pos = np.where(lane[None, :] < half, (t // GRID_W)[:, None], (t % GRID_W)[:, None]).astype(np.float64)
    ang = pos * inv[lane % nf][None, :]
    first = (lane % half) < nf
    cos = np.cos(ang)
    sin_a = np.where(first[None, :], -np.sin(ang), 0.0)
    sin_b = np.where(first[None, :], 0.0, np.sin(ang))
    one = np.ones((l_ctx, GLA_DK))
    zero = np.zeros((l_ctx, GLA_DK))
    cat = lambda ident, a: jnp.asarray(np.concatenate([ident, a, ident], 0), F32)
    return cat(one, cos), cat(zero, sin_a), cat(zero, sin_b)


def _gla_direction(q_ref, k_ref, v_ref, lr_ref, cos_ref, sa_ref, sb_ref, g2_ref, gb_ref, st_ref, o_ref, reverse):
    c = GLA_CHUNK
    nf = GLA_DK // 4
    gscale = GLA_DK ** -0.5
    row = lax.broadcasted_iota(jnp.int32, (c, c), 0)
    col = lax.broadcasted_iota(jnp.int32, (c, c), 1)
    seen = (col >= row) if reverse else (col <= row)
    tri = seen.astype(BF16)

    z = _dot(lr_ref[...].astype(BF16), g2_ref[...]) + gb_ref[...]
    g = (jnp.minimum(z, 0.0) - jnp.log1p(jnp.exp(-jnp.abs(z)))) * (1.0 / GLA_TAU)
    g_hi = g.astype(BF16)
    r1 = g - g_hi.astype(F32)
    g_mid = r1.astype(BF16)
    g_lo = (r1 - g_mid.astype(F32)).astype(BF16)
    b = _dot(tri, g_hi) + _dot(tri, g_mid) + _dot(tri, g_lo)
    b_last = b[0:1, :] if reverse else b[c - 1:c, :]
    cos = cos_ref[...]
    sa = sa_ref[...]
    sb = sb_ref[...]

    def rope(x):
        return x * cos + pltpu.roll(x, GLA_DK - nf, 1) * sa + pltpu.roll(x, nf, 1) * sb

    for h in range(GLA_HEADS):
        ks = slice(h * GLA_DK, (h + 1) * GLA_DK)
        vs = slice(h * GLA_DV, (h + 1) * GLA_DV)
        bh = b[:, ks]
        bl = b_last[:, ks]
        qh = rope(q_ref[:, ks]) * gscale
        kh = rope(k_ref[:, ks])
        q_dec = (qh * jnp.exp(bh)).astype(BF16)
        k_dec = (kh * jnp.exp(-bh)).astype(BF16)
        k_rem = (kh * jnp.exp(bl - bh)).astype(BF16)
        vh = v_ref[:, vs].astype(BF16)
        att = jnp.where(seen, _dot_nt(q_dec, k_dec), 0.0)
        st = st_ref[h]
        o_ref[:, vs] = _dot(att.astype(BF16), vh) + _dot_nt(q_dec, st.astype(BF16))
        st_ref[h] = st * jnp.exp(bl) + _dot_tn(vh, k_rem)


def _gla_kernel(qf, kf, vf, lrf, cf, saf, sbf, qb, kb, vb, lrb, cb, sab, sbb, g2_ref, gb_ref,
                of_ref, ob_ref, st_ref):
    @pl.when(pl.program_id(1) == 0)
    def _():
        st_ref[...] = jnp.zeros_like(st_ref)

    _gla_direction(qf, kf, vf, lrf, cf, saf, sbf, g2_ref.at[0], gb_ref.at[0], st_ref.at[0], of_ref, False)
    _gla_direction(qb, kb, vb, lrb, cb, sab, sbb, g2_ref.at[1], gb_ref.at[1], st_ref.at[1], ob_ref, True)


def gla_bidir(proj, g2, gb, tables, n_batch, l_lat, l_ctx, col_q, col_k, col_v, col_lr):
    c = GLA_CHUNK
    nc = l_ctx // c
    nl = l_lat // c
    nz = nl + 2 * nc
    steps = nl + nc
    wk = GLA_HEADS * GLA_DK
    wv = GLA_HEADS * GLA_DV
    t_rows = n_batch * (l_lat + l_ctx)

    def zblk(b, j):
        lat = b * nl + (j - nc)
        ctx = n_batch * nl + b * nc + jnp.where(j < nc, j, j - nc - nl)
        return jnp.where((j >= nc) & (j < nc + nl), lat, ctx)

    fwd = lambda b, i: zblk(b, i)
    bwd = lambda b, i: zblk(b, nz - 1 - i)

    def dir_specs(blk, tab):
        return [pl.BlockSpec((c, wk), lambda b, i: (blk(b, i), col_q)),
                pl.BlockSpec((c, wk), lambda b, i: (blk(b, i), col_k)),
                pl.BlockSpec((c, wv), lambda b, i: (blk(b, i), col_v)),
                pl.BlockSpec((c, 128), lambda b, i: (blk(b, i), col_lr)),
                pl.BlockSpec((c, GLA_DK), lambda b, i: (tab(i), 0)),
                pl.BlockSpec((c, GLA_DK), lambda b, i: (tab(i), 0)),
                pl.BlockSpec((c, GLA_DK), lambda b, i: (tab(i), 0))]

    cos, sa, sb = tables
    return pl.pallas_call(
        _gla_kernel,
        grid=(n_batch, steps),
        in_specs=(dir_specs(fwd, lambda i: i) + dir_specs(bwd, lambda i: nz - 1 - i)
                  + [pl.BlockSpec((2, 128, wk), lambda b, i: (0, 0, 0)),
                     pl.BlockSpec((2, 1, wk), lambda b, i: (0, 0, 0))]),
        out_specs=[pl.BlockSpec((c, wv), lambda b, i: (fwd(b, i), 0)),
                   pl.BlockSpec((c, wv), lambda b, i: (bwd(b, i), 0))],
        out_shape=[jax.ShapeDtypeStruct((t_rows, wv), F32), jax.ShapeDtypeStruct((t_rows, wv), F32)],
        scratch_shapes=[pltpu.VMEM((2, GLA_HEADS, GLA_DV, GLA_DK), F32)],
        compiler_params=_cparams("arbitrary", "arbitrary"),
        name="gla_bidir",
    )(proj, proj, proj, proj, cos, sa, sb, proj, proj, proj, proj, cos, sa, sb, g2, gb)


def _post_mix(out, x, m2_ref, m3_ref, m4_ref, lg_ref, lb_ref, wr_ref, alpha, x1_ref, h2_ref, lt_ref):
    y = alpha * x + m2_ref[...] * out
    mu = jnp.mean(y, -1, keepdims=True)
    yc = y - mu
    var = jnp.mean(yc * yc, -1, keepdims=True)
    x1 = yc * lax.rsqrt(var + LN_EPS) * lg_ref[...] + lb_ref[...]
    h2 = x1 * (1.0 + m4_ref[...]) + m3_ref[...]
    x1_ref[...] = x1
    n_exp = lt_ref.shape[0]
    h2_hi = h2.astype(BF16)
    h2_lo = (h2 - h2_hi.astype(F32)).astype(BF16)
    h2_ref[...] = h2_hi
    wr = wr_ref[...]
    wr_hi = wr.astype(BF16)
    wr_lo = (wr - wr_hi.astype(F32)).astype(BF16)
    both = _dot_nt(jnp.concatenate([wr_hi, wr_lo], 0), h2_hi)
    lt_ref[...] = both[:n_exp] + both[n_exp:] + _dot_nt(wr_hi, h2_lo)


def _even_out_kernel(al_ref, ac_ref, of_ref, ob_ref, r_ref, xl_ref, xc_ref, m2_ref, m3_ref, m4_ref, ng_ref,
                     wo_ref, lg_ref, lb_ref, wr_ref, x1_ref, h2_ref, lt_ref, *, alpha, n_lat_tiles):
    tile = pl.program_id(0)
    o = of_ref[...] + ob_ref[...]
    r = r_ref[...]
    gate = r * jax.nn.sigmoid(r)
    mixed = [_pick_rows(al_ref, ac_ref, tile, n_lat_tiles).astype(BF16)]
    for h in range(GLA_HEADS):
        vs = slice(h * GLA_DV, (h + 1) * GLA_DV)
        oh = o[:, vs]
        nrm = oh * lax.rsqrt(jnp.mean(oh * oh, -1, keepdims=True) + NORM_EPS) * ng_ref[...]
        mixed.append((nrm * gate[:, vs]).astype(BF16))
    out = _dot(jnp.concatenate(mixed, axis=1), wo_ref[...])
    x = _pick_rows(xl_ref, xc_ref, tile, n_lat_tiles)
    _post_mix(out, x, m2_ref, m3_ref, m4_ref, lg_ref, lb_ref, wr_ref, alpha, x1_ref, h2_ref, lt_ref)


def _post_specs(d, layer, seg, tm, n_exp):
    ins = [_mod_spec(d, layer, 2, seg), _mod_spec(d, layer, 3, seg), _mod_spec(d, layer, 4, seg)]
    tail = [pl.BlockSpec((1, d), lambda i: (0, 0)), pl.BlockSpec((1, d), lambda i: (0, 0)),
            pl.BlockSpec((n_exp, d), lambda i: (0, 0))]
    outs =[pl.BlockSpec((tm, d), lambda i: (i, 0)), pl.BlockSpec((tm, d), lambda i: (i, 0)),
            pl.BlockSpec((n_exp, tm), lambda i: (0, i))]
    return ins, tail, outs


def _post_shapes(t, d, n_exp):
    return [jax.ShapeDtypeStruct((t, d), F32), jax.ShapeDtypeStruct((t, d), BF16),
            jax.ShapeDtypeStruct((n_exp, t), F32)]


def even_out(a_lat, a_ctx, o_f, o_b, proj, col_r, x_lat, x_ctx, mods4, layer, norm_g, w_out_bf16, ln_g, ln_b,
             router_wt, alpha, seg_rows, n_batch, tm=256):
    t = x_lat.shape[0] + x_ctx.shape[0]
    d = x_lat.shape[1]
    na = a_lat.shape[1]
    wv = o_f.shape[1]
    n_exp = router_wt.shape[0]
    seg = _seg_fn(tm, seg_rows, n_batch)
    ins, tail, outs = _post_specs(d, layer, seg, tm, n_exp)
    tile_of = lambda i: i
    return pl.pallas_call(
        functools.partial(_even_out_kernel, alpha=alpha, n_lat_tiles=x_lat.shape[0] // tm),
        grid=(t // tm,),
        in_specs=(_two_source_specs(a_lat, a_ctx, tm, tile_of)
                  + [pl.BlockSpec((tm, wv), lambda i: (i, 0)),
                     pl.BlockSpec((tm, wv), lambda i: (i, 0)),
                     pl.BlockSpec((tm, wv), lambda i: (i, col_r))]
                  + _two_source_specs(x_lat, x_ctx, tm, tile_of) + ins
                  + [pl.BlockSpec((1, GLA_DV), lambda i: (0, 0)),
                     pl.BlockSpec((na + wv, d), lambda i: (0, 0))] + tail),
        out_specs=outs,
        out_shape=_post_shapes(t, d, n_exp),
        compiler_params=_cparams("arbitrary"),
        name="even_out",
    )(a_lat, a_ctx, o_f, o_b, proj, x_lat, x_ctx, mods4, mods4, mods4, norm_g.reshape(1, -1), w_out_bf16,
      ln_g.reshape(1, -1), ln_b.reshape(1, -1), router_wt)


def _odd_out_kernel(y_ref, x_ref, m2_ref, m3_ref, m4_ref, wg_ref, bg_ref, wo_ref,
                    lg_ref, lb_ref, wr_ref, x1_ref, h2_ref, lt_ref, *, alpha):
    g = jax.nn.gelu(y_ref[...], approximate=True)
    z = _dot(g.astype(BF16), wg_ref[...]) + bg_ref[...]
    v = g * jax.nn.sigmoid(z)
    out = _dot(v.astype(BF16), wo_ref[...])
    _post_mix(out, x_ref[...], m2_ref, m3_ref, m4_ref, lg_ref, lb_ref, wr_ref, alpha, x1_ref, h2_ref, lt_ref)


def odd_out(y, x, mods4, layer, w_glu_bf16, b_glu, w_out_bf16, ln_g, ln_b, router_wt,
            alpha, seg_rows, n_batch, tm=256):
    t, w5 = y.shape
    d = x.shape[1]
    n_exp = router_wt.shape[0]
    seg = _seg_fn(tm, seg_rows, n_batch)
    ins, tail, outs = _post_specs(d, layer, seg, tm, n_exp)
    return pl.pallas_call(
        functools.partial(_odd_out_kernel, alpha=alpha),
        grid=(t // tm,),
        in_specs=([pl.BlockSpec((tm, w5), lambda i: (i, 0)), pl.BlockSpec((tm, d), lambda i: (i, 0))] + ins
                  + [pl.BlockSpec((w5, w5), lambda i: (0, 0)),
                     pl.BlockSpec((1, w5), lambda i: (0, 0)),
                     pl.BlockSpec((w5, d), lambda i: (0, 0))] + tail),
        out_specs=outs,
        out_shape=_post_shapes(t, d, n_exp),
        compiler_params=_cparams("arbitrary"),
        name="odd_out",
    )(y, x, mods4, mods4, mods4, w_glu_bf16, b_glu.reshape(1, -1), w_out_bf16,
      ln_g.reshape(1, -1), ln_b.reshape(1, -1), router_wt)


def _route_kernel(lt_ref, rb_ref, idx_ref, w_ref):
    eg = N_EXPERTS // N_GROUPS
    logits = lt_ref[...]
    aff = jax.nn.sigmoid(logits)
    sel = aff + rb_ref[...]
    s = [sel[e:e + 1, :] for e in range(N_EXPERTS)]
    a = [aff[e:e + 1, :] for e in range(N_EXPERTS)]

    def top2_sum(v):
        hi1, lo1 = jnp.maximum(v[0], v[1]), jnp.minimum(v[0], v[1])
        hi2, lo2 = jnp.maximum(v[2], v[3]), jnp.minimum(v[2], v[3])
        return jnp.maximum(hi1, hi2) + jnp.maximum(jnp.minimum(hi1, hi2), jnp.maximum(lo1, lo2))

    best = top2_sum(s[0:eg])
    grp = jnp.zeros_like(best, dtype=jnp.int32)
    for g in range(1, N_GROUPS):
        sc = top2_sum(s[g * eg:(g + 1) * eg])
        better = sc > best
        best = jnp.where(better, sc, best)
        grp = jnp.where(better, g, grp)

    def pick(vals, j):
        out = vals[j]
        for g in range(1, N_GROUPS):
            out = jnp.where(grp == g, vals[g * eg + j], out)
        return out

    sv = [pick(s, j) for j in range(eg)]
    av = [pick(a, j) for j in range(eg)]

    def argmax_first(vals, exclude):
        bi = jnp.zeros_like(grp)
        bv = jnp.where(exclude == 0, -jnp.inf, vals[0]) if exclude is not None else vals[0]
        for j in range(1, eg):
            vj = jnp.where(exclude == j, -jnp.inf, vals[j]) if exclude is not None else vals[j]
            better = vj > bv
            bv = jnp.where(better, vj, bv)
            bi = jnp.where(better, j, bi)
        return bi

    i1 = argmax_first(sv, None)
    i2 = argmax_first(sv, i1)

    def take(vals, i):
        out = vals[0]
        for j in range(1, eg):
            out = jnp.where(i == j, vals[j], out)
        return out

    w1 = take(av, i1)
    w2 = take(av, i2)
    tot = w1 + w2
    idx_ref[0:1, :] = grp * eg + i1
    idx_ref[1:2, :] = grp * eg + i2
    w_ref[0:1, :] = w1 / tot
    w_ref[1:2, :] = w2 / tot


def route(logits_t, router_b, tile=1024):
    n_exp, t = logits_t.shape
    tile = math.gcd(tile, t)
    return pl.pallas_call(
        _route_kernel,
        grid=(t // tile,),
        in_specs=[pl.BlockSpec((n_exp, tile), lambda i: (0, i)),
                  pl.BlockSpec((n_exp, 1), lambda i: (0, 0))],
        out_specs=[pl.BlockSpec((TOP_K, tile), lambda i: (0, i)),
                   pl.BlockSpec((TOP_K, tile), lambda i: (0, i))],
        out_shape=[jax.ShapeDtypeStruct((TOP_K, t), jnp.int32), jax.ShapeDtypeStruct((TOP_K, t), F32)],
        compiler_params=_cparams("arbitrary"),
        name="moe_route",
    )(logits_t, router_b.reshape(n_exp, 1).astype(F32))


def moe_plan(idx, tm):
    t = idx.shape[1]
    n_pair = TOP_K * t
    n_tiles = (n_pair + N_EXPERTS * (tm - 1)) // tm
    e_flat = idx.reshape(-1)
    onehot = (e_flat[:, None] == jnp.arange(N_EXPERTS)[None, :]).astype(jnp.int32)
    running = jnp.cumsum(onehot, axis=0)
    counts = running[-1]
    rank = jnp.sum(onehot * running, 1) - 1
    tiles_per = (counts + tm - 1) // tm
    tile_end = jnp.cumsum(tiles_per)
    n_used = tile_end[-1]
    pstart = (tile_end - tiles_per) * tm
    pos = jnp.sum(onehot * pstart[None, :], 1) + rank
    tile_expert = jnp.minimum(jnp.sum((tile_end[None, :] <= jnp.arange(n_tiles)[:, None]).astype(jnp.int32), 1),
                              N_EXPERTS - 1).astype(jnp.int32)
    gidx = (jnp.arange(n_tiles * tm, dtype=jnp.int32) % t).at[pos].set(
        jnp.arange(n_pair, dtype=jnp.int32) % t, mode="promise_in_bounds", unique_indices=True)
    return gidx, tile_expert, n_used.reshape(1).astype(jnp.int32), pos.astype(jnp.int32)


def _expert_changed(te_ref):
    i = pl.program_id(0)
    return jnp.logical_or(i == 0, te_ref[i] != te_ref[jnp.maximum(i - 1, 0)])


def _ffn_up_kernel(te_ref, nu_ref, xs_ref, wg_ref, wu_ref, hid_ref, wg_bf, wu_bf):
    used = pl.program_id(0) < nu_ref[0]

    @pl.when(jnp.logical_and(used, _expert_changed(te_ref)))
    def _():
        wg_bf[...] = wg_ref[0].astype(BF16)
        wu_bf[...] = wu_ref[0].astype(BF16)

    @pl.when(used)
    def _():
        xs = xs_ref[...]
        g = _dot(xs, wg_bf[...])
        u = _dot(xs, wu_bf[...])
        hid_ref[...] = ((g * jax.nn.sigmoid(g)) * u).astype(BF16)

    @pl.when(jnp.logical_not(used))
    def _():
        hid_ref[...] = jnp.zeros_like(hid_ref)


def _ffn_down_kernel(te_ref, nu_ref, hid_ref, wd_ref, o_ref, wd_bf):
    used = pl.program_id(0) < nu_ref[0]

    @pl.when(jnp.logical_and(used, _expert_changed(te_ref)))
    def _():
        wd_bf[...] = wd_ref[0].astype(BF16)

    @pl.when(used)
    def _():
        o_ref[...] = _dot(hid_ref[...], wd_bf[...]).astype(o_ref.dtype)

    @pl.when(jnp.logical_not(used))
    def _():
        o_ref[...] = jnp.zeros_like(o_ref)


def grouped_ffn(xs, tile_expert, n_used, w_gate, w_up, w_down, layer, tm):
    p, d = xs.shape
    de = w_gate.shape[3]
    n_tiles = p // tm
    wmap = lambda i, te, nu: (layer, te[i], 0, 0)
    row_in = lambda i, te, nu: (jnp.minimum(i, nu[0] - 1), 0)
    row_out = lambda i, te, nu: (i, 0)
    hid = pl.pallas_call(
        _ffn_up_kernel,
        grid_spec=pltpu.PrefetchScalarGridSpec(
            num_scalar_prefetch=2,
            grid=(n_tiles,),
            in_specs=[pl.BlockSpec((tm, d), row_in),
                      pl.BlockSpec((None, 1, d, de), wmap),
                      pl.BlockSpec((None, 1, d, de), wmap)],
            out_specs=pl.BlockSpec((tm, de), row_out),
            scratch_shapes=[pltpu.VMEM((d, de), BF16), pltpu.VMEM((d, de), BF16)]),
        out_shape=jax.ShapeDtypeStruct((p, de), BF16),
        compiler_params=_cparams("arbitrary"),
        name="moe_ffn_up",
    )(tile_expert, n_used, xs, w_gate, w_up)
    return pl.pallas_call(
        _ffn_down_kernel,
        grid_spec=pltpu.PrefetchScalarGridSpec(
            num_scalar_prefetch=2,
            grid=(n_tiles,),
            in_specs=[pl.BlockSpec((tm, de), row_in),
                      pl.BlockSpec((None, 1, de, d), wmap)],
            out_specs=pl.BlockSpec((tm, d), row_out),
            scratch_shapes=[pltpu.VMEM((de, d), BF16)]),
        out_shape=jax.ShapeDtypeStruct((p, d), BF16),
        compiler_params=_cparams("arbitrary"),
        name="moe_ffn_down",
    )(tile_expert, n_used, hid, w_down)


def _final_kernel(x_ref, y0_ref, y1_ref, w_ref, m5_ref, lg_ref, lb_ref, o_ref, *, alpha):
    w = w_ref[...]
    y = w[:, 0:1] * y0_ref[...].astype(F32) + w[:, 1:2] * y1_ref[...].astype(F32)
    z = alpha * x_ref[...] + m5_ref[...] * y
    mu = jnp.mean(z, -1, keepdims=True)
    zc = z - mu
    var = jnp.mean(zc * zc, -1, keepdims=True)
    o_ref[...] = zc * lax.rsqrt(var + LN_EPS) * lg_ref[...] + lb_ref[...]


def final_norm(x1, yg, wts, mods4, layer, ln_g, ln_b, alpha, seg_rows, n_batch, tm=256):
    t, d = x1.shape
    seg = _seg_fn(tm, seg_rows, n_batch)
    row = pl.BlockSpec((tm, d), lambda i: (i, 0))
    vec = pl.BlockSpec((1, d), lambda i: (0, 0))
    return pl.pallas_call(
        functools.partial(_final_kernel, alpha=alpha),
        grid=(t // tm,),
        in_specs=[row, row, pl.BlockSpec((tm, d), lambda i: (i + t // tm, 0)),
                  pl.BlockSpec((tm, TOP_K), lambda i: (i, 0)),
                  _mod_spec(d, layer, 5, seg), vec, vec],
        out_specs=row,
        out_shape=jax.ShapeDtypeStruct((t, d), F32),
        compiler_params=_cparams("arbitrary"),
        name="final_norm",
    )(x1, yg, yg, wts, mods4, ln_g.reshape(1, -1), ln_b.reshape(1, -1))


def moe_block(x1, h2, logits_t, router_b, w_gate, w_up, w_down, mods4, layer, ln_g, ln_b,
              alpha, seg_rows, n_batch, tm=512):
    idx, wts = route(logits_t, router_b)
    gidx, tile_expert, n_used, pos = moe_plan(idx, tm)
    xs = h2.at[gidx].get(mode="promise_in_bounds")
    ys = grouped_ffn(xs, tile_expert, n_used, w_gate, w_up, w_down, layer, tm)
    yg = ys.at[pos].get(mode="promise_in_bounds")
    return final_norm(x1, yg, wts.T, mods4, layer, ln_g, ln_b, alpha, seg_rows, n_batch)


def s5_matrices(lam_re, lam_im, log_dt, b_re, b_im, c_re, c_im, d_skip):
    f32 = F32
    tc = S5_TC
    n_g, n_p = lam_re.shape[1], lam_re.shape[2]
    n_c = b_re.shape[-1]
    nb = 128 // n_c
    n_q = n_g // nb
    lr, li = lam_re.astype(f32), lam_im.astype(f32)
    dt = jnp.exp(log_dt.astype(f32))[..., None]

    def powers(jvals):
        j = jnp.asarray(np.asarray(jvals, np.float32))[:, None, None, None]
        mag = jnp.exp(lr * dt * j)
        return mag * jnp.cos(li * dt * j), mag * jnp.sin(li * dt * j)

    up = np.arange(tc)
    pw_re, pw_im = powers(np.arange(tc + 1))
    lb_re, lb_im = pw_re[1], pw_im[1]
    den = lr * lr + li * li
    fr = ((lb_re - 1.0) * lr + lb_im * li) / den
    fi = (lb_im * lr - (lb_re - 1.0) * li) / den
    br, bi = b_re.astype(f32), b_im.astype(f32)
    bb_re = fr[..., None] * br - fi[..., None] * bi
    bb_im = fr[..., None] * bi + fi[..., None] * br
    cr, ci = c_re.astype(f32), c_im.astype(f32)

    def times_b(p_re, p_im):
        return (p_re[..., None] * bb_re[None] - p_im[..., None] * bb_im[None],
                p_re[..., None] * bb_im[None] + p_im[..., None] * bb_re[None])

    e_re, e_im = times_b(pw_re, pw_im)
    kmat = jnp.sum(cr[None, :, :, :, :, None] * e_re[:, :, :, None, :, :]
                   - ci[None, :, :, :, :, None] * e_im[:, :, :, None, :, :], axis=4)
    def lag_slab(k_dir):
        return k_dir.reshape(tc, n_q, nb, n_c, n_c).transpose(1, 0, 4, 2, 3).reshape(n_q, tc, n_c, nb * n_c)
    skip = (jnp.eye(n_c, dtype=f32)[None, None, :, None, :]
            * d_skip.astype(f32).reshape(n_q, nb, n_c)[:, None, None, :, :]).reshape(n_q, 1, n_c, nb * n_c)
    k_c = jnp.concatenate([lag_slab(kmat[:tc, 0]), lag_slab(kmat[:tc, 1]), skip], 1)

    def w_slab(e):
        return e.reshape(tc, n_q, nb, n_p, n_c).transpose(1, 0, 4, 2, 3).reshape(n_q, tc, n_c, nb * n_p)
    ef_re, ef_im = times_b(*powers(tc - 1 - up))
    w_c = jnp.stack([w_slab(ef_re[:, 0]), w_slab(ef_im[:, 0]),
                     w_slab(e_re[:tc, 1]), w_slab(e_im[:tc, 1])], 2)

    def v_slabs(d, p_re, p_im):
        f_re = cr[d][None] * p_re[:, :, None, :] - ci[d][None] * p_im[:, :, None, :]
        f_im = cr[d][None] * p_im[:, :, None, :] + ci[d][None] * p_re[:, :, None, :]
        slab = lambda m: m.reshape(tc, n_q, nb, n_c, n_p).transpose(1, 0, 3, 2, 4).reshape(n_q, tc, n_c, nb * n_p)
        return slab(f_re), slab(-f_im)
    pb_re, pb_im = powers(tc - up)
    vt_c = jnp.stack(v_slabs(0, pw_re[1:, 0], pw_im[1:, 0]) + v_slabs(1, pb_re[:, 1], pb_im[:, 1]), 2)
    dec = lambda m: m.reshape(1, n_g * n_p // 128, 1, 128)
    decay = jnp.concatenate([dec(pw_re[tc, 0]), dec(pw_im[tc, 0]), dec(pw_re[tc, 1]), dec(pw_im[tc, 1])], 0)
    return k_c, w_c, vt_c, decay


def _s5_chunk_rows(ref, n):
    return jnp.concatenate([ref[pl.ds(s, n, stride=S5_TC), :] for s in range(S5_TC)], axis=1).astype(BF16)


def _s5_expand(slab, group_lanes):
    rows = 128
    tiled = jnp.concatenate([slab] * (rows // slab.shape[0]), axis=0)
    r = lax.broadcasted_iota(jnp.int32, tiled.shape, 0) // S5_CH
    l = lax.broadcasted_iota(jnp.int32, tiled.shape, 1) // group_lanes
    return jnp.where(r == l, tiled, 0.0).astype(BF16)


def _s5_in_kernel(ul_ref, uc_ref, wc_ref, fr_ref, fi_ref, br_ref, bi_ref, w_ref, *, n_lat, n_ctx, n_batch):
    b = pl.program_id(1)
    n_plane = wc_ref.shape[1]
    st = wc_ref.shape[3]

    @pl.when(b == 0)
    def _():
        for s in range(S5_TC):
            for i in range(n_plane):
                w_ref[s * 128:(s + 1) * 128, i * st:(i + 1) * st] = _s5_expand(wc_ref[s, i], S5_P)

    w_lat = _dot(_s5_chunk_rows(ul_ref, n_lat), w_ref[...])
    w_ctx = _dot(_s5_chunk_rows(uc_ref, n_ctx), w_ref[...])
    nv = fr_ref.shape[0]
    for i, ref in enumerate((fr_ref, fi_ref, br_ref, bi_ref)):
        for c in range(nv):
            lanes = slice((i * nv + c) * 128, (i * nv + c + 1) * 128)
            ref[c, pl.ds(b, n_ctx, stride=n_batch), :] = w_ctx[:, lanes]
            ref[c, pl.ds(n_ctx * n_batch + b, n_lat, stride=n_batch), :] = w_lat[:, lanes]
            ref[c, pl.ds((n_ctx + n_lat) * n_batch + b, n_ctx, stride=n_batch), :] = w_ctx[:, lanes]


def _s5_scan_kernel(wfr, wfi, wbr, wbi, dec_ref, xfr, xfi, xbr, xbi, *, n_tiles):
    nv = wfr.shape[0]
    low = lax.broadcasted_iota(jnp.int32, (nv, 8, 128), 1) < 4
    a_fr, a_fi, a_br, a_bi = dec_ref[0], dec_ref[1], dec_ref[2], dec_ref[3]

    def half_step(s_re, s_im, a_re, a_im, w_re, w_im):
        return a_re * s_re - a_im * s_im + w_re, a_re * s_im + a_im * s_re + w_im

    def one_dir(w_re_ref, w_im_ref, x_re_ref, x_im_ref, row0, s_re, s_im, a_re, a_im, first_low):
        first = low if first_low else jnp.logical_not(low)
        wt_re, wt_im = w_re_ref[:, pl.ds(row0, 8), :], w_im_ref[:, pl.ds(row0, 8), :]
        wr_re, wr_im = pltpu.roll(wt_re, 4, 1), pltpu.roll(wt_im, 4, 1)
        mid_re, mid_im = half_step(s_re, s_im, a_re, a_im, wr_re, wr_im)
        x_re_ref[:, pl.ds(row0, 8), :] = jnp.where(first, s_re, mid_re)
        x_im_ref[:, pl.ds(row0, 8), :] = jnp.where(first, s_im, mid_im)
        m_re = jnp.where(first, pltpu.roll(mid_re, 4, 1), mid_re)
        m_im = jnp.where(first, pltpu.roll(mid_im, 4, 1), mid_im)
        w2_re = jnp.where(first, wr_re, wt_re)
        w2_im = jnp.where(first, wr_im, wt_im)
        return half_step(m_re, m_im, a_re, a_im, w2_re, w2_im)

    def body(i, carry):
        f_re, f_im, b_re, b_im = carry
        rf = pl.multiple_of(i * 8, 8)
        rb = pl.multiple_of((n_tiles - 1 - i) * 8, 8)
        f_re, f_im = one_dir(wfr, wfi, xfr, xfi, rf, f_re, f_im, a_fr, a_fi, True)
        b_re, b_im = one_dir(wbr, wbi, xbr, xbi, rb, b_re, b_im, a_br, a_bi, False)
        return f_re, f_im, b_re, b_im

    z = jnp.zeros((nv, 8, 128), F32)
    lax.fori_loop(0, n_tiles, body, (z, z, z, z))


def _s5_out_kernel(ul_ref, fr_ref, fi_ref, br_ref, bi_ref, kc_ref, vc_ref, y_ref, mt_ref, vt_ref,
                   *, n_lat, n_ctx, n_batch):
    b = pl.program_id(1)
    tc = S5_TC

    @pl.when(b == 0)
    def _():
        lag = [_s5_expand(kc_ref[j], S5_CH) for j in range(2 * tc)]
        diag = _s5_expand(kc_ref[0] + kc_ref[tc] + kc_ref[2 * tc], S5_CH)
        for s in range(tc):
            for t in range(tc):
                blk = diag if s == t else (lag[t - s] if t > s else lag[tc + s - t])
                mt_ref[s * 128:(s + 1) * 128, t * 128:(t + 1) * 128] = blk
        for t in range(tc):
            for i in range(vt_ref.shape[0]):
                vt_ref[i, t * 128:(t + 1) * 128, :] = _s5_expand(vc_ref[t, i], S5_P)

    y = _dot(_s5_chunk_rows(ul_ref, n_lat), mt_ref[...])
    row0 = n_ctx * n_batch + b
    for i, ref in enumerate((fr_ref, fi_ref, br_ref, bi_ref)):
        xs = jnp.concatenate([ref[c, pl.ds(row0, n_lat, stride=n_batch), :] for c in range(ref.shape[0])], 1)
        y = y + _dot_nt(xs.astype(BF16), vt_ref[i])
    for s in range(S5_TC):
        y_ref[pl.ds(s, n_lat, stride=S5_TC), :] = y[:, s * 128:(s + 1) * 128]


def s5_bidir(u, mats, n_batch, l_lat, l_ctx):
    assert n_batch == 4, "the chunk scan packs two chunks of 4 batch rows per 8-sublane tile"
    k_c, w_c, vt_c, decay = mats
    tc = S5_TC
    wd = u.shape[1]
    n_q = wd // 128
    lane_q = tc * 128
    st_q = (128 // S5_CH) * S5_P
    n_lat, n_ctx = l_lat // tc, l_ctx // tc
    nk = n_lat + 2 * n_ctx
    assert nk % 2 == 0
    rows = nk * n_batch
    ctx0 = (n_batch * l_lat) // l_ctx
    dims = dict(n_lat=n_lat, n_ctx=n_ctx, n_batch=n_batch)

    nv = st_q // 128
    plane = jax.ShapeDtypeStruct((n_q * nv, rows, 128), F32)
    plane_spec = pl.BlockSpec((nv, rows, 128), lambda q, b: (q, 0, 0))
    ul_spec = pl.BlockSpec((l_lat, 128), lambda q, b: (b, q))
    uc_spec = pl.BlockSpec((l_ctx, 128), lambda q, b: (ctx0 + b, q))
    w_planes = pl.pallas_call(
        functools.partial(_s5_in_kernel, **dims),
        grid=(n_q, n_batch),
        in_specs=[ul_spec, uc_spec, pl.BlockSpec((None,) + w_c.shape[1:], lambda q, b: (q, 0, 0, 0, 0))],
        out_specs=[plane_spec] * 4,
        out_shape=[plane] * 4,
        scratch_shapes=[pltpu.VMEM((lane_q, 4 * st_q), BF16)],
        compiler_params=_cparams("arbitrary", "arbitrary"),
        name="s5_chunk_in",
    )(u, u, w_c)

    blk = pl.BlockSpec((nv, rows, 128), lambda j: (j, 0, 0))
    x_planes = pl.pallas_call(
        functools.partial(_s5_scan_kernel, n_tiles=rows // 8),
        grid=(n_q,),
        in_specs=[blk] * 4 + [pl.BlockSpec((4, nv, 1, 128), lambda j: (0, j, 0, 0))],
        out_specs=[blk] * 4,
        out_shape=[plane] * 4,
        compiler_params=_cparams("arbitrary"),
        name="s5_chunk_scan",
    )(*w_planes, decay)

    return pl.pallas_call(
        functools.partial(_s5_out_kernel, **dims),
        grid=(n_q, n_batch),
        in_specs=[ul_spec] + [plane_spec] * 4
                 + [pl.BlockSpec((None,) + k_c.shape[1:], lambda q, b: (q, 0, 0, 0)),
                    pl.BlockSpec((None,) + vt_c.shape[1:], lambda q, b: (q, 0, 0, 0, 0))],
        out_specs=pl.BlockSpec((l_lat, 128), lambda q, b: (b, q)),
        out_shape=jax.ShapeDtypeStruct((n_batch * l_lat, wd), F32),
        scratch_shapes=[pltpu.VMEM((lane_q, lane_q), BF16), pltpu.VMEM((4, lane_q, st_q), BF16)],
        compiler_params=_cparams("arbitrary", "arbitrary"),
        name="s5_chunk_out",
    )(u, *x_planes, k_c, vt_c)


def kernel(x, c, ctx, c_ctx, ada_w, ada_b, ln_mix_g, ln_mix_b, ln_ffn_g, ln_ffn_b, ev_w_in, ev_gate_w2,
           ev_gate_b, ev_rpb, ev_norm_g, ev_w_out, od_w_in, od_lam_re, od_lam_im, od_log_dt, od_b_re,
           od_b_im, od_c_re, od_c_im, od_d, od_w_glu, od_b_glu, od_w_out, router_w, router_b,
           moe_w_gate, moe_w_up, moe_w_down):
    n_batch, l_lat, d = x.shape
    l_ctx = ctx.shape[1]
    depth = ada_w.shape[0]
    assert depth == 2, "one even (NA + GLA) layer followed by one odd (S5) layer"
    alpha = (2.0 * depth) ** 0.25
    n_lat = n_batch * l_lat

    cvec = jnp.concatenate([c, c_ctx[None], jnp.zeros((8 - n_batch - 1, d), F32)], 0)
    mods = compute_mods(cvec, ada_w, ada_b)
    mods4 = mods.reshape(depth, 8, 1, N_MOD * d)
    x_lat, x_ctx = x.reshape(n_lat, d), ctx.reshape(n_batch * l_ctx, d)
    router_wt = router_w.T.astype(F32)

    na_w = NA_HEADS * NA_DH
    wk = GLA_HEADS * GLA_DK
    wv = GLA_HEADS * GLA_DV
    ev_in = ev_w_in.shape[2]
    pad = (-ev_in) % 256
    w_in = jnp.pad(ev_w_in[0], ((0, 0), (0, pad))).astype(BF16)
    proj = mod_matmul(x_lat, x_ctx, mods4, 0, w_in, l_lat, n_batch, tm=512, tn=(ev_in + pad) // 2)
    a_lat, a_ctx = na_attention(proj, na_bias_table(ev_rpb[0], l_lat // GRID_W), n_batch, l_lat, l_ctx)
    col_lr = (3 * na_w + 2 * wk + 2 * wv) // 128
    g2 = jnp.zeros((2, 128, wk), F32)
    g2 = g2.at[0, 0:GLA_RANK].set(ev_gate_w2[0, 0]).at[1, GLA_RANK:2 * GLA_RANK].set(ev_gate_w2[0, 1])
    o_f, o_b = gla_bidir(proj, g2.astype(BF16), ev_gate_b[0].reshape(2, 1, wk), rope_tables(l_lat, l_ctx),
                         n_batch, l_lat, l_ctx,
                         col_q=3 * na_w // wk, col_k=(3 * na_w + wk) // wk,
                         col_v=(3 * na_w + 2 * wk) // wv, col_lr=col_lr)
    x1, h2, logits_t = even_out(a_lat, a_ctx, o_f, o_b, proj, (3 * na_w + 2 * wk + wv) // wv, x_lat, x_ctx,
                                mods4, 0, ev_norm_g[0], ev_w_out[0].astype(BF16), ln_mix_g[0], ln_mix_b[0],
                                router_wt, alpha, l_lat, n_batch)
    w_gate, w_up, w_down = moe_w_gate, moe_w_up, moe_w_down
    rows = moe_block(x1, h2, logits_t, router_b, w_gate, w_up, w_down, mods4, 0,
                     ln_ffn_g[0], ln_ffn_b[0], alpha, l_lat, n_batch)

    u = mod_matmul(rows, None, mods4, 1, od_w_in[0].astype(BF16), l_lat, n_batch)
    mats = s5_matrices(od_lam_re[0], od_lam_im[0], od_log_dt[0], od_b_re[0], od_b_im[0],
                       od_c_re[0], od_c_im[0], od_d[0])
    y5 = s5_bidir(u, mats, n_batch, l_lat, l_ctx)
    x1, h2, logits_t = odd_out(y5, rows, mods4, 1, od_w_glu[0].astype(BF16), od_b_glu[0],
                               od_w_out[0].astype(BF16), ln_mix_g[1], ln_mix_b[1], router_wt,
                               alpha, l_lat, n_batch)
    out = moe_block(x1, h2, logits_t, router_b, w_gate, w_up, w_down, mods4, 1,
                    ln_ffn_g[1], ln_ffn_b[1], alpha, l_lat, n_batch)
    return out.reshape(n_batch, l_lat, d)
```

```python
import functools
import math

import numpy as np
import jax
import jax.numpy as jnp
from jax import lax
from jax.experimental import pallas as pl
from jax.experimental.pallas import tpu as pltpu

F32 = jnp.float32
BF16 = jnp.bfloat16
HIGHEST = lax.Precision.HIGHEST

N_MOD = 6
LN_EPS = 1e-5
NORM_EPS = 1e-6

GRID_W = 64
NA_HEADS = 8
NA_DH = 128
NA_KR = 8
NA_KC = 16

GLA_HEADS = 4
GLA_DK = 128
GLA_DV = 256
GLA_RANK = 16
GLA_TAU = 16.0
GLA_CHUNK = 64
ROPE_BASE = 10000.0

S5_CH = 16
S5_P = 64
S5_TC = 16

N_EXPERTS = 16
N_GROUPS = 4
TOP_K = 2

VMEM_LIMIT = 56 * 1024 * 1024
NEG_BIG = -1e30


def _cparams(*sem):
    return pltpu.CompilerParams(dimension_semantics=sem, vmem_limit_bytes=VMEM_LIMIT)


def _dot(a, b, precision=None):
    return jnp.dot(a, b, preferred_element_type=F32, precision=precision)


def _dot_nt(a, b, precision=None):
    return lax.dot_general(a, b, (((1,), (1,)), ((), ())), preferred_element_type=F32, precision=precision)


def _dot_tn(a, b):
    return lax.dot_general(a, b, (((0,), (0,)), ((), ())), preferred_element_type=F32)


def _mods_kernel(s_ref, w_ref, b_ref, o_ref):
    s = s_ref[...]
    s = s * jax.nn.sigmoid(s)
    o_ref[0] = _dot(s, w_ref[0], HIGHEST) + b_ref[0]


def compute_mods(cvec, ada_w, ada_b, tn=1024):
    n_layer, d, n = ada_w.shape
    tn = math.gcd(tn, n)
    return pl.pallas_call(
        _mods_kernel,
        grid=(n_layer, n // tn),
        in_specs=[pl.BlockSpec((8, d), lambda l, j: (0, 0)),
                  pl.BlockSpec((1, d, tn), lambda l, j: (l, 0, j)),
                  pl.BlockSpec((1, 1, tn), lambda l, j: (l, 0, j))],
        out_specs=pl.BlockSpec((1, 8, tn), lambda l, j: (l, 0, j)),
        out_shape=jax.ShapeDtypeStruct((n_layer, 8, n), F32),
        compiler_params=_cparams("arbitrary", "arbitrary"),
        name="ada_mods",
    )(cvec, ada_w, ada_b.reshape(n_layer, 1, n))


def _mod_spec(d, layer, which, seg_of_tile):
    return pl.BlockSpec((None, None, 1, d), lambda i, *_: (layer, seg_of_tile(i), 0, which))


def _seg_fn(tm, seg_rows, n_batch):
    return lambda i: jnp.minimum((i * tm) // seg_rows, n_batch)


def _two_source_specs(lat, ctx, tm, tile_of):
    n_lat_tiles = lat.shape[0] // tm
    d = lat.shape[1]
    return [pl.BlockSpec((tm, d), lambda *g: (jnp.minimum(tile_of(*g), n_lat_tiles - 1), 0)),
            pl.BlockSpec((tm, d), lambda *g: (jnp.maximum(tile_of(*g) - n_lat_tiles, 0), 0))]


def _pick_rows(lat_ref, ctx_ref, tile, n_lat_tiles):
    return jnp.where(tile < n_lat_tiles, lat_ref[...], ctx_ref[...])


def _modmm_kernel(xl_ref, xc_ref, s1_ref, s0_ref, w_ref, o_ref, *, n_lat_tiles):
    x = _pick_rows(xl_ref, xc_ref, pl.program_id(1), n_lat_tiles)
    h = x * (1.0 + s1_ref[...]) + s0_ref[...]
    o_ref[...] = _dot(h.astype(BF16), w_ref[...])


def mod_matmul(x_lat, x_ctx, mods4, layer, w_bf16, seg_rows, n_batch, tm=256, tn=None):
    if x_ctx is None:
        t, x_ctx = x_lat.shape[0], x_lat
    else:
        t = x_lat.shape[0] + x_ctx.shape[0]
    d = x_lat.shape[1]
    n = w_bf16.shape[1]
    tn = n if tn is None else tn
    seg = _seg_fn(tm, seg_rows, n_batch)
    return pl.pallas_call(
        functools.partial(_modmm_kernel, n_lat_tiles=x_lat.shape[0] // tm),
        grid=(n // tn, t // tm),
        in_specs=_two_source_specs(x_lat, x_ctx, tm, lambda j, i: i) + [
            pl.BlockSpec((None, None, 1, d), lambda j, i: (layer, seg(i), 0, 1)),
            pl.BlockSpec((None, None, 1, d), lambda j, i: (layer, seg(i), 0, 0)),
            pl.BlockSpec((d, tn), lambda j, i: (0, j))],
        out_specs=pl.BlockSpec((tm, tn), lambda j, i: (i, j)),
        out_shape=jax.ShapeDtypeStruct((t, n), F32),
        compiler_params=_cparams("arbitrary", "arbitrary"),
        name="mod_matmul",
    )(x_lat, x_ctx, mods4, mods4, w_bf16)


NA_RB = 4
NA_BAND = NA_RB + NA_KR - 1


def _na_row_start(r, rows):
    return min(max(r - NA_KR // 2, 0), rows - NA_KR)


def na_bias_table(rpb, rows):
    w = GRID_W
    q = np.arange(w)
    kc = np.arange(w)
    win0 = np.clip(q - NA_KC // 2, 0, w - NA_KC)
    ok = (kc[None, :] >= win0[:, None]) & (kc[None, :] < win0[:, None] + NA_KC)
    dc = np.clip(kc[None, :] - q[:, None] + NA_KC - 1, 0, 2 * NA_KC - 2)
    pick = ((dc[None] == np.arange(2 * NA_KC - 1)[:, None, None]) & ok[None]).astype(np.float32)
    colb = jnp.einsum("hrd,dqk->hrqk", rpb.astype(F32), jnp.asarray(pick), precision=HIGHEST)
    colb = jnp.where(ok[None, None], colb, NEG_BIG)
    neg = jnp.full((rpb.shape[0], w, w), NEG_BIG, F32)

    def block(r0):
        band0 = min(max(r0 - NA_KR // 2, 0), rows - NA_BAND)
        out = []
        for r in range(r0, r0 + NA_RB):
            rs = _na_row_start(r, rows)
            first = rs - r + NA_KR - 1
            cols = [neg] * (rs - band0) + [colb[:, first + j] for j in range(NA_KR)]
            cols += [neg] * (NA_BAND - len(cols))
            out.append(jnp.concatenate(cols, -1))
        return jnp.concatenate(out, 1)

    return jnp.stack([block(0), block(NA_RB), block(rows - NA_RB)], 1)


def _na_kernel(q_ref, k_ref, v_ref, qc_ref, kc_ref, vc_ref, bias_ref, o_ref, oc_ref, kbf, vbf, *, rows):
    w = GRID_W
    n_blk = rows // NA_RB
    scale = NA_DH ** -0.5
    kbf[...] = k_ref[...].astype(BF16)
    vbf[...] = v_ref[...].astype(BF16)
    kc = kc_ref[...].astype(BF16)
    vc = vc_ref[...].astype(BF16)

    def body(pair, carry):
        blocks = (2 * pair, 2 * pair + 1)
        q0, scores = [], []
        for i in blocks:
            r0 = i * NA_RB
            band0 = jnp.clip(r0 - NA_KR // 2, 0, rows - NA_BAND)
            variant = jnp.where(i == 0, 0, jnp.where(i == n_blk - 1, 2, 1))
            q0.append(pl.multiple_of(r0 * w, NA_RB * w))
            k0 = pl.multiple_of(band0 * w, w)
            q = (q_ref[pl.ds(q0[-1], NA_RB * w), :] * scale).astype(BF16)
            s_loc = _dot_nt(q, kbf[pl.ds(k0, NA_BAND * w), :]) + bias_ref[variant]
            scores.append((s_loc, _dot_nt(q, kc), k0))
        probs = []
        for s_loc, s_ctx, k0 in scores:
            m = jnp.maximum(jnp.max(s_loc, -1, keepdims=True), jnp.max(s_ctx, -1, keepdims=True))
            p_loc = jnp.exp(s_loc - m)
            p_ctx = jnp.exp(s_ctx - m)
            den = jnp.sum(p_loc, -1, keepdims=True) + jnp.sum(p_ctx, -1, keepdims=True)
            probs.append((p_loc.astype(BF16), p_ctx.astype(BF16), den, k0))
        for q_start, (p_loc, p_ctx, den, k0) in zip(q0, probs):
            o = _dot(p_loc, vbf[pl.ds(k0, NA_BAND * w), :]) + _dot(p_ctx, vc)
            o_ref[pl.ds(q_start, NA_RB * w), :] = o / den
        return carry

    lax.fori_loop(0, n_blk // 2, body, 0)

    qc = (qc_ref[...] * scale).astype(BF16)
    s = _dot_nt(qc, kc)
    p = jnp.exp(s - jnp.max(s, -1, keepdims=True))
    oc_ref[...] = _dot(p.astype(BF16), vc) / jnp.sum(p, -1, keepdims=True)


def na_attention(proj, bias_tab, n_batch, l_lat, l_ctx):
    h = NA_HEADS
    dh = NA_DH
    rows = l_lat // GRID_W
    ctx0 = (n_batch * l_lat) // l_ctx
    return pl.pallas_call(
        functools.partial(_na_kernel, rows=rows),
        grid=(n_batch, h),
        in_specs=[pl.BlockSpec((l_lat, dh), lambda b, hh: (b, hh)),
                  pl.BlockSpec((l_lat, dh), lambda b, hh: (b, h + hh)),
                  pl.BlockSpec((l_lat, dh), lambda b, hh: (b, 2 * h + hh)),
                  pl.BlockSpec((l_ctx, dh), lambda b, hh: (ctx0 + b, hh)),
                  pl.BlockSpec((l_ctx, dh), lambda b, hh: (ctx0 + b, h + hh)),
                  pl.BlockSpec((l_ctx, dh), lambda b, hh: (ctx0 + b, 2 * h + hh)),
                  pl.BlockSpec((None,) + bias_tab.shape[1:], lambda b, hh: (hh, 0, 0, 0))],
        out_specs=[pl.BlockSpec((l_lat, dh), lambda b, hh: (b, hh)),
                   pl.BlockSpec((l_ctx, dh), lambda b, hh: (b, hh))],
        out_shape=[jax.ShapeDtypeStruct((n_batch * l_lat, h * dh), F32),
                   jax.ShapeDtypeStruct((n_batch * l_ctx, h * dh), F32)],
        scratch_shapes=[pltpu.VMEM((l_lat, dh), BF16), pltpu.VMEM((l_lat, dh), BF16)],
        compiler_params=_cparams("arbitrary", "arbitrary"),
        name="na_attention",
    )(proj, proj, proj, proj, proj, proj, bias_tab)


def rope_tables(l_lat):
    half = GLA_DK // 2
    nf = half // 2
    inv = ROPE_BASE ** (-np.arange(nf, dtype=np.float64) / nf)
    t = np.arange(l_lat)
    lane = np.arange(GLA_DK)
    pos = np.where(lane[None, :] < half, (t // GRID_W)[:, None], (t % GRID_W)[:, None]).astype(np.float64)
    ang = pos * inv[lane % nf][None, :]
    first = (lane % half) < nf
    cos = np.cos(ang)
    sin_a = np.where(first[None, :], -np.sin(ang), 0.0)
    sin_b = np.where(first[None, :], 0.0, np.sin(ang))
    return jnp.asarray(cos, F32), jnp.asarray(sin_a, F32), jnp.asarray(sin_b, F32)


GLA_PREP_CHUNKS = 8


def _gla_prep_kernel(q_ref, k_ref, lr_ref, cos_ref, sa_ref, sb_ref, g2_ref, gb_ref,
                     qdf, kdf, krf, elf, qdb, kdb, krb, elb, *, n_lat_tiles):
    c = GLA_CHUNK
    nch = GLA_PREP_CHUNKS
    nf = GLA_DK // 4
    gscale = GLA_DK ** -0.5
    wk = GLA_HEADS * GLA_DK
    is_lat = pl.program_id(0) < n_lat_tiles
    cos = jnp.where(is_lat, cos_ref[...], 1.0)
    sa = jnp.where(is_lat, sa_ref[...], 0.0)
    sb = jnp.where(is_lat, sb_ref[...], 0.0)

    def rope(x):
        return x * cos + pltpu.roll(x, GLA_DK - nf, 1) * sa + pltpu.roll(x, nf, 1) * sb

    qs, ks_ = [], []
    for h in range(GLA_HEADS):
        hs = slice(h * GLA_DK, (h + 1) * GLA_DK)
        qs.append(rope(q_ref[:, hs]) * gscale)
        ks_.append(rope(k_ref[:, hs]))

    lr = lr_ref[...].astype(BF16)
    row = lax.broadcasted_iota(jnp.int32, (c, c), 0)
    col = lax.broadcasted_iota(jnp.int32, (c, c), 1)
    for d, (qd, kd, kr, el) in enumerate(((qdf, kdf, krf, elf), (qdb, kdb, krb, elb))):
        reverse = d == 1
        tri = ((col >= row) if reverse else (col <= row)).astype(BF16)
        z = _dot(lr, g2_ref[d]) + gb_ref[d]
        g = (jnp.minimum(z, 0.0) - jnp.log1p(jnp.exp(-jnp.abs(z)))) * (1.0 / GLA_TAU)
        g_hi = g.astype(BF16)
        r1 = g - g_hi.astype(F32)
        g_mid = r1.astype(BF16)
        g_lo = (r1 - g_mid.astype(F32)).astype(BF16)
        parts = []
        for ci in range(nch):
            rs = slice(ci * c, (ci + 1) * c)
            parts.append(_dot(tri, g_hi[rs]) + _dot(tri, g_mid[rs]) + _dot(tri, g_lo[rs]))
        b3 = jnp.concatenate(parts, 0).reshape(nch, c, wk)
        bl3 = b3[:, 0:1, :] if reverse else b3[:, c - 1:c, :]
        el[...] = jnp.exp(bl3)
        e_b = jnp.exp(b3).reshape(nch * c, wk)
        e_nb = jnp.exp(-b3).reshape(nch * c, wk)
        e_rem = jnp.exp(bl3 - b3).reshape(nch * c, wk)
        for h in range(GLA_HEADS):
            hs = slice(h * GLA_DK, (h + 1) * GLA_DK)
            qd[:, hs] = (qs[h] * e_b[:, hs]).astype(BF16)
            kd[:, hs] = (ks_[h] * e_nb[:, hs]).astype(BF16)
            kr[:, hs] = (ks_[h] * e_rem[:, hs]).astype(BF16)


def _gla_scan_kernel(qdf, kdf, krf, elf, vf, qdb, kdb, krb, elb, vb, of_ref, ob_ref, st_ref):
    @pl.when(pl.program_id(1) == 0)
    def _():
        st_ref[...] = jnp.zeros_like(st_ref)

    c = GLA_CHUNK
    row = lax.broadcasted_iota(jnp.int32, (c, c), 0)
    col = lax.broadcasted_iota(jnp.int32, (c, c), 1)
    dirs = ((qdf, kdf, krf, elf, vf, of_ref, col <= row), (qdb, kdb, krb, elb, vb, ob_ref, col >= row))
    chains = [(d, h) for d in range(2) for h in range(GLA_HEADS)]
    hs = lambda h: slice(h * GLA_DK, (h + 1) * GLA_DK)
    vs = lambda h: slice(h * GLA_DV, (h + 1) * GLA_DV)
    q_dec = [dirs[d][0][:, hs(h)] for d, h in chains]
    v_bf = [dirs[d][4][:, vs(h)].astype(BF16) for d, h in chains]
    att = [jnp.where(dirs[d][6], _dot_nt(q_dec[n], dirs[d][1][:, hs(h)]), 0.0).astype(BF16)
           for n, (d, h) in enumerate(chains)]
    state = [st_ref[d, h] for d, h in chains]
    for n, (d, h) in enumerate(chains):
        dirs[d][5][:, vs(h)] = _dot(att[n], v_bf[n]) + _dot_nt(q_dec[n], state[n].astype(BF16))
    for n, (d, h) in enumerate(chains):
        st_ref[d, h] = state[n] * dirs[d][3][:, hs(h)] + _dot_tn(v_bf[n], dirs[d][2][:, hs(h)])


def gla_bidir(proj, g2, gb, tables, n_batch, l_lat, l_ctx, col_q, col_k, col_v, col_lr):
    c = GLA_CHUNK
    nc = l_ctx // c
    nl = l_lat // c
    nz = nl + 2 * nc
    steps = nl + nc
    wk = GLA_HEADS * GLA_DK
    wv = GLA_HEADS * GLA_DV
    t_rows = n_batch * (l_lat + l_ctx)

    tp = GLA_PREP_CHUNKS * c
    lat_tiles = l_lat // tp
    n_lat_tiles = n_batch * lat_tiles
    cos, sa, sb = tables
    tab = pl.BlockSpec((tp, GLA_DK), lambda i: (jnp.where(i < n_lat_tiles, i % lat_tiles, 0), 0))
    row_bf = jax.ShapeDtypeStruct((t_rows, wk), BF16)
    last = jax.ShapeDtypeStruct((t_rows // c, 1, wk), F32)
    row_spec = pl.BlockSpec((tp, wk), lambda i: (i, 0))
    last_spec = pl.BlockSpec((GLA_PREP_CHUNKS, 1, wk), lambda i: (i, 0, 0))
    prep = pl.pallas_call(
        functools.partial(_gla_prep_kernel, n_lat_tiles=n_lat_tiles),
        grid=(t_rows // tp,),
        in_specs=[pl.BlockSpec((tp, wk), lambda i: (i, col_q)),
                  pl.BlockSpec((tp, wk), lambda i: (i, col_k)),
                  pl.BlockSpec((tp, 128), lambda i: (i, col_lr)),
                  tab, tab, tab,
                  pl.BlockSpec((2, 128, wk), lambda i: (0, 0, 0)),
                  pl.BlockSpec((2, 1, wk), lambda i: (0, 0, 0))],
        out_specs=[row_spec, row_spec, row_spec, last_spec] * 2,
        out_shape=[row_bf, row_bf, row_bf, last] * 2,
        compiler_params=_cparams("arbitrary"),
        name="gla_prep",
    )(proj, proj, proj, cos, sa, sb, g2, gb)

    def zblk(b, j):
        lat = b * nl + (j - nc)
        ctx = n_batch * nl + b * nc + jnp.where(j < nc, j, j - nc - nl)
        return jnp.where((j >= nc) & (j < nc + nl), lat, ctx)

    fwd = lambda b, i: zblk(b, i)
    bwd = lambda b, i: zblk(b, nz - 1 - i)

    def dir_specs(blk):
        return [pl.BlockSpec((c, wk), lambda b, i: (blk(b, i), 0)),
                pl.BlockSpec((c, wk), lambda b, i: (blk(b, i), 0)),
                pl.BlockSpec((c, wk), lambda b, i: (blk(b, i), 0)),
                pl.BlockSpec((None, 1, wk), lambda b, i: (blk(b, i), 0, 0)),
                pl.BlockSpec((c, wv), lambda b, i: (blk(b, i), col_v))]

    return pl.pallas_call(
        _gla_scan_kernel,
        grid=(n_batch, steps),
        in_specs=dir_specs(fwd) + dir_specs(bwd),
        out_specs=[pl.BlockSpec((c, wv), lambda b, i: (fwd(b, i), 0)),
                   pl.BlockSpec((c, wv), lambda b, i: (bwd(b, i), 0))],
        out_shape=[jax.ShapeDtypeStruct((t_rows, wv), F32), jax.ShapeDtypeStruct((t_rows, wv), F32)],
        scratch_shapes=[pltpu.VMEM((2, GLA_HEADS, GLA_DV, GLA_DK), F32)],
        compiler_params=_cparams("arbitrary", "arbitrary"),
        name="gla_scan",
    )(*prep[0:4], proj, *prep[4:8], proj)


def _post_mix(out, x, m2_ref, m3_ref, m4_ref, lg_ref, lb_ref, wr_ref, alpha, x1_ref, h2_ref, lt_ref):
    y = alpha * x + m2_ref[...] * out
    mu = jnp.mean(y, -1, keepdims=True)
    yc = y - mu
    var = jnp.mean(yc * yc, -1, keepdims=True)
    x1 = yc * lax.rsqrt(var + LN_EPS) * lg_ref[...] + lb_ref[...]
    h2 = x1 * (1.0 + m4_ref[...]) + m3_ref[...]
    x1_ref[...] = x1
    n_exp = lt_ref.shape[0]
    h2_hi = h2.astype(BF16)
    h2_lo = (h2 - h2_hi.astype(F32)).astype(BF16)
    h2_ref[...] = h2_hi
    wr = wr_ref[...]
    wr_hi = wr.astype(BF16)
    wr_lo = (wr - wr_hi.astype(F32)).astype(BF16)
    both = _dot_nt(jnp.concatenate([wr_hi, wr_lo], 0), h2_hi)
    lt_ref[...] = both[:n_exp] + both[n_exp:] + _dot_nt(wr_hi, h2_lo)


def _even_out_kernel(al_ref, ac_ref, of_ref, ob_ref, r_ref, xl_ref, xc_ref, m2_ref, m3_ref, m4_ref, ng_ref,
                     wo_ref, lg_ref, lb_ref, wr_ref, x1_ref, h2_ref, lt_ref, *, alpha, n_lat_tiles):
    tile = pl.program_id(0)
    o = of_ref[...] + ob_ref[...]
    r = r_ref[...]
    gate = r * jax.nn.sigmoid(r)
    mixed = [_pick_rows(al_ref, ac_ref, tile, n_lat_tiles).astype(BF16)]
    for h in range(GLA_HEADS):
        vs = slice(h * GLA_DV, (h + 1) * GLA_DV)
        oh = o[:, vs]
        nrm = oh * lax.rsqrt(jnp.mean(oh * oh, -1, keepdims=True) + NORM_EPS) * ng_ref[...]
        mixed.append((nrm * gate[:, vs]).astype(BF16))
    out = _dot(jnp.concatenate(mixed, axis=1), wo_ref[...])
    x = _pick_rows(xl_ref, xc_ref, tile, n_lat_tiles)
    _post_mix(out, x, m2_ref, m3_ref, m4_ref, lg_ref, lb_ref, wr_ref, alpha, x1_ref, h2_ref, lt_ref)


def _post_specs(d, layer, seg, tm, n_exp):
    ins = [_mod_spec(d, layer, 2, seg), _mod_spec(d, layer, 3, seg), _mod_spec(d, layer, 4, seg)]
    tail = [pl.BlockSpec((1, d), lambda i: (0, 0)), pl.BlockSpec((1, d), lambda i: (0, 0)),
            pl.BlockSpec((n_exp, d), lambda i: (0, 0))]
    outs =[pl.BlockSpec((tm, d), lambda i: (i, 0)), pl.BlockSpec((tm, d), lambda i: (i, 0)),
            pl.BlockSpec((n_exp, tm), lambda i: (0, i))]
    return ins, tail, outs


def _post_shapes(t, d, n_exp):
    return [jax.ShapeDtypeStruct((t, d), F32), jax.ShapeDtypeStruct((t, d), BF16),
            jax.ShapeDtypeStruct((n_exp, t), F32)]


def even_out(a_lat, a_ctx, o_f, o_b, proj, col_r, x_lat, x_ctx, mods4, layer, norm_g, w_out_bf16, ln_g, ln_b,
             router_wt, alpha, seg_rows, n_batch, tm=256):
    t = x_lat.shape[0] + x_ctx.shape[0]
    d = x_lat.shape[1]
    na = a_lat.shape[1]
    wv = o_f.shape[1]
    n_exp = router_wt.shape[0]
    seg = _seg_fn(tm, seg_rows, n_batch)
    ins, tail, outs = _post_specs(d, layer, seg, tm, n_exp)
    tile_of = lambda i: i
    return pl.pallas_call(
        functools.partial(_even_out_kernel, alpha=alpha, n_lat_tiles=x_lat.shape[0] // tm),
        grid=(t // tm,),
        in_specs=(_two_source_specs(a_lat, a_ctx, tm, tile_of)
                  + [pl.BlockSpec((tm, wv), lambda i: (i, 0)),
                     pl.BlockSpec((tm, wv), lambda i: (i, 0)),
                     pl.BlockSpec((tm, wv), lambda i: (i, col_r))]
                  + _two_source_specs(x_lat, x_ctx, tm, tile_of) + ins
                  + [pl.BlockSpec((1, GLA_DV), lambda i: (0, 0)),
                     pl.BlockSpec((na + wv, d), lambda i: (0, 0))] + tail),
        out_specs=outs,
        out_shape=_post_shapes(t, d, n_exp),
        compiler_params=_cparams("arbitrary"),
        name="even_out",
    )(a_lat, a_ctx, o_f, o_b, proj, x_lat, x_ctx, mods4, mods4, mods4, norm_g.reshape(1, -1), w_out_bf16,
      ln_g.reshape(1, -1), ln_b.reshape(1, -1), router_wt)


def _odd_out_kernel(y_ref, x_ref, m2_ref, m3_ref, m4_ref, wg_ref, bg_ref, wo_ref,
                    lg_ref, lb_ref, wr_ref, x1_ref, h2_ref, lt_ref, *, alpha):
    g = jax.nn.gelu(y_ref[...], approximate=True)
    z = _dot(g.astype(BF16), wg_ref[...]) + bg_ref[...]
    v = g * jax.nn.sigmoid(z)
    out = _dot(v.astype(BF16), wo_ref[...])
    _post_mix(out, x_ref[...], m2_ref, m3_ref, m4_ref, lg_ref, lb_ref, wr_ref, alpha, x1_ref, h2_ref, lt_ref)


def odd_out(y, x, mods4, layer, w_glu_bf16, b_glu, w_out_bf16, ln_g, ln_b, router_wt,
            alpha, seg_rows, n_batch, tm=256):
    t, w5 = y.shape
    d = x.shape[1]
    n_exp = router_wt.shape[0]
    seg = _seg_fn(tm, seg_rows, n_batch)
    ins, tail, outs = _post_specs(d, layer, seg, tm, n_exp)
    return pl.pallas_call(
        functools.partial(_odd_out_kernel, alpha=alpha),
        grid=(t // tm,),
        in_specs=([pl.BlockSpec((tm, w5), lambda i: (i, 0)), pl.BlockSpec((tm, d), lambda i: (i, 0))] + ins
                  + [pl.BlockSpec((w5, w5), lambda i: (0, 0)),
                     pl.BlockSpec((1, w5), lambda i: (0, 0)),
                     pl.BlockSpec((w5, d), lambda i: (0, 0))] + tail),
        out_specs=outs,
        out_shape=_post_shapes(t, d, n_exp),
        compiler_params=_cparams("arbitrary"),
        name="odd_out",
    )(y, x, mods4, mods4, mods4, w_glu_bf16, b_glu.reshape(1, -1), w_out_bf16,
      ln_g.reshape(1, -1), ln_b.reshape(1, -1), router_wt)


def _route_kernel(lt_ref, rb_ref, idx_ref, w_ref):
    eg = N_EXPERTS // N_GROUPS
    logits = lt_ref[...]
    aff = jax.nn.sigmoid(logits)
    sel = aff + rb_ref[...]
    s = [sel[e:e + 1, :] for e in range(N_EXPERTS)]
    a = [aff[e:e + 1, :] for e in range(N_EXPERTS)]

    def top2_sum(v):
        hi1, lo1 = jnp.maximum(v[0], v[1]), jnp.minimum(v[0], v[1])
        hi2, lo2 = jnp.maximum(v[2], v[3]), jnp.minimum(v[2], v[3])
        return jnp.maximum(hi1, hi2) + jnp.maximum(jnp.minimum(hi1, hi2), jnp.maximum(lo1, lo2))

    best = top2_sum(s[0:eg])
    grp = jnp.zeros_like(best, dtype=jnp.int32)
    for g in range(1, N_GROUPS):
        sc = top2_sum(s[g * eg:(g + 1) * eg])
        better = sc > best
        best = jnp.where(better, sc, best)
        grp = jnp.where(better, g, grp)

    def pick(vals, j):
        out = vals[j]
        for g in range(1, N_GROUPS):
            out = jnp.where(grp == g, vals[g * eg + j], out)
        return out

    sv = [pick(s, j) for j in range(eg)]
    av = [pick(a, j) for j in range(eg)]

    def argmax_first(vals, exclude):
        bi = jnp.zeros_like(grp)
        bv = jnp.where(exclude == 0, -jnp.inf, vals[0]) if exclude is not None else vals[0]
        for j in range(1, eg):
            vj = jnp.where(exclude == j, -jnp.inf, vals[j]) if exclude is not None else vals[j]
            better = vj > bv
            bv = jnp.where(better, vj, bv)
            bi = jnp.where(better, j, bi)
        return bi

    i1 = argmax_first(sv, None)
    i2 = argmax_first(sv, i1)

    def take(vals, i):
        out = vals[0]
        for j in range(1, eg):
            out = jnp.where(i == j, vals[j], out)
        return out

    w1 = take(av, i1)
    w2 = take(av, i2)
    tot = w1 + w2
    idx_ref[0:1, :] = grp * eg + i1
    idx_ref[1:2, :] = grp * eg + i2
    w_ref[0:1, :] = w1 / tot
    w_ref[1:2, :] = w2 / tot


def route(logits_t, router_b, tile=1024):
    n_exp, t = logits_t.shape
    tile = math.gcd(tile, t)
    return pl.pallas_call(
        _route_kernel,
        grid=(t // tile,),
        in_specs=[pl.BlockSpec((n_exp, tile), lambda i: (0, i)),
                  pl.BlockSpec((n_exp, 1), lambda i: (0, 0))],
        out_specs=[pl.BlockSpec((TOP_K, tile), lambda i: (0, i)),
                   pl.BlockSpec((TOP_K, tile), lambda i: (0, i))],
        out_shape=[jax.ShapeDtypeStruct((TOP_K, t), jnp.int32), jax.ShapeDtypeStruct((TOP_K, t), F32)],
        compiler_params=_cparams("arbitrary"),
        name="moe_route",
    )(logits_t, router_b.reshape(n_exp, 1).astype(F32))


def moe_plan(idx, tm):
    t = idx.shape[1]
    n_pair = TOP_K * t
    n_tiles = (n_pair + N_EXPERTS * (tm - 1)) // tm
    e_flat = idx.reshape(-1)
    onehot = (e_flat[:, None] == jnp.arange(N_EXPERTS)[None, :]).astype(jnp.int32)
    running = jnp.cumsum(onehot, axis=0)
    counts = running[-1]
    rank = jnp.sum(onehot * running, 1) - 1
    tiles_per = (counts + tm - 1) // tm
    tile_end = jnp.cumsum(tiles_per)
    n_used = tile_end[-1]
    pstart = (tile_end - tiles_per) * tm
    pos = jnp.sum(onehot * pstart[None, :], 1) + rank
    tile_expert = jnp.minimum(jnp.sum((tile_end[None, :] <= jnp.arange(n_tiles)[:, None]).astype(jnp.int32), 1),
                              N_EXPERTS - 1).astype(jnp.int32)
    gidx = (jnp.arange(n_tiles * tm, dtype=jnp.int32) % t).at[pos].set(
        jnp.arange(n_pair, dtype=jnp.int32) % t, mode="promise_in_bounds", unique_indices=True)
    return gidx, tile_expert, n_used.reshape(1).astype(jnp.int32), pos.astype(jnp.int32)


def _expert_changed(te_ref):
    i = pl.program_id(0)
    return jnp.logical_or(i == 0, te_ref[i] != te_ref[jnp.maximum(i - 1, 0)])


def _ffn_up_kernel(te_ref, nu_ref, xs_ref, wg_ref, wu_ref, hid_ref, wg_bf, wu_bf):
    used = pl.program_id(0) < nu_ref[0]

    @pl.when(jnp.logical_and(used, _expert_changed(te_ref)))
    def _():
        wg_bf[...] = wg_ref[0].astype(BF16)
        wu_bf[...] = wu_ref[0].astype(BF16)

    @pl.when(used)
    def _():
        xs = xs_ref[...]
        g = _dot(xs, wg_bf[...])
        u = _dot(xs, wu_bf[...])
        hid_ref[...] = ((g * jax.nn.sigmoid(g)) * u).astype(BF16)

    @pl.when(jnp.logical_not(used))
    def _():
        hid_ref[...] = jnp.zeros_like(hid_ref)


def _ffn_down_kernel(te_ref, nu_ref, hid_ref, wd_ref, o_ref, wd_bf):
    used = pl.program_id(0) < nu_ref[0]

    @pl.when(jnp.logical_and(used, _expert_changed(te_ref)))
    def _():
        wd_bf[...] = wd_ref[0].astype(BF16)

    @pl.when(used)
    def _():
        o_ref[...] = _dot(hid_ref[...], wd_bf[...]).astype(o_ref.dtype)

    @pl.when(jnp.logical_not(used))
    def _():
        o_ref[...] = jnp.zeros_like(o_ref)


def grouped_ffn(xs, tile_expert, n_used, w_gate, w_up, w_down, layer, tm):
    p, d = xs.shape
    de = w_gate.shape[3]
    n_tiles = p // tm
    wmap = lambda i, te, nu: (layer, te[i], 0, 0)
    row_in = lambda i, te, nu: (jnp.minimum(i, nu[0] - 1), 0)
    row_out = lambda i, te, nu: (i, 0)
    hid = pl.pallas_call(
        _ffn_up_kernel,
        grid_spec=pltpu.PrefetchScalarGridSpec(
            num_scalar_prefetch=2,
            grid=(n_tiles,),
            in_specs=[pl.BlockSpec((tm, d), row_in),
                      pl.BlockSpec((None, 1, d, de), wmap),
                      pl.BlockSpec((None, 1, d, de), wmap)],
            out_specs=pl.BlockSpec((tm, de), row_out),
            scratch_shapes=[pltpu.VMEM((d, de), BF16), pltpu.VMEM((d, de), BF16)]),
        out_shape=jax.ShapeDtypeStruct((p, de), BF16),
        compiler_params=_cparams("arbitrary"),
        name="moe_ffn_up",
    )(tile_expert, n_used, xs, w_gate, w_up)
    return pl.pallas_call(
        _ffn_down_kernel,
        grid_spec=pltpu.PrefetchScalarGridSpec(
            num_scalar_prefetch=2,
            grid=(n_tiles,),
            in_specs=[pl.BlockSpec((tm, de), row_in),
                      pl.BlockSpec((None, 1, de, d), wmap)],
            out_specs=pl.BlockSpec((tm, d), row_out),
            scratch_shapes=[pltpu.VMEM((de, d), BF16)]),
        out_shape=jax.ShapeDtypeStruct((p, d), BF16),
        compiler_params=_cparams("arbitrary"),
        name="moe_ffn_down",
    )(tile_expert, n_used, hid, w_down)


def _final_kernel(x_ref, y0_ref, y1_ref, w_ref, m5_ref, lg_ref, lb_ref, o_ref, *, alpha):
    w = w_ref[...]
    y = w[:, 0:1] * y0_ref[...].astype(F32) + w[:, 1:2] * y1_ref[...].astype(F32)
    z = alpha * x_ref[...] + m5_ref[...] * y
    mu = jnp.mean(z, -1, keepdims=True)
    zc = z - mu
    var = jnp.mean(zc * zc, -1, keepdims=True)
    o_ref[...] = zc * lax.rsqrt(var + LN_EPS) * lg_ref[...] + lb_ref[...]


def final_norm(x1, yg, wts, mods4, layer, ln_g, ln_b, alpha, seg_rows, n_batch, tm=256):
    t, d = x1.shape
    seg = _seg_fn(tm, seg_rows, n_batch)
    row = pl.BlockSpec((tm, d), lambda i: (i, 0))
    vec = pl.BlockSpec((1, d), lambda i: (0, 0))
    return pl.pallas_call(
        functools.partial(_final_kernel, alpha=alpha),
        grid=(t // tm,),
        in_specs=[row, row, pl.BlockSpec((tm, d), lambda i: (i + t // tm, 0)),
                  pl.BlockSpec((tm, TOP_K), lambda i: (i, 0)),
                  _mod_spec(d, layer, 5, seg), vec, vec],
        out_specs=row,
        out_shape=jax.ShapeDtypeStruct((t, d), F32),
        compiler_params=_cparams("arbitrary"),
        name="final_norm",
    )(x1, yg, yg, wts, mods4, ln_g.reshape(1, -1), ln_b.reshape(1, -1))


def moe_block(x1, h2, logits_t, router_b, w_gate, w_up, w_down, mods4, layer, ln_g, ln_b,
              alpha, seg_rows, n_batch, tm=512):
    idx, wts = route(logits_t, router_b)
    gidx, tile_expert, n_used, pos = moe_plan(idx, tm)
    xs = h2.at[gidx].get(mode="promise_in_bounds")
    ys = grouped_ffn(xs, tile_expert, n_used, w_gate, w_up, w_down, layer, tm)
    yg = ys.at[pos].get(mode="promise_in_bounds")
    return final_norm(x1, yg, wts.T, mods4, layer, ln_g, ln_b, alpha, seg_rows, n_batch)


def s5_matrices(lam_re, lam_im, log_dt, b_re, b_im, c_re, c_im, d_skip):
    f32 = F32
    tc = S5_TC
    n_g, n_p = lam_re.shape[1], lam_re.shape[2]
    n_c = b_re.shape[-1]
    nb = 128 // n_c
    n_q = n_g // nb
    lr, li = lam_re.astype(f32), lam_im.astype(f32)
    dt = jnp.exp(log_dt.astype(f32))[..., None]

    def powers(jvals):
        j = jnp.asarray(np.asarray(jvals, np.float32))[:, None, None, None]
        mag = jnp.exp(lr * dt * j)
        return mag * jnp.cos(li * dt * j), mag * jnp.sin(li * dt * j)

    up = np.arange(tc)
    pw_re, pw_im = powers(np.arange(tc + 1))
    lb_re, lb_im = pw_re[1], pw_im[1]
    den = lr * lr + li * li
    fr = ((lb_re - 1.0) * lr + lb_im * li) / den
    fi = (lb_im * lr - (lb_re - 1.0) * li) / den
    br, bi = b_re.astype(f32), b_im.astype(f32)
    bb_re = fr[..., None] * br - fi[..., None] * bi
    bb_im = fr[..., None] * bi + fi[..., None] * br
    cr, ci = c_re.astype(f32), c_im.astype(f32)

    def times_b(p_re, p_im):
        return (p_re[..., None] * bb_re[None] - p_im[..., None] * bb_im[None],
                p_re[..., None] * bb_im[None] + p_im[..., None] * bb_re[None])

    e_re, e_im = times_b(pw_re, pw_im)
    kmat = jnp.sum(cr[None, :, :, :, :, None] * e_re[:, :, :, None, :, :]
                   - ci[None, :, :, :, :, None] * e_im[:, :, :, None, :, :], axis=4)
    def lag_slab(k_dir):
        return k_dir.reshape(tc, n_q, nb, n_c, n_c).transpose(1, 0, 4, 2, 3).reshape(n_q, tc, n_c, nb * n_c)
    skip = (jnp.eye(n_c, dtype=f32)[None, None, :, None, :]
            * d_skip.astype(f32).reshape(n_q, nb, n_c)[:, None, None, :, :]).reshape(n_q, 1, n_c, nb * n_c)
    k_c = jnp.concatenate([lag_slab(kmat[:tc, 0]), lag_slab(kmat[:tc, 1]), skip], 1)

    def w_slab(e):
        return e.reshape(tc, n_q, nb, n_p, n_c).transpose(1, 0, 4, 2, 3).reshape(n_q, tc, n_c, nb * n_p)
    ef_re, ef_im = times_b(*powers(tc - 1 - up))
    w_c = jnp.stack([w_slab(ef_re[:, 0]), w_slab(ef_im[:, 0]),
                     w_slab(e_re[:tc, 1]), w_slab(e_im[:tc, 1])], 2)

    def v_slabs(d, p_re, p_im):
        f_re = cr[d][None] * p_re[:, :, None, :] - ci[d][None] * p_im[:, :, None, :]
        f_im = cr[d][None] * p_im[:, :, None, :] + ci[d][None] * p_re[:, :, None, :]
        slab = lambda m: m.reshape(tc, n_q, nb, n_c, n_p).transpose(1, 0, 3, 2, 4).reshape(n_q, tc, n_c, nb * n_p)
        return slab(f_re), slab(-f_im)
    pb_re, pb_im = powers(tc - up)
    vt_c = jnp.stack(v_slabs(0, pw_re[1:, 0], pw_im[1:, 0]) + v_slabs(1, pb_re[:, 1], pb_im[:, 1]), 2)
    dec = lambda m: m.reshape(1, n_g * n_p // 128, 1, 128)
    decay = jnp.concatenate([dec(pw_re[tc, 0]), dec(pw_im[tc, 0]), dec(pw_re[tc, 1]), dec(pw_im[tc, 1])], 0)
    return k_c, w_c, vt_c, decay


def _s5_chunk_rows(ref, n):
    return jnp.concatenate([ref[pl.ds(s, n, stride=S5_TC), :] for s in range(S5_TC)], axis=1).astype(BF16)


def _s5_expand(slab, group_lanes):
    rows = 128
    tiled = jnp.concatenate([slab] * (rows // slab.shape[0]), axis=0)
    r = lax.broadcasted_iota(jnp.int32, tiled.shape, 0) // S5_CH
    l = lax.broadcasted_iota(jnp.int32, tiled.shape, 1) // group_lanes
    return jnp.where(r == l, tiled, 0.0).astype(BF16)


def _s5_in_kernel(ul_ref, uc_ref, wc_ref, fr_ref, fi_ref, br_ref, bi_ref, w_ref, *, n_lat, n_ctx, n_batch):
    b = pl.program_id(1)
    n_plane = wc_ref.shape[1]
    st = wc_ref.shape[3]

    @pl.when(b == 0)
    def _():
        for s in range(S5_TC):
            for i in range(n_plane):
                w_ref[s * 128:(s + 1) * 128, i * st:(i + 1) * st] = _s5_expand(wc_ref[s, i], S5_P)

    w_lat = _dot(_s5_chunk_rows(ul_ref, n_lat), w_ref[...])
    w_ctx = _dot(_s5_chunk_rows(uc_ref, n_ctx), w_ref[...])
    nv = fr_ref.shape[0]
    for i, ref in enumerate((fr_ref, fi_ref, br_ref, bi_ref)):
        for c in range(nv):
            lanes = slice((i * nv + c) * 128, (i * nv + c + 1) * 128)
            ref[c, pl.ds(b, n_ctx, stride=n_batch), :] = w_ctx[:, lanes]
            ref[c, pl.ds(n_ctx * n_batch + b, n_lat, stride=n_batch), :] = w_lat[:, lanes]
            ref[c, pl.ds((n_ctx + n_lat) * n_batch + b, n_ctx, stride=n_batch), :] = w_ctx[:, lanes]


def _s5_scan_kernel(wfr, wfi, wbr, wbi, dec_ref, xfr, xfi, xbr, xbi, *, n_tiles):
    nv = wfr.shape[0]
    low = lax.broadcasted_iota(jnp.int32, (nv, 8, 128), 1) < 4
    a_fr, a_fi, a_br, a_bi = dec_ref[0], dec_ref[1], dec_ref[2], dec_ref[3]

    def half_step(s_re, s_im, a_re, a_im, w_re, w_im):
        return a_re * s_re - a_im * s_im + w_re, a_re * s_im + a_im * s_re + w_im

    def one_dir(w_re_ref, w_im_ref, x_re_ref, x_im_ref, row0, s_re, s_im, a_re, a_im, first_low):
        first = low if first_low else jnp.logical_not(low)
        wt_re, wt_im = w_re_ref[:, pl.ds(row0, 8), :], w_im_ref[:, pl.ds(row0, 8), :]
        wr_re, wr_im = pltpu.roll(wt_re, 4, 1), pltpu.roll(wt_im, 4, 1)
        mid_re, mid_im = half_step(s_re, s_im, a_re, a_im, wr_re, wr_im)
        x_re_ref[:, pl.ds(row0, 8), :] = jnp.where(first, s_re, mid_re)
        x_im_ref[:, pl.ds(row0, 8), :] = jnp.where(first, s_im, mid_im)
        m_re = jnp.where(first, pltpu.roll(mid_re, 4, 1), mid_re)
        m_im = jnp.where(first, pltpu.roll(mid_im, 4, 1), mid_im)
        w2_re = jnp.where(first, wr_re, wt_re)
        w2_im = jnp.where(first, wr_im, wt_im)
        return half_step(m_re, m_im, a_re, a_im, w2_re, w2_im)

    def body(i, carry):
        f_re, f_im, b_re, b_im = carry
        rf = pl.multiple_of(i * 8, 8)
        rb = pl.multiple_of((n_tiles - 1 - i) * 8, 8)
        f_re, f_im = one_dir(wfr, wfi, xfr, xfi, rf, f_re, f_im, a_fr, a_fi, True)
        b_re, b_im = one_dir(wbr, wbi, xbr, xbi, rb, b_re, b_im, a_br, a_bi, False)
        return f_re, f_im, b_re, b_im

    z = jnp.zeros((nv, 8, 128), F32)
    lax.fori_loop(0, n_tiles, body, (z, z, z, z))


def _s5_out_kernel(ul_ref, fr_ref, fi_ref, br_ref, bi_ref, kc_ref, vc_ref, y_ref, mt_ref, vt_ref,
                   *, n_lat, n_ctx, n_batch):
    b = pl.program_id(1)
    tc = S5_TC

    @pl.when(b == 0)
    def _():
        lag = [_s5_expand(kc_ref[j], S5_CH) for j in range(2 * tc)]
        diag = _s5_expand(kc_ref[0] + kc_ref[tc] + kc_ref[2 * tc], S5_CH)
        for s in range(tc):
            for t in range(tc):
                blk = diag if s == t else (lag[t - s] if t > s else lag[tc + s - t])
                mt_ref[s * 128:(s + 1) * 128, t * 128:(t + 1) * 128] = blk
        for t in range(tc):
            for i in range(vt_ref.shape[0]):
                vt_ref[i, t * 128:(t + 1) * 128, :] = _s5_expand(vc_ref[t, i], S5_P)

    y = _dot(_s5_chunk_rows(ul_ref, n_lat), mt_ref[...])
    row0 = n_ctx * n_batch + b
    for i, ref in enumerate((fr_ref, fi_ref, br_ref, bi_ref)):
        xs = jnp.concatenate([ref[c, pl.ds(row0, n_lat, stride=n_batch), :] for c in range(ref.shape[0])], 1)
        y = y + _dot_nt(xs.astype(BF16), vt_ref[i])
    for s in range(S5_TC):
        y_ref[pl.ds(s, n_lat, stride=S5_TC), :] = y[:, s * 128:(s + 1) * 128]


def s5_bidir(u, mats, n_batch, l_lat, l_ctx):
    assert n_batch == 4, "the chunk scan packs two chunks of 4 batch rows per 8-sublane tile"
    k_c, w_c, vt_c, decay = mats
    tc = S5_TC
    wd = u.shape[1]
    n_q = wd // 128
    lane_q = tc * 128
    st_q = (128 // S5_CH) * S5_P
    n_lat, n_ctx = l_lat // tc, l_ctx // tc
    nk = n_lat + 2 * n_ctx
    assert nk % 2 == 0
    rows = nk * n_batch
    ctx0 = (n_batch * l_lat) // l_ctx
    dims = dict(n_lat=n_lat, n_ctx=n_ctx, n_batch=n_batch)

    nv = st_q // 128
    plane = jax.ShapeDtypeStruct((n_q * nv, rows, 128), F32)
    plane_spec = pl.BlockSpec((nv, rows, 128), lambda q, b: (q, 0, 0))
    ul_spec = pl.BlockSpec((l_lat, 128), lambda q, b: (b, q))
    uc_spec = pl.BlockSpec((l_ctx, 128), lambda q, b: (ctx0 + b, q))
    w_planes = pl.pallas_call(
        functools.partial(_s5_in_kernel, **dims),
        grid=(n_q, n_batch),
        in_specs=[ul_spec, uc_spec, pl.BlockSpec((None,) + w_c.shape[1:], lambda q, b: (q, 0, 0, 0, 0))],
        out_specs=[plane_spec] * 4,
        out_shape=[plane] * 4,
        scratch_shapes=[pltpu.VMEM((lane_q, 4 * st_q), BF16)],
        compiler_params=_cparams("arbitrary", "arbitrary"),
        name="s5_chunk_in",
    )(u, u, w_c)

    blk = pl.BlockSpec((nv, rows, 128), lambda j: (j, 0, 0))
    x_planes = pl.pallas_call(
        functools.partial(_s5_scan_kernel, n_tiles=rows // 8),
        grid=(n_q,),
        in_specs=[blk] * 4 + [pl.BlockSpec((4, nv, 1, 128), lambda j: (0, j, 0, 0))],
        out_specs=[blk] * 4,
        out_shape=[plane] * 4,
        compiler_params=_cparams("arbitrary"),
        name="s5_chunk_scan",
    )(*w_planes, decay)

    return pl.pallas_call(
        functools.partial(_s5_out_kernel, **dims),
        grid=(n_q, n_batch),
        in_specs=[ul_spec] + [plane_spec] * 4
                 + [pl.BlockSpec((None,) + k_c.shape[1:], lambda q, b: (q, 0, 0, 0)),
                    pl.BlockSpec((None,) + vt_c.shape[1:], lambda q, b: (q, 0, 0, 0, 0))],
        out_specs=pl.BlockSpec((l_lat, 128), lambda q, b: (b, q)),
        out_shape=jax.ShapeDtypeStruct((n_batch * l_lat, wd), F32),
        scratch_shapes=[pltpu.VMEM((lane_q, lane_q), BF16), pltpu.VMEM((4, lane_q, st_q), BF16)],
        compiler_params=_cparams("arbitrary", "arbitrary"),
        name="s5_chunk_out",
    )(u, *x_planes, k_c, vt_c)


def kernel(x, c, ctx, c_ctx, ada_w, ada_b, ln_mix_g, ln_mix_b, ln_ffn_g, ln_ffn_b, ev_w_in, ev_gate_w2,
           ev_gate_b, ev_rpb, ev_norm_g, ev_w_out, od_w_in, od_lam_re, od_lam_im, od_log_dt, od_b_re,
           od_b_im, od_c_re, od_c_im, od_d, od_w_glu, od_b_glu, od_w_out, router_w, router_b,
           moe_w_gate, moe_w_up, moe_w_down):
    n_batch, l_lat, d = x.shape
    l_ctx = ctx.shape[1]
    depth = ada_w.shape[0]
    assert depth == 2, "one even (NA + GLA) layer followed by one odd (S5) layer"
    alpha = (2.0 * depth) ** 0.25
    n_lat = n_batch * l_lat

    cvec = jnp.concatenate([c, c_ctx[None], jnp.zeros((8 - n_batch - 1, d), F32)], 0)
    mods = compute_mods(cvec, ada_w, ada_b)
    mods4 = mods.reshape(depth, 8, 1, N_MOD * d)
    x_lat, x_ctx = x.reshape(n_lat, d), ctx.reshape(n_batch * l_ctx, d)
    router_wt = router_w.T.astype(F32)

    na_w = NA_HEADS * NA_DH
    wk = GLA_HEADS * GLA_DK
    wv = GLA_HEADS * GLA_DV
    ev_in = ev_w_in.shape[2]
    pad = (-ev_in) % 256
    w_in = jnp.pad(ev_w_in[0], ((0, 0), (0, pad))).astype(BF16)
    proj = mod_matmul(x_lat, x_ctx, mods4, 0, w_in, l_lat, n_batch, tm=512, tn=(ev_in + pad) // 2)
    a_lat, a_ctx = na_attention(proj, na_bias_table(ev_rpb[0], l_lat // GRID_W), n_batch, l_lat, l_ctx)
    col_lr = (3 * na_w + 2 * wk + 2 * wv) // 128
    g2 = jnp.zeros((2, 128, wk), F32)
    g2 = g2.at[0, 0:GLA_RANK].set(ev_gate_w2[0, 0]).at[1, GLA_RANK:2 * GLA_RANK].set(ev_gate_w2[0, 1])
    o_f, o_b = gla_bidir(proj, g2.astype(BF16), ev_gate_b[0].reshape(2, 1, wk), rope_tables(l_lat),
                         n_batch, l_lat, l_ctx,
                         col_q=3 * na_w // wk, col_k=(3 * na_w + wk) // wk,
                         col_v=(3 * na_w + 2 * wk) // wv, col_lr=col_lr)
    x1, h2, logits_t = even_out(a_lat, a_ctx, o_f, o_b, proj, (3 * na_w + 2 * wk + wv) // wv, x_lat, x_ctx,
                                mods4, 0, ev_norm_g[0], ev_w_out[0].astype(BF16), ln_mix_g[0], ln_mix_b[0],
                                router_wt, alpha, l_lat, n_batch)
    w_gate, w_up, w_down = moe_w_gate, moe_w_up, moe_w_down
    rows = moe_block(x1, h2, logits_t, router_b, w_gate, w_up, w_down, mods4, 0,
                     ln_ffn_g[0], ln_ffn_b[0], alpha, l_lat, n_batch)

    u = mod_matmul(rows, None, mods4, 1, od_w_in[0].astype(BF16), l_lat, n_batch)
    mats = s5_matrices(od_lam_re[0], od_lam_im[0], od_log_dt[0], od_b_re[0], od_b_im[0],
                       od_c_re[0], od_c_im[0], od_d[0])
    y5 = s5_bidir(u, mats, n_batch, l_lat, l_ctx)
    x1, h2, logits_t = odd_out(y5, rows, mods4, 1, od_w_glu[0].astype(BF16), od_b_glu[0],
                               od_w_out[0].astype(BF16), ln_mix_g[1], ln_mix_b[1], router_wt,
                               alpha, l_lat, n_batch)
    out = moe_block(x1, h2, logits_t, router_b, w_gate, w_up, w_down, mods4, 1,
                    ln_ffn_g[1], ln_ffn_b[1], alpha, l_lat, n_batch)
    return out.reshape(n_batch, l_lat, d)
```

```python
import functools
import math

import numpy as np
import jax
import jax.numpy as jnp
from jax import lax
from jax.experimental import pallas as pl
from jax.experimental.pallas import tpu as pltpu

F32 = jnp.float32
BF16 = jnp.bfloat16
HIGHEST = lax.Precision.HIGHEST

N_MOD = 6
LN_EPS = 1e-5
NORM_EPS = 1e-6

GRID_W = 64
NA_HEADS = 8
NA_DH = 128
NA_KR = 8
NA_KC = 16

GLA_HEADS = 4
GLA_DK = 128
GLA_DV = 256
GLA_RANK = 16
GLA_TAU = 16.0
GLA_CHUNK = 64
ROPE_BASE = 10000.0

S5_CH = 16
S5_P = 64
S5_TC = 16

N_EXPERTS = 16
N_GROUPS = 4
TOP_K = 2

VMEM_LIMIT = 56 * 1024 * 1024
NEG_BIG = -1e30


def _cparams(*sem):
    return pltpu.CompilerParams(dimension_semantics=sem, vmem_limit_bytes=VMEM_LIMIT)


def _dot(a, b, precision=None):
    return jnp.dot(a, b, preferred_element_type=F32, precision=precision)


def _dot_nt(a, b, precision=None):
    return lax.dot_general(a, b, (((1,), (1,)), ((), ())), preferred_element_type=F32, precision=precision)


def _dot_tn(a, b):
    return lax.dot_general(a, b, (((0,), (0,)), ((), ())), preferred_element_type=F32)


def _mods_kernel(s_ref, w_ref, b_ref, o_ref):
    s = s_ref[...]
    s = s * jax.nn.sigmoid(s)
    o_ref[0] = _dot(s, w_ref[0], HIGHEST) + b_ref[0]


def compute_mods(cvec, ada_w, ada_b, tn=1024):
    n_layer, d, n = ada_w.shape
    tn = math.gcd(tn, n)
    return pl.pallas_call(
        _mods_kernel,
        grid=(n_layer, n // tn),
        in_specs=[pl.BlockSpec((8, d), lambda l, j: (0, 0)),
                  pl.BlockSpec((1, d, tn), lambda l, j: (l, 0, j)),
                  pl.BlockSpec((1, 1, tn), lambda l, j: (l, 0, j))],
        out_specs=pl.BlockSpec((1, 8, tn), lambda l, j: (l, 0, j)),
        out_shape=jax.ShapeDtypeStruct((n_layer, 8, n), F32),
        compiler_params=_cparams("arbitrary", "arbitrary"),
        name="ada_mods",
    )(cvec, ada_w, ada_b.reshape(n_layer, 1, n))


def _mod_spec(d, layer, which, seg_of_tile):
    return pl.BlockSpec((None, None, 1, d), lambda i, *_: (layer, seg_of_tile(i), 0, which))


def _seg_fn(tm, seg_rows, n_batch):
    return lambda i: jnp.minimum((i * tm) // seg_rows, n_batch)


def _two_source_specs(lat, ctx, tm, tile_of):
    n_lat_tiles = lat.shape[0] // tm
    d = lat.shape[1]
    return [pl.BlockSpec((tm, d), lambda *g: (jnp.minimum(tile_of(*g), n_lat_tiles - 1), 0)),
            pl.BlockSpec((tm, d), lambda *g: (jnp.maximum(tile_of(*g) - n_lat_tiles, 0), 0))]


def _pick_rows(lat_ref, ctx_ref, tile, n_lat_tiles):
    return jnp.where(tile < n_lat_tiles, lat_ref[...], ctx_ref[...])


def _modmm_kernel(xl_ref, xc_ref, s1_ref, s0_ref, w_ref, o_ref, *, n_lat_tiles):
    x = _pick_rows(xl_ref, xc_ref, pl.program_id(1), n_lat_tiles)
    h = x * (1.0 + s1_ref[...]) + s0_ref[...]
    o_ref[...] = _dot(h.astype(BF16), w_ref[...])


def mod_matmul(x_lat, x_ctx, mods4, layer, w_bf16, seg_rows, n_batch, tm=256, tn=None):
    if x_ctx is None:
        t, x_ctx = x_lat.shape[0], x_lat
    else:
        t = x_lat.shape[0] + x_ctx.shape[0]
    d = x_lat.shape[1]
    n = w_bf16.shape[1]
    tn = n if tn is None else tn
    seg = _seg_fn(tm, seg_rows, n_batch)
    return pl.pallas_call(
        functools.partial(_modmm_kernel, n_lat_tiles=x_lat.shape[0] // tm),
        grid=(n // tn, t // tm),
        in_specs=_two_source_specs(x_lat, x_ctx, tm, lambda j, i: i) + [
            pl.BlockSpec((None, None, 1, d), lambda j, i: (layer, seg(i), 0, 1)),
            pl.BlockSpec((None, None, 1, d), lambda j, i: (layer, seg(i), 0, 0)),
            pl.BlockSpec((d, tn), lambda j, i: (0, j))],
        out_specs=pl.BlockSpec((tm, tn), lambda j, i: (i, j)),
        out_shape=jax.ShapeDtypeStruct((t, n), F32),
        compiler_params=_cparams("arbitrary", "arbitrary"),
        name="mod_matmul",
    )(x_lat, x_ctx, mods4, mods4, w_bf16)


NA_RB = 4
NA_BAND = NA_RB + NA_KR - 1


def _na_row_start(r, rows):
    return min(max(r - NA_KR // 2, 0), rows - NA_KR)


def na_bias_table(rpb, rows):
    w = GRID_W
    q = np.arange(w)
    kc = np.arange(w)
    win0 = np.clip(q - NA_KC // 2, 0, w - NA_KC)
    ok = (kc[None, :] >= win0[:, None]) & (kc[None, :] < win0[:, None] + NA_KC)
    dc = np.clip(kc[None, :] - q[:, None] + NA_KC - 1, 0, 2 * NA_KC - 2)
    pick = ((dc[None] == np.arange(2 * NA_KC - 1)[:, None, None]) & ok[None]).astype(np.float32)
    colb = jnp.einsum("hrd,dqk->hrqk", rpb.astype(F32), jnp.asarray(pick), precision=HIGHEST)
    colb = jnp.where(ok[None, None], colb, NEG_BIG)
    neg = jnp.full((rpb.shape[0], w, w), NEG_BIG, F32)

    def block(r0):
        band0 = min(max(r0 - NA_KR // 2, 0), rows - NA_BAND)
        out = []
        for r in range(r0, r0 + NA_RB):
            rs = _na_row_start(r, rows)
            first = rs - r + NA_KR - 1
            cols = [neg] * (rs - band0) + [colb[:, first + j] for j in range(NA_KR)]
            cols += [neg] * (NA_BAND - len(cols))
            out.append(jnp.concatenate(cols, -1))
        return jnp.concatenate(out, 1)

    return jnp.stack([block(0), block(NA_RB), block(rows - NA_RB)], 1)


def _na_kernel(q_ref, k_ref, v_ref, qc_ref, kc_ref, vc_ref, bias_ref, o_ref, oc_ref, kbf, vbf, *, rows):
    w = GRID_W
    n_blk = rows // NA_RB
    scale = NA_DH ** -0.5
    kbf[...] = k_ref[...].astype(BF16)
    vbf[...] = v_ref[...].astype(BF16)
    kc = kc_ref[...].astype(BF16)
    vc = vc_ref[...].astype(BF16)

    def body(pair, carry):
        blocks = (2 * pair, 2 * pair + 1)
        q0, scores = [], []
        for i in blocks:
            r0 = i * NA_RB
            band0 = jnp.clip(r0 - NA_KR // 2, 0, rows - NA_BAND)
            variant = jnp.where(i == 0, 0, jnp.where(i == n_blk - 1, 2, 1))
            q0.append(pl.multiple_of(r0 * w, NA_RB * w))
            k0 = pl.multiple_of(band0 * w, w)
            q = (q_ref[pl.ds(q0[-1], NA_RB * w), :] * scale).astype(BF16)
            s_loc = _dot_nt(q, kbf[pl.ds(k0, NA_BAND * w), :]) + bias_ref[variant]
            scores.append((s_loc, _dot_nt(q, kc), k0))
        probs = []
        for s_loc, s_ctx, k0 in scores:
            m = jnp.maximum(jnp.max(s_loc, -1, keepdims=True), jnp.max(s_ctx, -1, keepdims=True))
            p_loc = jnp.exp(s_loc - m)
            p_ctx = jnp.exp(s_ctx - m)
            den = jnp.sum(p_loc, -1, keepdims=True) + jnp.sum(p_ctx, -1, keepdims=True)
            probs.append((p_loc.astype(BF16), p_ctx.astype(BF16), den, k0))
        for q_start, (p_loc, p_ctx, den, k0) in zip(q0, probs):
            o = _dot(p_loc, vbf[pl.ds(k0, NA_BAND * w), :]) + _dot(p_ctx, vc)
            o_ref[pl.ds(q_start, NA_RB * w), :] = o / den
        return carry

    lax.fori_loop(0, n_blk // 2, body, 0)

    qc = (qc_ref[...] * scale).astype(BF16)
    s = _dot_nt(qc, kc)
    p = jnp.exp(s - jnp.max(s, -1, keepdims=True))
    oc_ref[...] = _dot(p.astype(BF16), vc) / jnp.sum(p, -1, keepdims=True)


def na_attention(proj, bias_tab, n_batch, l_lat, l_ctx):
    h = NA_HEADS
    dh = NA_DH
    rows = l_lat // GRID_W
    ctx0 = (n_batch * l_lat) // l_ctx
    return pl.pallas_call(
        functools.partial(_na_kernel, rows=rows),
        grid=(n_batch, h),
        in_specs=[pl.BlockSpec((l_lat, dh), lambda b, hh: (b, hh)),
                  pl.BlockSpec((l_lat, dh), lambda b, hh: (b, h + hh)),
                  pl.BlockSpec((l_lat, dh), lambda b, hh: (b, 2 * h + hh)),
                  pl.BlockSpec((l_ctx, dh), lambda b, hh: (ctx0 + b, hh)),
                  pl.BlockSpec((l_ctx, dh), lambda b, hh: (ctx0 + b, h + hh)),
                  pl.BlockSpec((l_ctx, dh), lambda b, hh: (ctx0 + b, 2 * h + hh)),
                  pl.BlockSpec((None,) + bias_tab.shape[1:], lambda b, hh: (hh, 0, 0, 0))],
        out_specs=[pl.BlockSpec((l_lat, dh), lambda b, hh: (b, hh)),
                   pl.BlockSpec((l_ctx, dh), lambda b, hh: (b, hh))],
        out_shape=[jax.ShapeDtypeStruct((n_batch * l_lat, h * dh), F32),
                   jax.ShapeDtypeStruct((n_batch * l_ctx, h * dh), F32)],
        scratch_shapes=[pltpu.VMEM((l_lat, dh), BF16), pltpu.VMEM((l_lat, dh), BF16)],
        compiler_params=_cparams("arbitrary", "arbitrary"),
        name="na_attention",
    )(proj, proj, proj, proj, proj, proj, bias_tab)


def rope_tables(l_lat):
    half = GLA_DK // 2
    nf = half // 2
    inv = ROPE_BASE ** (-np.arange(nf, dtype=np.float64) / nf)
    t = np.arange(l_lat)
    lane = np.arange(GLA_DK)
    pos = np.where(lane[None, :] < half, (t // GRID_W)[:, None], (t % GRID_W)[:, None]).astype(np.float64)
    ang = pos * inv[lane % nf][None, :]
    first = (lane % half) < nf
    cos = np.cos(ang)
    sin_a = np.where(first[None, :], -np.sin(ang), 0.0)
    sin_b = np.where(first[None, :], 0.0, np.sin(ang))
    return jnp.asarray(cos, F32), jnp.asarray(sin_a, F32), jnp.asarray(sin_b, F32)


GLA_PREP_CHUNKS = 8


def _gla_prep_kernel(q_ref, k_ref, lr_ref, cos_ref, sa_ref, sb_ref, g2_ref, gb_ref,
                     qdf, kdf, krf, elf, qdb, kdb, krb, elb, *, n_lat_tiles):
    c = GLA_CHUNK
    nch = GLA_PREP_CHUNKS
    nf = GLA_DK // 4
    gscale = GLA_DK ** -0.5
    wk = GLA_HEADS * GLA_DK
    is_lat = pl.program_id(0) < n_lat_tiles
    cos = jnp.where(is_lat, cos_ref[...], 1.0)
    sa = jnp.where(is_lat, sa_ref[...], 0.0)
    sb = jnp.where(is_lat, sb_ref[...], 0.0)

    def rope(x):
        return x * cos + pltpu.roll(x, GLA_DK - nf, 1) * sa + pltpu.roll(x, nf, 1) * sb

    qs, ks_ = [], []
    for h in range(GLA_HEADS):
        hs = slice(h * GLA_DK, (h + 1) * GLA_DK)
        qs.append(rope(q_ref[:, hs]) * gscale)
        ks_.append(rope(k_ref[:, hs]))

    lr = lr_ref[...].astype(BF16)
    row = lax.broadcasted_iota(jnp.int32, (c, c), 0)
    col = lax.broadcasted_iota(jnp.int32, (c, c), 1)
    for d, (qd, kd, kr, el) in enumerate(((qdf, kdf, krf, elf), (qdb, kdb, krb, elb))):
        reverse = d == 1
        tri = ((col >= row) if reverse else (col <= row)).astype(BF16)
        z = _dot(lr, g2_ref[d]) + gb_ref[d]
        g = (jnp.minimum(z, 0.0) - jnp.log1p(jnp.exp(-jnp.abs(z)))) * (1.0 / GLA_TAU)
        g_hi = g.astype(BF16)
        r1 = g - g_hi.astype(F32)
        g_mid = r1.astype(BF16)
        g_lo = (r1 - g_mid.astype(F32)).astype(BF16)
        parts = []
        for ci in range(nch):
            rs = slice(ci * c, (ci + 1) * c)
            parts.append(_dot(tri, g_hi[rs]) + _dot(tri, g_mid[rs]) + _dot(tri, g_lo[rs]))
        b3 = jnp.concatenate(parts, 0).reshape(nch, c, wk)
        bl3 = b3[:, 0:1, :] if reverse else b3[:, c - 1:c, :]
        el[...] = jnp.exp(bl3)
        e_b = jnp.exp(b3).reshape(nch * c, wk)
        e_nb = jnp.exp(-b3).reshape(nch * c, wk)
        e_rem = jnp.exp(bl3 - b3).reshape(nch * c, wk)
        for h in range(GLA_HEADS):
            hs = slice(h * GLA_DK, (h + 1) * GLA_DK)
            qd[:, hs] = (qs[h] * e_b[:, hs]).astype(BF16)
            kd[:, hs] = (ks_[h] * e_nb[:, hs]).astype(BF16)
            kr[:, hs] = (ks_[h] * e_rem[:, hs]).astype(BF16)


def _gla_scan_kernel(qdf, kdf, krf, elf, vf, qdb, kdb, krb, elb, vb, of_ref, ob_ref, st_ref):
    @pl.when(pl.program_id(1) == 0)
    def _():
        st_ref[...] = jnp.zeros_like(st_ref)

    c = GLA_CHUNK
    row = lax.broadcasted_iota(jnp.int32, (c, c), 0)
    col = lax.broadcasted_iota(jnp.int32, (c, c), 1)
    dirs = ((qdf, kdf, krf, elf, vf, of_ref, col <= row), (qdb, kdb, krb, elb, vb, ob_ref, col >= row))
    chains = [(d, h) for d in range(2) for h in range(GLA_HEADS)]
    hs = lambda h: slice(h * GLA_DK, (h + 1) * GLA_DK)
    vs = lambda h: slice(h * GLA_DV, (h + 1) * GLA_DV)
    q_dec = [dirs[d][0][:, hs(h)] for d, h in chains]
    v_bf = [dirs[d][4][:, vs(h)].astype(BF16) for d, h in chains]
    att = [jnp.where(dirs[d][6], _dot_nt(q_dec[n], dirs[d][1][:, hs(h)]), 0.0).astype(BF16)
           for n, (d, h) in enumerate(chains)]
    state = [st_ref[d, h] for d, h in chains]
    for n, (d, h) in enumerate(chains):
        dirs[d][5][:, vs(h)] = _dot(att[n], v_bf[n]) + _dot_nt(q_dec[n], state[n].astype(BF16))
    for n, (d, h) in enumerate(chains):
        st_ref[d, h] = state[n] * dirs[d][3][:, hs(h)] + _dot_tn(v_bf[n], dirs[d][2][:, hs(h)])


def gla_bidir(proj, g2, gb, tables, n_batch, l_lat, l_ctx, col_q, col_k, col_v, col_lr):
    c = GLA_CHUNK
    nc = l_ctx // c
    nl = l_lat // c
    nz = nl + 2 * nc
    steps = nl + nc
    wk = GLA_HEADS * GLA_DK
    wv = GLA_HEADS * GLA_DV
    t_rows = n_batch * (l_lat + l_ctx)

    tp = GLA_PREP_CHUNKS * c
    lat_tiles = l_lat // tp
    n_lat_tiles = n_batch * lat_tiles
    cos, sa, sb = tables
    tab = pl.BlockSpec((tp, GLA_DK), lambda i: (jnp.where(i < n_lat_tiles, i % lat_tiles, 0), 0))
    row_bf = jax.ShapeDtypeStruct((t_rows, wk), BF16)
    last = jax.ShapeDtypeStruct((t_rows // c, 1, wk), F32)
    row_spec = pl.BlockSpec((tp, wk), lambda i: (i, 0))
    last_spec = pl.BlockSpec((GLA_PREP_CHUNKS, 1, wk), lambda i: (i, 0, 0))
    prep = pl.pallas_call(
        functools.partial(_gla_prep_kernel, n_lat_tiles=n_lat_tiles),
        grid=(t_rows // tp,),
        in_specs=[pl.BlockSpec((tp, wk), lambda i: (i, col_q)),
                  pl.BlockSpec((tp, wk), lambda i: (i, col_k)),
                  pl.BlockSpec((tp, 128), lambda i: (i, col_lr)),
                  tab, tab, tab,
                  pl.BlockSpec((2, 128, wk), lambda i: (0, 0, 0)),
                  pl.BlockSpec((2, 1, wk), lambda i: (0, 0, 0))],
        out_specs=[row_spec, row_spec, row_spec, last_spec] * 2,
        out_shape=[row_bf, row_bf, row_bf, last] * 2,
        compiler_params=_cparams("arbitrary"),
        name="gla_prep",
    )(proj, proj, proj, cos, sa, sb, g2, gb)

    def zblk(b, j):
        lat = b * nl + (j - nc)
        ctx = n_batch * nl + b * nc + jnp.where(j < nc, j, j - nc - nl)
        return jnp.where((j >= nc) & (j < nc + nl), lat, ctx)

    fwd = lambda b, i: zblk(b, i)
    bwd = lambda b, i: zblk(b, nz - 1 - i)

    def dir_specs(blk):
        return [pl.BlockSpec((c, wk), lambda b, i: (blk(b, i), 0)),
                pl.BlockSpec((c, wk), lambda b, i: (blk(b, i), 0)),
                pl.BlockSpec((c, wk), lambda b, i: (blk(b, i), 0)),
                pl.BlockSpec((None, 1, wk), lambda b, i: (blk(b, i), 0, 0)),
                pl.BlockSpec((c, wv), lambda b, i: (blk(b, i), col_v))]

    return pl.pallas_call(
        _gla_scan_kernel,
        grid=(n_batch, steps),
        in_specs=dir_specs(fwd) + dir_specs(bwd),
        out_specs=[pl.BlockSpec((c, wv), lambda b, i: (fwd(b, i), 0)),
                   pl.BlockSpec((c, wv), lambda b, i: (bwd(b, i), 0))],
        out_shape=[jax.ShapeDtypeStruct((t_rows, wv), F32), jax.ShapeDtypeStruct((t_rows, wv), F32)],
        scratch_shapes=[pltpu.VMEM((2, GLA_HEADS, GLA_DV, GLA_DK), F32)],
        compiler_params=_cparams("arbitrary", "arbitrary"),
        name="gla_scan",
    )(*prep[0:4], proj, *prep[4:8], proj)


def _post_mix(out, x, m2_ref, m3_ref, m4_ref, lg_ref, lb_ref, wr_ref, alpha, x1_ref, h2_ref, lt_ref, rows):
    y = alpha * x + m2_ref[...] * out
    mu = jnp.mean(y, -1, keepdims=True)
    yc = y - mu
    var = jnp.mean(yc * yc, -1, keepdims=True)
    x1 = yc * lax.rsqrt(var + LN_EPS) * lg_ref[...] + lb_ref[...]
    h2 = x1 * (1.0 + m4_ref[...]) + m3_ref[...]
    x1_ref[rows, :] = x1
    n_exp = lt_ref.shape[0]
    h2_hi = h2.astype(BF16)
    h2_lo = (h2 - h2_hi.astype(F32)).astype(BF16)
    h2_ref[rows, :] = h2_hi
    wr = wr_ref[...]
    wr_hi = wr.astype(BF16)
    wr_lo = (wr - wr_hi.astype(F32)).astype(BF16)
    both = _dot_nt(jnp.concatenate([wr_hi, wr_lo], 0), h2_hi)
    lt_ref[:, rows] = both[:n_exp] + both[n_exp:] + _dot_nt(wr_hi, h2_lo)


def _row_halves(n):
    return (slice(0, n // 2), slice(n // 2, n))


def _even_out_kernel(al_ref, ac_ref, of_ref, ob_ref, r_ref, xl_ref, xc_ref, m2_ref, m3_ref, m4_ref, ng_ref,
                     wo_ref, lg_ref, lb_ref, wr_ref, x1_ref, h2_ref, lt_ref, *, alpha, n_lat_tiles):
    is_lat = pl.program_id(0) < n_lat_tiles
    outs = []
    for rows in _row_halves(of_ref.shape[0]):
        o = of_ref[rows, :] + ob_ref[rows, :]
        r = r_ref[rows, :]
        gate = r * jax.nn.sigmoid(r)
        mixed = [jnp.where(is_lat, al_ref[rows, :], ac_ref[rows, :]).astype(BF16)]
        for h in range(GLA_HEADS):
            vs = slice(h * GLA_DV, (h + 1) * GLA_DV)
            oh = o[:, vs]
            nrm = oh * lax.rsqrt(jnp.mean(oh * oh, -1, keepdims=True) + NORM_EPS) * ng_ref[...]
            mixed.append((nrm * gate[:, vs]).astype(BF16))
        outs.append(_dot(jnp.concatenate(mixed, axis=1), wo_ref[...]))
    x = jnp.where(is_lat, xl_ref[...], xc_ref[...])
    _post_mix(jnp.concatenate(outs, 0), x, m2_ref, m3_ref, m4_ref, lg_ref, lb_ref, wr_ref, alpha,
              x1_ref, h2_ref, lt_ref, slice(None))


def _post_specs(d, layer, seg, tm, n_exp):
    ins = [_mod_spec(d, layer, 2, seg), _mod_spec(d, layer, 3, seg), _mod_spec(d, layer, 4, seg)]
    tail = [pl.BlockSpec((1, d), lambda i: (0, 0)), pl.BlockSpec((1, d), lambda i: (0, 0)),
            pl.BlockSpec((n_exp, d), lambda i: (0, 0))]
    outs =[pl.BlockSpec((tm, d), lambda i: (i, 0)), pl.BlockSpec((tm, d), lambda i: (i, 0)),
            pl.BlockSpec((n_exp, tm), lambda i: (0, i))]
    return ins, tail, outs


def _post_shapes(t, d, n_exp):
    return [jax.ShapeDtypeStruct((t, d), F32), jax.ShapeDtypeStruct((t, d), BF16),
            jax.ShapeDtypeStruct((n_exp, t), F32)]


def even_out(a_lat, a_ctx, o_f, o_b, proj, col_r, x_lat, x_ctx, mods4, layer, norm_g, w_out_bf16, ln_g, ln_b,
             router_wt, alpha, seg_rows, n_batch, tm=256):
    t = x_lat.shape[0] + x_ctx.shape[0]
    d = x_lat.shape[1]
    na = a_lat.shape[1]
    wv = o_f.shape[1]
    n_exp = router_wt.shape[0]
    seg = _seg_fn(tm, seg_rows, n_batch)
    ins, tail, outs = _post_specs(d, layer, seg, tm, n_exp)
    tile_of = lambda i: i
    return pl.pallas_call(
        functools.partial(_even_out_kernel, alpha=alpha, n_lat_tiles=x_lat.shape[0] // tm),
        grid=(t // tm,),
        in_specs=(_two_source_specs(a_lat, a_ctx, tm, tile_of)
                  + [pl.BlockSpec((tm, wv), lambda i: (i, 0)),
                     pl.BlockSpec((tm, wv), lambda i: (i, 0)),
                     pl.BlockSpec((tm, wv), lambda i: (i, col_r))]
                  + _two_source_specs(x_lat, x_ctx, tm, tile_of) + ins
                  + [pl.BlockSpec((1, GLA_DV), lambda i: (0, 0)),
                     pl.BlockSpec((na + wv, d), lambda i: (0, 0))] + tail),
        out_specs=outs,
        out_shape=_post_shapes(t, d, n_exp),
        compiler_params=_cparams("arbitrary"),
        name="even_out",
    )(a_lat, a_ctx, o_f, o_b, proj, x_lat, x_ctx, mods4, mods4, mods4, norm_g.reshape(1, -1), w_out_bf16,
      ln_g.reshape(1, -1), ln_b.reshape(1, -1), router_wt)


def _odd_out_kernel(y_ref, x_ref, m2_ref, m3_ref, m4_ref, wg_ref, bg_ref, wo_ref,
                    lg_ref, lb_ref, wr_ref, x1_ref, h2_ref, lt_ref, *, alpha):
    outs = []
    for r in _row_halves(y_ref.shape[0]):
        g = jax.nn.gelu(y_ref[r, :], approximate=True)
        z = _dot(g.astype(BF16), wg_ref[...]) + bg_ref[...]
        outs.append(_dot((g * jax.nn.sigmoid(z)).astype(BF16), wo_ref[...]))
    _post_mix(jnp.concatenate(outs, 0), x_ref[...], m2_ref, m3_ref, m4_ref, lg_ref, lb_ref, wr_ref, alpha,
              x1_ref, h2_ref, lt_ref, slice(None))


def odd_out(y, x, mods4, layer, w_glu_bf16, b_glu, w_out_bf16, ln_g, ln_b, router_wt,
            alpha, seg_rows, n_batch, tm=256):
    t, w5 = y.shape
    d = x.shape[1]
    n_exp = router_wt.shape[0]
    seg = _seg_fn(tm, seg_rows, n_batch)
    ins, tail, outs = _post_specs(d, layer, seg, tm, n_exp)
    return pl.pallas_call(
        functools.partial(_odd_out_kernel, alpha=alpha),
        grid=(t // tm,),
        in_specs=([pl.BlockSpec((tm, w5), lambda i: (i, 0)), pl.BlockSpec((tm, d), lambda i: (i, 0))] + ins
                  + [pl.BlockSpec((w5, w5), lambda i: (0, 0)),
                     pl.BlockSpec((1, w5), lambda i: (0, 0)),
                     pl.BlockSpec((w5, d), lambda i: (0, 0))] + tail),
        out_specs=outs,
        out_shape=_post_shapes(t, d, n_exp),
        compiler_params=_cparams("arbitrary"),
        name="odd_out",
    )(y, x, mods4, mods4, mods4, w_glu_bf16, b_glu.reshape(1, -1), w_out_bf16,
      ln_g.reshape(1, -1), ln_b.reshape(1, -1), router_wt)


def _route_kernel(lt_ref, rb_ref, idx_ref, w_ref):
    eg = N_EXPERTS // N_GROUPS
    logits = lt_ref[...]
    aff = jax.nn.sigmoid(logits)
    sel = aff + rb_ref[...]
    s = [sel[e:e + 1, :] for e in range(N_EXPERTS)]
    a = [aff[e:e + 1, :] for e in range(N_EXPERTS)]

    def top2_sum(v):
        hi1, lo1 = jnp.maximum(v[0], v[1]), jnp.minimum(v[0], v[1])
        hi2, lo2 = jnp.maximum(v[2], v[3]), jnp.minimum(v[2], v[3])
        return jnp.maximum(hi1, hi2) + jnp.maximum(jnp.minimum(hi1, hi2), jnp.maximum(lo1, lo2))

    best = top2_sum(s[0:eg])
    grp = jnp.zeros_like(best, dtype=jnp.int32)
    for g in range(1, N_GROUPS):
        sc = top2_sum(s[g * eg:(g + 1) * eg])
        better = sc > best
        best = jnp.where(better, sc, best)
        grp = jnp.where(better, g, grp)

    def pick(vals, j):
        out = vals[j]
        for g in range(1, N_GROUPS):
            out = jnp.where(grp == g, vals[g * eg + j], out)
        return out

    sv = [pick(s, j) for j in range(eg)]
    av = [pick(a, j) for j in range(eg)]

    def argmax_first(vals, exclude):
        bi = jnp.zeros_like(grp)
        bv = jnp.where(exclude == 0, -jnp.inf, vals[0]) if exclude is not None else vals[0]
        for j in range(1, eg):
            vj = jnp.where(exclude == j, -jnp.inf, vals[j]) if exclude is not None else vals[j]
            better = vj > bv
            bv = jnp.where(better, vj, bv)
            bi = jnp.where(better, j, bi)
        return bi

    i1 = argmax_first(sv, None)
    i2 = argmax_first(sv, i1)

    def take(vals, i):
        out = vals[0]
        for j in range(1, eg):
            out = jnp.where(i == j, vals[j], out)
        return out

    w1 = take(av, i1)
    w2 = take(av, i2)
    tot = w1 + w2
    idx_ref[0:1, :] = grp * eg + i1
    idx_ref[1:2, :] = grp * eg + i2
    w_ref[0:1, :] = w1 / tot
    w_ref[1:2, :] = w2 / tot


def route(logits_t, router_b, tile=1024):
    n_exp, t = logits_t.shape
    tile = math.gcd(tile, t)
    return pl.pallas_call(
        _route_kernel,
        grid=(t // tile,),
        in_specs=[pl.BlockSpec((n_exp, tile), lambda i: (0, i)),
                  pl.BlockSpec((n_exp, 1), lambda i: (0, 0))],
        out_specs=[pl.BlockSpec((TOP_K, tile), lambda i: (0, i)),
                   pl.BlockSpec((TOP_K, tile), lambda i: (0, i))],
        out_shape=[jax.ShapeDtypeStruct((TOP_K, t), jnp.int32), jax.ShapeDtypeStruct((TOP_K, t), F32)],
        compiler_params=_cparams("arbitrary"),
        name="moe_route",
    )(logits_t, router_b.reshape(n_exp, 1).astype(F32))


def moe_plan(idx, tm):
    t = idx.shape[1]
    n_pair = TOP_K * t
    n_tiles = (n_pair + N_EXPERTS * (tm - 1)) // tm
    e_flat = idx.reshape(-1)
    onehot = (e_flat[:, None] == jnp.arange(N_EXPERTS)[None, :]).astype(jnp.int32)
    running = jnp.cumsum(onehot, axis=0)
    counts = running[-1]
    rank = jnp.sum(onehot * running, 1) - 1
    tiles_per = (counts + tm - 1) // tm
    tile_end = jnp.cumsum(tiles_per)
    n_used = tile_end[-1]
    pstart = (tile_end - tiles_per) * tm
    pos = jnp.sum(onehot * pstart[None, :], 1) + rank
    tile_expert = jnp.minimum(jnp.sum((tile_end[None, :] <= jnp.arange(n_tiles)[:, None]).astype(jnp.int32), 1),
                              N_EXPERTS - 1).astype(jnp.int32)
    gidx = (jnp.arange(n_tiles * tm, dtype=jnp.int32) % t).at[pos].set(
        jnp.arange(n_pair, dtype=jnp.int32) % t, mode="promise_in_bounds", unique_indices=True)
    return gidx, tile_expert, n_used.reshape(1).astype(jnp.int32), pos.astype(jnp.int32)


def _expert_changed(te_ref):
    i = pl.program_id(0)
    return jnp.logical_or(i == 0, te_ref[i] != te_ref[jnp.maximum(i - 1, 0)])


def _ffn_up_kernel(te_ref, nu_ref, xs_ref, wg_ref, wu_ref, hid_ref, wg_bf, wu_bf):
    used = pl.program_id(0) < nu_ref[0]

    @pl.when(jnp.logical_and(used, _expert_changed(te_ref)))
    def _():
        wg_bf[...] = wg_ref[0].astype(BF16)
        wu_bf[...] = wu_ref[0].astype(BF16)

    @pl.when(used)
    def _():
        xs = xs_ref[...]
        g = _dot(xs, wg_bf[...])
        u = _dot(xs, wu_bf[...])
        hid_ref[...] = ((g * jax.nn.sigmoid(g)) * u).astype(BF16)

    @pl.when(jnp.logical_not(used))
    def _():
        hid_ref[...] = jnp.zeros_like(hid_ref)


def _ffn_down_kernel(te_ref, nu_ref, hid_ref, wd_ref, o_ref, wd_bf):
    used = pl.program_id(0) < nu_ref[0]

    @pl.when(jnp.logical_and(used, _expert_changed(te_ref)))
    def _():
        wd_bf[...] = wd_ref[0].astype(BF16)

    @pl.when(used)
    def _():
        o_ref[...] = _dot(hid_ref[...], wd_bf[...]).astype(o_ref.dtype)

    @pl.when(jnp.logical_not(used))
    def _():
        o_ref[...] = jnp.zeros_like(o_ref)


def grouped_ffn(xs, tile_expert, n_used, w_gate, w_up, w_down, layer, tm):
    p, d = xs.shape
    de = w_gate.shape[3]
    n_tiles = p // tm
    wmap = lambda i, te, nu: (layer, te[i], 0, 0)
    row_in = lambda i, te, nu: (jnp.minimum(i, nu[0] - 1), 0)
    row_out = lambda i, te, nu: (i, 0)
    hid = pl.pallas_call(
        _ffn_up_kernel,
        grid_spec=pltpu.PrefetchScalarGridSpec(
            num_scalar_prefetch=2,
            grid=(n_tiles,),
            in_specs=[pl.BlockSpec((tm, d), row_in),
                      pl.BlockSpec((None, 1, d, de), wmap),
                      pl.BlockSpec((None, 1, d, de), wmap)],
            out_specs=pl.BlockSpec((tm, de), row_out),
            scratch_shapes=[pltpu.VMEM((d, de), BF16), pltpu.VMEM((d, de), BF16)]),
        out_shape=jax.ShapeDtypeStruct((p, de), BF16),
        compiler_params=_cparams("arbitrary"),
        name="moe_ffn_up",
    )(tile_expert, n_used, xs, w_gate, w_up)
    return pl.pallas_call(
        _ffn_down_kernel,
        grid_spec=pltpu.PrefetchScalarGridSpec(
            num_scalar_prefetch=2,
            grid=(n_tiles,),
            in_specs=[pl.BlockSpec((tm, de), row_in),
                      pl.BlockSpec((None, 1, de, d), wmap)],
            out_specs=pl.BlockSpec((tm, d), row_out),
            scratch_shapes=[pltpu.VMEM((de, d), BF16)]),
        out_shape=jax.ShapeDtypeStruct((p, d), BF16),
        compiler_params=_cparams("arbitrary"),
        name="moe_ffn_down",
    )(tile_expert, n_used, hid, w_down)


def _final_kernel(x_ref, y0_ref, y1_ref, w_ref, m5_ref, lg_ref, lb_ref, o_ref, *, alpha):
    w = w_ref[...]
    y = w[:, 0:1] * y0_ref[...].astype(F32) + w[:, 1:2] * y1_ref[...].astype(F32)
    z = alpha * x_ref[...] + m5_ref[...] * y
    mu = jnp.mean(z, -1, keepdims=True)
    zc = z - mu
    var = jnp.mean(zc * zc, -1, keepdims=True)
    o_ref[...] = zc * lax.rsqrt(var + LN_EPS) * lg_ref[...] + lb_ref[...]


def final_norm(x1, yg, wts, mods4, layer, ln_g, ln_b, alpha, seg_rows, n_batch, tm=256):
    t, d = x1.shape
    seg = _seg_fn(tm, seg_rows, n_batch)
    row = pl.BlockSpec((tm, d), lambda i: (i, 0))
    vec = pl.BlockSpec((1, d), lambda i: (0, 0))
    return pl.pallas_call(
        functools.partial(_final_kernel, alpha=alpha),
        grid=(t // tm,),
        in_specs=[row, row, pl.BlockSpec((tm, d), lambda i: (i + t // tm, 0)),
                  pl.BlockSpec((tm, TOP_K), lambda i: (i, 0)),
                  _mod_spec(d, layer, 5, seg), vec, vec],
        out_specs=row,
        out_shape=jax.ShapeDtypeStruct((t, d), F32),
        compiler_params=_cparams("arbitrary"),
        name="final_norm",
    )(x1, yg, yg, wts, mods4, ln_g.reshape(1, -1), ln_b.reshape(1, -1))


def moe_block(x1, h2, logits_t, router_b, w_gate, w_up, w_down, mods4, layer, ln_g, ln_b,
              alpha, seg_rows, n_batch, tm=512):
    idx, wts = route(logits_t, router_b)
    gidx, tile_expert, n_used, pos = moe_plan(idx, tm)
    xs = h2.at[gidx].get(mode="promise_in_bounds")
    ys = grouped_ffn(xs, tile_expert, n_used, w_gate, w_up, w_down, layer, tm)
    yg = ys.at[pos].get(mode="promise_in_bounds")
    return final_norm(x1, yg, wts.T, mods4, layer, ln_g, ln_b, alpha, seg_rows, n_batch)


def s5_matrices(lam_re, lam_im, log_dt, b_re, b_im, c_re, c_im, d_skip):
    f32 = F32
    tc = S5_TC
    n_g, n_p = lam_re.shape[1], lam_re.shape[2]
    n_c = b_re.shape[-1]
    nb = 128 // n_c
    n_q = n_g // nb
    lr, li = lam_re.astype(f32), lam_im.astype(f32)
    dt = jnp.exp(log_dt.astype(f32))[..., None]

    def powers(jvals):
        j = jnp.asarray(np.asarray(jvals, np.float32))[:, None, None, None]
        mag = jnp.exp(lr * dt * j)
        return mag * jnp.cos(li * dt * j), mag * jnp.sin(li * dt * j)

    up = np.arange(tc)
    pw_re, pw_im = powers(np.arange(tc + 1))
    lb_re, lb_im = pw_re[1], pw_im[1]
    den = lr * lr + li * li
    fr = ((lb_re - 1.0) * lr + lb_im * li) / den
    fi = (lb_im * lr - (lb_re - 1.0) * li) / den
    br, bi = b_re.astype(f32), b_im.astype(f32)
    bb_re = fr[..., None] * br - fi[..., None] * bi
    bb_im = fr[..., None] * bi + fi[..., None] * br
    cr, ci = c_re.astype(f32), c_im.astype(f32)

    def times_b(p_re, p_im):
        return (p_re[..., None] * bb_re[None] - p_im[..., None] * bb_im[None],
                p_re[..., None] * bb_im[None] + p_im[..., None] * bb_re[None])

    e_re, e_im = times_b(pw_re, pw_im)
    kmat = jnp.sum(cr[None, :, :, :, :, None] * e_re[:, :, :, None, :, :]
                   - ci[None, :, :, :, :, None] * e_im[:, :, :, None, :, :], axis=4)
    def lag_slab(k_dir):
        return k_dir.reshape(tc, n_q, nb, n_c, n_c).transpose(1, 0, 4, 2, 3).reshape(n_q, tc, n_c, nb * n_c)
    skip = (jnp.eye(n_c, dtype=f32)[None, None, :, None, :]
            * d_skip.astype(f32).reshape(n_q, nb, n_c)[:, None, None, :, :]).reshape(n_q, 1, n_c, nb * n_c)
    k_c = jnp.concatenate([lag_slab(kmat[:tc, 0]), lag_slab(kmat[:tc, 1]), skip], 1)

    def w_slab(e):
        return e.reshape(tc, n_q, nb, n_p, n_c).transpose(1, 0, 4, 2, 3).reshape(n_q, tc, n_c, nb * n_p)
    ef_re, ef_im = times_b(*powers(tc - 1 - up))
    w_c = jnp.stack([w_slab(ef_re[:, 0]), w_slab(ef_im[:, 0]),
                     w_slab(e_re[:tc, 1]), w_slab(e_im[:tc, 1])], 2)

    def v_slabs(d, p_re, p_im):
        f_re = cr[d][None] * p_re[:, :, None, :] - ci[d][None] * p_im[:, :, None, :]
        f_im = cr[d][None] * p_im[:, :, None, :] + ci[d][None] * p_re[:, :, None, :]
        slab = lambda m: m.reshape(tc, n_q, nb, n_c, n_p).transpose(1, 0, 3, 2, 4).reshape(n_q, tc, n_c, nb * n_p)
        return slab(f_re), slab(-f_im)
    pb_re, pb_im = powers(tc - up)
    vt_c = jnp.stack(v_slabs(0, pw_re[1:, 0], pw_im[1:, 0]) + v_slabs(1, pb_re[:, 1], pb_im[:, 1]), 2)
    dec = lambda m: m.reshape(1, n_g * n_p // 128, 1, 128)
    decay = jnp.concatenate([dec(pw_re[tc, 0]), dec(pw_im[tc, 0]), dec(pw_re[tc, 1]), dec(pw_im[tc, 1])], 0)
    return k_c, w_c, vt_c, decay


def _s5_chunk_rows(ref, n):
    return jnp.concatenate([ref[pl.ds(s, n, stride=S5_TC), :] for s in range(S5_TC)], axis=1).astype(BF16)


def _s5_expand(slab, group_lanes):
    rows = 128
    tiled = jnp.concatenate([slab] * (rows // slab.shape[0]), axis=0)
    r = lax.broadcasted_iota(jnp.int32, tiled.shape, 0) // S5_CH
    l = lax.broadcasted_iota(jnp.int32, tiled.shape, 1) // group_lanes
    return jnp.where(r == l, tiled, 0.0).astype(BF16)


def _s5_in_kernel(ul_ref, uc_ref, wc_ref, fr_ref, fi_ref, br_ref, bi_ref, w_ref, *, n_lat, n_ctx, n_batch):
    b = pl.program_id(1)
    n_plane = wc_ref.shape[1]
    st = wc_ref.shape[3]

    @pl.when(b == 0)
    def _():
        for s in range(S5_TC):
            for i in range(n_plane):
                w_ref[s * 128:(s + 1) * 128, i * st:(i + 1) * st] = _s5_expand(wc_ref[s, i], S5_P)

    w_lat = _dot(_s5_chunk_rows(ul_ref, n_lat), w_ref[...])
    w_ctx = _dot(_s5_chunk_rows(uc_ref, n_ctx), w_ref[...])
    nv = fr_ref.shape[0]
    for i, ref in enumerate((fr_ref, fi_ref, br_ref, bi_ref)):
        for c in range(nv):
            lanes = slice((i * nv + c) * 128, (i * nv + c + 1) * 128)
            ref[c, pl.ds(b, n_ctx, stride=n_batch), :] = w_ctx[:, lanes]
            ref[c, pl.ds(n_ctx * n_batch + b, n_lat, stride=n_batch), :] = w_lat[:, lanes]
            ref[c, pl.ds((n_ctx + n_lat) * n_batch + b, n_ctx, stride=n_batch), :] = w_ctx[:, lanes]


def _s5_scan_kernel(wfr, wfi, wbr, wbi, dec_ref, xfr, xfi, xbr, xbi, *, n_tiles):
    nv = wfr.shape[0]
    low = lax.broadcasted_iota(jnp.int32, (nv, 8, 128), 1) < 4
    a_fr, a_fi, a_br, a_bi = dec_ref[0], dec_ref[1], dec_ref[2], dec_ref[3]

    def half_step(s_re, s_im, a_re, a_im, w_re, w_im):
        return a_re * s_re - a_im * s_im + w_re, a_re * s_im + a_im * s_re + w_im

    def one_dir(w_re_ref, w_im_ref, x_re_ref, x_im_ref, row0, s_re, s_im, a_re, a_im, first_low):
        first = low if first_low else jnp.logical_not(low)
        wt_re, wt_im = w_re_ref[:, pl.ds(row0, 8), :], w_im_ref[:, pl.ds(row0, 8), :]
        wr_re, wr_im = pltpu.roll(wt_re, 4, 1), pltpu.roll(wt_im, 4, 1)
        mid_re, mid_im = half_step(s_re, s_im, a_re, a_im, wr_re, wr_im)
        x_re_ref[:, pl.ds(row0, 8), :] = jnp.where(first, s_re, mid_re)
        x_im_ref[:, pl.ds(row0, 8), :] = jnp.where(first, s_im, mid_im)
        m_re = jnp.where(first, pltpu.roll(mid_re, 4, 1), mid_re)
        m_im = jnp.where(first, pltpu.roll(mid_im, 4, 1), mid_im)
        w2_re = jnp.where(first, wr_re, wt_re)
        w2_im = jnp.where(first, wr_im, wt_im)
        return half_step(m_re, m_im, a_re, a_im, w2_re, w2_im)

    def body(i, carry):
        f_re, f_im, b_re, b_im = carry
        rf = pl.multiple_of(i * 8, 8)
        rb = pl.multiple_of((n_tiles - 1 - i) * 8, 8)
        f_re, f_im = one_dir(wfr, wfi, xfr, xfi, rf, f_re, f_im, a_fr, a_fi, True)
        b_re, b_im = one_dir(wbr, wbi, xbr, xbi, rb, b_re, b_im, a_br, a_bi, False)
        return f_re, f_im, b_re, b_im

    z = jnp.zeros((nv, 8, 128), F32)
    lax.fori_loop(0, n_tiles, body, (z, z, z, z))


def _s5_out_kernel(ul_ref, fr_ref, fi_ref, br_ref, bi_ref, kc_ref, vc_ref, y_ref, mt_ref, vt_ref,
                   *, n_lat, n_ctx, n_batch):
    b = pl.program_id(1)
    tc = S5_TC

    @pl.when(b == 0)
    def _():
        lag = [_s5_expand(kc_ref[j], S5_CH) for j in range(2 * tc)]
        diag = _s5_expand(kc_ref[0] + kc_ref[tc] + kc_ref[2 * tc], S5_CH)
        for s in range(tc):
            for t in range(tc):
                blk = diag if s == t else (lag[t - s] if t > s else lag[tc + s - t])
                mt_ref[s * 128:(s + 1) * 128, t * 128:(t + 1) * 128] = blk
        for t in range(tc):
            for i in range(vt_ref.shape[0]):
                vt_ref[i, t * 128:(t + 1) * 128, :] = _s5_expand(vc_ref[t, i], S5_P)

    y = _dot(_s5_chunk_rows(ul_ref, n_lat), mt_ref[...])
    row0 = n_ctx * n_batch + b
    for i, ref in enumerate((fr_ref, fi_ref, br_ref, bi_ref)):
        xs = jnp.concatenate([ref[c, pl.ds(row0, n_lat, stride=n_batch), :] for c in range(ref.shape[0])], 1)
        y = y + _dot_nt(xs.astype(BF16), vt_ref[i])
    for s in range(S5_TC):
        y_ref[pl.ds(s, n_lat, stride=S5_TC), :] = y[:, s * 128:(s + 1) * 128]


def s5_bidir(u, mats, n_batch, l_lat, l_ctx):
    assert n_batch == 4, "the chunk scan packs two chunks of 4 batch rows per 8-sublane tile"
    k_c, w_c, vt_c, decay = mats
    tc = S5_TC
    wd = u.shape[1]
    n_q = wd // 128
    lane_q = tc * 128
    st_q = (128 // S5_CH) * S5_P
    n_lat, n_ctx = l_lat // tc, l_ctx // tc
    nk = n_lat + 2 * n_ctx
    assert nk % 2 == 0
    rows = nk * n_batch
    ctx0 = (n_batch * l_lat) // l_ctx
    dims = dict(n_lat=n_lat, n_ctx=n_ctx, n_batch=n_batch)

    nv = st_q // 128
    plane = jax.ShapeDtypeStruct((n_q * nv, rows, 128), F32)
    plane_spec = pl.BlockSpec((nv, rows, 128), lambda q, b: (q, 0, 0))
    ul_spec = pl.BlockSpec((l_lat, 128), lambda q, b: (b, q))
    uc_spec = pl.BlockSpec((l_ctx, 128), lambda q, b: (ctx0 + b, q))
    w_planes = pl.pallas_call(
        functools.partial(_s5_in_kernel, **dims),
        grid=(n_q, n_batch),
        in_specs=[ul_spec, uc_spec, pl.BlockSpec((None,) + w_c.shape[1:], lambda q, b: (q, 0, 0, 0, 0))],
        out_specs=[plane_spec] * 4,
        out_shape=[plane] * 4,
        scratch_shapes=[pltpu.VMEM((lane_q, 4 * st_q), BF16)],
        compiler_params=_cparams("arbitrary", "arbitrary"),
        name="s5_chunk_in",
    )(u, u, w_c)

    blk = pl.BlockSpec((nv, rows, 128), lambda j: (j, 0, 0))
    x_planes = pl.pallas_call(
        functools.partial(_s5_scan_kernel, n_tiles=rows // 8),
        grid=(n_q,),
        in_specs=[blk] * 4 + [pl.BlockSpec((4, nv, 1, 128), lambda j: (0, j, 0, 0))],
        out_specs=[blk] * 4,
        out_shape=[plane] * 4,
        compiler_params=_cparams("arbitrary"),
        name="s5_chunk_scan",
    )(*w_planes, decay)

    return pl.pallas_call(
        functools.partial(_s5_out_kernel, **dims),
        grid=(n_q, n_batch),
        in_specs=[ul_spec] + [plane_spec] * 4
                 + [pl.BlockSpec((None,) + k_c.shape[1:], lambda q, b: (q, 0, 0, 0)),
                    pl.BlockSpec((None,) + vt_c.shape[1:], lambda q, b: (q, 0, 0, 0, 0))],
        out_specs=pl.BlockSpec((l_lat, 128), lambda q, b: (b, q)),
        out_shape=jax.ShapeDtypeStruct((n_batch * l_lat, wd), F32),
        scratch_shapes=[pltpu.VMEM((lane_q, lane_q), BF16), pltpu.VMEM((4, lane_q, st_q), BF16)],
        compiler_params=_cparams("arbitrary", "arbitrary"),
        name="s5_chunk_out",
    )(u, *x_planes, k_c, vt_c)


def kernel(x, c, ctx, c_ctx, ada_w, ada_b, ln_mix_g, ln_mix_b, ln_ffn_g, ln_ffn_b, ev_w_in, ev_gate_w2,
           ev_gate_b, ev_rpb, ev_norm_g, ev_w_out, od_w_in, od_lam_re, od_lam_im, od_log_dt, od_b_re,
           od_b_im, od_c_re, od_c_im, od_d, od_w_glu, od_b_glu, od_w_out, router_w, router_b,
           moe_w_gate, moe_w_up, moe_w_down):
    n_batch, l_lat, d = x.shape
    l_ctx = ctx.shape[1]
    depth = ada_w.shape[0]
    assert depth == 2, "one even (NA + GLA) layer followed by one odd (S5) layer"
    alpha = (2.0 * depth) ** 0.25
    n_lat = n_batch * l_lat

    cvec = jnp.concatenate([c, c_ctx[None], jnp.zeros((8 - n_batch - 1, d), F32)], 0)
    mods = compute_mods(cvec, ada_w, ada_b)
    mods4 = mods.reshape(depth, 8, 1, N_MOD * d)
    x_lat, x_ctx = x.reshape(n_lat, d), ctx.reshape(n_batch * l_ctx, d)
    router_wt = router_w.T.astype(F32)

    na_w = NA_HEADS * NA_DH
    wk = GLA_HEADS * GLA_DK
    wv = GLA_HEADS * GLA_DV
    ev_in = ev_w_in.shape[2]
    pad = (-ev_in) % 256
    w_in = jnp.pad(ev_w_in[0], ((0, 0), (0, pad))).astype(BF16)
    proj = mod_matmul(x_lat, x_ctx, mods4, 0, w_in, l_lat, n_batch, tm=512, tn=(ev_in + pad) // 2)
    a_lat, a_ctx = na_attention(proj, na_bias_table(ev_rpb[0], l_lat // GRID_W), n_batch, l_lat, l_ctx)
    col_lr = (3 * na_w + 2 * wk + 2 * wv) // 128
    g2 = jnp.zeros((2, 128, wk), F32)
    g2 = g2.at[0, 0:GLA_RANK].set(ev_gate_w2[0, 0]).at[1, GLA_RANK:2 * GLA_RANK].set(ev_gate_w2[0, 1])
    o_f, o_b = gla_bidir(proj, g2.astype(BF16), ev_gate_b[0].reshape(2, 1, wk), rope_tables(l_lat),
                         n_batch, l_lat, l_ctx,
                         col_q=3 * na_w // wk, col_k=(3 * na_w + wk) // wk,
                         col_v=(3 * na_w + 2 * wk) // wv, col_lr=col_lr)
    x1, h2, logits_t = even_out(a_lat, a_ctx, o_f, o_b, proj, (3 * na_w + 2 * wk + wv) // wv, x_lat, x_ctx,
                                mods4, 0, ev_norm_g[0], ev_w_out[0].astype(BF16), ln_mix_g[0], ln_mix_b[0],
                                router_wt, alpha, l_lat, n_batch)
    w_gate, w_up, w_down = moe_w_gate, moe_w_up, moe_w_down
    rows = moe_block(x1, h2, logits_t, router_b, w_gate, w_up, w_down, mods4, 0,
                     ln_ffn_g[0], ln_ffn_b[0], alpha, l_lat, n_batch)

    u = mod_matmul(rows, None, mods4, 1, od_w_in[0].astype(BF16), l_lat, n_batch)
    mats = s5_matrices(od_lam_re[0], od_lam_im[0], od_log_dt[0], od_b_re[0], od_b_im[0],
                       od_c_re[0], od_c_im[0], od_d[0])
    y5 = s5_bidir(u, mats, n_batch, l_lat, l_ctx)
    x1, h2, logits_t = odd_out(y5, rows, mods4, 1, od_w_glu[0].astype(BF16), od_b_glu[0],
                               od_w_out[0].astype(BF16), ln_mix_g[1], ln_mix_b[1], router_wt,
                               alpha, l_lat, n_batch)
    out = moe_block(x1, h2, logits_t, router_b, w_gate, w_up, w_down, mods4, 1,
                    ln_ffn_g[1], ln_ffn_b[1], alpha, l_lat, n_batch)
    return out.reshape(n_batch, l_lat, d)
```

```python
import functools
import math

import numpy as np
import jax
import jax.numpy as jnp
from jax import lax
from jax.experimental import pallas as pl
from jax.experimental.pallas import tpu as pltpu

F32 = jnp.float32
BF16 = jnp.bfloat16
HIGHEST = lax.Precision.HIGHEST

N_MOD = 6
LN_EPS = 1e-5
NORM_EPS = 1e-6

GRID_W = 64
NA_HEADS = 8
NA_DH = 128
NA_KR = 8
NA_KC = 16

GLA_HEADS = 4
GLA_DK = 128
GLA_DV = 256
GLA_RANK = 16
GLA_TAU = 16.0
GLA_CHUNK = 64
ROPE_BASE = 10000.0

S5_CH = 16
S5_P = 64
S5_TC = 16

N_EXPERTS = 16
N_GROUPS = 4
TOP_K = 2

VMEM_LIMIT = 56 * 1024 * 1024
NEG_BIG = -1e30


def _cparams(*sem):
    return pltpu.CompilerParams(dimension_semantics=sem, vmem_limit_bytes=VMEM_LIMIT)


def _dot(a, b, precision=None):
    return jnp.dot(a, b, preferred_element_type=F32, precision=precision)


def _dot_nt(a, b, precision=None):
    return lax.dot_general(a, b, (((1,), (1,)), ((), ())), preferred_element_type=F32, precision=precision)


def _dot_tn(a, b):
    return lax.dot_general(a, b, (((0,), (0,)), ((), ())), preferred_element_type=F32)


def _mods_kernel(s_ref, w_ref, b_ref, o_ref):
    s = s_ref[...]
    s = s * jax.nn.sigmoid(s)
    o_ref[0] = _dot(s, w_ref[0], HIGHEST) + b_ref[0]


def compute_mods(cvec, ada_w, ada_b, tn=1024):
    n_layer, d, n = ada_w.shape
    tn = math.gcd(tn, n)
    return pl.pallas_call(
        _mods_kernel,
        grid=(n_layer, n // tn),
        in_specs=[pl.BlockSpec((8, d), lambda l, j: (0, 0)),
                  pl.BlockSpec((1, d, tn), lambda l, j: (l, 0, j)),
                  pl.BlockSpec((1, 1, tn), lambda l, j: (l, 0, j))],
        out_specs=pl.BlockSpec((1, 8, tn), lambda l, j: (l, 0, j)),
        out_shape=jax.ShapeDtypeStruct((n_layer, 8, n), F32),
        compiler_params=_cparams("arbitrary", "arbitrary"),
        name="ada_mods",
    )(cvec, ada_w, ada_b.reshape(n_layer, 1, n))


def _mod_spec(d, layer, which, seg_of_tile):
    return pl.BlockSpec((None, None, 1, d), lambda i, *_: (layer, seg_of_tile(i), 0, which))


def _seg_fn(tm, seg_rows, n_batch):
    return lambda i: jnp.minimum((i * tm) // seg_rows, n_batch)


def _two_source_specs(lat, ctx, tm, tile_of):
    n_lat_tiles = lat.shape[0] // tm
    d = lat.shape[1]
    return [pl.BlockSpec((tm, d), lambda *g: (jnp.minimum(tile_of(*g), n_lat_tiles - 1), 0)),
            pl.BlockSpec((tm, d), lambda *g: (jnp.maximum(tile_of(*g) - n_lat_tiles, 0), 0))]


def _pick_rows(lat_ref, ctx_ref, tile, n_lat_tiles):
    return jnp.where(tile < n_lat_tiles, lat_ref[...], ctx_ref[...])


def _modmm_kernel(xl_ref, xc_ref, s1_ref, s0_ref, w_ref, o_ref, *, n_lat_tiles):
    x = _pick_rows(xl_ref, xc_ref, pl.program_id(1), n_lat_tiles)
    h = x * (1.0 + s1_ref[...]) + s0_ref[...]
    o_ref[...] = _dot(h.astype(BF16), w_ref[...])


def mod_matmul(x_lat, x_ctx, mods4, layer, w_bf16, seg_rows, n_batch, tm=256, tn=None):
    if x_ctx is None:
        t, x_ctx = x_lat.shape[0], x_lat
    else:
        t = x_lat.shape[0] + x_ctx.shape[0]
    d = x_lat.shape[1]
    n = w_bf16.shape[1]
    tn = n if tn is None else tn
    seg = _seg_fn(tm, seg_rows, n_batch)
    return pl.pallas_call(
        functools.partial(_modmm_kernel, n_lat_tiles=x_lat.shape[0] // tm),
        grid=(n // tn, t // tm),
        in_specs=_two_source_specs(x_lat, x_ctx, tm, lambda j, i: i) + [
            pl.BlockSpec((None, None, 1, d), lambda j, i: (layer, seg(i), 0, 1)),
            pl.BlockSpec((None, None, 1, d), lambda j, i: (layer, seg(i), 0, 0)),
            pl.BlockSpec((d, tn), lambda j, i: (0, j))],
        out_specs=pl.BlockSpec((tm, tn), lambda j, i: (i, j)),
        out_shape=jax.ShapeDtypeStruct((t, n), F32),
        compiler_params=_cparams("arbitrary", "arbitrary"),
        name="mod_matmul",
    )(x_lat, x_ctx, mods4, mods4, w_bf16)


NA_RB = 4
NA_BAND = NA_RB + NA_KR - 1


def _na_row_start(r, rows):
    return min(max(r - NA_KR // 2, 0), rows - NA_KR)


def na_bias_table(rpb, rows):
    w = GRID_W
    q = np.arange(w)
    kc = np.arange(w)
    win0 = np.clip(q - NA_KC // 2, 0, w - NA_KC)
    ok = (kc[None, :] >= win0[:, None]) & (kc[None, :] < win0[:, None] + NA_KC)
    dc = np.clip(kc[None, :] - q[:, None] + NA_KC - 1, 0, 2 * NA_KC - 2)
    pick = ((dc[None] == np.arange(2 * NA_KC - 1)[:, None, None]) & ok[None]).astype(np.float32)
    colb = jnp.einsum("hrd,dqk->hrqk", rpb.astype(F32), jnp.asarray(pick), precision=HIGHEST)
    colb = jnp.where(ok[None, None], colb, NEG_BIG)
    neg = jnp.full((rpb.shape[0], w, w), NEG_BIG, F32)

    def block(r0):
        band0 = min(max(r0 - NA_KR // 2, 0), rows - NA_BAND)
        out = []
        for r in range(r0, r0 + NA_RB):
            rs = _na_row_start(r, rows)
            first = rs - r + NA_KR - 1
            cols = [neg] * (rs - band0) + [colb[:, first + j] for j in range(NA_KR)]
            cols += [neg] * (NA_BAND - len(cols))
            out.append(jnp.concatenate(cols, -1))
        return jnp.concatenate(out, 1)

    return jnp.stack([block(0), block(NA_RB), block(rows - NA_RB)], 1)


def _na_kernel(q_ref, k_ref, v_ref, qc_ref, kc_ref, vc_ref, bias_ref, o_ref, oc_ref, kbf, vbf, *, rows):
    w = GRID_W
    n_blk = rows // NA_RB
    scale = NA_DH ** -0.5
    kbf[...] = k_ref[...].astype(BF16)
    vbf[...] = v_ref[...].astype(BF16)
    kc = kc_ref[...].astype(BF16)
    vc = vc_ref[...].astype(BF16)

    def body(pair, carry):
        blocks = (2 * pair, 2 * pair + 1)
        q0, scores = [], []
        for i in blocks:
            r0 = i * NA_RB
            band0 = jnp.clip(r0 - NA_KR // 2, 0, rows - NA_BAND)
            variant = jnp.where(i == 0, 0, jnp.where(i == n_blk - 1, 2, 1))
            q0.append(pl.multiple_of(r0 * w, NA_RB * w))
            k0 = pl.multiple_of(band0 * w, w)
            q = (q_ref[pl.ds(q0[-1], NA_RB * w), :] * scale).astype(BF16)
            s_loc = _dot_nt(q, kbf[pl.ds(k0, NA_BAND * w), :]) + bias_ref[variant]
            scores.append((s_loc, _dot_nt(q, kc), k0))
        probs = []
        for s_loc, s_ctx, k0 in scores:
            m = jnp.maximum(jnp.max(s_loc, -1, keepdims=True), jnp.max(s_ctx, -1, keepdims=True))
            p_loc = jnp.exp(s_loc - m)
            p_ctx = jnp.exp(s_ctx - m)
            den = jnp.sum(p_loc, -1, keepdims=True) + jnp.sum(p_ctx, -1, keepdims=True)
            probs.append((p_loc.astype(BF16), p_ctx.astype(BF16), den, k0))
        for q_start, (p_loc, p_ctx, den, k0) in zip(q0, probs):
            o = _dot(p_loc, vbf[pl.ds(k0, NA_BAND * w), :]) + _dot(p_ctx, vc)
            o_ref[pl.ds(q_start, NA_RB * w), :] = o / den
        return carry

    lax.fori_loop(0, n_blk // 2, body, 0)

    qc = (qc_ref[...] * scale).astype(BF16)
    s = _dot_nt(qc, kc)
    p = jnp.exp(s - jnp.max(s, -1, keepdims=True))
    oc_ref[...] = _dot(p.astype(BF16), vc) / jnp.sum(p, -1, keepdims=True)


def na_attention(proj, bias_tab, n_batch, l_lat, l_ctx):
    h = NA_HEADS
    dh = NA_DH
    rows = l_lat // GRID_W
    ctx0 = (n_batch * l_lat) // l_ctx
    return pl.pallas_call(
        functools.partial(_na_kernel, rows=rows),
        grid=(n_batch, h),
        in_specs=[pl.BlockSpec((l_lat, dh), lambda b, hh: (b, hh)),
                  pl.BlockSpec((l_lat, dh), lambda b, hh: (b, h + hh)),
                  pl.BlockSpec((l_lat, dh), lambda b, hh: (b, 2 * h + hh)),
                  pl.BlockSpec((l_ctx, dh), lambda b, hh: (ctx0 + b, hh)),
                  pl.BlockSpec((l_ctx, dh), lambda b, hh: (ctx0 + b, h + hh)),
                  pl.BlockSpec((l_ctx, dh), lambda b, hh: (ctx0 + b, 2 * h + hh)),
                  pl.BlockSpec((None,) + bias_tab.shape[1:], lambda b, hh: (hh, 0, 0, 0))],
        out_specs=[pl.BlockSpec((l_lat, dh), lambda b, hh: (b, hh)),
                   pl.BlockSpec((l_ctx, dh), lambda b, hh: (b, hh))],
        out_shape=[jax.ShapeDtypeStruct((n_batch * l_lat, h * dh), F32),
                   jax.ShapeDtypeStruct((n_batch * l_ctx, h * dh), F32)],
        scratch_shapes=[pltpu.VMEM((l_lat, dh), BF16), pltpu.VMEM((l_lat, dh), BF16)],
        compiler_params=_cparams("arbitrary", "arbitrary"),
        name="na_attention",
    )(proj, proj, proj, proj, proj, proj, bias_tab)


def rope_tables(l_lat):
    half = GLA_DK // 2
    nf = half // 2
    inv = ROPE_BASE ** (-np.arange(nf, dtype=np.float64) / nf)
    t = np.arange(l_lat)
    lane = np.arange(GLA_DK)
    pos = np.where(lane[None, :] < half, (t // GRID_W)[:, None], (t % GRID_W)[:, None]).astype(np.float64)
    ang = pos * inv[lane % nf][None, :]
    first = (lane % half) < nf
    cos = np.cos(ang)
    sin_a = np.where(first[None, :], -np.sin(ang), 0.0)
    sin_b = np.where(first[None, :], 0.0, np.sin(ang))
    return jnp.asarray(cos, F32), jnp.asarray(sin_a, F32), jnp.asarray(sin_b, F32)


GLA_PREP_CHUNKS = 8


def _gla_prep_kernel(q_ref, k_ref, lr_ref, cos_ref, sa_ref, sb_ref, g2_ref, gb_ref,
                     qdf, kdf, krf, elf, qdb, kdb, krb, elb, *, n_lat_tiles):
    c = GLA_CHUNK
    nch = GLA_PREP_CHUNKS
    nf = GLA_DK // 4
    gscale = GLA_DK ** -0.5
    wk = GLA_HEADS * GLA_DK
    is_lat = pl.program_id(0) < n_lat_tiles
    cos = jnp.where(is_lat, cos_ref[...], 1.0)
    sa = jnp.where(is_lat, sa_ref[...], 0.0)
    sb = jnp.where(is_lat, sb_ref[...], 0.0)

    def rope(x):
        return x * cos + pltpu.roll(x, GLA_DK - nf, 1) * sa + pltpu.roll(x, nf, 1) * sb

    qs, ks_ = [], []
    for h in range(GLA_HEADS):
        hs = slice(h * GLA_DK, (h + 1) * GLA_DK)
        qs.append(rope(q_ref[:, hs]) * gscale)
        ks_.append(rope(k_ref[:, hs]))

    lr = lr_ref[...].astype(BF16)
    row = lax.broadcasted_iota(jnp.int32, (c, c), 0)
    col = lax.broadcasted_iota(jnp.int32, (c, c), 1)
    for d, (qd, kd, kr, el) in enumerate(((qdf, kdf, krf, elf), (qdb, kdb, krb, elb))):
        reverse = d == 1
        tri = ((col >= row) if reverse else (col <= row)).astype(BF16)
        z = _dot(lr, g2_ref[d]) + gb_ref[d]
        g = (jnp.minimum(z, 0.0) - jnp.log1p(jnp.exp(-jnp.abs(z)))) * (1.0 / GLA_TAU)
        g_hi = g.astype(BF16)
        r1 = g - g_hi.astype(F32)
        g_mid = r1.astype(BF16)
        g_lo = (r1 - g_mid.astype(F32)).astype(BF16)
        parts = []
        for ci in range(nch):
            rs = slice(ci * c, (ci + 1) * c)
            parts.append(_dot(tri, g_hi[rs]) + _dot(tri, g_mid[rs]) + _dot(tri, g_lo[rs]))
        b3 = jnp.concatenate(parts, 0).reshape(nch, c, wk)
        bl3 = b3[:, 0:1, :] if reverse else b3[:, c - 1:c, :]
        el[...] = jnp.exp(bl3)
        e_b = jnp.exp(b3).reshape(nch * c, wk)
        e_nb = jnp.exp(-b3).reshape(nch * c, wk)
        e_rem = jnp.exp(bl3 - b3).reshape(nch * c, wk)
        for h in range(GLA_HEADS):
            hs = slice(h * GLA_DK, (h + 1) * GLA_DK)
            qd[:, hs] = (qs[h] * e_b[:, hs]).astype(BF16)
            kd[:, hs] = (ks_[h] * e_nb[:, hs]).astype(BF16)
            kr[:, hs] = (ks_[h] * e_rem[:, hs]).astype(BF16)


def _gla_scan_kernel(qdf, kdf, krf, elf, vf, qdb, kdb, krb, elb, vb, of_ref, ob_ref, st_ref):
    @pl.when(pl.program_id(1) == 0)
    def _():
        st_ref[...] = jnp.zeros_like(st_ref)

    c = GLA_CHUNK
    row = lax.broadcasted_iota(jnp.int32, (c, c), 0)
    col = lax.broadcasted_iota(jnp.int32, (c, c), 1)
    dirs = ((qdf, kdf, krf, elf, vf, of_ref, col <= row, (0, 1)), (qdb, kdb, krb, elb, vb, ob_ref, col >= row, (1, 0)))
    chains = [(d, h) for d in range(2) for h in range(GLA_HEADS)]
    hs = lambda h: slice(h * GLA_DK, (h + 1) * GLA_DK)
    vs = lambda h: slice(h * GLA_DV, (h + 1) * GLA_DV)
    rows = lambda k: slice(k * c, (k + 1) * c)
    order = lambda d, step: rows(dirs[d][7][step])
    q_dec = [[dirs[d][0][order(d, s), hs(h)] for d, h in chains] for s in range(2)]
    v_bf = [[dirs[d][4][order(d, s), vs(h)].astype(BF16) for d, h in chains] for s in range(2)]
    att = [[jnp.where(dirs[d][6], _dot_nt(q_dec[s][n], dirs[d][1][order(d, s), hs(h)]), 0.0).astype(BF16)
            for n, (d, h) in enumerate(chains)] for s in range(2)]
    state = [st_ref[d, h] for d, h in chains]
    for s in range(2):
        for n, (d, h) in enumerate(chains):
            dirs[d][5][order(d, s), vs(h)] = (_dot(att[s][n], v_bf[s][n])
                                              + _dot_nt(q_dec[s][n], state[n].astype(BF16)))
        state = [state[n] * dirs[d][3][dirs[d][7][s], :, hs(h)]
                 + _dot_tn(v_bf[s][n], dirs[d][2][order(d, s), hs(h)]) for n, (d, h) in enumerate(chains)]
    for n, (d, h) in enumerate(chains):
        st_ref[d, h] = state[n]


def gla_bidir(proj, g2, gb, tables, n_batch, l_lat, l_ctx, col_q, col_k, col_v, col_lr):
    c = GLA_CHUNK
    nc = l_ctx // c
    nl = l_lat // c
    nz = nl + 2 * nc
    steps = nl + nc
    wk = GLA_HEADS * GLA_DK
    wv = GLA_HEADS * GLA_DV
    t_rows = n_batch * (l_lat + l_ctx)

    tp = GLA_PREP_CHUNKS * c
    lat_tiles = l_lat // tp
    n_lat_tiles = n_batch * lat_tiles
    cos, sa, sb = tables
    tab = pl.BlockSpec((tp, GLA_DK), lambda i: (jnp.where(i < n_lat_tiles, i % lat_tiles, 0), 0))
    row_bf = jax.ShapeDtypeStruct((t_rows, wk), BF16)
    last = jax.ShapeDtypeStruct((t_rows // c, 1, wk), F32)
    row_spec = pl.BlockSpec((tp, wk), lambda i: (i, 0))
    last_spec = pl.BlockSpec((GLA_PREP_CHUNKS, 1, wk), lambda i: (i, 0, 0))
    prep = pl.pallas_call(
        functools.partial(_gla_prep_kernel, n_lat_tiles=n_lat_tiles),
        grid=(t_rows // tp,),
        in_specs=[pl.BlockSpec((tp, wk), lambda i: (i, col_q)),
                  pl.BlockSpec((tp, wk), lambda i: (i, col_k)),
                  pl.BlockSpec((tp, 128), lambda i: (i, col_lr)),
                  tab, tab, tab,
                  pl.BlockSpec((2, 128, wk), lambda i: (0, 0, 0)),
                  pl.BlockSpec((2, 1, wk), lambda i: (0, 0, 0))],
        out_specs=[row_spec, row_spec, row_spec, last_spec] * 2,
        out_shape=[row_bf, row_bf, row_bf, last] * 2,
        compiler_params=_cparams("arbitrary"),
        name="gla_prep",
    )(proj, proj, proj, cos, sa, sb, g2, gb)

    def zblk(b, j):
        lat = b * nl + (j - nc)
        ctx = n_batch * nl + b * nc + jnp.where(j < nc, j, j - nc - nl)
        return jnp.where((j >= nc) & (j < nc + nl), lat, ctx)

    assert nc % 2 == 0 and nl % 2 == 0
    fwd = lambda b, i: zblk(b, 2 * i) // 2
    bwd = lambda b, i: zblk(b, nz - 2 - 2 * i) // 2
    c2 = 2 * c

    def dir_specs(blk):
        return [pl.BlockSpec((c2, wk), lambda b, i: (blk(b, i), 0)),
                pl.BlockSpec((c2, wk), lambda b, i: (blk(b, i), 0)),
                pl.BlockSpec((c2, wk), lambda b, i: (blk(b, i), 0)),
                pl.BlockSpec((2, 1, wk), lambda b, i: (blk(b, i), 0, 0)),
                pl.BlockSpec((c2, wv), lambda b, i: (blk(b, i), col_v))]

    return pl.pallas_call(
        _gla_scan_kernel,
        grid=(n_batch, steps // 2),
        in_specs=dir_specs(fwd) + dir_specs(bwd),
        out_specs=[pl.BlockSpec((c2, wv), lambda b, i: (fwd(b, i), 0)),
                   pl.BlockSpec((c2, wv), lambda b, i: (bwd(b, i), 0))],
        out_shape=[jax.ShapeDtypeStruct((t_rows, wv), F32), jax.ShapeDtypeStruct((t_rows, wv), F32)],
        scratch_shapes=[pltpu.VMEM((2, GLA_HEADS, GLA_DV, GLA_DK), F32)],
        compiler_params=_cparams("arbitrary", "arbitrary"),
        name="gla_scan",
    )(*prep[0:4], proj, *prep[4:8], proj)


def _post_mix(out, x, m2_ref, m3_ref, m4_ref, lg_ref, lb_ref, wr_ref, alpha, x1_ref, h2_ref, lt_ref, rows):
    y = alpha * x + m2_ref[...] * out
    mu = jnp.mean(y, -1, keepdims=True)
    yc = y - mu
    var = jnp.mean(yc * yc, -1, keepdims=True)
    x1 = yc * lax.rsqrt(var + LN_EPS) * lg_ref[...] + lb_ref[...]
    h2 = x1 * (1.0 + m4_ref[...]) + m3_ref[...]
    x1_ref[rows, :] = x1
    n_exp = lt_ref.shape[0]
    h2_hi = h2.astype(BF16)
    h2_lo = (h2 - h2_hi.astype(F32)).astype(BF16)
    h2_ref[rows, :] = h2_hi
    wr = wr_ref[...]
    wr_hi = wr.astype(BF16)
    wr_lo = (wr - wr_hi.astype(F32)).astype(BF16)
    both = _dot_nt(jnp.concatenate([wr_hi, wr_lo], 0), h2_hi)
    lt_ref[:, rows] = both[:n_exp] + both[n_exp:] + _dot_nt(wr_hi, h2_lo)


def _row_halves(n):
    return (slice(0, n // 2), slice(n // 2, n))


def _even_out_kernel(al_ref, ac_ref, of_ref, ob_ref, r_ref, xl_ref, xc_ref, m2_ref, m3_ref, m4_ref, ng_ref,
                     wo_ref, lg_ref, lb_ref, wr_ref, x1_ref, h2_ref, lt_ref, *, alpha, n_lat_tiles):
    is_lat = pl.program_id(0) < n_lat_tiles
    outs = []
    for rows in _row_halves(of_ref.shape[0]):
        o = of_ref[rows, :] + ob_ref[rows, :]
        r = r_ref[rows, :]
        gate = r * jax.nn.sigmoid(r)
        mixed = [jnp.where(is_lat, al_ref[rows, :], ac_ref[rows, :]).astype(BF16)]
        for h in range(GLA_HEADS):
            vs = slice(h * GLA_DV, (h + 1) * GLA_DV)
            oh = o[:, vs]
            nrm = oh * lax.rsqrt(jnp.mean(oh * oh, -1, keepdims=True) + NORM_EPS) * ng_ref[...]
            mixed.append((nrm * gate[:, vs]).astype(BF16))
        outs.append(_dot(jnp.concatenate(mixed, axis=1), wo_ref[...]))
    x = jnp.where(is_lat, xl_ref[...], xc_ref[...])
    _post_mix(jnp.concatenate(outs, 0), x, m2_ref, m3_ref, m4_ref, lg_ref, lb_ref, wr_ref, alpha,
              x1_ref, h2_ref, lt_ref, slice(None))


def _post_specs(d, layer, seg, tm, n_exp):
    ins = [_mod_spec(d, layer, 2, seg), _mod_spec(d, layer, 3, seg), _mod_spec(d, layer, 4, seg)]
    tail = [pl.BlockSpec((1, d), lambda i: (0, 0)), pl.BlockSpec((1, d), lambda i: (0, 0)),
            pl.BlockSpec((n_exp, d), lambda i: (0, 0))]
    outs =[pl.BlockSpec((tm, d), lambda i: (i, 0)), pl.BlockSpec((tm, d), lambda i: (i, 0)),
            pl.BlockSpec((n_exp, tm), lambda i: (0, i))]
    return ins, tail, outs


def _post_shapes(t, d, n_exp):
    return [jax.ShapeDtypeStruct((t, d), F32), jax.ShapeDtypeStruct((t, d), BF16),
            jax.ShapeDtypeStruct((n_exp, t), F32)]


def even_out(a_lat, a_ctx, o_f, o_b, proj, col_r, x_lat, x_ctx, mods4, layer, norm_g, w_out_bf16, ln_g, ln_b,
             router_wt, alpha, seg_rows, n_batch, tm=256):
    t = x_lat.shape[0] + x_ctx.shape[0]
    d = x_lat.shape[1]
    na = a_lat.shape[1]
    wv = o_f.shape[1]
    n_exp = router_wt.shape[0]
    seg = _seg_fn(tm, seg_rows, n_batch)
    ins, tail, outs = _post_specs(d, layer, seg, tm, n_exp)
    tile_of = lambda i: i
    return pl.pallas_call(
        functools.partial(_even_out_kernel, alpha=alpha, n_lat_tiles=x_lat.shape[0] // tm),
        grid=(t // tm,),
        in_specs=(_two_source_specs(a_lat, a_ctx, tm, tile_of)
                  + [pl.BlockSpec((tm, wv), lambda i: (i, 0)),
                     pl.BlockSpec((tm, wv), lambda i: (i, 0)),
                     pl.BlockSpec((tm, wv), lambda i: (i, col_r))]
                  + _two_source_specs(x_lat, x_ctx, tm, tile_of) + ins
                  + [pl.BlockSpec((1, GLA_DV), lambda i: (0, 0)),
                     pl.BlockSpec((na + wv, d), lambda i: (0, 0))] + tail),
        out_specs=outs,
        out_shape=_post_shapes(t, d, n_exp),
        compiler_params=_cparams("arbitrary"),
        name="even_out",
    )(a_lat, a_ctx, o_f, o_b, proj, x_lat, x_ctx, mods4, mods4, mods4, norm_g.reshape(1, -1), w_out_bf16,
      ln_g.reshape(1, -1), ln_b.reshape(1, -1), router_wt)


def _odd_out_kernel(y_ref, x_ref, m2_ref, m3_ref, m4_ref, wg_ref, bg_ref, wo_ref,
                    lg_ref, lb_ref, wr_ref, x1_ref, h2_ref, lt_ref, *, alpha):
    outs = []
    for r in _row_halves(y_ref.shape[0]):
        g = jax.nn.gelu(y_ref[r, :], approximate=True)
        z = _dot(g.astype(BF16), wg_ref[...]) + bg_ref[...]
        outs.append(_dot((g * jax.nn.sigmoid(z)).astype(BF16), wo_ref[...]))
    _post_mix(jnp.concatenate(outs, 0), x_ref[...], m2_ref, m3_ref, m4_ref, lg_ref, lb_ref, wr_ref, alpha,
              x1_ref, h2_ref, lt_ref, slice(None))


def odd_out(y, x, mods4, layer, w_glu_bf16, b_glu, w_out_bf16, ln_g, ln_b, router_wt,
            alpha, seg_rows, n_batch, tm=256):
    t, w5 = y.shape
    d = x.shape[1]
    n_exp = router_wt.shape[0]
    seg = _seg_fn(tm, seg_rows, n_batch)
    ins, tail, outs = _post_specs(d, layer, seg, tm, n_exp)
    return pl.pallas_call(
        functools.partial(_odd_out_kernel, alpha=alpha),
        grid=(t // tm,),
        in_specs=([pl.BlockSpec((tm, w5), lambda i: (i, 0)), pl.BlockSpec((tm, d), lambda i: (i, 0))] + ins
                  + [pl.BlockSpec((w5, w5), lambda i: (0, 0)),
                     pl.BlockSpec((1, w5), lambda i: (0, 0)),
                     pl.BlockSpec((w5, d), lambda i: (0, 0))] + tail),
        out_specs=outs,
        out_shape=_post_shapes(t, d, n_exp),
        compiler_params=_cparams("arbitrary"),
        name="odd_out",
    )(y, x, mods4, mods4, mods4, w_glu_bf16, b_glu.reshape(1, -1), w_out_bf16,
      ln_g.reshape(1, -1), ln_b.reshape(1, -1), router_wt)


def _route_kernel(lt_ref, rb_ref, idx_ref, w_ref):
    eg = N_EXPERTS // N_GROUPS
    logits = lt_ref[...]
    aff = jax.nn.sigmoid(logits)
    sel = aff + rb_ref[...]
    s = [sel[e:e + 1, :] for e in range(N_EXPERTS)]
    a = [aff[e:e + 1, :] for e in range(N_EXPERTS)]

    def top2_sum(v):
        hi1, lo1 = jnp.maximum(v[0], v[1]), jnp.minimum(v[0], v[1])
        hi2, lo2 = jnp.maximum(v[2], v[3]), jnp.minimum(v[2], v[3])
        return jnp.maximum(hi1, hi2) + jnp.maximum(jnp.minimum(hi1, hi2), jnp.maximum(lo1, lo2))

    best = top2_sum(s[0:eg])
    grp = jnp.zeros_like(best, dtype=jnp.int32)
    for g in range(1, N_GROUPS):
        sc = top2_sum(s[g * eg:(g + 1) * eg])
        better = sc > best
        best = jnp.where(better, sc, best)
        grp = jnp.where(better, g, grp)

    def pick(vals, j):
        out = vals[j]
        for g in range(1, N_GROUPS):
            out = jnp.where(grp == g, vals[g * eg + j], out)
        return out

    sv = [pick(s, j) for j in range(eg)]
    av = [pick(a, j) for j in range(eg)]

    def argmax_first(vals, exclude):
        bi = jnp.zeros_like(grp)
        bv = jnp.where(exclude == 0, -jnp.inf, vals[0]) if exclude is not None else vals[0]
        for j in range(1, eg):
            vj = jnp.where(exclude == j, -jnp.inf, vals[j]) if exclude is not None else vals[j]
            better = vj > bv
            bv = jnp.where(better, vj, bv)
            bi = jnp.where(better, j, bi)
        return bi

    i1 = argmax_first(sv, None)
    i2 = argmax_first(sv, i1)

    def take(vals, i):
        out = vals[0]
        for j in range(1, eg):
            out = jnp.where(i == j, vals[j], out)
        return out

    w1 = take(av, i1)
    w2 = take(av, i2)
    tot = w1 + w2
    idx_ref[0:1, :] = grp * eg + i1
    idx_ref[1:2, :] = grp * eg + i2
    w_ref[0:1, :] = w1 / tot
    w_ref[1:2, :] = w2 / tot


def route(logits_t, router_b, tile=1024):
    n_exp, t = logits_t.shape
    tile = math.gcd(tile, t)
    return pl.pallas_call(
        _route_kernel,
        grid=(t // tile,),
        in_specs=[pl.BlockSpec((n_exp, tile), lambda i: (0, i)),
                  pl.BlockSpec((n_exp, 1), lambda i: (0, 0))],
        out_specs=[pl.BlockSpec((TOP_K, tile), lambda i: (0, i)),
                   pl.BlockSpec((TOP_K, tile), lambda i: (0, i))],
        out_shape=[jax.ShapeDtypeStruct((TOP_K, t), jnp.int32), jax.ShapeDtypeStruct((TOP_K, t), F32)],
        compiler_params=_cparams("arbitrary"),
        name="moe_route",
    )(logits_t, router_b.reshape(n_exp, 1).astype(F32))


def moe_plan(idx, tm):
    t = idx.shape[1]
    n_pair = TOP_K * t
    n_tiles = (n_pair + N_EXPERTS * (tm - 1)) // tm
    e_flat = idx.reshape(-1)
    onehot = (e_flat[:, None] == jnp.arange(N_EXPERTS)[None, :]).astype(jnp.int32)
    running = jnp.cumsum(onehot, axis=0)
    counts = running[-1]
    rank = jnp.sum(onehot * running, 1) - 1
    tiles_per = (counts + tm - 1) // tm
    tile_end = jnp.cumsum(tiles_per)
    n_used = tile_end[-1]
    pstart = (tile_end - tiles_per) * tm
    pos = jnp.sum(onehot * pstart[None, :], 1) + rank
    tile_expert = jnp.minimum(jnp.sum((tile_end[None, :] <= jnp.arange(n_tiles)[:, None]).astype(jnp.int32), 1),
                              N_EXPERTS - 1).astype(jnp.int32)
    pos = pos.astype(jnp.int32)
    gidx = invert_rows(pos, jnp.arange(n_tiles * tm, dtype=jnp.int32) % t, t)
    return gidx, tile_expert, n_used.reshape(1).astype(jnp.int32), pos


def _invert_kernel(pos_ref, init_ref, out_ref, *, n_tok):
    del init_ref
    n_pair = pos_ref.shape[0]

    def body(i, carry):
        out_ref[pos_ref[i]] = jnp.where(i >= n_tok, i - n_tok, i)
        return carry

    lax.fori_loop(0, n_pair, body, 0, unroll=8)


def invert_rows(pos, init, n_tok):
    smem = pl.BlockSpec(memory_space=pltpu.SMEM)
    return pl.pallas_call(
        functools.partial(_invert_kernel, n_tok=n_tok),
        in_specs=[smem, smem],
        out_specs=smem,
        out_shape=jax.ShapeDtypeStruct(init.shape, jnp.int32),
        input_output_aliases={1: 0},
        name="moe_invert_rows",
    )(pos, init)


def _expert_changed(te_ref):
    i = pl.program_id(0)
    return jnp.logical_or(i == 0, te_ref[i] != te_ref[jnp.maximum(i - 1, 0)])


def _ffn_up_kernel(te_ref, nu_ref, xs_ref, wg_ref, wu_ref, hid_ref, wg_bf, wu_bf):
    used = pl.program_id(0) < nu_ref[0]

    @pl.when(jnp.logical_and(used, _expert_changed(te_ref)))
    def _():
        wg_bf[...] = wg_ref[0].astype(BF16)
        wu_bf[...] = wu_ref[0].astype(BF16)

    @pl.when(used)
    def _():
        xs = xs_ref[...]
        g = _dot(xs, wg_bf[...])
        u = _dot(xs, wu_bf[...])
        hid_ref[...] = ((g * jax.nn.sigmoid(g)) * u).astype(BF16)

    @pl.when(jnp.logical_not(used))
    def _():
        hid_ref[...] = jnp.zeros_like(hid_ref)


def _ffn_down_kernel(te_ref, nu_ref, hid_ref, wd_ref, o_ref, wd_bf):
    used = pl.program_id(0) < nu_ref[0]

    @pl.when(jnp.logical_and(used, _expert_changed(te_ref)))
    def _():
        wd_bf[...] = wd_ref[0].astype(BF16)

    @pl.when(used)
    def _():
        o_ref[...] = _dot(hid_ref[...], wd_bf[...]).astype(o_ref.dtype)

    @pl.when(jnp.logical_not(used))
    def _():
        o_ref[...] = jnp.zeros_like(o_ref)


def grouped_ffn(xs, tile_expert, n_used, w_gate, w_up, w_down, layer, tm):
    p, d = xs.shape
    de = w_gate.shape[3]
    n_tiles = p // tm
    wmap = lambda i, te, nu: (layer, te[i], 0, 0)
    row_in = lambda i, te, nu: (jnp.minimum(i, nu[0] - 1), 0)
    row_out = lambda i, te, nu: (i, 0)
    hid = pl.pallas_call(
        _ffn_up_kernel,
        grid_spec=pltpu.PrefetchScalarGridSpec(
            num_scalar_prefetch=2,
            grid=(n_tiles,),
            in_specs=[pl.BlockSpec((tm, d), row_in),
                      pl.BlockSpec((None, 1, d, de), wmap),
                      pl.BlockSpec((None, 1, d, de), wmap)],
            out_specs=pl.BlockSpec((tm, de), row_out),
            scratch_shapes=[pltpu.VMEM((d, de), BF16), pltpu.VMEM((d, de), BF16)]),
        out_shape=jax.ShapeDtypeStruct((p, de), BF16),
        compiler_params=_cparams("arbitrary"),
        name="moe_ffn_up",
    )(tile_expert, n_used, xs, w_gate, w_up)
    return pl.pallas_call(
        _ffn_down_kernel,
        grid_spec=pltpu.PrefetchScalarGridSpec(
            num_scalar_prefetch=2,
            grid=(n_tiles,),
            in_specs=[pl.BlockSpec((tm, de), row_in),
                      pl.BlockSpec((None, 1, de, d), wmap)],
            out_specs=pl.BlockSpec((tm, d), row_out),
            scratch_shapes=[pltpu.VMEM((de, d), BF16)]),
        out_shape=jax.ShapeDtypeStruct((p, d), BF16),
        compiler_params=_cparams("arbitrary"),
        name="moe_ffn_down",
    )(tile_expert, n_used, hid, w_down)


def _final_kernel(x_ref, y0_ref, y1_ref, w_ref, m5_ref, lg_ref, lb_ref, o_ref, *, alpha):
    w = w_ref[...]
    y = w[:, 0:1] * y0_ref[...].astype(F32) + w[:, 1:2] * y1_ref[...].astype(F32)
    z = alpha * x_ref[...] + m5_ref[...] * y
    mu = jnp.mean(z, -1, keepdims=True)
    zc = z - mu
    var = jnp.mean(zc * zc, -1, keepdims=True)
    o_ref[...] = zc * lax.rsqrt(var + LN_EPS) * lg_ref[...] + lb_ref[...]


def final_norm(x1, yg, wts, mods4, layer, ln_g, ln_b, alpha, seg_rows, n_batch, tm=256):
    t, d = x1.shape
    seg = _seg_fn(tm, seg_rows, n_batch)
    row = pl.BlockSpec((tm, d), lambda i: (i, 0))
    vec = pl.BlockSpec((1, d), lambda i: (0, 0))
    return pl.pallas_call(
        functools.partial(_final_kernel, alpha=alpha),
        grid=(t // tm,),
        in_specs=[row, row, pl.BlockSpec((tm, d), lambda i: (i + t // tm, 0)),
                  pl.BlockSpec((tm, TOP_K), lambda i: (i, 0)),
                  _mod_spec(d, layer, 5, seg), vec, vec],
        out_specs=row,
        out_shape=jax.ShapeDtypeStruct((t, d), F32),
        compiler_params=_cparams("arbitrary"),
        name="final_norm",
    )(x1, yg, yg, wts, mods4, ln_g.reshape(1, -1), ln_b.reshape(1, -1))


def moe_block(x1, h2, logits_t, router_b, w_gate, w_up, w_down, mods4, layer, ln_g, ln_b,
              alpha, seg_rows, n_batch, tm=512):
    idx, wts = route(logits_t, router_b)
    gidx, tile_expert, n_used, pos = moe_plan(idx, tm)
    xs = h2.at[gidx].get(mode="promise_in_bounds")
    ys = grouped_ffn(xs, tile_expert, n_used, w_gate, w_up, w_down, layer, tm)
    yg = ys.at[pos].get(mode="promise_in_bounds")
    return final_norm(x1, yg, wts.T, mods4, layer, ln_g, ln_b, alpha, seg_rows, n_batch)


def s5_matrices(lam_re, lam_im, log_dt, b_re, b_im, c_re, c_im, d_skip):
    f32 = F32
    tc = S5_TC
    n_g, n_p = lam_re.shape[1], lam_re.shape[2]
    n_c = b_re.shape[-1]
    nb = 128 // n_c
    n_q = n_g // nb
    lr, li = lam_re.astype(f32), lam_im.astype(f32)
    dt = jnp.exp(log_dt.astype(f32))[..., None]

    def powers(jvals):
        j = jnp.asarray(np.asarray(jvals, np.float32))[:, None, None, None]
        mag = jnp.exp(lr * dt * j)
        return mag * jnp.cos(li * dt * j), mag * jnp.sin(li * dt * j)

    up = np.arange(tc)
    pw_re, pw_im = powers(np.arange(tc + 1))
    lb_re, lb_im = pw_re[1], pw_im[1]
    den = lr * lr + li * li
    fr = ((lb_re - 1.0) * lr + lb_im * li) / den
    fi = (lb_im * lr - (lb_re - 1.0) * li) / den
    br, bi = b_re.astype(f32), b_im.astype(f32)
    bb_re = fr[..., None] * br - fi[..., None] * bi
    bb_im = fr[..., None] * bi + fi[..., None] * br
    cr, ci = c_re.astype(f32), c_im.astype(f32)

    def times_b(p_re, p_im):
        return (p_re[..., None] * bb_re[None] - p_im[..., None] * bb_im[None],
                p_re[..., None] * bb_im[None] + p_im[..., None] * bb_re[None])

    e_re, e_im = times_b(pw_re, pw_im)
    kmat = jnp.sum(cr[None, :, :, :, :, None] * e_re[:, :, :, None, :, :]
                   - ci[None, :, :, :, :, None] * e_im[:, :, :, None, :, :], axis=4)
    def lag_slab(k_dir):
        return k_dir.reshape(tc, n_q, nb, n_c, n_c).transpose(1, 0, 4, 2, 3).reshape(n_q, tc, n_c, nb * n_c)
    skip = (jnp.eye(n_c, dtype=f32)[None, None, :, None, :]
            * d_skip.astype(f32).reshape(n_q, nb, n_c)[:, None, None, :, :]).reshape(n_q, 1, n_c, nb * n_c)
    k_c = jnp.concatenate([lag_slab(kmat[:tc, 0]), lag_slab(kmat[:tc, 1]), skip], 1)

    def w_slab(e):
        return e.reshape(tc, n_q, nb, n_p, n_c).transpose(1, 0, 4, 2, 3).reshape(n_q, tc, n_c, nb * n_p)
    ef_re, ef_im = times_b(*powers(tc - 1 - up))
    w_c = jnp.stack([w_slab(ef_re[:, 0]), w_slab(ef_im[:, 0]),
                     w_slab(e_re[:tc, 1]), w_slab(e_im[:tc, 1])], 2)

    def v_slabs(d, p_re, p_im):
        f_re = cr[d][None] * p_re[:, :, None, :] - ci[d][None] * p_im[:, :, None, :]
        f_im = cr[d][None] * p_im[:, :, None, :] + ci[d][None] * p_re[:, :, None, :]
        slab = lambda m: m.reshape(tc, n_q, nb, n_c, n_p).transpose(1, 0, 3, 2, 4).reshape(n_q, tc, n_c, nb * n_p)
        return slab(f_re), slab(-f_im)
    pb_re, pb_im = powers(tc - up)
    vt_c = jnp.stack(v_slabs(0, pw_re[1:, 0], pw_im[1:, 0]) + v_slabs(1, pb_re[:, 1], pb_im[:, 1]), 2)
    dec = lambda m: m.reshape(1, n_g * n_p // 128, 1, 128)
    decay = jnp.concatenate([dec(pw_re[tc, 0]), dec(pw_im[tc, 0]), dec(pw_re[tc, 1]), dec(pw_im[tc, 1])], 0)
    return k_c, w_c, vt_c, decay


def _s5_chunk_rows(ref, n):
    return jnp.concatenate([ref[pl.ds(s, n, stride=S5_TC), :] for s in range(S5_TC)], axis=1).astype(BF16)


def _s5_expand(slab, group_lanes):
    rows = 128
    tiled = jnp.concatenate([slab] * (rows // slab.shape[0]), axis=0)
    r = lax.broadcasted_iota(jnp.int32, tiled.shape, 0) // S5_CH
    l = lax.broadcasted_iota(jnp.int32, tiled.shape, 1) // group_lanes
    return jnp.where(r == l, tiled, 0.0).astype(BF16)


def _s5_in_kernel(ul_ref, uc_ref, wc_ref, fr_ref, fi_ref, br_ref, bi_ref, w_ref, *, n_lat, n_ctx, n_batch):
    b = pl.program_id(1)
    n_plane = wc_ref.shape[1]
    st = wc_ref.shape[3]

    @pl.when(b == 0)
    def _():
        for s in range(S5_TC):
            for i in range(n_plane):
                w_ref[s * 128:(s + 1) * 128, i * st:(i + 1) * st] = _s5_expand(wc_ref[s, i], S5_P)

    w_lat = _dot(_s5_chunk_rows(ul_ref, n_lat), w_ref[...])
    w_ctx = _dot(_s5_chunk_rows(uc_ref, n_ctx), w_ref[...])
    nv = fr_ref.shape[0]
    for i, ref in enumerate((fr_ref, fi_ref, br_ref, bi_ref)):
        for c in range(nv):
            lanes = slice((i * nv + c) * 128, (i * nv + c + 1) * 128)
            ref[c, pl.ds(b, n_ctx, stride=n_batch), :] = w_ctx[:, lanes]
            ref[c, pl.ds(n_ctx * n_batch + b, n_lat, stride=n_batch), :] = w_lat[:, lanes]
            ref[c, pl.ds((n_ctx + n_lat) * n_batch + b, n_ctx, stride=n_batch), :] = w_ctx[:, lanes]


def _s5_scan_kernel(wfr, wfi, wbr, wbi, dec_ref, xfr, xfi, xbr, xbi, *, n_tiles):
    nv = wfr.shape[0]
    low = lax.broadcasted_iota(jnp.int32, (nv, 8, 128), 1) < 4
    a_fr, a_fi, a_br, a_bi = dec_ref[0], dec_ref[1], dec_ref[2], dec_ref[3]

    def half_step(s_re, s_im, a_re, a_im, w_re, w_im):
        return a_re * s_re - a_im * s_im + w_re, a_re * s_im + a_im * s_re + w_im

    def one_dir(w_re_ref, w_im_ref, x_re_ref, x_im_ref, row0, s_re, s_im, a_re, a_im, first_low):
        first = low if first_low else jnp.logical_not(low)
        wt_re, wt_im = w_re_ref[:, pl.ds(row0, 8), :], w_im_ref[:, pl.ds(row0, 8), :]
        wr_re, wr_im = pltpu.roll(wt_re, 4, 1), pltpu.roll(wt_im, 4, 1)
        mid_re, mid_im = half_step(s_re, s_im, a_re, a_im, wr_re, wr_im)
        x_re_ref[:, pl.ds(row0, 8), :] = jnp.where(first, s_re, mid_re)
        x_im_ref[:, pl.ds(row0, 8), :] = jnp.where(first, s_im, mid_im)
        m_re = jnp.where(first, pltpu.roll(mid_re, 4, 1), mid_re)
        m_im = jnp.where(first, pltpu.roll(mid_im, 4, 1), mid_im)
        w2_re = jnp.where(first, wr_re, wt_re)
        w2_im = jnp.where(first, wr_im, wt_im)
        return half_step(m_re, m_im, a_re, a_im, w2_re, w2_im)

    def body(i, carry):
        f_re, f_im, b_re, b_im = carry
        rf = pl.multiple_of(i * 8, 8)
        rb = pl.multiple_of((n_tiles - 1 - i) * 8, 8)
        f_re, f_im = one_dir(wfr, wfi, xfr, xfi, rf, f_re, f_im, a_fr, a_fi, True)
        b_re, b_im = one_dir(wbr, wbi, xbr, xbi, rb, b_re, b_im, a_br, a_bi, False)
        return f_re, f_im, b_re, b_im

    z = jnp.zeros((nv, 8, 128), F32)
    lax.fori_loop(0, n_tiles, body, (z, z, z, z))


def _s5_out_kernel(ul_ref, fr_ref, fi_ref, br_ref, bi_ref, kc_ref, vc_ref, y_ref, mt_ref, vt_ref,
                   *, n_lat, n_ctx, n_batch):
    b = pl.program_id(1)
    tc = S5_TC

    @pl.when(b == 0)
    def _():
        lag = [_s5_expand(kc_ref[j], S5_CH) for j in range(2 * tc)]
        diag = _s5_expand(kc_ref[0] + kc_ref[tc] + kc_ref[2 * tc], S5_CH)
        for s in range(tc):
            for t in range(tc):
                blk = diag if s == t else (lag[t - s] if t > s else lag[tc + s - t])
                mt_ref[s * 128:(s + 1) * 128, t * 128:(t + 1) * 128] = blk
        for t in range(tc):
            for i in range(vt_ref.shape[0]):
                vt_ref[i, t * 128:(t + 1) * 128, :] = _s5_expand(vc_ref[t, i], S5_P)

    y = _dot(_s5_chunk_rows(ul_ref, n_lat), mt_ref[...])
    row0 = n_ctx * n_batch + b
    for i, ref in enumerate((fr_ref, fi_ref, br_ref, bi_ref)):
        xs = jnp.concatenate([ref[c, pl.ds(row0, n_lat, stride=n_batch), :] for c in range(ref.shape[0])], 1)
        y = y + _dot_nt(xs.astype(BF16), vt_ref[i])
    for s in range(S5_TC):
        y_ref[pl.ds(s, n_lat, stride=S5_TC), :] = y[:, s * 128:(s + 1) * 128]


def s5_bidir(u, mats, n_batch, l_lat, l_ctx):
    assert n_batch == 4, "the chunk scan packs two chunks of 4 batch rows per 8-sublane tile"
    k_c, w_c, vt_c, decay = mats
    tc = S5_TC
    wd = u.shape[1]
    n_q = wd // 128
    lane_q = tc * 128
    st_q = (128 // S5_CH) * S5_P
    n_lat, n_ctx = l_lat // tc, l_ctx // tc
    nk = n_lat + 2 * n_ctx
    assert nk % 2 == 0
    rows = nk * n_batch
    ctx0 = (n_batch * l_lat) // l_ctx
    dims = dict(n_lat=n_lat, n_ctx=n_ctx, n_batch=n_batch)

    nv = st_q // 128
    plane = jax.ShapeDtypeStruct((n_q * nv, rows, 128), F32)
    plane_spec = pl.BlockSpec((nv, rows, 128), lambda q, b: (q, 0, 0))
    ul_spec = pl.BlockSpec((l_lat, 128), lambda q, b: (b, q))
    uc_spec = pl.BlockSpec((l_ctx, 128), lambda q, b: (ctx0 + b, q))
    w_planes = pl.pallas_call(
        functools.partial(_s5_in_kernel, **dims),
        grid=(n_q, n_batch),
        in_specs=[ul_spec, uc_spec, pl.BlockSpec((None,) + w_c.shape[1:], lambda q, b: (q, 0, 0, 0, 0))],
        out_specs=[plane_spec] * 4,
        out_shape=[plane] * 4,
        scratch_shapes=[pltpu.VMEM((lane_q, 4 * st_q), BF16)],
        compiler_params=_cparams("arbitrary", "arbitrary"),
        name="s5_chunk_in",
    )(u, u, w_c)

    blk = pl.BlockSpec((nv, rows, 128), lambda j: (j, 0, 0))
    x_planes = pl.pallas_call(
        functools.partial(_s5_scan_kernel, n_tiles=rows // 8),
        grid=(n_q,),
        in_specs=[blk] * 4 + [pl.BlockSpec((4, nv, 1, 128), lambda j: (0, j, 0, 0))],
        out_specs=[blk] * 4,
        out_shape=[plane] * 4,
        compiler_params=_cparams("arbitrary"),
        name="s5_chunk_scan",
    )(*w_planes, decay)

    return pl.pallas_call(
        functools.partial(_s5_out_kernel, **dims),
        grid=(n_q, n_batch),
        in_specs=[ul_spec] + [plane_spec] * 4
                 + [pl.BlockSpec((None,) + k_c.shape[1:], lambda q, b: (q, 0, 0, 0)),
                    pl.BlockSpec((None,) + vt_c.shape[1:], lambda q, b: (q, 0, 0, 0, 0))],
        out_specs=pl.BlockSpec((l_lat, 128), lambda q, b: (b, q)),
        out_shape=jax.ShapeDtypeStruct((n_batch * l_lat, wd), F32),
        scratch_shapes=[pltpu.VMEM((lane_q, lane_q), BF16), pltpu.VMEM((4, lane_q, st_q), BF16)],
        compiler_params=_cparams("arbitrary", "arbitrary"),
        name="s5_chunk_out",
    )(u, *x_planes, k_c, vt_c)


def kernel(x, c, ctx, c_ctx, ada_w, ada_b, ln_mix_g, ln_mix_b, ln_ffn_g, ln_ffn_b, ev_w_in, ev_gate_w2,
           ev_gate_b, ev_rpb, ev_norm_g, ev_w_out, od_w_in, od_lam_re, od_lam_im, od_log_dt, od_b_re,
           od_b_im, od_c_re, od_c_im, od_d, od_w_glu, od_b_glu, od_w_out, router_w, router_b,
           moe_w_gate, moe_w_up, moe_w_down):
    n_batch, l_lat, d = x.shape
    l_ctx = ctx.shape[1]
    depth = ada_w.shape[0]
    assert depth == 2, "one even (NA + GLA) layer followed by one odd (S5) layer"
    alpha = (2.0 * depth) ** 0.25
    n_lat = n_batch * l_lat

    cvec = jnp.concatenate([c, c_ctx[None], jnp.zeros((8 - n_batch - 1, d), F32)], 0)
    mods = compute_mods(cvec, ada_w, ada_b)
    mods4 = mods.reshape(depth, 8, 1, N_MOD * d)
    x_lat, x_ctx = x.reshape(n_lat, d), ctx.reshape(n_batch * l_ctx, d)
    router_wt = router_w.T.astype(F32)

    na_w = NA_HEADS * NA_DH
    wk = GLA_HEADS * GLA_DK
    wv = GLA_HEADS * GLA_DV
    ev_in = ev_w_in.shape[2]
    pad = (-ev_in) % 256
    w_in = jnp.pad(ev_w_in[0], ((0, 0), (0, pad))).astype(BF16)
    proj = mod_matmul(x_lat, x_ctx, mods4, 0, w_in, l_lat, n_batch, tm=512, tn=(ev_in + pad) // 2)
    a_lat, a_ctx = na_attention(proj, na_bias_table(ev_rpb[0], l_lat // GRID_W), n_batch, l_lat, l_ctx)
    col_lr = (3 * na_w + 2 * wk + 2 * wv) // 128
    g2 = jnp.zeros((2, 128, wk), F32)
    g2 = g2.at[0, 0:GLA_RANK].set(ev_gate_w2[0, 0]).at[1, GLA_RANK:2 * GLA_RANK].set(ev_gate_w2[0, 1])
    o_f, o_b = gla_bidir(proj, g2.astype(BF16), ev_gate_b[0].reshape(2, 1, wk), rope_tables(l_lat),
                         n_batch, l_lat, l_ctx,
                         col_q=3 * na_w // wk, col_k=(3 * na_w + wk) // wk,
                         col_v=(3 * na_w + 2 * wk) // wv, col_lr=col_lr)
    x1, h2, logits_t = even_out(a_lat, a_ctx, o_f, o_b, proj, (3 * na_w + 2 * wk + wv) // wv, x_lat, x_ctx,
                                mods4, 0, ev_norm_g[0], ev_w_out[0].astype(BF16), ln_mix_g[0], ln_mix_b[0],
                                router_wt, alpha, l_lat, n_batch)
    w_gate, w_up, w_down = moe_w_gate, moe_w_up, moe_w_down
    rows = moe_block(x1, h2, logits_t, router_b, w_gate, w_up, w_down, mods4, 0,
                     ln_ffn_g[0], ln_ffn_b[0], alpha, l_lat, n_batch)

    u = mod_matmul(rows, None, mods4, 1, od_w_in[0].astype(BF16), l_lat, n_batch)
    mats = s5_matrices(od_lam_re[0], od_lam_im[0], od_log_dt[0], od_b_re[0], od_b_im[0],
                       od_c_re[0], od_c_im[0], od_d[0])
    y5 = s5_bidir(u, mats, n_batch, l_lat, l_ctx)
    x1, h2, logits_t = odd_out(y5, rows, mods4, 1, od_w_glu[0].astype(BF16), od_b_glu[0],
                               od_w_out[0].astype(BF16), ln_mix_g[1], ln_mix_b[1], router_wt,
                               alpha, l_lat, n_batch)
    out = moe_block(x1, h2, logits_t, router_b, w_gate, w_up, w_down, mods4, 1,
                    ln_ffn_g[1], ln_ffn_b[1], alpha, l_lat, n_batch)
    return out.reshape(n_batch, l_lat, d)
```

```python
import functools
import math

import numpy as np
import jax
import jax.numpy as jnp
from jax import lax
from jax.experimental import pallas as pl
from jax.experimental.pallas import tpu as pltpu

F32 = jnp.float32
BF16 = jnp.bfloat16
HIGHEST = lax.Precision.HIGHEST

N_MOD = 6
LN_EPS = 1e-5
NORM_EPS = 1e-6

GRID_W = 64
NA_HEADS = 8
NA_DH = 128
NA_KR = 8
NA_KC = 16

GLA_HEADS = 4
GLA_DK = 128
GLA_DV = 256
GLA_RANK = 16
GLA_TAU = 16.0
GLA_CHUNK = 64
ROPE_BASE = 10000.0

S5_CH = 16
S5_P = 64
S5_TC = 16

N_EXPERTS = 16
N_GROUPS = 4
TOP_K = 2

VMEM_LIMIT = 56 * 1024 * 1024
NEG_BIG = -1e30


def _cparams(*sem):
    return pltpu.CompilerParams(dimension_semantics=sem, vmem_limit_bytes=VMEM_LIMIT)


def _dot(a, b, precision=None):
    return jnp.dot(a, b, preferred_element_type=F32, precision=precision)


def _dot_nt(a, b, precision=None):
    return lax.dot_general(a, b, (((1,), (1,)), ((), ())), preferred_element_type=F32, precision=precision)


def _dot_tn(a, b):
    return lax.dot_general(a, b, (((0,), (0,)), ((), ())), preferred_element_type=F32)


def _mods_kernel(s_ref, w_ref, b_ref, o_ref):
    s = s_ref[...]
    s = s * jax.nn.sigmoid(s)
    o_ref[0] = _dot(s, w_ref[0], HIGHEST) + b_ref[0]


def compute_mods(cvec, ada_w, ada_b, tn=1024):
    n_layer, d, n = ada_w.shape
    tn = math.gcd(tn, n)
    return pl.pallas_call(
        _mods_kernel,
        grid=(n_layer, n // tn),
        in_specs=[pl.BlockSpec((8, d), lambda l, j: (0, 0)),
                  pl.BlockSpec((1, d, tn), lambda l, j: (l, 0, j)),
                  pl.BlockSpec((1, 1, tn), lambda l, j: (l, 0, j))],
        out_specs=pl.BlockSpec((1, 8, tn), lambda l, j: (l, 0, j)),
        out_shape=jax.ShapeDtypeStruct((n_layer, 8, n), F32),
        compiler_params=_cparams("arbitrary", "arbitrary"),
        name="ada_mods",
    )(cvec, ada_w, ada_b.reshape(n_layer, 1, n))


def _mod_spec(d, layer, which, seg_of_tile):
    return pl.BlockSpec((None, None, 1, d), lambda i, *_: (layer, seg_of_tile(i), 0, which))


def _seg_fn(tm, seg_rows, n_batch):
    return lambda i: jnp.minimum((i * tm) // seg_rows, n_batch)


def _two_source_specs(lat, ctx, tm, tile_of):
    n_lat_tiles = lat.shape[0] // tm
    d = lat.shape[1]
    return [pl.BlockSpec((tm, d), lambda *g: (jnp.minimum(tile_of(*g), n_lat_tiles - 1), 0)),
            pl.BlockSpec((tm, d), lambda *g: (jnp.maximum(tile_of(*g) - n_lat_tiles, 0), 0))]


def _pick_rows(lat_ref, ctx_ref, tile, n_lat_tiles):
    return jnp.where(tile < n_lat_tiles, lat_ref[...], ctx_ref[...])


def _modmm_kernel(xl_ref, xc_ref, s1_ref, s0_ref, w_ref, o_ref, *, n_lat_tiles):
    x = _pick_rows(xl_ref, xc_ref, pl.program_id(1), n_lat_tiles)
    h = x * (1.0 + s1_ref[...]) + s0_ref[...]
    o_ref[...] = _dot(h.astype(BF16), w_ref[...])


def mod_matmul(x_lat, x_ctx, mods4, layer, w_bf16, seg_rows, n_batch, tm=256, tn=None):
    if x_ctx is None:
        t, x_ctx = x_lat.shape[0], x_lat
    else:
        t = x_lat.shape[0] + x_ctx.shape[0]
    d = x_lat.shape[1]
    n = w_bf16.shape[1]
    tn = n if tn is None else tn
    seg = _seg_fn(tm, seg_rows, n_batch)
    return pl.pallas_call(
        functools.partial(_modmm_kernel, n_lat_tiles=x_lat.shape[0] // tm),
        grid=(n // tn, t // tm),
        in_specs=_two_source_specs(x_lat, x_ctx, tm, lambda j, i: i) + [
            pl.BlockSpec((None, None, 1, d), lambda j, i: (layer, seg(i), 0, 1)),
            pl.BlockSpec((None, None, 1, d), lambda j, i: (layer, seg(i), 0, 0)),
            pl.BlockSpec((d, tn), lambda j, i: (0, j))],
        out_specs=pl.BlockSpec((tm, tn), lambda j, i: (i, j)),
        out_shape=jax.ShapeDtypeStruct((t, n), F32),
        compiler_params=_cparams("arbitrary", "arbitrary"),
        name="mod_matmul",
    )(x_lat, x_ctx, mods4, mods4, w_bf16)


NA_RB = 4
NA_BAND = NA_RB + NA_KR - 1


def _na_row_start(r, rows):
    return min(max(r - NA_KR // 2, 0), rows - NA_KR)


def na_bias_table(rpb, rows):
    w = GRID_W
    q = np.arange(w)
    kc = np.arange(w)
    win0 = np.clip(q - NA_KC // 2, 0, w - NA_KC)
    ok = (kc[None, :] >= win0[:, None]) & (kc[None, :] < win0[:, None] + NA_KC)
    dc = np.clip(kc[None, :] - q[:, None] + NA_KC - 1, 0, 2 * NA_KC - 2)
    pick = ((dc[None] == np.arange(2 * NA_KC - 1)[:, None, None]) & ok[None]).astype(np.float32)
    colb = jnp.einsum("hrd,dqk->hrqk", rpb.astype(F32), jnp.asarray(pick), precision=HIGHEST)
    colb = jnp.where(ok[None, None], colb, NEG_BIG)
    neg = jnp.full((rpb.shape[0], w, w), NEG_BIG, F32)

    def block(r0):
        band0 = min(max(r0 - NA_KR // 2, 0), rows - NA_BAND)
        out = []
        for r in range(r0, r0 + NA_RB):
            rs = _na_row_start(r, rows)
            first = rs - r + NA_KR - 1
            cols = [neg] * (rs - band0) + [colb[:, first + j] for j in range(NA_KR)]
            cols += [neg] * (NA_BAND - len(cols))
            out.append(jnp.concatenate(cols, -1))
        return jnp.concatenate(out, 1)

    return jnp.stack([block(0), block(NA_RB), block(rows - NA_RB)], 1)


def _na_kernel(q_ref, k_ref, v_ref, qc_ref, kc_ref, vc_ref, bias_ref, o_ref, oc_ref, kbf, vbf, *, rows):
    w = GRID_W
    n_blk = rows // NA_RB
    scale = NA_DH ** -0.5
    kbf[...] = k_ref[...].astype(BF16)
    vbf[...] = v_ref[...].astype(BF16)
    kc = kc_ref[...].astype(BF16)
    vc = vc_ref[...].astype(BF16)

    def body(pair, carry):
        blocks = (2 * pair, 2 * pair + 1)
        q0, scores = [], []
        for i in blocks:
            r0 = i * NA_RB
            band0 = jnp.clip(r0 - NA_KR // 2, 0, rows - NA_BAND)
            variant = jnp.where(i == 0, 0, jnp.where(i == n_blk - 1, 2, 1))
            q0.append(pl.multiple_of(r0 * w, NA_RB * w))
            k0 = pl.multiple_of(band0 * w, w)
            q = (q_ref[pl.ds(q0[-1], NA_RB * w), :] * scale).astype(BF16)
            s_loc = _dot_nt(q, kbf[pl.ds(k0, NA_BAND * w), :]) + bias_ref[variant]
            scores.append((s_loc, _dot_nt(q, kc), k0))
        probs = []
        for s_loc, s_ctx, k0 in scores:
            m = jnp.maximum(jnp.max(s_loc, -1, keepdims=True), jnp.max(s_ctx, -1, keepdims=True))
            p_loc = jnp.exp(s_loc - m)
            p_ctx = jnp.exp(s_ctx - m)
            den = jnp.sum(p_loc, -1, keepdims=True) + jnp.sum(p_ctx, -1, keepdims=True)
            probs.append((p_loc.astype(BF16), p_ctx.astype(BF16), den, k0))
        for q_start, (p_loc, p_ctx, den, k0) in zip(q0, probs):
            o = _dot(p_loc, vbf[pl.ds(k0, NA_BAND * w), :]) + _dot(p_ctx, vc)
            o_ref[pl.ds(q_start, NA_RB * w), :] = o / den
        return carry

    lax.fori_loop(0, n_blk // 2, body, 0)

    qc = (qc_ref[...] * scale).astype(BF16)
    s = _dot_nt(qc, kc)
    p = jnp.exp(s - jnp.max(s, -1, keepdims=True))
    oc_ref[...] = _dot(p.astype(BF16), vc) / jnp.sum(p, -1, keepdims=True)


def na_attention(proj, bias_tab, n_batch, l_lat, l_ctx):
    h = NA_HEADS
    dh = NA_DH
    rows = l_lat // GRID_W
    ctx0 = (n_batch * l_lat) // l_ctx
    return pl.pallas_call(
        functools.partial(_na_kernel, rows=rows),
        grid=(n_batch, h),
        in_specs=[pl.BlockSpec((l_lat, dh), lambda b, hh: (b, hh)),
                  pl.BlockSpec((l_lat, dh), lambda b, hh: (b, h + hh)),
                  pl.BlockSpec((l_lat, dh), lambda b, hh: (b, 2 * h + hh)),
                  pl.BlockSpec((l_ctx, dh), lambda b, hh: (ctx0 + b, hh)),
                  pl.BlockSpec((l_ctx, dh), lambda b, hh: (ctx0 + b, h + hh)),
                  pl.BlockSpec((l_ctx, dh), lambda b, hh: (ctx0 + b, 2 * h + hh)),
                  pl.BlockSpec((None,) + bias_tab.shape[1:], lambda b, hh: (hh, 0, 0, 0))],
        out_specs=[pl.BlockSpec((l_lat, dh), lambda b, hh: (b, hh)),
                   pl.BlockSpec((l_ctx, dh), lambda b, hh: (b, hh))],
        out_shape=[jax.ShapeDtypeStruct((n_batch * l_lat, h * dh), F32),
                   jax.ShapeDtypeStruct((n_batch * l_ctx, h * dh), F32)],
        scratch_shapes=[pltpu.VMEM((l_lat, dh), BF16), pltpu.VMEM((l_lat, dh), BF16)],
        compiler_params=_cparams("arbitrary", "arbitrary"),
        name="na_attention",
    )(proj, proj, proj, proj, proj, proj, bias_tab)


def rope_tables(l_lat):
    half = GLA_DK // 2
    nf = half // 2
    inv = ROPE_BASE ** (-np.arange(nf, dtype=np.float64) / nf)
    t = np.arange(l_lat)
    lane = np.arange(GLA_DK)
    pos = np.where(lane[None, :] < half, (t // GRID_W)[:, None], (t % GRID_W)[:, None]).astype(np.float64)
    ang = pos * inv[lane % nf][None, :]
    first = (lane % half) < nf
    cos = np.cos(ang)
    sin_a = np.where(first[None, :], -np.sin(ang), 0.0)
    sin_b = np.where(first[None, :], 0.0, np.sin(ang))
    return jnp.asarray(cos, F32), jnp.asarray(sin_a, F32), jnp.asarray(sin_b, F32)


GLA_PREP_CHUNKS = 8


def _gla_prep_kernel(q_ref, k_ref, lr_ref, cos_ref, sa_ref, sb_ref, g2_ref, gb_ref,
                     qdf, kdf, krf, elf, qdb, kdb, krb, elb, *, n_lat_tiles):
    c = GLA_CHUNK
    nch = GLA_PREP_CHUNKS
    nf = GLA_DK // 4
    gscale = GLA_DK ** -0.5
    wk = GLA_HEADS * GLA_DK
    is_lat = pl.program_id(0) < n_lat_tiles
    cos = jnp.where(is_lat, cos_ref[...], 1.0)
    sa = jnp.where(is_lat, sa_ref[...], 0.0)
    sb = jnp.where(is_lat, sb_ref[...], 0.0)

    def rope(x):
        return x * cos + pltpu.roll(x, GLA_DK - nf, 1) * sa + pltpu.roll(x, nf, 1) * sb

    qs, ks_ = [], []
    for h in range(GLA_HEADS):
        hs = slice(h * GLA_DK, (h + 1) * GLA_DK)
        qs.append(rope(q_ref[:, hs]) * gscale)
        ks_.append(rope(k_ref[:, hs]))

    lr = lr_ref[...].astype(BF16)
    row = lax.broadcasted_iota(jnp.int32, (c, c), 0)
    col = lax.broadcasted_iota(jnp.int32, (c, c), 1)
    for d, (qd, kd, kr, el) in enumerate(((qdf, kdf, krf, elf), (qdb, kdb, krb, elb))):
        reverse = d == 1
        tri = ((col >= row) if reverse else (col <= row)).astype(BF16)
        z = _dot(lr, g2_ref[d]) + gb_ref[d]
        g = (jnp.minimum(z, 0.0) - jnp.log1p(jnp.exp(-jnp.abs(z)))) * (1.0 / GLA_TAU)
        g_hi = g.astype(BF16)
        r1 = g - g_hi.astype(F32)
        g_mid = r1.astype(BF16)
        g_lo = (r1 - g_mid.astype(F32)).astype(BF16)
        parts = []
        for ci in range(nch):
            rs = slice(ci * c, (ci + 1) * c)
            parts.append(_dot(tri, g_hi[rs]) + _dot(tri, g_mid[rs]) + _dot(tri, g_lo[rs]))
        b3 = jnp.concatenate(parts, 0).reshape(nch, c, wk)
        bl3 = b3[:, 0:1, :] if reverse else b3[:, c - 1:c, :]
        el[...] = jnp.exp(bl3)
        e_b = jnp.exp(b3).reshape(nch * c, wk)
        e_nb = jnp.exp(-b3).reshape(nch * c, wk)
        e_rem = jnp.exp(bl3 - b3).reshape(nch * c, wk)
        for h in range(GLA_HEADS):
            hs = slice(h * GLA_DK, (h + 1) * GLA_DK)
            qd[:, hs] = (qs[h] * e_b[:, hs]).astype(BF16)
            kd[:, hs] = (ks_[h] * e_nb[:, hs]).astype(BF16)
            kr[:, hs] = (ks_[h] * e_rem[:, hs]).astype(BF16)


def _gla_scan_kernel(qdf, kdf, krf, elf, vf, qdb, kdb, krb, elb, vb, of_ref, ob_ref, st_ref):
    @pl.when(pl.program_id(1) == 0)
    def _():
        st_ref[...] = jnp.zeros_like(st_ref)

    c = GLA_CHUNK
    row = lax.broadcasted_iota(jnp.int32, (c, c), 0)
    col = lax.broadcasted_iota(jnp.int32, (c, c), 1)
    dirs = ((qdf, kdf, krf, elf, vf, of_ref, col <= row, (0, 1)), (qdb, kdb, krb, elb, vb, ob_ref, col >= row, (1, 0)))
    chains = [(d, h) for d in range(2) for h in range(GLA_HEADS)]
    hs = lambda h: slice(h * GLA_DK, (h + 1) * GLA_DK)
    vs = lambda h: slice(h * GLA_DV, (h + 1) * GLA_DV)
    rows = lambda k: slice(k * c, (k + 1) * c)
    order = lambda d, step: rows(dirs[d][7][step])
    q_dec = [[dirs[d][0][order(d, s), hs(h)] for d, h in chains] for s in range(2)]
    v_bf = [[dirs[d][4][order(d, s), vs(h)].astype(BF16) for d, h in chains] for s in range(2)]
    att = [[jnp.where(dirs[d][6], _dot_nt(q_dec[s][n], dirs[d][1][order(d, s), hs(h)]), 0.0).astype(BF16)
            for n, (d, h) in enumerate(chains)] for s in range(2)]
    state = [st_ref[d, h] for d, h in chains]
    for s in range(2):
        for n, (d, h) in enumerate(chains):
            dirs[d][5][order(d, s), vs(h)] = (_dot(att[s][n], v_bf[s][n])
                                              + _dot_nt(q_dec[s][n], state[n].astype(BF16)))
        state = [state[n] * dirs[d][3][dirs[d][7][s], :, hs(h)]
                 + _dot_tn(v_bf[s][n], dirs[d][2][order(d, s), hs(h)]) for n, (d, h) in enumerate(chains)]
    for n, (d, h) in enumerate(chains):
        st_ref[d, h] = state[n]


def gla_bidir(proj, g2, gb, tables, n_batch, l_lat, l_ctx, col_q, col_k, col_v, col_lr):
    c = GLA_CHUNK
    nc = l_ctx // c
    nl = l_lat // c
    nz = nl + 2 * nc
    steps = nl + nc
    wk = GLA_HEADS * GLA_DK
    wv = GLA_HEADS * GLA_DV
    t_rows = n_batch * (l_lat + l_ctx)

    tp = GLA_PREP_CHUNKS * c
    lat_tiles = l_lat // tp
    n_lat_tiles = n_batch * lat_tiles
    cos, sa, sb = tables
    tab = pl.BlockSpec((tp, GLA_DK), lambda i: (jnp.where(i < n_lat_tiles, i % lat_tiles, 0), 0))
    row_bf = jax.ShapeDtypeStruct((t_rows, wk), BF16)
    last = jax.ShapeDtypeStruct((t_rows // c, 1, wk), F32)
    row_spec = pl.BlockSpec((tp, wk), lambda i: (i, 0))
    last_spec = pl.BlockSpec((GLA_PREP_CHUNKS, 1, wk), lambda i: (i, 0, 0))
    prep = pl.pallas_call(
        functools.partial(_gla_prep_kernel, n_lat_tiles=n_lat_tiles),
        grid=(t_rows // tp,),
        in_specs=[pl.BlockSpec((tp, wk), lambda i: (i, col_q)),
                  pl.BlockSpec((tp, wk), lambda i: (i, col_k)),
                  pl.BlockSpec((tp, 128), lambda i: (i, col_lr)),
                  tab, tab, tab,
                  pl.BlockSpec((2, 128, wk), lambda i: (0, 0, 0)),
                  pl.BlockSpec((2, 1, wk), lambda i: (0, 0, 0))],
        out_specs=[row_spec, row_spec, row_spec, last_spec] * 2,
        out_shape=[row_bf, row_bf, row_bf, last] * 2,
        compiler_params=_cparams("arbitrary"),
        name="gla_prep",
    )(proj, proj, proj, cos, sa, sb, g2, gb)

    def zblk(b, j):
        lat = b * nl + (j - nc)
        ctx = n_batch * nl + b * nc + jnp.where(j < nc, j, j - nc - nl)
        return jnp.where((j >= nc) & (j < nc + nl), lat, ctx)

    assert nc % 2 == 0 and nl % 2 == 0
    fwd = lambda b, i: zblk(b, 2 * i) // 2
    bwd = lambda b, i: zblk(b, nz - 2 - 2 * i) // 2
    c2 = 2 * c

    def dir_specs(blk):
        return [pl.BlockSpec((c2, wk), lambda b, i: (blk(b, i), 0)),
                pl.BlockSpec((c2, wk), lambda b, i: (blk(b, i), 0)),
                pl.BlockSpec((c2, wk), lambda b, i: (blk(b, i), 0)),
                pl.BlockSpec((2, 1, wk), lambda b, i: (blk(b, i), 0, 0)),
                pl.BlockSpec((c2, wv), lambda b, i: (blk(b, i), col_v))]

    return pl.pallas_call(
        _gla_scan_kernel,
        grid=(n_batch, steps // 2),
        in_specs=dir_specs(fwd) + dir_specs(bwd),
        out_specs=[pl.BlockSpec((c2, wv), lambda b, i: (fwd(b, i), 0)),
                   pl.BlockSpec((c2, wv), lambda b, i: (bwd(b, i), 0))],
        out_shape=[jax.ShapeDtypeStruct((t_rows, wv), F32), jax.ShapeDtypeStruct((t_rows, wv), F32)],
        scratch_shapes=[pltpu.VMEM((2, GLA_HEADS, GLA_DV, GLA_DK), F32)],
        compiler_params=_cparams("arbitrary", "arbitrary"),
        name="gla_scan",
    )(*prep[0:4], proj, *prep[4:8], proj)


def _post_mix(out, x, m2_ref, m3_ref, m4_ref, lg_ref, lb_ref, wr_ref, alpha, x1_ref, h2_ref, lt_ref, rows):
    y = alpha * x + m2_ref[...] * out
    mu = jnp.mean(y, -1, keepdims=True)
    yc = y - mu
    var = jnp.mean(yc * yc, -1, keepdims=True)
    x1 = yc * lax.rsqrt(var + LN_EPS) * lg_ref[...] + lb_ref[...]
    h2 = x1 * (1.0 + m4_ref[...]) + m3_ref[...]
    x1_ref[rows, :] = x1
    n_exp = lt_ref.shape[0]
    h2_hi = h2.astype(BF16)
    h2_lo = (h2 - h2_hi.astype(F32)).astype(BF16)
    h2_ref[rows, :] = h2_hi
    wr = wr_ref[...]
    wr_hi = wr.astype(BF16)
    wr_lo = (wr - wr_hi.astype(F32)).astype(BF16)
    both = _dot_nt(jnp.concatenate([wr_hi, wr_lo], 0), h2_hi)
    lt_ref[:, rows] = both[:n_exp] + both[n_exp:] + _dot_nt(wr_hi, h2_lo)


def _row_halves(n):
    return (slice(0, n // 2), slice(n // 2, n))


def _even_out_kernel(al_ref, ac_ref, of_ref, ob_ref, r_ref, xl_ref, xc_ref, m2_ref, m3_ref, m4_ref, ng_ref,
                     wo_ref, lg_ref, lb_ref, wr_ref, x1_ref, h2_ref, lt_ref, *, alpha, n_lat_tiles):
    is_lat = pl.program_id(0) < n_lat_tiles
    outs = []
    for rows in _row_halves(of_ref.shape[0]):
        o = of_ref[rows, :] + ob_ref[rows, :]
        r = r_ref[rows, :]
        gate = r * jax.nn.sigmoid(r)
        mixed = [jnp.where(is_lat, al_ref[rows, :], ac_ref[rows, :]).astype(BF16)]
        for h in range(GLA_HEADS):
            vs = slice(h * GLA_DV, (h + 1) * GLA_DV)
            oh = o[:, vs]
            nrm = oh * lax.rsqrt(jnp.mean(oh * oh, -1, keepdims=True) + NORM_EPS) * ng_ref[...]
            mixed.append((nrm * gate[:, vs]).astype(BF16))
        outs.append(_dot(jnp.concatenate(mixed, axis=1), wo_ref[...]))
    x = jnp.where(is_lat, xl_ref[...], xc_ref[...])
    _post_mix(jnp.concatenate(outs, 0), x, m2_ref, m3_ref, m4_ref, lg_ref, lb_ref, wr_ref, alpha,
              x1_ref, h2_ref, lt_ref, slice(None))


def _post_specs(d, layer, seg, tm, n_exp):
    ins = [_mod_spec(d, layer, 2, seg), _mod_spec(d, layer, 3, seg), _mod_spec(d, layer, 4, seg)]
    tail = [pl.BlockSpec((1, d), lambda i: (0, 0)), pl.BlockSpec((1, d), lambda i: (0, 0)),
            pl.BlockSpec((n_exp, d), lambda i: (0, 0))]
    outs =[pl.BlockSpec((tm, d), lambda i: (i, 0)), pl.BlockSpec((tm, d), lambda i: (i, 0)),
            pl.BlockSpec((n_exp, tm), lambda i: (0, i))]
    return ins, tail, outs


def _post_shapes(t, d, n_exp):
    return [jax.ShapeDtypeStruct((t, d), F32), jax.ShapeDtypeStruct((t, d), BF16),
            jax.ShapeDtypeStruct((n_exp, t), F32)]


def even_out(a_lat, a_ctx, o_f, o_b, proj, col_r, x_lat, x_ctx, mods4, layer, norm_g, w_out_bf16, ln_g, ln_b,
             router_wt, alpha, seg_rows, n_batch, tm=256):
    t = x_lat.shape[0] + x_ctx.shape[0]
    d = x_lat.shape[1]
    na = a_lat.shape[1]
    wv = o_f.shape[1]
    n_exp = router_wt.shape[0]
    seg = _seg_fn(tm, seg_rows, n_batch)
    ins, tail, outs = _post_specs(d, layer, seg, tm, n_exp)
    tile_of = lambda i: i
    return pl.pallas_call(
        functools.partial(_even_out_kernel, alpha=alpha, n_lat_tiles=x_lat.shape[0] // tm),
        grid=(t // tm,),
        in_specs=(_two_source_specs(a_lat, a_ctx, tm, tile_of)
                  + [pl.BlockSpec((tm, wv), lambda i: (i, 0)),
                     pl.BlockSpec((tm, wv), lambda i: (i, 0)),
                     pl.BlockSpec((tm, wv), lambda i: (i, col_r))]
                  + _two_source_specs(x_lat, x_ctx, tm, tile_of) + ins
                  + [pl.BlockSpec((1, GLA_DV), lambda i: (0, 0)),
                     pl.BlockSpec((na + wv, d), lambda i: (0, 0))] + tail),
        out_specs=outs,
        out_shape=_post_shapes(t, d, n_exp),
        compiler_params=_cparams("arbitrary"),
        name="even_out",
    )(a_lat, a_ctx, o_f, o_b, proj, x_lat, x_ctx, mods4, mods4, mods4, norm_g.reshape(1, -1), w_out_bf16,
      ln_g.reshape(1, -1), ln_b.reshape(1, -1), router_wt)


def _odd_out_kernel(y_ref, x_ref, m2_ref, m3_ref, m4_ref, wg_ref, bg_ref, wo_ref,
                    lg_ref, lb_ref, wr_ref, x1_ref, h2_ref, lt_ref, *, alpha):
    outs = []
    for r in _row_halves(y_ref.shape[0]):
        g = jax.nn.gelu(y_ref[r, :], approximate=True)
        z = _dot(g.astype(BF16), wg_ref[...]) + bg_ref[...]
        outs.append(_dot((g * jax.nn.sigmoid(z)).astype(BF16), wo_ref[...]))
    _post_mix(jnp.concatenate(outs, 0), x_ref[...], m2_ref, m3_ref, m4_ref, lg_ref, lb_ref, wr_ref, alpha,
              x1_ref, h2_ref, lt_ref, slice(None))


def odd_out(y, x, mods4, layer, w_glu_bf16, b_glu, w_out_bf16, ln_g, ln_b, router_wt,
            alpha, seg_rows, n_batch, tm=256):
    t, w5 = y.shape
    d = x.shape[1]
    n_exp = router_wt.shape[0]
    seg = _seg_fn(tm, seg_rows, n_batch)
    ins, tail, outs = _post_specs(d, layer, seg, tm, n_exp)
    return pl.pallas_call(
        functools.partial(_odd_out_kernel, alpha=alpha),
        grid=(t // tm,),
        in_specs=([pl.BlockSpec((tm, w5), lambda i: (i, 0)), pl.BlockSpec((tm, d), lambda i: (i, 0))] + ins
                  + [pl.BlockSpec((w5, w5), lambda i: (0, 0)),
                     pl.BlockSpec((1, w5), lambda i: (0, 0)),
                     pl.BlockSpec((w5, d), lambda i: (0, 0))] + tail),
        out_specs=outs,
        out_shape=_post_shapes(t, d, n_exp),
        compiler_params=_cparams("arbitrary"),
        name="odd_out",
    )(y, x, mods4, mods4, mods4, w_glu_bf16, b_glu.reshape(1, -1), w_out_bf16,
      ln_g.reshape(1, -1), ln_b.reshape(1, -1), router_wt)


def _route_kernel(lt_ref, rb_ref, idx_ref, w_ref):
    eg = N_EXPERTS // N_GROUPS
    logits = lt_ref[...]
    aff = jax.nn.sigmoid(logits)
    sel = aff + rb_ref[...]
    s = [sel[e:e + 1, :] for e in range(N_EXPERTS)]
    a = [aff[e:e + 1, :] for e in range(N_EXPERTS)]

    def top2_sum(v):
        hi1, lo1 = jnp.maximum(v[0], v[1]), jnp.minimum(v[0], v[1])
        hi2, lo2 = jnp.maximum(v[2], v[3]), jnp.minimum(v[2], v[3])
        return jnp.maximum(hi1, hi2) + jnp.maximum(jnp.minimum(hi1, hi2), jnp.maximum(lo1, lo2))

    best = top2_sum(s[0:eg])
    grp = jnp.zeros_like(best, dtype=jnp.int32)
    for g in range(1, N_GROUPS):
        sc = top2_sum(s[g * eg:(g + 1) * eg])
        better = sc > best
        best = jnp.where(better, sc, best)
        grp = jnp.where(better, g, grp)

    def pick(vals, j):
        out = vals[j]
        for g in range(1, N_GROUPS):
            out = jnp.where(grp == g, vals[g * eg + j], out)
        return out

    sv = [pick(s, j) for j in range(eg)]
    av = [pick(a, j) for j in range(eg)]

    def argmax_first(vals, exclude):
        bi = jnp.zeros_like(grp)
        bv = jnp.where(exclude == 0, -jnp.inf, vals[0]) if exclude is not None else vals[0]
        for j in range(1, eg):
            vj = jnp.where(exclude == j, -jnp.inf, vals[j]) if exclude is not None else vals[j]
            better = vj > bv
            bv = jnp.where(better, vj, bv)
            bi = jnp.where(better, j, bi)
        return bi

    i1 = argmax_first(sv, None)
    i2 = argmax_first(sv, i1)

    def take(vals, i):
        out = vals[0]
        for j in range(1, eg):
            out = jnp.where(i == j, vals[j], out)
        return out

    w1 = take(av, i1)
    w2 = take(av, i2)
    tot = w1 + w2
    idx_ref[0:1, :] = grp * eg + i1
    idx_ref[1:2, :] = grp * eg + i2
    w_ref[0:1, :] = w1 / tot
    w_ref[1:2, :] = w2 / tot


def route(logits_t, router_b, tile=1024):
    n_exp, t = logits_t.shape
    tile = math.gcd(tile, t)
    return pl.pallas_call(
        _route_kernel,
        grid=(t // tile,),
        in_specs=[pl.BlockSpec((n_exp, tile), lambda i: (0, i)),
                  pl.BlockSpec((n_exp, 1), lambda i: (0, 0))],
        out_specs=[pl.BlockSpec((TOP_K, tile), lambda i: (0, i)),
                   pl.BlockSpec((TOP_K, tile), lambda i: (0, i))],
        out_shape=[jax.ShapeDtypeStruct((TOP_K, t), jnp.int32), jax.ShapeDtypeStruct((TOP_K, t), F32)],
        compiler_params=_cparams("arbitrary"),
        name="moe_route",
    )(logits_t, router_b.reshape(n_exp, 1).astype(F32))


def moe_plan(idx, tm):
    t = idx.shape[1]
    n_pair = TOP_K * t
    n_tiles = (n_pair + N_EXPERTS * (tm - 1)) // tm
    e_flat = idx.reshape(-1)
    onehot = (e_flat[:, None] == jnp.arange(N_EXPERTS)[None, :]).astype(jnp.int32)
    running = jnp.cumsum(onehot, axis=0)
    counts = running[-1]
    rank = jnp.sum(onehot * running, 1) - 1
    tiles_per = (counts + tm - 1) // tm
    tile_end = jnp.cumsum(tiles_per)
    n_used = tile_end[-1]
    pstart = (tile_end - tiles_per) * tm
    pos = jnp.sum(onehot * pstart[None, :], 1) + rank
    tile_expert = jnp.minimum(jnp.sum((tile_end[None, :] <= jnp.arange(n_tiles)[:, None]).astype(jnp.int32), 1),
                              N_EXPERTS - 1).astype(jnp.int32)
    gidx = (jnp.arange(n_tiles * tm, dtype=jnp.int32) % t).at[pos].set(
        jnp.arange(n_pair, dtype=jnp.int32) % t, mode="promise_in_bounds", unique_indices=True)
    return gidx, tile_expert, n_used.reshape(1).astype(jnp.int32), pos.astype(jnp.int32)


def _expert_changed(te_ref, tile0):
    i = pl.program_id(0) + tile0
    return jnp.logical_or(pl.program_id(0) == 0, te_ref[i] != te_ref[jnp.maximum(i - 1, 0)])


def _ffn_up_kernel(te_ref, nu_ref, xs_ref, wg_ref, wu_ref, hid_ref, wg_bf, wu_bf, *, tile0):
    used = pl.program_id(0) + tile0 < nu_ref[0]

    @pl.when(jnp.logical_and(used, _expert_changed(te_ref, tile0)))
    def _():
        wg_bf[...] = wg_ref[0].astype(BF16)
        wu_bf[...] = wu_ref[0].astype(BF16)

    @pl.when(used)
    def _():
        xs = xs_ref[...]
        g = _dot(xs, wg_bf[...])
        u = _dot(xs, wu_bf[...])
        hid_ref[...] = ((g * jax.nn.sigmoid(g)) * u).astype(BF16)

    @pl.when(jnp.logical_not(used))
    def _():
        hid_ref[...] = jnp.zeros_like(hid_ref)


def _ffn_down_kernel(te_ref, nu_ref, hid_ref, wd_ref, *rest, tile0):
    o_ref, wd_bf = rest[-2], rest[-1]
    used = pl.program_id(0) + tile0 < nu_ref[0]

    @pl.when(jnp.logical_and(used, _expert_changed(te_ref, tile0)))
    def _():
        wd_bf[...] = wd_ref[0].astype(BF16)

    @pl.when(used)
    def _():
        o_ref[...] = _dot(hid_ref[...], wd_bf[...]).astype(o_ref.dtype)

    @pl.when(jnp.logical_not(used))
    def _():
        o_ref[...] = jnp.zeros_like(o_ref)


def grouped_ffn(xs, tile_expert, n_used, w_gate, w_up, w_down, layer, tm, tile0=0, p_total=None, ys_prev=None):
    p, d = xs.shape
    p_total = p if p_total is None else p_total
    de = w_gate.shape[3]
    n_tiles = p // tm
    wmap = lambda i, te, nu: (layer, te[i + tile0], 0, 0)
    row_in = lambda i, te, nu: (jnp.minimum(i, jnp.maximum(nu[0] - 1 - tile0, 0)), 0)
    hid = pl.pallas_call(
        functools.partial(_ffn_up_kernel, tile0=tile0),
        grid_spec=pltpu.PrefetchScalarGridSpec(
            num_scalar_prefetch=2,
            grid=(n_tiles,),
            in_specs=[pl.BlockSpec((tm, d), row_in),
                      pl.BlockSpec((None, 1, d, de), wmap),
                      pl.BlockSpec((None, 1, d, de), wmap)],
            out_specs=pl.BlockSpec((tm, de), lambda i, te, nu: (i, 0)),
            scratch_shapes=[pltpu.VMEM((d, de), BF16), pltpu.VMEM((d, de), BF16)]),
        out_shape=jax.ShapeDtypeStruct((p, de), BF16),
        compiler_params=_cparams("arbitrary"),
        name="moe_ffn_up",
    )(tile_expert, n_used, xs, w_gate, w_up)
    prev_specs, prev_args, alias = [], [], {}
    if ys_prev is not None:
        prev_specs, prev_args, alias = [pl.BlockSpec(memory_space=pl.ANY)], [ys_prev], {4: 0}
    return pl.pallas_call(
        functools.partial(_ffn_down_kernel, tile0=tile0),
        grid_spec=pltpu.PrefetchScalarGridSpec(
            num_scalar_prefetch=2,
            grid=(n_tiles,),
            in_specs=[pl.BlockSpec((tm, de), row_in),
                      pl.BlockSpec((None, 1, de, d), wmap)] + prev_specs,
            out_specs=pl.BlockSpec((tm, d), lambda i, te, nu: (i + tile0, 0)),
            scratch_shapes=[pltpu.VMEM((de, d), BF16)]),
        out_shape=jax.ShapeDtypeStruct((p_total, d), BF16),
        input_output_aliases=alias,
        compiler_params=_cparams("arbitrary"),
        name="moe_ffn_down",
    )(tile_expert, n_used, hid, w_down, *prev_args)


def _final_kernel(x_ref, y0_ref, y1_ref, w_ref, m5_ref, lg_ref, lb_ref, *rest, alpha):
    o_ref = rest[-1]
    w = w_ref[...]
    y = w[:, 0:1] * y0_ref[...].astype(F32) + w[:, 1:2] * y1_ref[...].astype(F32)
    z = alpha * x_ref[...] + m5_ref[...] * y
    mu = jnp.mean(z, -1, keepdims=True)
    zc = z - mu
    var = jnp.mean(zc * zc, -1, keepdims=True)
    o_ref[...] = zc * lax.rsqrt(var + LN_EPS) * lg_ref[...] + lb_ref[...]


def final_norm(x1, yg, wts, mods4, layer, ln_g, ln_b, alpha, seg_rows, n_batch, tile0=0, out_prev=None, tm=256):
    t, d = x1.shape
    n_tiles = yg.shape[0] // (TOP_K * tm)
    seg = _seg_fn(tm, seg_rows, n_batch)
    row = pl.BlockSpec((tm, d), lambda i: (i + tile0, 0))
    vec = pl.BlockSpec((1, d), lambda i: (0, 0))
    prev_specs, prev_args, alias = [], [], {}
    if out_prev is not None:
        prev_specs, prev_args, alias = [pl.BlockSpec(memory_space=pl.ANY)], [out_prev], {7: 0}
    return pl.pallas_call(
        functools.partial(_final_kernel, alpha=alpha),
        grid=(n_tiles,),
        in_specs=[row, pl.BlockSpec((tm, d), lambda i: (i, 0)), pl.BlockSpec((tm, d), lambda i: (i + n_tiles, 0)),
                  pl.BlockSpec((tm, TOP_K), lambda i: (i + tile0, 0)),
                  _mod_spec(d, layer, 5, lambda i: seg(i + tile0)), vec, vec] + prev_specs,
        out_specs=row,
        out_shape=jax.ShapeDtypeStruct((t, d), F32),
        input_output_aliases=alias,
        compiler_params=_cparams("arbitrary"),
        name="final_norm",
    )(x1, yg, yg, wts, mods4, ln_g.reshape(1, -1), ln_b.reshape(1, -1), *prev_args)


def moe_block(x1, h2, logits_t, router_b, w_gate, w_up, w_down, mods4, layer, ln_g, ln_b,
              alpha, seg_rows, n_batch, tm=512, tm_out=256):
    t = x1.shape[0]
    idx, wts = route(logits_t, router_b)
    gidx, tile_expert, n_used, pos = moe_plan(idx, tm)
    p_total = gidx.shape[0]
    n_tiles = p_total // tm
    cut = (n_tiles // 2) * tm
    ys = None
    for lo, hi in ((0, cut), (cut, p_total)):
        xs = h2.at[gidx[lo:hi]].get(mode="promise_in_bounds")
        ys = grouped_ffn(xs, tile_expert, n_used, w_gate, w_up, w_down, layer, tm,
                         tile0=lo // tm, p_total=p_total, ys_prev=ys)
    half = (t // tm_out // 2) * tm_out
    pos2 = pos.reshape(TOP_K, t)
    out = None
    for lo, hi in ((0, half), (half, t)):
        yg = ys.at[pos2[:, lo:hi].reshape(-1)].get(mode="promise_in_bounds")
        out = final_norm(x1, yg, wts.T, mods4, layer, ln_g, ln_b, alpha, seg_rows, n_batch,
                         tile0=lo // tm_out, out_prev=out, tm=tm_out)
    return out


def s5_matrices(lam_re, lam_im, log_dt, b_re, b_im, c_re, c_im, d_skip):
    f32 = F32
    tc = S5_TC
    n_g, n_p = lam_re.shape[1], lam_re.shape[2]
    n_c = b_re.shape[-1]
    nb = 128 // n_c
    n_q = n_g // nb
    lr, li = lam_re.astype(f32), lam_im.astype(f32)
    dt = jnp.exp(log_dt.astype(f32))[..., None]

    def powers(jvals):
        j = jnp.asarray(np.asarray(jvals, np.float32))[:, None, None, None]
        mag = jnp.exp(lr * dt * j)
        return mag * jnp.cos(li * dt * j), mag * jnp.sin(li * dt * j)

    up = np.arange(tc)
    pw_re, pw_im = powers(np.arange(tc + 1))
    lb_re, lb_im = pw_re[1], pw_im[1]
    den = lr * lr + li * li
    fr = ((lb_re - 1.0) * lr + lb_im * li) / den
    fi = (lb_im * lr - (lb_re - 1.0) * li) / den
    br, bi = b_re.astype(f32), b_im.astype(f32)
    bb_re = fr[..., None] * br - fi[..., None] * bi
    bb_im = fr[..., None] * bi + fi[..., None] * br
    cr, ci = c_re.astype(f32), c_im.astype(f32)

    def times_b(p_re, p_im):
        return (p_re[..., None] * bb_re[None] - p_im[..., None] * bb_im[None],
                p_re[..., None] * bb_im[None] + p_im[..., None] * bb_re[None])

    e_re, e_im = times_b(pw_re, pw_im)
    kmat = jnp.sum(cr[None, :, :, :, :, None] * e_re[:, :, :, None, :, :]
                   - ci[None, :, :, :, :, None] * e_im[:, :, :, None, :, :], axis=4)
    def lag_slab(k_dir):
        return k_dir.reshape(tc, n_q, nb, n_c, n_c).transpose(1, 0, 4, 2, 3).reshape(n_q, tc, n_c, nb * n_c)
    skip = (jnp.eye(n_c, dtype=f32)[None, None, :, None, :]
            * d_skip.astype(f32).reshape(n_q, nb, n_c)[:, None, None, :, :]).reshape(n_q, 1, n_c, nb * n_c)
    k_c = jnp.concatenate([lag_slab(kmat[:tc, 0]), lag_slab(kmat[:tc, 1]), skip], 1)

    def w_slab(e):
        return e.reshape(tc, n_q, nb, n_p, n_c).transpose(1, 0, 4, 2, 3).reshape(n_q, tc, n_c, nb * n_p)
    ef_re, ef_im = times_b(*powers(tc - 1 - up))
    w_c = jnp.stack([w_slab(ef_re[:, 0]), w_slab(ef_im[:, 0]),
                     w_slab(e_re[:tc, 1]), w_slab(e_im[:tc, 1])], 2)

    def v_slabs(d, p_re, p_im):
        f_re = cr[d][None] * p_re[:, :, None, :] - ci[d][None] * p_im[:, :, None, :]
        f_im = cr[d][None] * p_im[:, :, None, :] + ci[d][None] * p_re[:, :, None, :]
        slab = lambda m: m.reshape(tc, n_q, nb, n_c, n_p).transpose(1, 0, 3, 2, 4).reshape(n_q, tc, n_c, nb * n_p)
        return slab(f_re), slab(-f_im)
    pb_re, pb_im = powers(tc - up)
    vt_c = jnp.stack(v_slabs(0, pw_re[1:, 0], pw_im[1:, 0]) + v_slabs(1, pb_re[:, 1], pb_im[:, 1]), 2)
    dec = lambda m: m.reshape(1, n_g * n_p // 128, 1, 128)
    decay = jnp.concatenate([dec(pw_re[tc, 0]), dec(pw_im[tc, 0]), dec(pw_re[tc, 1]), dec(pw_im[tc, 1])], 0)
    return k_c, w_c, vt_c, decay


def _s5_chunk_rows(ref, n):
    return jnp.concatenate([ref[pl.ds(s, n, stride=S5_TC), :] for s in range(S5_TC)], axis=1).astype(BF16)


def _s5_expand(slab, group_lanes):
    rows = 128
    tiled = jnp.concatenate([slab] * (rows // slab.shape[0]), axis=0)
    r = lax.broadcasted_iota(jnp.int32, tiled.shape, 0) // S5_CH
    l = lax.broadcasted_iota(jnp.int32, tiled.shape, 1) // group_lanes
    return jnp.where(r == l, tiled, 0.0).astype(BF16)


def _s5_in_kernel(ul_ref, uc_ref, wc_ref, fr_ref, fi_ref, br_ref, bi_ref, w_ref, *, n_lat, n_ctx, n_batch):
    b = pl.program_id(1)
    n_plane = wc_ref.shape[1]
    st = wc_ref.shape[3]

    @pl.when(b == 0)
    def _():
        for s in range(S5_TC):
            for i in range(n_plane):
                w_ref[s * 128:(s + 1) * 128, i * st:(i + 1) * st] = _s5_expand(wc_ref[s, i], S5_P)

    w_lat = _dot(_s5_chunk_rows(ul_ref, n_lat), w_ref[...])
    w_ctx = _dot(_s5_chunk_rows(uc_ref, n_ctx), w_ref[...])
    nv = fr_ref.shape[0]
    for i, ref in enumerate((fr_ref, fi_ref, br_ref, bi_ref)):
        for c in range(nv):
            lanes = slice((i * nv + c) * 128, (i * nv + c + 1) * 128)
            ref[c, pl.ds(b, n_ctx, stride=n_batch), :] = w_ctx[:, lanes]
            ref[c, pl.ds(n_ctx * n_batch + b, n_lat, stride=n_batch), :] = w_lat[:, lanes]
            ref[c, pl.ds((n_ctx + n_lat) * n_batch + b, n_ctx, stride=n_batch), :] = w_ctx[:, lanes]


def _s5_scan_kernel(wfr, wfi, wbr, wbi, dec_ref, xfr, xfi, xbr, xbi, *, n_tiles):
    nv = wfr.shape[0]
    low = lax.broadcasted_iota(jnp.int32, (nv, 8, 128), 1) < 4
    a_fr, a_fi, a_br, a_bi = dec_ref[0], dec_ref[1], dec_ref[2], dec_ref[3]

    def half_step(s_re, s_im, a_re, a_im, w_re, w_im):
        return a_re * s_re - a_im * s_im + w_re, a_re * s_im + a_im * s_re + w_im

    def one_dir(w_re_ref, w_im_ref, x_re_ref, x_im_ref, row0, s_re, s_im, a_re, a_im, first_low):
        first = low if first_low else jnp.logical_not(low)
        wt_re, wt_im = w_re_ref[:, pl.ds(row0, 8), :], w_im_ref[:, pl.ds(row0, 8), :]
        wr_re, wr_im = pltpu.roll(wt_re, 4, 1), pltpu.roll(wt_im, 4, 1)
        mid_re, mid_im = half_step(s_re, s_im, a_re, a_im, wr_re, wr_im)
        x_re_ref[:, pl.ds(row0, 8), :] = jnp.where(first, s_re, mid_re)
        x_im_ref[:, pl.ds(row0, 8), :] = jnp.where(first, s_im, mid_im)
        m_re = jnp.where(first, pltpu.roll(mid_re, 4, 1), mid_re)
        m_im = jnp.where(first, pltpu.roll(mid_im, 4, 1), mid_im)
        w2_re = jnp.where(first, wr_re, wt_re)
        w2_im = jnp.where(first, wr_im, wt_im)
        return half_step(m_re, m_im, a_re, a_im, w2_re, w2_im)

    def body(i, carry):
        f_re, f_im, b_re, b_im = carry
        rf = pl.multiple_of(i * 8, 8)
        rb = pl.multiple_of((n_tiles - 1 - i) * 8, 8)
        f_re, f_im = one_dir(wfr, wfi, xfr, xfi, rf, f_re, f_im, a_fr, a_fi, True)
        b_re, b_im = one_dir(wbr, wbi, xbr, xbi, rb, b_re, b_im, a_br, a_bi, False)
        return f_re, f_im, b_re, b_im

    z = jnp.zeros((nv, 8, 128), F32)
    lax.fori_loop(0, n_tiles, body, (z, z, z, z))


def _s5_out_kernel(ul_ref, fr_ref, fi_ref, br_ref, bi_ref, kc_ref, vc_ref, y_ref, mt_ref, vt_ref,
                   *, n_lat, n_ctx, n_batch):
    b = pl.program_id(1)
    tc = S5_TC

    @pl.when(b == 0)
    def _():
        lag = [_s5_expand(kc_ref[j], S5_CH) for j in range(2 * tc)]
        diag = _s5_expand(kc_ref[0] + kc_ref[tc] + kc_ref[2 * tc], S5_CH)
        for s in range(tc):
            for t in range(tc):
                blk = diag if s == t else (lag[t - s] if t > s else lag[tc + s - t])
                mt_ref[s * 128:(s + 1) * 128, t * 128:(t + 1) * 128] = blk
        for t in range(tc):
            for i in range(vt_ref.shape[0]):
                vt_ref[i, t * 128:(t + 1) * 128, :] = _s5_expand(vc_ref[t, i], S5_P)

    y = _dot(_s5_chunk_rows(ul_ref, n_lat), mt_ref[...])
    row0 = n_ctx * n_batch + b
    for i, ref in enumerate((fr_ref, fi_ref, br_ref, bi_ref)):
        xs = jnp.concatenate([ref[c, pl.ds(row0, n_lat, stride=n_batch), :] for c in range(ref.shape[0])], 1)
        y = y + _dot_nt(xs.astype(BF16), vt_ref[i])
    for s in range(S5_TC):
        y_ref[pl.ds(s, n_lat, stride=S5_TC), :] = y[:, s * 128:(s + 1) * 128]


def s5_bidir(u, mats, n_batch, l_lat, l_ctx):
    assert n_batch == 4, "the chunk scan packs two chunks of 4 batch rows per 8-sublane tile"
    k_c, w_c, vt_c, decay = mats
    tc = S5_TC
    wd = u.shape[1]
    n_q = wd // 128
    lane_q = tc * 128
    st_q = (128 // S5_CH) * S5_P
    n_lat, n_ctx = l_lat // tc, l_ctx // tc
    nk = n_lat + 2 * n_ctx
    assert nk % 2 == 0
    rows = nk * n_batch
    ctx0 = (n_batch * l_lat) // l_ctx
    dims = dict(n_lat=n_lat, n_ctx=n_ctx, n_batch=n_batch)

    nv = st_q // 128
    plane = jax.ShapeDtypeStruct((n_q * nv, rows, 128), F32)
    plane_spec = pl.BlockSpec((nv, rows, 128), lambda q, b: (q, 0, 0))
    ul_spec = pl.BlockSpec((l_lat, 128), lambda q, b: (b, q))
    uc_spec = pl.BlockSpec((l_ctx, 128), lambda q, b: (ctx0 + b, q))
    w_planes = pl.pallas_call(
        functools.partial(_s5_in_kernel, **dims),
        grid=(n_q, n_batch),
        in_specs=[ul_spec, uc_spec, pl.BlockSpec((None,) + w_c.shape[1:], lambda q, b: (q, 0, 0, 0, 0))],
        out_specs=[plane_spec] * 4,
        out_shape=[plane] * 4,
        scratch_shapes=[pltpu.VMEM((lane_q, 4 * st_q), BF16)],
        compiler_params=_cparams("arbitrary", "arbitrary"),
        name="s5_chunk_in",
    )(u, u, w_c)

    blk = pl.BlockSpec((nv, rows, 128), lambda j: (j, 0, 0))
    x_planes = pl.pallas_call(
        functools.partial(_s5_scan_kernel, n_tiles=rows // 8),
        grid=(n_q,),
        in_specs=[blk] * 4 + [pl.BlockSpec((4, nv, 1, 128), lambda j: (0, j, 0, 0))],
        out_specs=[blk] * 4,
        out_shape=[plane] * 4,
        compiler_params=_cparams("arbitrary"),
        name="s5_chunk_scan",
    )(*w_planes, decay)

    return pl.pallas_call(
        functools.partial(_s5_out_kernel, **dims),
        grid=(n_q, n_batch),
        in_specs=[ul_spec] + [plane_spec] * 4
                 + [pl.BlockSpec((None,) + k_c.shape[1:], lambda q, b: (q, 0, 0, 0)),
                    pl.BlockSpec((None,) + vt_c.shape[1:], lambda q, b: (q, 0, 0, 0, 0))],
        out_specs=pl.BlockSpec((l_lat, 128), lambda q, b: (b, q)),
        out_shape=jax.ShapeDtypeStruct((n_batch * l_lat, wd), F32),
        scratch_shapes=[pltpu.VMEM((lane_q, lane_q), BF16), pltpu.VMEM((4, lane_q, st_q), BF16)],
        compiler_params=_cparams("arbitrary", "arbitrary"),
        name="s5_chunk_out",
    )(u, *x_planes, k_c, vt_c)


def kernel(x, c, ctx, c_ctx, ada_w, ada_b, ln_mix_g, ln_mix_b, ln_ffn_g, ln_ffn_b, ev_w_in, ev_gate_w2,
           ev_gate_b, ev_rpb, ev_norm_g, ev_w_out, od_w_in, od_lam_re, od_lam_im, od_log_dt, od_b_re,
           od_b_im, od_c_re, od_c_im, od_d, od_w_glu, od_b_glu, od_w_out, router_w, router_b,
           moe_w_gate, moe_w_up, moe_w_down):
    n_batch, l_lat, d = x.shape
    l_ctx = ctx.shape[1]
    depth = ada_w.shape[0]
    assert depth == 2, "one even (NA + GLA) layer followed by one odd (S5) layer"
    alpha = (2.0 * depth) ** 0.25
    n_lat = n_batch * l_lat

    cvec = jnp.concatenate([c, c_ctx[None], jnp.zeros((8 - n_batch - 1, d), F32)], 0)
    mods = compute_mods(cvec, ada_w, ada_b)
    mods4 = mods.reshape(depth, 8, 1, N_MOD * d)
    x_lat, x_ctx = x.reshape(n_lat, d), ctx.reshape(n_batch * l_ctx, d)
    router_wt = router_w.T.astype(F32)

    na_w = NA_HEADS * NA_DH
    wk = GLA_HEADS * GLA_DK
    wv = GLA_HEADS * GLA_DV
    ev_in = ev_w_in.shape[2]
    pad = (-ev_in) % 256
    w_in = jnp.pad(ev_w_in[0], ((0, 0), (0, pad))).astype(BF16)
    proj = mod_matmul(x_lat, x_ctx, mods4, 0, w_in, l_lat, n_batch, tm=512, tn=(ev_in + pad) // 2)
    a_lat, a_ctx = na_attention(proj, na_bias_table(ev_rpb[0], l_lat // GRID_W), n_batch, l_lat, l_ctx)
    col_lr = (3 * na_w + 2 * wk + 2 * wv) // 128
    g2 = jnp.zeros((2, 128, wk), F32)
    g2 = g2.at[0, 0:GLA_RANK].set(ev_gate_w2[0, 0]).at[1, GLA_RANK:2 * GLA_RANK].set(ev_gate_w2[0, 1])
    o_f, o_b = gla_bidir(proj, g2.astype(BF16), ev_gate_b[0].reshape(2, 1, wk), rope_tables(l_lat),
                         n_batch, l_lat, l_ctx,
                         col_q=3 * na_w // wk, col_k=(3 * na_w + wk) // wk,
                         col_v=(3 * na_w + 2 * wk) // wv, col_lr=col_lr)
    x1, h2, logits_t = even_out(a_lat, a_ctx, o_f, o_b, proj, (3 * na_w + 2 * wk + wv) // wv, x_lat, x_ctx,
                                mods4, 0, ev_norm_g[0], ev_w_out[0].astype(BF16), ln_mix_g[0], ln_mix_b[0],
                                router_wt, alpha, l_lat, n_batch)
    w_gate, w_up, w_down = moe_w_gate, moe_w_up, moe_w_down
    rows = moe_block(x1, h2, logits_t, router_b, w_gate, w_up, w_down, mods4, 0,
                     ln_ffn_g[0], ln_ffn_b[0], alpha, l_lat, n_batch)

    u = mod_matmul(rows, None, mods4, 1, od_w_in[0].astype(BF16), l_lat, n_batch)
    mats = s5_matrices(od_lam_re[0], od_lam_im[0], od_log_dt[0], od_b_re[0], od_b_im[0],
                       od_c_re[0], od_c_im[0], od_d[0])
    y5 = s5_bidir(u, mats, n_batch, l_lat, l_ctx)
    x1, h2, logits_t = odd_out(y5, rows, mods4, 1, od_w_glu[0].astype(BF16), od_b_glu[0],
                               od_w_out[0].astype(BF16), ln_mix_g[1], ln_mix_b[1], router_wt,
                               alpha, l_lat, n_batch)
    out = moe_block(x1, h2, logits_t, router_b, w_gate, w_up, w_down, mods4, 1,
                    ln_ffn_g[1], ln_ffn_b[1], alpha, l_lat, n_batch)
    return out.reshape(n_batch, l_lat, d)
```

```python
import functools
import math

import numpy as np
import jax
import jax.numpy as jnp
from jax import lax
from jax.experimental import pallas as pl
from jax.experimental.pallas import tpu as pltpu

F32 = jnp.float32
BF16 = jnp.bfloat16
HIGHEST = lax.Precision.HIGHEST

N_MOD = 6
LN_EPS = 1e-5
NORM_EPS = 1e-6

GRID_W = 64
NA_HEADS = 8
NA_DH = 128
NA_KR = 8
NA_KC = 16

GLA_HEADS = 4
GLA_DK = 128
GLA_DV = 256
GLA_RANK = 16
GLA_TAU = 16.0
GLA_CHUNK = 64
ROPE_BASE = 10000.0

S5_CH = 16
S5_P = 64
S5_TC = 16

N_EXPERTS = 16
N_GROUPS = 4
TOP_K = 2

VMEM_LIMIT = 56 * 1024 * 1024
NEG_BIG = -1e30


def _cparams(*sem):
    return pltpu.CompilerParams(dimension_semantics=sem, vmem_limit_bytes=VMEM_LIMIT)


def _dot(a, b, precision=None):
    return jnp.dot(a, b, preferred_element_type=F32, precision=precision)


def _dot_nt(a, b, precision=None):
    return lax.dot_general(a, b, (((1,), (1,)), ((), ())), preferred_element_type=F32, precision=precision)


def _dot_tn(a, b):
    return lax.dot_general(a, b, (((0,), (0,)), ((), ())), preferred_element_type=F32)


def _mods_kernel(s_ref, w_ref, b_ref, o_ref):
    s = s_ref[...]
    s = s * jax.nn.sigmoid(s)
    w = w_ref[0]
    s_hi, w_hi = s.astype(BF16), w.astype(BF16)
    s_lo = (s - s_hi.astype(F32)).astype(BF16)
    w_lo = (w - w_hi.astype(F32)).astype(BF16)
    o_ref[0] = _dot(s_hi, w_hi) + _dot(s_lo, w_hi) + _dot(s_hi, w_lo) + b_ref[0]


def compute_mods(cvec, ada_w, ada_b, tn=1024):
    n_layer, d, n = ada_w.shape
    tn = math.gcd(tn, n)
    return pl.pallas_call(
        _mods_kernel,
        grid=(n_layer, n // tn),
        in_specs=[pl.BlockSpec((8, d), lambda l, j: (0, 0)),
                  pl.BlockSpec((1, d, tn), lambda l, j: (l, 0, j)),
                  pl.BlockSpec((1, 1, tn), lambda l, j: (l, 0, j))],
        out_specs=pl.BlockSpec((1, 8, tn), lambda l, j: (l, 0, j)),
        out_shape=jax.ShapeDtypeStruct((n_layer, 8, n), F32),
        compiler_params=_cparams("arbitrary", "arbitrary"),
        name="ada_mods",
    )(cvec, ada_w, ada_b.reshape(n_layer, 1, n))


def _mod_spec(d, layer, which, seg_of_tile):
    return pl.BlockSpec((None, None, 1, d), lambda i, *_: (layer, seg_of_tile(i), 0, which))


def _seg_fn(tm, seg_rows, n_batch):
    return lambda i: jnp.minimum((i * tm) // seg_rows, n_batch)


def _two_source_specs(lat, ctx, tm, tile_of):
    n_lat_tiles = lat.shape[0] // tm
    d = lat.shape[1]
    return [pl.BlockSpec((tm, d), lambda *g: (jnp.minimum(tile_of(*g), n_lat_tiles - 1), 0)),
            pl.BlockSpec((tm, d), lambda *g: (jnp.maximum(tile_of(*g) - n_lat_tiles, 0), 0))]


def _pick_rows(lat_ref, ctx_ref, tile, n_lat_tiles):
    return jnp.where(tile < n_lat_tiles, lat_ref[...], ctx_ref[...])


def _modmm_kernel(xl_ref, xc_ref, s1_ref, s0_ref, w_ref, o_ref, *, n_lat_tiles):
    x = _pick_rows(xl_ref, xc_ref, pl.program_id(1), n_lat_tiles)
    h = x * (1.0 + s1_ref[...]) + s0_ref[...]
    o_ref[...] = _dot(h.astype(BF16), w_ref[...])


def mod_matmul(x_lat, x_ctx, mods4, layer, w_bf16, seg_rows, n_batch, tm=256, tn=None):
    if x_ctx is None:
        t, x_ctx = x_lat.shape[0], x_lat
    else:
        t = x_lat.shape[0] + x_ctx.shape[0]
    d = x_lat.shape[1]
    n = w_bf16.shape[1]
    tn = n if tn is None else tn
    seg = _seg_fn(tm, seg_rows, n_batch)
    return pl.pallas_call(
        functools.partial(_modmm_kernel, n_lat_tiles=x_lat.shape[0] // tm),
        grid=(n // tn, t // tm),
        in_specs=_two_source_specs(x_lat, x_ctx, tm, lambda j, i: i) + [
            pl.BlockSpec((None, None, 1, d), lambda j, i: (layer, seg(i), 0, 1)),
            pl.BlockSpec((None, None, 1, d), lambda j, i: (layer, seg(i), 0, 0)),
            pl.BlockSpec((d, tn), lambda j, i: (0, j))],
        out_specs=pl.BlockSpec((tm, tn), lambda j, i: (i, j)),
        out_shape=jax.ShapeDtypeStruct((t, n), F32),
        compiler_params=_cparams("arbitrary", "arbitrary"),
        name="mod_matmul",
    )(x_lat, x_ctx, mods4, mods4, w_bf16)


NA_RB = 4
NA_BAND = NA_RB + NA_KR - 1


def _na_row_start(r, rows):
    return min(max(r - NA_KR // 2, 0), rows - NA_KR)


def na_bias_table(rpb, rows):
    w = GRID_W
    q = np.arange(w)
    kc = np.arange(w)
    win0 = np.clip(q - NA_KC // 2, 0, w - NA_KC)
    ok = (kc[None, :] >= win0[:, None]) & (kc[None, :] < win0[:, None] + NA_KC)
    dc = np.clip(kc[None, :] - q[:, None] + NA_KC - 1, 0, 2 * NA_KC - 2)
    pick = ((dc[None] == np.arange(2 * NA_KC - 1)[:, None, None]) & ok[None]).astype(np.float32)
    colb = jnp.einsum("hrd,dqk->hrqk", rpb.astype(F32), jnp.asarray(pick), precision=HIGHEST)
    colb = jnp.where(ok[None, None], colb, NEG_BIG)
    neg = jnp.full((rpb.shape[0], w, w), NEG_BIG, F32)

    def block(r0):
        band0 = min(max(r0 - NA_KR // 2, 0), rows - NA_BAND)
        out = []
        for r in range(r0, r0 + NA_RB):
            rs = _na_row_start(r, rows)
            first = rs - r + NA_KR - 1
            cols = [neg] * (rs - band0) + [colb[:, first + j] for j in range(NA_KR)]
            cols += [neg] * (NA_BAND - len(cols))
            out.append(jnp.concatenate(cols, -1))
        return jnp.concatenate(out, 1)

    return jnp.stack([block(0), block(NA_RB), block(rows - NA_RB)], 1)


def _na_kernel(q_ref, k_ref, v_ref, qc_ref, kc_ref, vc_ref, bias_ref, o_ref, oc_ref, kbf, vbf, *, rows):
    w = GRID_W
    n_blk = rows // NA_RB
    scale = NA_DH ** -0.5
    kbf[...] = k_ref[...].astype(BF16)
    vbf[...] = v_ref[...].astype(BF16)
    kc = kc_ref[...].astype(BF16)
    vc = vc_ref[...].astype(BF16)

    def body(pair, carry):
        blocks = (2 * pair, 2 * pair + 1)
        q0, scores = [], []
        for i in blocks:
            r0 = i * NA_RB
            band0 = jnp.clip(r0 - NA_KR // 2, 0, rows - NA_BAND)
            variant = jnp.where(i == 0, 0, jnp.where(i == n_blk - 1, 2, 1))
            q0.append(pl.multiple_of(r0 * w, NA_RB * w))
            k0 = pl.multiple_of(band0 * w, w)
            q = (q_ref[pl.ds(q0[-1], NA_RB * w), :] * scale).astype(BF16)
            s_loc = _dot_nt(q, kbf[pl.ds(k0, NA_BAND * w), :]) + bias_ref[variant]
            scores.append((s_loc, _dot_nt(q, kc), k0))
        probs = []
        for s_loc, s_ctx, k0 in scores:
            m = jnp.maximum(jnp.max(s_loc, -1, keepdims=True), jnp.max(s_ctx, -1, keepdims=True))
            p_loc = jnp.exp(s_loc - m)
            p_ctx = jnp.exp(s_ctx - m)
            den = jnp.sum(p_loc, -1, keepdims=True) + jnp.sum(p_ctx, -1, keepdims=True)
            probs.append((p_loc.astype(BF16), p_ctx.astype(BF16), den, k0))
        for q_start, (p_loc, p_ctx, den, k0) in zip(q0, probs):
            o = _dot(p_loc, vbf[pl.ds(k0, NA_BAND * w), :]) + _dot(p_ctx, vc)
            o_ref[pl.ds(q_start, NA_RB * w), :] = o / den
        return carry

    lax.fori_loop(0, n_blk // 2, body, 0)

    qc = (qc_ref[...] * scale).astype(BF16)
    s = _dot_nt(qc, kc)
    p = jnp.exp(s - jnp.max(s, -1, keepdims=True))
    oc_ref[...] = _dot(p.astype(BF16), vc) / jnp.sum(p, -1, keepdims=True)


def na_attention(proj, bias_tab, n_batch, l_lat, l_ctx):
    h = NA_HEADS
    dh = NA_DH
    rows = l_lat // GRID_W
    ctx0 = (n_batch * l_lat) // l_ctx
    return pl.pallas_call(
        functools.partial(_na_kernel, rows=rows),
        grid=(n_batch, h),
        in_specs=[pl.BlockSpec((l_lat, dh), lambda b, hh: (b, hh)),
                  pl.BlockSpec((l_lat, dh), lambda b, hh: (b, h + hh)),
                  pl.BlockSpec((l_lat, dh), lambda b, hh: (b, 2 * h + hh)),
                  pl.BlockSpec((l_ctx, dh), lambda b, hh: (ctx0 + b, hh)),
                  pl.BlockSpec((l_ctx, dh), lambda b, hh: (ctx0 + b, h + hh)),
                  pl.BlockSpec((l_ctx, dh), lambda b, hh: (ctx0 + b, 2 * h + hh)),
                  pl.BlockSpec((None,) + bias_tab.shape[1:], lambda b, hh: (hh, 0, 0, 0))],
        out_specs=[pl.BlockSpec((l_lat, dh), lambda b, hh: (b, hh)),
                   pl.BlockSpec((l_ctx, dh), lambda b, hh: (b, hh))],
        out_shape=[jax.ShapeDtypeStruct((n_batch * l_lat, h * dh), F32),
                   jax.ShapeDtypeStruct((n_batch * l_ctx, h * dh), F32)],
        scratch_shapes=[pltpu.VMEM((l_lat, dh), BF16), pltpu.VMEM((l_lat, dh), BF16)],
        compiler_params=_cparams("arbitrary", "arbitrary"),
        name="na_attention",
    )(proj, proj, proj, proj, proj, proj, bias_tab)


def rope_tables(l_lat):
    half = GLA_DK // 2
    nf = half // 2
    inv = ROPE_BASE ** (-np.arange(nf, dtype=np.float64) / nf)
    t = np.arange(l_lat)
    lane = np.arange(GLA_DK)
    pos = np.where(lane[None, :] < half, (t // GRID_W)[:, None], (t % GRID_W)[:, None]).astype(np.float64)
    ang = pos * inv[lane % nf][None, :]
    first = (lane % half) < nf
    cos = np.cos(ang)
    sin_a = np.where(first[None, :], -np.sin(ang), 0.0)
    sin_b = np.where(first[None, :], 0.0, np.sin(ang))
    return jnp.asarray(cos, F32), jnp.asarray(sin_a, F32), jnp.asarray(sin_b, F32)


GLA_PREP_CHUNKS = 8


def _gla_prep_kernel(q_ref, k_ref, lr_ref, cos_ref, sa_ref, sb_ref, g2_ref, gb_ref,
                     qdf, kdf, krf, elf, qdb, kdb, krb, elb, *, n_lat_tiles):
    c = GLA_CHUNK
    nch = GLA_PREP_CHUNKS
    nf = GLA_DK // 4
    gscale = GLA_DK ** -0.5
    wk = GLA_HEADS * GLA_DK
    is_lat = pl.program_id(0) < n_lat_tiles
    cos = jnp.where(is_lat, cos_ref[...], 1.0)
    sa = jnp.where(is_lat, sa_ref[...], 0.0)
    sb = jnp.where(is_lat, sb_ref[...], 0.0)

    def rope(x):
        return x * cos + pltpu.roll(x, GLA_DK - nf, 1) * sa + pltpu.roll(x, nf, 1) * sb

    qs, ks_ = [], []
    for h in range(GLA_HEADS):
        hs = slice(h * GLA_DK, (h + 1) * GLA_DK)
        qs.append(rope(q_ref[:, hs]) * gscale)
        ks_.append(rope(k_ref[:, hs]))

    lr = lr_ref[...].astype(BF16)
    row = lax.broadcasted_iota(jnp.int32, (c, c), 0)
    col = lax.broadcasted_iota(jnp.int32, (c, c), 1)
    for d, (qd, kd, kr, el) in enumerate(((qdf, kdf, krf, elf), (qdb, kdb, krb, elb))):
        reverse = d == 1
        tri = ((col >= row) if reverse else (col <= row)).astype(BF16)
        z = _dot(lr, g2_ref[d]) + gb_ref[d]
        g = (jnp.minimum(z, 0.0) - jnp.log1p(jnp.exp(-jnp.abs(z)))) * (1.0 / GLA_TAU)
        g_hi = g.astype(BF16)
        r1 = g - g_hi.astype(F32)
        g_mid = r1.astype(BF16)
        g_lo = (r1 - g_mid.astype(F32)).astype(BF16)
        parts = []
        for ci in range(nch):
            rs = slice(ci * c, (ci + 1) * c)
            parts.append(_dot(tri, g_hi[rs]) + _dot(tri, g_mid[rs]) + _dot(tri, g_lo[rs]))
        b3 = jnp.concatenate(parts, 0).reshape(nch, c, wk)
        bl3 = b3[:, 0:1, :] if reverse else b3[:, c - 1:c, :]
        el[...] = jnp.exp(bl3)
        e_b = jnp.exp(b3).reshape(nch * c, wk)
        e_nb = jnp.exp(-b3).reshape(nch * c, wk)
        e_rem = jnp.exp(bl3 - b3).reshape(nch * c, wk)
        for h in range(GLA_HEADS):
            hs = slice(h * GLA_DK, (h + 1) * GLA_DK)
            qd[:, hs] = (qs[h] * e_b[:, hs]).astype(BF16)
            kd[:, hs] = (ks_[h] * e_nb[:, hs]).astype(BF16)
            kr[:, hs] = (ks_[h] * e_rem[:, hs]).astype(BF16)


def _gla_scan_kernel(qdf, kdf, krf, elf, vf, qdb, kdb, krb, elb, vb, of_ref, ob_ref, st_ref):
    @pl.when(pl.program_id(1) == 0)
    def _():
        st_ref[...] = jnp.zeros_like(st_ref)

    c = GLA_CHUNK
    row = lax.broadcasted_iota(jnp.int32, (c, c), 0)
    col = lax.broadcasted_iota(jnp.int32, (c, c), 1)
    dirs = ((qdf, kdf, krf, elf, vf, of_ref, col <= row, (0, 1)), (qdb, kdb, krb, elb, vb, ob_ref, col >= row, (1, 0)))
    chains = [(d, h) for d in range(2) for h in range(GLA_HEADS)]
    hs = lambda h: slice(h * GLA_DK, (h + 1) * GLA_DK)
    vs = lambda h: slice(h * GLA_DV, (h + 1) * GLA_DV)
    rows = lambda k: slice(k * c, (k + 1) * c)
    order = lambda d, step: rows(dirs[d][7][step])
    q_dec = [[dirs[d][0][order(d, s), hs(h)] for d, h in chains] for s in range(2)]
    v_bf = [[dirs[d][4][order(d, s), vs(h)].astype(BF16) for d, h in chains] for s in range(2)]
    att = [[jnp.where(dirs[d][6], _dot_nt(q_dec[s][n], dirs[d][1][order(d, s), hs(h)]), 0.0).astype(BF16)
            for n, (d, h) in enumerate(chains)] for s in range(2)]
    state = [st_ref[d, h] for d, h in chains]
    for s in range(2):
        for n, (d, h) in enumerate(chains):
            dirs[d][5][order(d, s), vs(h)] = (_dot(att[s][n], v_bf[s][n])
                                              + _dot_nt(q_dec[s][n], state[n].astype(BF16)))
        state = [state[n] * dirs[d][3][dirs[d][7][s], :, hs(h)]
                 + _dot_tn(v_bf[s][n], dirs[d][2][order(d, s), hs(h)]) for n, (d, h) in enumerate(chains)]
    for n, (d, h) in enumerate(chains):
        st_ref[d, h] = state[n]


def gla_bidir(proj, g2, gb, tables, n_batch, l_lat, l_ctx, col_q, col_k, col_v, col_lr):
    c = GLA_CHUNK
    nc = l_ctx // c
    nl = l_lat // c
    nz = nl + 2 * nc
    steps = nl + nc
    wk = GLA_HEADS * GLA_DK
    wv = GLA_HEADS * GLA_DV
    t_rows = n_batch * (l_lat + l_ctx)

    tp = GLA_PREP_CHUNKS * c
    lat_tiles = l_lat // tp
    n_lat_tiles = n_batch * lat_tiles
    cos, sa, sb = tables
    tab = pl.BlockSpec((tp, GLA_DK), lambda i: (jnp.where(i < n_lat_tiles, i % lat_tiles, 0), 0))
    row_bf = jax.ShapeDtypeStruct((t_rows, wk), BF16)
    last = jax.ShapeDtypeStruct((t_rows // c, 1, wk), F32)
    row_spec = pl.BlockSpec((tp, wk), lambda i: (i, 0))
    last_spec = pl.BlockSpec((GLA_PREP_CHUNKS, 1, wk), lambda i: (i, 0, 0))
    prep = pl.pallas_call(
        functools.partial(_gla_prep_kernel, n_lat_tiles=n_lat_tiles),
        grid=(t_rows // tp,),
        in_specs=[pl.BlockSpec((tp, wk), lambda i: (i, col_q)),
                  pl.BlockSpec((tp, wk), lambda i: (i, col_k)),
                  pl.BlockSpec((tp, 128), lambda i: (i, col_lr)),
                  tab, tab, tab,
                  pl.BlockSpec((2, 128, wk), lambda i: (0, 0, 0)),
                  pl.BlockSpec((2, 1, wk), lambda i: (0, 0, 0))],
        out_specs=[row_spec, row_spec, row_spec, last_spec] * 2,
        out_shape=[row_bf, row_bf, row_bf, last] * 2,
        compiler_params=_cparams("arbitrary"),
        name="gla_prep",
    )(proj, proj, proj, cos, sa, sb, g2, gb)

    def zblk(b, j):
        lat = b * nl + (j - nc)
        ctx = n_batch * nl + b * nc + jnp.where(j < nc, j, j - nc - nl)
        return jnp.where((j >= nc) & (j < nc + nl), lat, ctx)

    assert nc % 2 == 0 and nl % 2 == 0
    fwd = lambda b, i: zblk(b, 2 * i) // 2
    bwd = lambda b, i: zblk(b, nz - 2 - 2 * i) // 2
    c2 = 2 * c

    def dir_specs(blk):
        return [pl.BlockSpec((c2, wk), lambda b, i: (blk(b, i), 0)),
                pl.BlockSpec((c2, wk), lambda b, i: (blk(b, i), 0)),
                pl.BlockSpec((c2, wk), lambda b, i: (blk(b, i), 0)),
                pl.BlockSpec((2, 1, wk), lambda b, i: (blk(b, i), 0, 0)),
                pl.BlockSpec((c2, wv), lambda b, i: (blk(b, i), col_v))]

    return pl.pallas_call(
        _gla_scan_kernel,
        grid=(n_batch, steps // 2),
        in_specs=dir_specs(fwd) + dir_specs(bwd),
        out_specs=[pl.BlockSpec((c2, wv), lambda b, i: (fwd(b, i), 0)),
                   pl.BlockSpec((c2, wv), lambda b, i: (bwd(b, i), 0))],
        out_shape=[jax.ShapeDtypeStruct((t_rows, wv), F32), jax.ShapeDtypeStruct((t_rows, wv), F32)],
        scratch_shapes=[pltpu.VMEM((2, GLA_HEADS, GLA_DV, GLA_DK), F32)],
        compiler_params=_cparams("arbitrary", "arbitrary"),
        name="gla_scan",
    )(*prep[0:4], proj, *prep[4:8], proj)


def _post_mix(out, x, m2_ref, m3_ref, m4_ref, lg_ref, lb_ref, wr_ref, alpha, x1_ref, h2_ref, lt_ref, rows):
    y = alpha * x + m2_ref[...] * out
    mu = jnp.mean(y, -1, keepdims=True)
    yc = y - mu
    var = jnp.mean(yc * yc, -1, keepdims=True)
    x1 = yc * lax.rsqrt(var + LN_EPS) * lg_ref[...] + lb_ref[...]
    h2 = x1 * (1.0 + m4_ref[...]) + m3_ref[...]
    x1_ref[rows, :] = x1
    n_exp = lt_ref.shape[0]
    h2_hi = h2.astype(BF16)
    h2_lo = (h2 - h2_hi.astype(F32)).astype(BF16)
    h2_ref[rows, :] = h2_hi
    wr = wr_ref[...]
    wr_hi = wr.astype(BF16)
    wr_lo = (wr - wr_hi.astype(F32)).astype(BF16)
    both = _dot_nt(jnp.concatenate([wr_hi, wr_lo], 0), h2_hi)
    lt_ref[:, rows] = both[:n_exp] + both[n_exp:] + _dot_nt(wr_hi, h2_lo)


def _row_halves(n):
    return (slice(0, n // 2), slice(n // 2, n))


def _even_out_kernel(al_ref, ac_ref, of_ref, ob_ref, r_ref, xl_ref, xc_ref, m2_ref, m3_ref, m4_ref, ng_ref,
                     wo_ref, lg_ref, lb_ref, wr_ref, x1_ref, h2_ref, lt_ref, *, alpha, n_lat_tiles):
    is_lat = pl.program_id(0) < n_lat_tiles
    outs = []
    for rows in _row_halves(of_ref.shape[0]):
        o = of_ref[rows, :] + ob_ref[rows, :]
        r = r_ref[rows, :]
        gate = r * jax.nn.sigmoid(r)
        mixed = [jnp.where(is_lat, al_ref[rows, :], ac_ref[rows, :]).astype(BF16)]
        for h in range(GLA_HEADS):
            vs = slice(h * GLA_DV, (h + 1) * GLA_DV)
            oh = o[:, vs]
            nrm = oh * lax.rsqrt(jnp.mean(oh * oh, -1, keepdims=True) + NORM_EPS) * ng_ref[...]
            mixed.append((nrm * gate[:, vs]).astype(BF16))
        outs.append(_dot(jnp.concatenate(mixed, axis=1), wo_ref[...]))
    x = jnp.where(is_lat, xl_ref[...], xc_ref[...])
    _post_mix(jnp.concatenate(outs, 0), x, m2_ref, m3_ref, m4_ref, lg_ref, lb_ref, wr_ref, alpha,
              x1_ref, h2_ref, lt_ref, slice(None))


def _post_specs(d, layer, seg, tm, n_exp):
    ins = [_mod_spec(d, layer, 2, seg), _mod_spec(d, layer, 3, seg), _mod_spec(d, layer, 4, seg)]
    tail = [pl.BlockSpec((1, d), lambda i: (0, 0)), pl.BlockSpec((1, d), lambda i: (0, 0)),
            pl.BlockSpec((n_exp, d), lambda i: (0, 0))]
    outs =[pl.BlockSpec((tm, d), lambda i: (i, 0)), pl.BlockSpec((tm, d), lambda i: (i, 0)),
            pl.BlockSpec((n_exp, tm), lambda i: (0, i))]
    return ins, tail, outs


def _post_shapes(t, d, n_exp):
    return [jax.ShapeDtypeStruct((t, d), F32), jax.ShapeDtypeStruct((t, d), BF16),
            jax.ShapeDtypeStruct((n_exp, t), F32)]


def even_out(a_lat, a_ctx, o_f, o_b, proj, col_r, x_lat, x_ctx, mods4, layer, norm_g, w_out_bf16, ln_g, ln_b,
             router_wt, alpha, seg_rows, n_batch, tm=256):
    t = x_lat.shape[0] + x_ctx.shape[0]
    d = x_lat.shape[1]
    na = a_lat.shape[1]
    wv = o_f.shape[1]
    n_exp = router_wt.shape[0]
    seg = _seg_fn(tm, seg_rows, n_batch)
    ins, tail, outs = _post_specs(d, layer, seg, tm, n_exp)
    tile_of = lambda i: i
    return pl.pallas_call(
        functools.partial(_even_out_kernel, alpha=alpha, n_lat_tiles=x_lat.shape[0] // tm),
        grid=(t // tm,),
        in_specs=(_two_source_specs(a_lat, a_ctx, tm, tile_of)
                  + [pl.BlockSpec((tm, wv), lambda i: (i, 0)),
                     pl.BlockSpec((tm, wv), lambda i: (i, 0)),
                     pl.BlockSpec((tm, wv), lambda i: (i, col_r))]
                  + _two_source_specs(x_lat, x_ctx, tm, tile_of) + ins
                  + [pl.BlockSpec((1, GLA_DV), lambda i: (0, 0)),
                     pl.BlockSpec((na + wv, d), lambda i: (0, 0))] + tail),
        out_specs=outs,
        out_shape=_post_shapes(t, d, n_exp),
        compiler_params=_cparams("arbitrary"),
        name="even_out",
    )(a_lat, a_ctx, o_f, o_b, proj, x_lat, x_ctx, mods4, mods4, mods4, norm_g.reshape(1, -1), w_out_bf16,
      ln_g.reshape(1, -1), ln_b.reshape(1, -1), router_wt)


def _odd_out_kernel(y_ref, x_ref, m2_ref, m3_ref, m4_ref, wg_ref, bg_ref, wo_ref,
                    lg_ref, lb_ref, wr_ref, x1_ref, h2_ref, lt_ref, *, alpha):
    outs = []
    for r in _row_halves(y_ref.shape[0]):
        g = jax.nn.gelu(y_ref[r, :], approximate=True)
        z = _dot(g.astype(BF16), wg_ref[...]) + bg_ref[...]
        outs.append(_dot((g * jax.nn.sigmoid(z)).astype(BF16), wo_ref[...]))
    _post_mix(jnp.concatenate(outs, 0), x_ref[...], m2_ref, m3_ref, m4_ref, lg_ref, lb_ref, wr_ref, alpha,
              x1_ref, h2_ref, lt_ref, slice(None))


def odd_out(y, x, mods4, layer, w_glu_bf16, b_glu, w_out_bf16, ln_g, ln_b, router_wt,
            alpha, seg_rows, n_batch, tm=256):
    t, w5 = y.shape
    d = x.shape[1]
    n_exp = router_wt.shape[0]
    seg = _seg_fn(tm, seg_rows, n_batch)
    ins, tail, outs = _post_specs(d, layer, seg, tm, n_exp)
    return pl.pallas_call(
        functools.partial(_odd_out_kernel, alpha=alpha),
        grid=(t // tm,),
        in_specs=([pl.BlockSpec((tm, w5), lambda i: (i, 0)), pl.BlockSpec((tm, d), lambda i: (i, 0))] + ins
                  + [pl.BlockSpec((w5, w5), lambda i: (0, 0)),
                     pl.BlockSpec((1, w5), lambda i: (0, 0)),
                     pl.BlockSpec((w5, d), lambda i: (0, 0))] + tail),
        out_specs=outs,
        out_shape=_post_shapes(t, d, n_exp),
        compiler_params=_cparams("arbitrary"),
        name="odd_out",
    )(y, x, mods4, mods4, mods4, w_glu_bf16, b_glu.reshape(1, -1), w_out_bf16,
      ln_g.reshape(1, -1), ln_b.reshape(1, -1), router_wt)


def _route_kernel(lt_ref, rb_ref, idx_ref, w_ref):
    eg = N_EXPERTS // N_GROUPS
    logits = lt_ref[...]
    aff = jax.nn.sigmoid(logits)
    sel = aff + rb_ref[...]
    s = [sel[e:e + 1, :] for e in range(N_EXPERTS)]
    a = [aff[e:e + 1, :] for e in range(N_EXPERTS)]

    def top2_sum(v):
        hi1, lo1 = jnp.maximum(v[0], v[1]), jnp.minimum(v[0], v[1])
        hi2, lo2 = jnp.maximum(v[2], v[3]), jnp.minimum(v[2], v[3])
        return jnp.maximum(hi1, hi2) + jnp.maximum(jnp.minimum(hi1, hi2), jnp.maximum(lo1, lo2))

    best = top2_sum(s[0:eg])
    grp = jnp.zeros_like(best, dtype=jnp.int32)
    for g in range(1, N_GROUPS):
        sc = top2_sum(s[g * eg:(g + 1) * eg])
        better = sc > best
        best = jnp.where(better, sc, best)
        grp = jnp.where(better, g, grp)

    def pick(vals, j):
        out = vals[j]
        for g in range(1, N_GROUPS):
            out = jnp.where(grp == g, vals[g * eg + j], out)
        return out

    sv = [pick(s, j) for j in range(eg)]
    av = [pick(a, j) for j in range(eg)]

    def argmax_first(vals, exclude):
        bi = jnp.zeros_like(grp)
        bv = jnp.where(exclude == 0, -jnp.inf, vals[0]) if exclude is not None else vals[0]
        for j in range(1, eg):
            vj = jnp.where(exclude == j, -jnp.inf, vals[j]) if exclude is not None else vals[j]
            better = vj > bv
            bv = jnp.where(better, vj, bv)
            bi = jnp.where(better, j, bi)
        return bi

    i1 = argmax_first(sv, None)
    i2 = argmax_first(sv, i1)

    def take(vals, i):
        out = vals[0]
        for j in range(1, eg):
            out = jnp.where(i == j, vals[j], out)
        return out

    w1 = take(av, i1)
    w2 = take(av, i2)
    tot = w1 + w2
    idx_ref[0:1, :] = grp * eg + i1
    idx_ref[1:2, :] = grp * eg + i2
    w_ref[0:1, :] = w1 / tot
    w_ref[1:2, :] = w2 / tot


def route(logits_t, router_b, tile=1024):
    n_exp, t = logits_t.shape
    tile = math.gcd(tile, t)
    return pl.pallas_call(
        _route_kernel,
        grid=(t // tile,),
        in_specs=[pl.BlockSpec((n_exp, tile), lambda i: (0, i)),
                  pl.BlockSpec((n_exp, 1), lambda i: (0, 0))],
        out_specs=[pl.BlockSpec((TOP_K, tile), lambda i: (0, i)),
                   pl.BlockSpec((TOP_K, tile), lambda i: (0, i))],
        out_shape=[jax.ShapeDtypeStruct((TOP_K, t), jnp.int32), jax.ShapeDtypeStruct((TOP_K, t), F32)],
        compiler_params=_cparams("arbitrary"),
        name="moe_route",
    )(logits_t, router_b.reshape(n_exp, 1).astype(F32))


def moe_plan(idx, tm):
    t = idx.shape[1]
    n_pair = TOP_K * t
    n_tiles = (n_pair + N_EXPERTS * (tm - 1)) // tm
    e_flat = idx.reshape(-1)
    onehot = (e_flat[:, None] == jnp.arange(N_EXPERTS)[None, :]).astype(jnp.int32)
    running = jnp.cumsum(onehot, axis=0)
    counts = running[-1]
    rank = jnp.sum(onehot * running, 1) - 1
    tiles_per = (counts + tm - 1) // tm
    tile_end = jnp.cumsum(tiles_per)
    n_used = tile_end[-1]
    pstart = (tile_end - tiles_per) * tm
    pos = jnp.sum(onehot * pstart[None, :], 1) + rank
    tile_expert = jnp.minimum(jnp.sum((tile_end[None, :] <= jnp.arange(n_tiles)[:, None]).astype(jnp.int32), 1),
                              N_EXPERTS - 1).astype(jnp.int32)
    gidx = (jnp.arange(n_tiles * tm, dtype=jnp.int32) % t).at[pos].set(
        jnp.arange(n_pair, dtype=jnp.int32) % t, mode="promise_in_bounds", unique_indices=True)
    return gidx, tile_expert, n_used.reshape(1).astype(jnp.int32), pos.astype(jnp.int32)


def _expert_changed(te_ref, tile0):
    i = pl.program_id(0) + tile0
    return jnp.logical_or(pl.program_id(0) == 0, te_ref[i] != te_ref[jnp.maximum(i - 1, 0)])


def _ffn_up_kernel(te_ref, nu_ref, xs_ref, wg_ref, wu_ref, hid_ref, wg_bf, wu_bf, *, tile0):
    used = pl.program_id(0) + tile0 < nu_ref[0]

    @pl.when(jnp.logical_and(used, _expert_changed(te_ref, tile0)))
    def _():
        wg_bf[...] = wg_ref[0].astype(BF16)
        wu_bf[...] = wu_ref[0].astype(BF16)

    @pl.when(used)
    def _():
        xs = xs_ref[...]
        g = _dot(xs, wg_bf[...])
        u = _dot(xs, wu_bf[...])
        hid_ref[...] = ((g * jax.nn.sigmoid(g)) * u).astype(BF16)

    @pl.when(jnp.logical_not(used))
    def _():
        hid_ref[...] = jnp.zeros_like(hid_ref)


def _ffn_down_kernel(te_ref, nu_ref, hid_ref, wd_ref, *rest, tile0):
    o_ref, wd_bf = rest[-2], rest[-1]
    used = pl.program_id(0) + tile0 < nu_ref[0]

    @pl.when(jnp.logical_and(used, _expert_changed(te_ref, tile0)))
    def _():
        wd_bf[...] = wd_ref[0].astype(BF16)

    @pl.when(used)
    def _():
        o_ref[...] = _dot(hid_ref[...], wd_bf[...]).astype(o_ref.dtype)

    @pl.when(jnp.logical_not(used))
    def _():
        o_ref[...] = jnp.zeros_like(o_ref)


def grouped_ffn(xs, tile_expert, n_used, w_gate, w_up, w_down, layer, tm, tile0=0, p_total=None, ys_prev=None):
    p, d = xs.shape
    p_total = p if p_total is None else p_total
    de = w_gate.shape[3]
    n_tiles = p // tm
    wmap = lambda i, te, nu: (layer, te[i + tile0], 0, 0)
    row_in = lambda i, te, nu: (jnp.minimum(i, jnp.maximum(nu[0] - 1 - tile0, 0)), 0)
    hid = pl.pallas_call(
        functools.partial(_ffn_up_kernel, tile0=tile0),
        grid_spec=pltpu.PrefetchScalarGridSpec(
            num_scalar_prefetch=2,
            grid=(n_tiles,),
            in_specs=[pl.BlockSpec((tm, d), row_in),
                      pl.BlockSpec((None, 1, d, de), wmap),
                      pl.BlockSpec((None, 1, d, de), wmap)],
            out_specs=pl.BlockSpec((tm, de), lambda i, te, nu: (i, 0)),
            scratch_shapes=[pltpu.VMEM((d, de), BF16), pltpu.VMEM((d, de), BF16)]),
        out_shape=jax.ShapeDtypeStruct((p, de), BF16),
        compiler_params=_cparams("arbitrary"),
        name="moe_ffn_up",
    )(tile_expert, n_used, xs, w_gate, w_up)
    prev_specs, prev_args, alias = [], [], {}
    if ys_prev is not None:
        prev_specs, prev_args, alias = [pl.BlockSpec(memory_space=pl.ANY)], [ys_prev], {4: 0}
    return pl.pallas_call(
        functools.partial(_ffn_down_kernel, tile0=tile0),
        grid_spec=pltpu.PrefetchScalarGridSpec(
            num_scalar_prefetch=2,
            grid=(n_tiles,),
            in_specs=[pl.BlockSpec((tm, de), row_in),
                      pl.BlockSpec((None, 1, de, d), wmap)] + prev_specs,
            out_specs=pl.BlockSpec((tm, d), lambda i, te, nu: (i + tile0, 0)),
            scratch_shapes=[pltpu.VMEM((de, d), BF16)]),
        out_shape=jax.ShapeDtypeStruct((p_total, d), BF16),
        input_output_aliases=alias,
        compiler_params=_cparams("arbitrary"),
        name="moe_ffn_down",
    )(tile_expert, n_used, hid, w_down, *prev_args)


def _final_kernel(x_ref, y0_ref, y1_ref, w_ref, m5_ref, lg_ref, lb_ref, *rest, alpha):
    o_ref = rest[-1]
    w = w_ref[...]
    y = w[:, 0:1] * y0_ref[...].astype(F32) + w[:, 1:2] * y1_ref[...].astype(F32)
    z = alpha * x_ref[...] + m5_ref[...] * y
    mu = jnp.mean(z, -1, keepdims=True)
    zc = z - mu
    var = jnp.mean(zc * zc, -1, keepdims=True)
    o_ref[...] = zc * lax.rsqrt(var + LN_EPS) * lg_ref[...] + lb_ref[...]


def final_norm(x1, yg, wts, mods4, layer, ln_g, ln_b, alpha, seg_rows, n_batch, tile0=0, out_prev=None, tm=256):
    t, d = x1.shape
    n_tiles = yg.shape[0] // (TOP_K * tm)
    seg = _seg_fn(tm, seg_rows, n_batch)
    row = pl.BlockSpec((tm, d), lambda i: (i + tile0, 0))
    vec = pl.BlockSpec((1, d), lambda i: (0, 0))
    prev_specs, prev_args, alias = [], [], {}
    if out_prev is not None:
        prev_specs, prev_args, alias = [pl.BlockSpec(memory_space=pl.ANY)], [out_prev], {7: 0}
    return pl.pallas_call(
        functools.partial(_final_kernel, alpha=alpha),
        grid=(n_tiles,),
        in_specs=[row, pl.BlockSpec((tm, d), lambda i: (i, 0)), pl.BlockSpec((tm, d), lambda i: (i + n_tiles, 0)),
                  pl.BlockSpec((tm, TOP_K), lambda i: (i + tile0, 0)),
                  _mod_spec(d, layer, 5, lambda i: seg(i + tile0)), vec, vec] + prev_specs,
        out_specs=row,
        out_shape=jax.ShapeDtypeStruct((t, d), F32),
        input_output_aliases=alias,
        compiler_params=_cparams("arbitrary"),
        name="final_norm",
    )(x1, yg, yg, wts, mods4, ln_g.reshape(1, -1), ln_b.reshape(1, -1), *prev_args)


def moe_block(x1, h2, logits_t, router_b, w_gate, w_up, w_down, mods4, layer, ln_g, ln_b,
              alpha, seg_rows, n_batch, tm=512, tm_out=512):
    t = x1.shape[0]
    idx, wts = route(logits_t, router_b)
    gidx, tile_expert, n_used, pos = moe_plan(idx, tm)
    p_total = gidx.shape[0]
    n_tiles = p_total // tm
    cut = (n_tiles // 2) * tm
    ys = None
    for lo, hi in ((0, cut), (cut, p_total)):
        xs = h2.at[gidx[lo:hi]].get(mode="promise_in_bounds")
        ys = grouped_ffn(xs, tile_expert, n_used, w_gate, w_up, w_down, layer, tm,
                         tile0=lo // tm, p_total=p_total, ys_prev=ys)
    half = (t // tm_out // 2) * tm_out
    pos2 = pos.reshape(TOP_K, t)
    out = None
    for lo, hi in ((0, half), (half, t)):
        yg = ys.at[pos2[:, lo:hi].reshape(-1)].get(mode="promise_in_bounds")
        out = final_norm(x1, yg, wts.T, mods4, layer, ln_g, ln_b, alpha, seg_rows, n_batch,
                         tile0=lo // tm_out, out_prev=out, tm=tm_out)
    return out


def s5_matrices(lam_re, lam_im, log_dt, b_re, b_im, c_re, c_im, d_skip):
    f32 = F32
    tc = S5_TC
    n_g, n_p = lam_re.shape[1], lam_re.shape[2]
    n_c = b_re.shape[-1]
    nb = 128 // n_c
    n_q = n_g // nb
    lr, li = lam_re.astype(f32), lam_im.astype(f32)
    dt = jnp.exp(log_dt.astype(f32))[..., None]

    def powers(jvals):
        j = jnp.asarray(np.asarray(jvals, np.float32))[:, None, None, None]
        mag = jnp.exp(lr * dt * j)
        return mag * jnp.cos(li * dt * j), mag * jnp.sin(li * dt * j)

    up = np.arange(tc)
    pw_re, pw_im = powers(np.arange(tc + 1))
    lb_re, lb_im = pw_re[1], pw_im[1]
    den = lr * lr + li * li
    fr = ((lb_re - 1.0) * lr + lb_im * li) / den
    fi = (lb_im * lr - (lb_re - 1.0) * li) / den
    br, bi = b_re.astype(f32), b_im.astype(f32)
    bb_re = fr[..., None] * br - fi[..., None] * bi
    bb_im = fr[..., None] * bi + fi[..., None] * br
    cr, ci = c_re.astype(f32), c_im.astype(f32)

    def times_b(p_re, p_im):
        return (p_re[..., None] * bb_re[None] - p_im[..., None] * bb_im[None],
                p_re[..., None] * bb_im[None] + p_im[..., None] * bb_re[None])

    e_re, e_im = times_b(pw_re, pw_im)
    kmat = jnp.sum(cr[None, :, :, :, :, None] * e_re[:tc, :, :, None, :, :]
                   - ci[None, :, :, :, :, None] * e_im[:tc, :, :, None, :, :], axis=4)
    def lag_slab(k_dir):
        return k_dir.reshape(tc, n_q, nb, n_c, n_c).transpose(1, 0, 4, 2, 3).reshape(n_q, tc, n_c, nb * n_c)
    skip = (jnp.eye(n_c, dtype=f32)[None, None, :, None, :]
            * d_skip.astype(f32).reshape(n_q, nb, n_c)[:, None, None, :, :]).reshape(n_q, 1, n_c, nb * n_c)
    k_c = jnp.concatenate([lag_slab(kmat[:tc, 0]), lag_slab(kmat[:tc, 1]), skip], 1)

    def w_slab(e):
        return e.reshape(tc, n_q, nb, n_p, n_c).transpose(1, 0, 4, 2, 3).reshape(n_q, tc, n_c, nb * n_p)
    ef_re, ef_im = times_b(*powers(tc - 1 - up))
    w_c = jnp.stack([w_slab(ef_re[:, 0]), w_slab(ef_im[:, 0]),
                     w_slab(e_re[:tc, 1]), w_slab(e_im[:tc, 1])], 2)

    def v_slabs(d, p_re, p_im):
        f_re = cr[d][None] * p_re[:, :, None, :] - ci[d][None] * p_im[:, :, None, :]
        f_im = cr[d][None] * p_im[:, :, None, :] + ci[d][None] * p_re[:, :, None, :]
        slab = lambda m: m.reshape(tc, n_q, nb, n_c, n_p).transpose(1, 0, 3, 2, 4).reshape(n_q, tc, n_c, nb * n_p)
        return slab(f_re), slab(-f_im)
    pb_re, pb_im = powers(tc - up)
    vt_c = jnp.stack(v_slabs(0, pw_re[1:, 0], pw_im[1:, 0]) + v_slabs(1, pb_re[:, 1], pb_im[:, 1]), 2)
    dec = lambda m: m.reshape(1, n_g * n_p // 128, 1, 128)
    decay = jnp.concatenate([dec(pw_re[tc, 0]), dec(pw_im[tc, 0]), dec(pw_re[tc, 1]), dec(pw_im[tc, 1])], 0)
    return k_c, w_c, vt_c, decay


def _s5_chunk_rows(ref, n):
    return jnp.concatenate([ref[pl.ds(s, n, stride=S5_TC), :] for s in range(S5_TC)], axis=1).astype(BF16)


def _s5_expand(slab, group_lanes):
    rows = 128
    tiled = jnp.concatenate([slab] * (rows // slab.shape[0]), axis=0)
    r = lax.broadcasted_iota(jnp.int32, tiled.shape, 0) // S5_CH
    l = lax.broadcasted_iota(jnp.int32, tiled.shape, 1) // group_lanes
    return jnp.where(r == l, tiled, 0.0).astype(BF16)


def _s5_in_kernel(ul_ref, uc_ref, wc_ref, fr_ref, fi_ref, br_ref, bi_ref, w_ref, *, n_lat, n_ctx, n_batch):
    b = pl.program_id(1)
    n_plane = wc_ref.shape[1]
    st = wc_ref.shape[3]

    @pl.when(b == 0)
    def _():
        for s in range(S5_TC):
            for i in range(n_plane):
                w_ref[s * 128:(s + 1) * 128, i * st:(i + 1) * st] = _s5_expand(wc_ref[s, i], S5_P)

    w_lat = _dot(_s5_chunk_rows(ul_ref, n_lat), w_ref[...])
    w_ctx = _dot(_s5_chunk_rows(uc_ref, n_ctx), w_ref[...])
    nv = fr_ref.shape[0]
    for i, ref in enumerate((fr_ref, fi_ref, br_ref, bi_ref)):
        for c in range(nv):
            lanes = slice((i * nv + c) * 128, (i * nv + c + 1) * 128)
            ref[c, pl.ds(b, n_ctx, stride=n_batch), :] = w_ctx[:, lanes]
            ref[c, pl.ds(n_ctx * n_batch + b, n_lat, stride=n_batch), :] = w_lat[:, lanes]
            ref[c, pl.ds((n_ctx + n_lat) * n_batch + b, n_ctx, stride=n_batch), :] = w_ctx[:, lanes]


def _s5_scan_kernel(wfr, wfi, wbr, wbi, dec_ref, xfr, xfi, xbr, xbi, *, n_tiles):
    nv = wfr.shape[0]
    low = lax.broadcasted_iota(jnp.int32, (nv, 8, 128), 1) < 4
    a_fr, a_fi, a_br, a_bi = dec_ref[0], dec_ref[1], dec_ref[2], dec_ref[3]

    def half_step(s_re, s_im, a_re, a_im, w_re, w_im):
        return a_re * s_re - a_im * s_im + w_re, a_re * s_im + a_im * s_re + w_im

    def one_dir(w_re_ref, w_im_ref, x_re_ref, x_im_ref, row0, s_re, s_im, a_re, a_im, first_low):
        first = low if first_low else jnp.logical_not(low)
        wt_re, wt_im = w_re_ref[:, pl.ds(row0, 8), :], w_im_ref[:, pl.ds(row0, 8), :]
        wr_re, wr_im = pltpu.roll(wt_re, 4, 1), pltpu.roll(wt_im, 4, 1)
        mid_re, mid_im = half_step(s_re, s_im, a_re, a_im, wr_re, wr_im)
        x_re_ref[:, pl.ds(row0, 8), :] = jnp.where(first, s_re, mid_re)
        x_im_ref[:, pl.ds(row0, 8), :] = jnp.where(first, s_im, mid_im)
        m_re = jnp.where(first, pltpu.roll(mid_re, 4, 1), mid_re)
        m_im = jnp.where(first, pltpu.roll(mid_im, 4, 1), mid_im)
        w2_re = jnp.where(first, wr_re, wt_re)
        w2_im = jnp.where(first, wr_im, wt_im)
        return half_step(m_re, m_im, a_re, a_im, w2_re, w2_im)

    def body(i, carry):
        f_re, f_im, b_re, b_im = carry
        rf = pl.multiple_of(i * 8, 8)
        rb = pl.multiple_of((n_tiles - 1 - i) * 8, 8)
        f_re, f_im = one_dir(wfr, wfi, xfr, xfi, rf, f_re, f_im, a_fr, a_fi, True)
        b_re, b_im = one_dir(wbr, wbi, xbr, xbi, rb, b_re, b_im, a_br, a_bi, False)
        return f_re, f_im, b_re, b_im

    z = jnp.zeros((nv, 8, 128), F32)
    lax.fori_loop(0, n_tiles, body, (z, z, z, z))


def _s5_out_kernel(ul_ref, fr_ref, fi_ref, br_ref, bi_ref, kc_ref, vc_ref, y_ref, mt_ref, vt_ref,
                   *, n_lat, n_ctx, n_batch):
    b = pl.program_id(1)
    tc = S5_TC

    @pl.when(b == 0)
    def _():
        lag = [_s5_expand(kc_ref[j], S5_CH) for j in range(2 * tc)]
        diag = _s5_expand(kc_ref[0] + kc_ref[tc] + kc_ref[2 * tc], S5_CH)
        for s in range(tc):
            for t in range(tc):
                blk = diag if s == t else (lag[t - s] if t > s else lag[tc + s - t])
                mt_ref[s * 128:(s + 1) * 128, t * 128:(t + 1) * 128] = blk
        for t in range(tc):
            for i in range(vt_ref.shape[0]):
                vt_ref[i, t * 128:(t + 1) * 128, :] = _s5_expand(vc_ref[t, i], S5_P)

    y = _dot(_s5_chunk_rows(ul_ref, n_lat), mt_ref[...])
    row0 = n_ctx * n_batch + b
    for i, ref in enumerate((fr_ref, fi_ref, br_ref, bi_ref)):
        xs = jnp.concatenate([ref[c, pl.ds(row0, n_lat, stride=n_batch), :] for c in range(ref.shape[0])], 1)
        y = y + _dot_nt(xs.astype(BF16), vt_ref[i])
    for s in range(S5_TC):
        y_ref[pl.ds(s, n_lat, stride=S5_TC), :] = y[:, s * 128:(s + 1) * 128]


def s5_bidir(u, mats, n_batch, l_lat, l_ctx):
    assert n_batch == 4, "the chunk scan packs two chunks of 4 batch rows per 8-sublane tile"
    k_c, w_c, vt_c, decay = mats
    tc = S5_TC
    wd = u.shape[1]
    n_q = wd // 128
    lane_q = tc * 128
    st_q = (128 // S5_CH) * S5_P
    n_lat, n_ctx = l_lat // tc, l_ctx // tc
    nk = n_lat + 2 * n_ctx
    assert nk % 2 == 0
    rows = nk * n_batch
    ctx0 = (n_batch * l_lat) // l_ctx
    dims = dict(n_lat=n_lat, n_ctx=n_ctx, n_batch=n_batch)

    nv = st_q // 128
    plane = jax.ShapeDtypeStruct((n_q * nv, rows, 128), F32)
    plane_spec = pl.BlockSpec((nv, rows, 128), lambda q, b: (q, 0, 0))
    ul_spec = pl.BlockSpec((l_lat, 128), lambda q, b: (b, q))
    uc_spec = pl.BlockSpec((l_ctx, 128), lambda q, b: (ctx0 + b, q))
    w_planes = pl.pallas_call(
        functools.partial(_s5_in_kernel, **dims),
        grid=(n_q, n_batch),
        in_specs=[ul_spec, uc_spec, pl.BlockSpec((None,) + w_c.shape[1:], lambda q, b: (q, 0, 0, 0, 0))],
        out_specs=[plane_spec] * 4,
        out_shape=[plane] * 4,
        scratch_shapes=[pltpu.VMEM((lane_q, 4 * st_q), BF16)],
        compiler_params=_cparams("arbitrary", "arbitrary"),
        name="s5_chunk_in",
    )(u, u, w_c)

    blk = pl.BlockSpec((nv, rows, 128), lambda j: (j, 0, 0))
    x_planes = pl.pallas_call(
        functools.partial(_s5_scan_kernel, n_tiles=rows // 8),
        grid=(n_q,),
        in_specs=[blk] * 4 + [pl.BlockSpec((4, nv, 1, 128), lambda j: (0, j, 0, 0))],
        out_specs=[blk] * 4,
        out_shape=[plane] * 4,
        compiler_params=_cparams("arbitrary"),
        name="s5_chunk_scan",
    )(*w_planes, decay)

    return pl.pallas_call(
        functools.partial(_s5_out_kernel, **dims),
        grid=(n_q, n_batch),
        in_specs=[ul_spec] + [plane_spec] * 4
                 + [pl.BlockSpec((None,) + k_c.shape[1:], lambda q, b: (q, 0, 0, 0)),
                    pl.BlockSpec((None,) + vt_c.shape[1:], lambda q, b: (q, 0, 0, 0, 0))],
        out_specs=pl.BlockSpec((l_lat, 128), lambda q, b: (b, q)),
        out_shape=jax.ShapeDtypeStruct((n_batch * l_lat, wd), F32),
        scratch_shapes=[pltpu.VMEM((lane_q, lane_q), BF16), pltpu.VMEM((4, lane_q, st_q), BF16)],
        compiler_params=_cparams("arbitrary", "arbitrary"),
        name="s5_chunk_out",
    )(u, *x_planes, k_c, vt_c)


def kernel(x, c, ctx, c_ctx, ada_w, ada_b, ln_mix_g, ln_mix_b, ln_ffn_g, ln_ffn_b, ev_w_in, ev_gate_w2,
           ev_gate_b, ev_rpb, ev_norm_g, ev_w_out, od_w_in, od_lam_re, od_lam_im, od_log_dt, od_b_re,
           od_b_im, od_c_re, od_c_im, od_d, od_w_glu, od_b_glu, od_w_out, router_w, router_b,
           moe_w_gate, moe_w_up, moe_w_down):
    n_batch, l_lat, d = x.shape
    l_ctx = ctx.shape[1]
    depth = ada_w.shape[0]
    assert depth == 2, "one even (NA + GLA) layer followed by one odd (S5) layer"
    alpha = (2.0 * depth) ** 0.25
    n_lat = n_batch * l_lat

    cvec = jnp.concatenate([c, c_ctx[None], jnp.zeros((8 - n_batch - 1, d), F32)], 0)
    mods = compute_mods(cvec, ada_w, ada_b)
    mods4 = mods.reshape(depth, 8, 1, N_MOD * d)
    x_lat, x_ctx = x.reshape(n_lat, d), ctx.reshape(n_batch * l_ctx, d)
    router_wt = router_w.T.astype(F32)

    na_w = NA_HEADS * NA_DH
    wk = GLA_HEADS * GLA_DK
    wv = GLA_HEADS * GLA_DV
    ev_in = ev_w_in.shape[2]
    pad = (-ev_in) % 256
    w_in = jnp.pad(ev_w_in[0], ((0, 0), (0, pad))).astype(BF16)
    proj = mod_matmul(x_lat, x_ctx, mods4, 0, w_in, l_lat, n_batch, tm=512, tn=(ev_in + pad) // 2)
    a_lat, a_ctx = na_attention(proj, na_bias_table(ev_rpb[0], l_lat // GRID_W), n_batch, l_lat, l_ctx)
    col_lr = (3 * na_w + 2 * wk + 2 * wv) // 128
    g2 = jnp.zeros((2, 128, wk), F32)
    g2 = g2.at[0, 0:GLA_RANK].set(ev_gate_w2[0, 0]).at[1, GLA_RANK:2 * GLA_RANK].set(ev_gate_w2[0, 1])
    o_f, o_b = gla_bidir(proj, g2.astype(BF16), ev_gate_b[0].reshape(2, 1, wk), rope_tables(l_lat),
                         n_batch, l_lat, l_ctx,
                         col_q=3 * na_w // wk, col_k=(3 * na_w + wk) // wk,
                         col_v=(3 * na_w + 2 * wk) // wv, col_lr=col_lr)
    x1, h2, logits_t = even_out(a_lat, a_ctx, o_f, o_b, proj, (3 * na_w + 2 * wk + wv) // wv, x_lat, x_ctx,
                                mods4, 0, ev_norm_g[0], ev_w_out[0].astype(BF16), ln_mix_g[0], ln_mix_b[0],
                                router_wt, alpha, l_lat, n_batch)
    w_gate, w_up, w_down = moe_w_gate, moe_w_up, moe_w_down
    rows = moe_block(x1, h2, logits_t, router_b, w_gate, w_up, w_down, mods4, 0,
                     ln_ffn_g[0], ln_ffn_b[0], alpha, l_lat, n_batch)

    u = mod_matmul(rows, None, mods4, 1, od_w_in[0].astype(BF16), l_lat, n_batch, tm=512)
    mats = s5_matrices(od_lam_re[0], od_lam_im[0], od_log_dt[0], od_b_re[0], od_b_im[0],
                       od_c_re[0], od_c_im[0], od_d[0])
    y5 = s5_bidir(u, mats, n_batch, l_lat, l_ctx)
    x1, h2, logits_t = odd_out(y5, rows, mods4, 1, od_w_glu[0].astype(BF16), od_b_glu[0],
                               od_w_out[0].astype(BF16), ln_mix_g[1], ln_mix_b[1], router_wt,
                               alpha, l_lat, n_batch)
    out = moe_block(x1, h2, logits_t, router_b, w_gate, w_up, w_down, mods4, 1,
                    ln_ffn_g[1], ln_ffn_b[1], alpha, l_lat, n_batch)
    return out.reshape(n_batch, l_lat, d)
```

```python
import functools
import math

import numpy as np
import jax
import jax.numpy as jnp
from jax import lax
from jax.experimental import pallas as pl
from jax.experimental.pallas import tpu as pltpu

F32 = jnp.float32
BF16 = jnp.bfloat16
HIGHEST = lax.Precision.HIGHEST

N_MOD = 6
LN_EPS = 1e-5
NORM_EPS = 1e-6

GRID_W = 64
NA_HEADS = 8
NA_DH = 128
NA_KR = 8
NA_KC = 16

GLA_HEADS = 4
GLA_DK = 128
GLA_DV = 256
GLA_RANK = 16
GLA_TAU = 16.0
GLA_CHUNK = 64
ROPE_BASE = 10000.0

S5_CH = 16
S5_P = 64
S5_TC = 16

N_EXPERTS = 16
N_GROUPS = 4
TOP_K = 2
MOE_PARTS = 3

VMEM_LIMIT = 56 * 1024 * 1024
NEG_BIG = -1e30


def _cparams(*sem):
    return pltpu.CompilerParams(dimension_semantics=sem, vmem_limit_bytes=VMEM_LIMIT)


def _dot(a, b):
    return jnp.dot(a, b, preferred_element_type=F32)


def _dot_nt(a, b):
    return lax.dot_general(a, b, (((1,), (1,)), ((), ())), preferred_element_type=F32)


def _dot_tn(a, b):
    return lax.dot_general(a, b, (((0,), (0,)), ((), ())), preferred_element_type=F32)


def _mods_kernel(s_ref, w_ref, b_ref, o_ref):
    s = s_ref[...]
    s = s * jax.nn.sigmoid(s)
    w = w_ref[0]
    s_hi, w_hi = s.astype(BF16), w.astype(BF16)
    s_lo = (s - s_hi.astype(F32)).astype(BF16)
    w_lo = (w - w_hi.astype(F32)).astype(BF16)
    o_ref[0] = _dot(s_hi, w_hi) + _dot(s_lo, w_hi) + _dot(s_hi, w_lo) + b_ref[0]


def compute_mods(cvec, ada_w, ada_b, tn=1024):
    n_layer, d, n = ada_w.shape
    tn = math.gcd(tn, n)
    return pl.pallas_call(
        _mods_kernel,
        grid=(n_layer, n // tn),
        in_specs=[pl.BlockSpec((8, d), lambda l, j: (0, 0)),
                  pl.BlockSpec((1, d, tn), lambda l, j: (l, 0, j)),
                  pl.BlockSpec((1, 1, tn), lambda l, j: (l, 0, j))],
        out_specs=pl.BlockSpec((1, 8, tn), lambda l, j: (l, 0, j)),
        out_shape=jax.ShapeDtypeStruct((n_layer, 8, n), F32),
        compiler_params=_cparams("arbitrary", "arbitrary"),
        name="ada_mods",
    )(cvec, ada_w, ada_b.reshape(n_layer, 1, n))


def _mod_spec(d, layer, which, seg_of_tile):
    return pl.BlockSpec((None, None, 1, d), lambda i, *_: (layer, seg_of_tile(i), 0, which))


def _seg_fn(tm, seg_rows, n_batch):
    return lambda i: jnp.minimum((i * tm) // seg_rows, n_batch)


def _two_source_specs(lat, ctx, tm, tile_of):
    n_lat_tiles = lat.shape[0] // tm
    d = lat.shape[1]
    return [pl.BlockSpec((tm, d), lambda *g: (jnp.minimum(tile_of(*g), n_lat_tiles - 1), 0)),
            pl.BlockSpec((tm, d), lambda *g: (jnp.maximum(tile_of(*g) - n_lat_tiles, 0), 0))]


def _pick_rows(lat_ref, ctx_ref, tile, n_lat_tiles):
    return jnp.where(tile < n_lat_tiles, lat_ref[...], ctx_ref[...])


def _modmm_kernel(xl_ref, xc_ref, s1_ref, s0_ref, w_ref, o_ref, *, n_lat_tiles):
    x = _pick_rows(xl_ref, xc_ref, pl.program_id(1), n_lat_tiles)
    h = x * (1.0 + s1_ref[...]) + s0_ref[...]
    o_ref[...] = _dot(h.astype(BF16), w_ref[...])


def mod_matmul(x_lat, x_ctx, mods4, layer, w_bf16, seg_rows, n_batch, tm=256, tn=None):
    if x_ctx is None:
        t, x_ctx = x_lat.shape[0], x_lat
    else:
        t = x_lat.shape[0] + x_ctx.shape[0]
    d = x_lat.shape[1]
    n = w_bf16.shape[1]
    tn = n if tn is None else tn
    seg = _seg_fn(tm, seg_rows, n_batch)
    return pl.pallas_call(
        functools.partial(_modmm_kernel, n_lat_tiles=x_lat.shape[0] // tm),
        grid=(n // tn, t // tm),
        in_specs=_two_source_specs(x_lat, x_ctx, tm, lambda j, i: i) + [
            pl.BlockSpec((None, None, 1, d), lambda j, i: (layer, seg(i), 0, 1)),
            pl.BlockSpec((None, None, 1, d), lambda j, i: (layer, seg(i), 0, 0)),
            pl.BlockSpec((d, tn), lambda j, i: (0, j))],
        out_specs=pl.BlockSpec((tm, tn), lambda j, i: (i, j)),
        out_shape=jax.ShapeDtypeStruct((t, n), F32),
        compiler_params=_cparams("arbitrary", "arbitrary"),
        name="mod_matmul",
    )(x_lat, x_ctx, mods4, mods4, w_bf16)


NA_RB = 4
NA_BAND = NA_RB + NA_KR - 1


def _na_row_start(r, rows):
    return min(max(r - NA_KR // 2, 0), rows - NA_KR)


def na_bias_table(rpb, rows):
    w = GRID_W
    q = np.arange(w)
    kc = np.arange(w)
    win0 = np.clip(q - NA_KC // 2, 0, w - NA_KC)
    ok = (kc[None, :] >= win0[:, None]) & (kc[None, :] < win0[:, None] + NA_KC)
    dc = np.clip(kc[None, :] - q[:, None] + NA_KC - 1, 0, 2 * NA_KC - 2)
    pick = ((dc[None] == np.arange(2 * NA_KC - 1)[:, None, None]) & ok[None]).astype(np.float32)
    colb = jnp.einsum("hrd,dqk->hrqk", rpb.astype(F32), jnp.asarray(pick), precision=HIGHEST)
    colb = jnp.where(ok[None, None], colb, NEG_BIG)
    neg = jnp.full((rpb.shape[0], w, w), NEG_BIG, F32)

    def block(r0):
        band0 = min(max(r0 - NA_KR // 2, 0), rows - NA_BAND)
        out = []
        for r in range(r0, r0 + NA_RB):
            rs = _na_row_start(r, rows)
            first = rs - r + NA_KR - 1
            cols = [neg] * (rs - band0) + [colb[:, first + j] for j in range(NA_KR)]
            cols += [neg] * (NA_BAND - len(cols))
            out.append(jnp.concatenate(cols, -1))
        return jnp.concatenate(out, 1)

    return jnp.stack([block(0), block(NA_RB), block(rows - NA_RB)], 1)


def _na_kernel(q_ref, k_ref, v_ref, qc_ref, kc_ref, vc_ref, bias_ref, o_ref, oc_ref, kbf, vbf, *, rows):
    w = GRID_W
    n_blk = rows // NA_RB
    scale = NA_DH ** -0.5
    kbf[...] = k_ref[...].astype(BF16)
    vbf[...] = v_ref[...].astype(BF16)
    kc = kc_ref[...].astype(BF16)
    vc = vc_ref[...].astype(BF16)

    def body(pair, carry):
        blocks = (2 * pair, 2 * pair + 1)
        q0, scores = [], []
        for i in blocks:
            r0 = i * NA_RB
            band0 = jnp.clip(r0 - NA_KR // 2, 0, rows - NA_BAND)
            variant = jnp.where(i == 0, 0, jnp.where(i == n_blk - 1, 2, 1))
            q0.append(pl.multiple_of(r0 * w, NA_RB * w))
            k0 = pl.multiple_of(band0 * w, w)
            q = (q_ref[pl.ds(q0[-1], NA_RB * w), :] * scale).astype(BF16)
            s_loc = _dot_nt(q, kbf[pl.ds(k0, NA_BAND * w), :]) + bias_ref[variant]
            scores.append((s_loc, _dot_nt(q, kc), k0))
        probs = []
        for s_loc, s_ctx, k0 in scores:
            m = jnp.maximum(jnp.max(s_loc, -1, keepdims=True), jnp.max(s_ctx, -1, keepdims=True))
            p_loc = jnp.exp(s_loc - m)
            p_ctx = jnp.exp(s_ctx - m)
            den = jnp.sum(p_loc, -1, keepdims=True) + jnp.sum(p_ctx, -1, keepdims=True)
            probs.append((p_loc.astype(BF16), p_ctx.astype(BF16), den, k0))
        for q_start, (p_loc, p_ctx, den, k0) in zip(q0, probs):
            o = _dot(p_loc, vbf[pl.ds(k0, NA_BAND * w), :]) + _dot(p_ctx, vc)
            o_ref[pl.ds(q_start, NA_RB * w), :] = o / den
        return carry

    lax.fori_loop(0, n_blk // 2, body, 0)

    qc = (qc_ref[...] * scale).astype(BF16)
    s = _dot_nt(qc, kc)
    p = jnp.exp(s - jnp.max(s, -1, keepdims=True))
    oc_ref[...] = _dot(p.astype(BF16), vc) / jnp.sum(p, -1, keepdims=True)


def na_attention(proj, bias_tab, n_batch, l_lat, l_ctx):
    h = NA_HEADS
    dh = NA_DH
    rows = l_lat // GRID_W
    ctx0 = (n_batch * l_lat) // l_ctx
    return pl.pallas_call(
        functools.partial(_na_kernel, rows=rows),
        grid=(n_batch, h),
        in_specs=[pl.BlockSpec((l_lat, dh), lambda b, hh: (b, hh)),
                  pl.BlockSpec((l_lat, dh), lambda b, hh: (b, h + hh)),
                  pl.BlockSpec((l_lat, dh), lambda b, hh: (b, 2 * h + hh)),
                  pl.BlockSpec((l_ctx, dh), lambda b, hh: (ctx0 + b, hh)),
                  pl.BlockSpec((l_ctx, dh), lambda b, hh: (ctx0 + b, h + hh)),
                  pl.BlockSpec((l_ctx, dh), lambda b, hh: (ctx0 + b, 2 * h + hh)),
                  pl.BlockSpec((None,) + bias_tab.shape[1:], lambda b, hh: (hh, 0, 0, 0))],
        out_specs=[pl.BlockSpec((l_lat, dh), lambda b, hh: (b, hh)),
                   pl.BlockSpec((l_ctx, dh), lambda b, hh: (b, hh))],
        out_shape=[jax.ShapeDtypeStruct((n_batch * l_lat, h * dh), F32),
                   jax.ShapeDtypeStruct((n_batch * l_ctx, h * dh), F32)],
        scratch_shapes=[pltpu.VMEM((l_lat, dh), BF16), pltpu.VMEM((l_lat, dh), BF16)],
        compiler_params=_cparams("arbitrary", "arbitrary"),
        name="na_attention",
    )(proj, proj, proj, proj, proj, proj, bias_tab)


def rope_tables(l_lat):
    half = GLA_DK // 2
    nf = half // 2
    inv = ROPE_BASE ** (-np.arange(nf, dtype=np.float64) / nf)
    t = np.arange(l_lat)
    lane = np.arange(GLA_DK)
    pos = np.where(lane[None, :] < half, (t // GRID_W)[:, None], (t % GRID_W)[:, None]).astype(np.float64)
    ang = pos * inv[lane % nf][None, :]
    first = (lane % half) < nf
    cos = np.cos(ang)
    sin_a = np.where(first[None, :], -np.sin(ang), 0.0)
    sin_b = np.where(first[None, :], 0.0, np.sin(ang))
    return jnp.asarray(cos, F32), jnp.asarray(sin_a, F32), jnp.asarray(sin_b, F32)


GLA_PREP_CHUNKS = 8


def _gla_prep_kernel(q_ref, k_ref, lr_ref, cos_ref, sa_ref, sb_ref, g2_ref, gb_ref,
                     qdf, kdf, krf, elf, qdb, kdb, krb, elb, *, n_lat_tiles):
    c = GLA_CHUNK
    nch = GLA_PREP_CHUNKS
    nf = GLA_DK // 4
    gscale = GLA_DK ** -0.5
    wk = GLA_HEADS * GLA_DK
    is_lat = pl.program_id(0) < n_lat_tiles
    cos = jnp.where(is_lat, cos_ref[...], 1.0)
    sa = jnp.where(is_lat, sa_ref[...], 0.0)
    sb = jnp.where(is_lat, sb_ref[...], 0.0)

    def rope(x):
        return x * cos + pltpu.roll(x, GLA_DK - nf, 1) * sa + pltpu.roll(x, nf, 1) * sb

    qs, ks_ = [], []
    for h in range(GLA_HEADS):
        hs = slice(h * GLA_DK, (h + 1) * GLA_DK)
        qs.append(rope(q_ref[:, hs]) * gscale)
        ks_.append(rope(k_ref[:, hs]))

    lr = lr_ref[...].astype(BF16)
    row = lax.broadcasted_iota(jnp.int32, (c, c), 0)
    col = lax.broadcasted_iota(jnp.int32, (c, c), 1)
    for d, (qd, kd, kr, el) in enumerate(((qdf, kdf, krf, elf), (qdb, kdb, krb, elb))):
        reverse = d == 1
        tri = ((col >= row) if reverse else (col <= row)).astype(BF16)
        z = _dot(lr, g2_ref[d]) + gb_ref[d]
        g = (jnp.minimum(z, 0.0) - jnp.log1p(jnp.exp(-jnp.abs(z)))) * (1.0 / GLA_TAU)
        g_hi = g.astype(BF16)
        r1 = g - g_hi.astype(F32)
        g_mid = r1.astype(BF16)
        g_lo = (r1 - g_mid.astype(F32)).astype(BF16)
        parts = []
        for ci in range(nch):
            rs = slice(ci * c, (ci + 1) * c)
            parts.append(_dot(tri, g_hi[rs]) + _dot(tri, g_mid[rs]) + _dot(tri, g_lo[rs]))
        b3 = jnp.concatenate(parts, 0).reshape(nch, c, wk)
        bl3 = b3[:, 0:1, :] if reverse else b3[:, c - 1:c, :]
        el[...] = jnp.exp(bl3)
        e_b = jnp.exp(b3).reshape(nch * c, wk)
        e_nb = jnp.exp(-b3).reshape(nch * c, wk)
        e_rem = jnp.exp(bl3 - b3).reshape(nch * c, wk)
        for h in range(GLA_HEADS):
            hs = slice(h * GLA_DK, (h + 1) * GLA_DK)
            qd[:, hs] = (qs[h] * e_b[:, hs]).astype(BF16)
            kd[:, hs] = (ks_[h] * e_nb[:, hs]).astype(BF16)
            kr[:, hs] = (ks_[h] * e_rem[:, hs]).astype(BF16)


def _gla_scan_kernel(qdf, kdf, krf, elf, vf, qdb, kdb, krb, elb, vb, of_ref, ob_ref, st_ref):
    @pl.when(pl.program_id(1) == 0)
    def _():
        st_ref[...] = jnp.zeros_like(st_ref)

    c = GLA_CHUNK
    row = lax.broadcasted_iota(jnp.int32, (c, c), 0)
    col = lax.broadcasted_iota(jnp.int32, (c, c), 1)
    dirs = ((qdf, kdf, krf, elf, vf, of_ref, col <= row, (0, 1)), (qdb, kdb, krb, elb, vb, ob_ref, col >= row, (1, 0)))
    chains = [(d, h) for d in range(2) for h in range(GLA_HEADS)]
    hs = lambda h: slice(h * GLA_DK, (h + 1) * GLA_DK)
    vs = lambda h: slice(h * GLA_DV, (h + 1) * GLA_DV)
    rows = lambda k: slice(k * c, (k + 1) * c)
    order = lambda d, step: rows(dirs[d][7][step])
    q_dec = [[dirs[d][0][order(d, s), hs(h)] for d, h in chains] for s in range(2)]
    v_bf = [[dirs[d][4][order(d, s), vs(h)].astype(BF16) for d, h in chains] for s in range(2)]
    att = [[jnp.where(dirs[d][6], _dot_nt(q_dec[s][n], dirs[d][1][order(d, s), hs(h)]), 0.0).astype(BF16)
            for n, (d, h) in enumerate(chains)] for s in range(2)]
    state = [st_ref[d, h] for d, h in chains]
    for s in range(2):
        for n, (d, h) in enumerate(chains):
            dirs[d][5][order(d, s), vs(h)] = (_dot(att[s][n], v_bf[s][n])
                                              + _dot_nt(q_dec[s][n], state[n].astype(BF16)))
        state = [state[n] * dirs[d][3][dirs[d][7][s], :, hs(h)]
                 + _dot_tn(v_bf[s][n], dirs[d][2][order(d, s), hs(h)]) for n, (d, h) in enumerate(chains)]
    for n, (d, h) in enumerate(chains):
        st_ref[d, h] = state[n]


def gla_bidir(proj, g2, gb, tables, n_batch, l_lat, l_ctx, col_q, col_k, col_v, col_lr):
    c = GLA_CHUNK
    nc = l_ctx // c
    nl = l_lat // c
    nz = nl + 2 * nc
    steps = nl + nc
    wk = GLA_HEADS * GLA_DK
    wv = GLA_HEADS * GLA_DV
    t_rows = n_batch * (l_lat + l_ctx)

    tp = GLA_PREP_CHUNKS * c
    lat_tiles = l_lat // tp
    n_lat_tiles = n_batch * lat_tiles
    cos, sa, sb = tables
    tab = pl.BlockSpec((tp, GLA_DK), lambda i: (jnp.where(i < n_lat_tiles, i % lat_tiles, 0), 0))
    row_bf = jax.ShapeDtypeStruct((t_rows, wk), BF16)
    last = jax.ShapeDtypeStruct((t_rows // c, 1, wk), F32)
    row_spec = pl.BlockSpec((tp, wk), lambda i: (i, 0))
    last_spec = pl.BlockSpec((GLA_PREP_CHUNKS, 1, wk), lambda i: (i, 0, 0))
    prep = pl.pallas_call(
        functools.partial(_gla_prep_kernel, n_lat_tiles=n_lat_tiles),
        grid=(t_rows // tp,),
        in_specs=[pl.BlockSpec((tp, wk), lambda i: (i, col_q)),
                  pl.BlockSpec((tp, wk), lambda i: (i, col_k)),
                  pl.BlockSpec((tp, 128), lambda i: (i, col_lr)),
                  tab, tab, tab,
                  pl.BlockSpec((2, 128, wk), lambda i: (0, 0, 0)),
                  pl.BlockSpec((2, 1, wk), lambda i: (0, 0, 0))],
        out_specs=[row_spec, row_spec, row_spec, last_spec] * 2,
        out_shape=[row_bf, row_bf, row_bf, last] * 2,
        compiler_params=_cparams("arbitrary"),
        name="gla_prep",
    )(proj, proj, proj, cos, sa, sb, g2, gb)

    def zblk(b, j):
        lat = b * nl + (j - nc)
        ctx = n_batch * nl + b * nc + jnp.where(j < nc, j, j - nc - nl)
        return jnp.where((j >= nc) & (j < nc + nl), lat, ctx)

    assert nc % 2 == 0 and nl % 2 == 0
    fwd = lambda b, i: zblk(b, 2 * i) // 2
    bwd = lambda b, i: zblk(b, nz - 2 - 2 * i) // 2
    c2 = 2 * c

    def dir_specs(blk):
        return [pl.BlockSpec((c2, wk), lambda b, i: (blk(b, i), 0)),
                pl.BlockSpec((c2, wk), lambda b, i: (blk(b, i), 0)),
                pl.BlockSpec((c2, wk), lambda b, i: (blk(b, i), 0)),
                pl.BlockSpec((2, 1, wk), lambda b, i: (blk(b, i), 0, 0)),
                pl.BlockSpec((c2, wv), lambda b, i: (blk(b, i), col_v))]

    return pl.pallas_call(
        _gla_scan_kernel,
        grid=(n_batch, steps // 2),
        in_specs=dir_specs(fwd) + dir_specs(bwd),
        out_specs=[pl.BlockSpec((c2, wv), lambda b, i: (fwd(b, i), 0)),
                   pl.BlockSpec((c2, wv), lambda b, i: (bwd(b, i), 0))],
        out_shape=[jax.ShapeDtypeStruct((t_rows, wv), F32), jax.ShapeDtypeStruct((t_rows, wv), F32)],
        scratch_shapes=[pltpu.VMEM((2, GLA_HEADS, GLA_DV, GLA_DK), F32)],
        compiler_params=_cparams("arbitrary", "arbitrary"),
        name="gla_scan",
    )(*prep[0:4], proj, *prep[4:8], proj)


def _post_mix(out, x, m2_ref, m3_ref, m4_ref, lg_ref, lb_ref, wr_ref, alpha, x1_ref, h2_ref, lt_ref, rows):
    y = alpha * x + m2_ref[...] * out
    mu = jnp.mean(y, -1, keepdims=True)
    yc = y - mu
    var = jnp.mean(yc * yc, -1, keepdims=True)
    x1 = yc * lax.rsqrt(var + LN_EPS) * lg_ref[...] + lb_ref[...]
    h2 = x1 * (1.0 + m4_ref[...]) + m3_ref[...]
    x1_ref[rows, :] = x1
    n_exp = lt_ref.shape[0]
    h2_hi = h2.astype(BF16)
    h2_lo = (h2 - h2_hi.astype(F32)).astype(BF16)
    h2_ref[rows, :] = h2_hi
    wr = wr_ref[...]
    wr_hi = wr.astype(BF16)
    wr_lo = (wr - wr_hi.astype(F32)).astype(BF16)
    both = _dot_nt(jnp.concatenate([wr_hi, wr_lo], 0), h2_hi)
    lt_ref[:, rows] = both[:n_exp] + both[n_exp:] + _dot_nt(wr_hi, h2_lo)


def _row_halves(n):
    return (slice(0, n // 2), slice(n // 2, n))


def _even_out_kernel(al_ref, ac_ref, of_ref, ob_ref, r_ref, xl_ref, xc_ref, m2_ref, m3_ref, m4_ref, ng_ref,
                     wo_ref, lg_ref, lb_ref, wr_ref, x1_ref, h2_ref, lt_ref, *, alpha, n_lat_tiles):
    is_lat = pl.program_id(0) < n_lat_tiles
    outs = []
    for rows in _row_halves(of_ref.shape[0]):
        o = of_ref[rows, :] + ob_ref[rows, :]
        r = r_ref[rows, :]
        gate = r * jax.nn.sigmoid(r)
        mixed = [jnp.where(is_lat, al_ref[rows, :], ac_ref[rows, :]).astype(BF16)]
        for h in range(GLA_HEADS):
            vs = slice(h * GLA_DV, (h + 1) * GLA_DV)
            oh = o[:, vs]
            nrm = oh * lax.rsqrt(jnp.mean(oh * oh, -1, keepdims=True) + NORM_EPS) * ng_ref[...]
            mixed.append((nrm * gate[:, vs]).astype(BF16))
        outs.append(_dot(jnp.concatenate(mixed, axis=1), wo_ref[...]))
    x = jnp.where(is_lat, xl_ref[...], xc_ref[...])
    _post_mix(jnp.concatenate(outs, 0), x, m2_ref, m3_ref, m4_ref, lg_ref, lb_ref, wr_ref, alpha,
              x1_ref, h2_ref, lt_ref, slice(None))


def _post_specs(d, layer, seg, tm, n_exp):
    ins = [_mod_spec(d, layer, 2, seg), _mod_spec(d, layer, 3, seg), _mod_spec(d, layer, 4, seg)]
    tail = [pl.BlockSpec((1, d), lambda i: (0, 0)), pl.BlockSpec((1, d), lambda i: (0, 0)),
            pl.BlockSpec((n_exp, d), lambda i: (0, 0))]
    outs =[pl.BlockSpec((tm, d), lambda i: (i, 0)), pl.BlockSpec((tm, d), lambda i: (i, 0)),
            pl.BlockSpec((n_exp, tm), lambda i: (0, i))]
    return ins, tail, outs


def _post_shapes(t, d, n_exp):
    return [jax.ShapeDtypeStruct((t, d), F32), jax.ShapeDtypeStruct((t, d), BF16),
            jax.ShapeDtypeStruct((n_exp, t), F32)]


def even_out(a_lat, a_ctx, o_f, o_b, proj, col_r, x_lat, x_ctx, mods4, layer, norm_g, w_out_bf16, ln_g, ln_b,
             router_wt, alpha, seg_rows, n_batch, tm=256):
    t = x_lat.shape[0] + x_ctx.shape[0]
    d = x_lat.shape[1]
    na = a_lat.shape[1]
    wv = o_f.shape[1]
    n_exp = router_wt.shape[0]
    seg = _seg_fn(tm, seg_rows, n_batch)
    ins, tail, outs = _post_specs(d, layer, seg, tm, n_exp)
    tile_of = lambda i: i
    return pl.pallas_call(
        functools.partial(_even_out_kernel, alpha=alpha, n_lat_tiles=x_lat.shape[0] // tm),
        grid=(t // tm,),
        in_specs=(_two_source_specs(a_lat, a_ctx, tm, tile_of)
                  + [pl.BlockSpec((tm, wv), lambda i: (i, 0)),
                     pl.BlockSpec((tm, wv), lambda i: (i, 0)),
                     pl.BlockSpec((tm, wv), lambda i: (i, col_r))]
                  + _two_source_specs(x_lat, x_ctx, tm, tile_of) + ins
                  + [pl.BlockSpec((1, GLA_DV), lambda i: (0, 0)),
                     pl.BlockSpec((na + wv, d), lambda i: (0, 0))] + tail),
        out_specs=outs,
        out_shape=_post_shapes(t, d, n_exp),
        compiler_params=_cparams("arbitrary"),
        name="even_out",
    )(a_lat, a_ctx, o_f, o_b, proj, x_lat, x_ctx, mods4, mods4, mods4, norm_g.reshape(1, -1), w_out_bf16,
      ln_g.reshape(1, -1), ln_b.reshape(1, -1), router_wt)


def _odd_out_kernel(y_ref, x_ref, m2_ref, m3_ref, m4_ref, wg_ref, bg_ref, wo_ref,
                    lg_ref, lb_ref, wr_ref, x1_ref, h2_ref, lt_ref, *, alpha):
    outs = []
    for r in _row_halves(y_ref.shape[0]):
        g = jax.nn.gelu(y_ref[r, :], approximate=True)
        z = _dot(g.astype(BF16), wg_ref[...]) + bg_ref[...]
        outs.append(_dot((g * jax.nn.sigmoid(z)).astype(BF16), wo_ref[...]))
    _post_mix(jnp.concatenate(outs, 0), x_ref[...], m2_ref, m3_ref, m4_ref, lg_ref, lb_ref, wr_ref, alpha,
              x1_ref, h2_ref, lt_ref, slice(None))


def odd_out(y, x, mods4, layer, w_glu_bf16, b_glu, w_out_bf16, ln_g, ln_b, router_wt,
            alpha, seg_rows, n_batch, tm=256):
    t, w5 = y.shape
    d = x.shape[1]
    n_exp = router_wt.shape[0]
    seg = _seg_fn(tm, seg_rows, n_batch)
    ins, tail, outs = _post_specs(d, layer, seg, tm, n_exp)
    return pl.pallas_call(
        functools.partial(_odd_out_kernel, alpha=alpha),
        grid=(t // tm,),
        in_specs=([pl.BlockSpec((tm, w5), lambda i: (i, 0)), pl.BlockSpec((tm, d), lambda i: (i, 0))] + ins
                  + [pl.BlockSpec((w5, w5), lambda i: (0, 0)),
                     pl.BlockSpec((1, w5), lambda i: (0, 0)),
                     pl.BlockSpec((w5, d), lambda i: (0, 0))] + tail),
        out_specs=outs,
        out_shape=_post_shapes(t, d, n_exp),
        compiler_params=_cparams("arbitrary"),
        name="odd_out",
    )(y, x, mods4, mods4, mods4, w_glu_bf16, b_glu.reshape(1, -1), w_out_bf16,
      ln_g.reshape(1, -1), ln_b.reshape(1, -1), router_wt)


def _route_kernel(lt_ref, rb_ref, idx_ref, w_ref):
    eg = N_EXPERTS // N_GROUPS
    logits = lt_ref[...]
    aff = jax.nn.sigmoid(logits)
    sel = aff + rb_ref[...]
    s = [sel[e:e + 1, :] for e in range(N_EXPERTS)]
    a = [aff[e:e + 1, :] for e in range(N_EXPERTS)]

    def top2_sum(v):
        hi1, lo1 = jnp.maximum(v[0], v[1]), jnp.minimum(v[0], v[1])
        hi2, lo2 = jnp.maximum(v[2], v[3]), jnp.minimum(v[2], v[3])
        return jnp.maximum(hi1, hi2) + jnp.maximum(jnp.minimum(hi1, hi2), jnp.maximum(lo1, lo2))

    best = top2_sum(s[0:eg])
    grp = jnp.zeros_like(best, dtype=jnp.int32)
    for g in range(1, N_GROUPS):
        sc = top2_sum(s[g * eg:(g + 1) * eg])
        better = sc > best
        best = jnp.where(better, sc, best)
        grp = jnp.where(better, g, grp)

    def pick(vals, j):
        out = vals[j]
        for g in range(1, N_GROUPS):
            out = jnp.where(grp == g, vals[g * eg + j], out)
        return out

    sv = [pick(s, j) for j in range(eg)]
    av = [pick(a, j) for j in range(eg)]

    def argmax_first(vals, exclude):
        bi = jnp.zeros_like(grp)
        bv = jnp.where(exclude == 0, -jnp.inf, vals[0]) if exclude is not None else vals[0]
        for j in range(1, eg):
            vj = jnp.where(exclude == j, -jnp.inf, vals[j]) if exclude is not None else vals[j]
            better = vj > bv
            bv = jnp.where(better, vj, bv)
            bi = jnp.where(better, j, bi)
        return bi

    i1 = argmax_first(sv, None)
    i2 = argmax_first(sv, i1)

    def take(vals, i):
        out = vals[0]
        for j in range(1, eg):
            out = jnp.where(i == j, vals[j], out)
        return out

    w1 = take(av, i1)
    w2 = take(av, i2)
    tot = w1 + w2
    idx_ref[0:1, :] = grp * eg + i1
    idx_ref[1:2, :] = grp * eg + i2
    w_ref[0:1, :] = w1 / tot
    w_ref[1:2, :] = w2 / tot


def route(logits_t, router_b, tile=1024):
    n_exp, t = logits_t.shape
    tile = math.gcd(tile, t)
    return pl.pallas_call(
        _route_kernel,
        grid=(t // tile,),
        in_specs=[pl.BlockSpec((n_exp, tile), lambda i: (0, i)),
                  pl.BlockSpec((n_exp, 1), lambda i: (0, 0))],
        out_specs=[pl.BlockSpec((TOP_K, tile), lambda i: (0, i)),
                   pl.BlockSpec((TOP_K, tile), lambda i: (0, i))],
        out_shape=[jax.ShapeDtypeStruct((TOP_K, t), jnp.int32), jax.ShapeDtypeStruct((TOP_K, t), F32)],
        compiler_params=_cparams("arbitrary"),
        name="moe_route",
    )(logits_t, router_b.reshape(n_exp, 1).astype(F32))


def moe_plan(idx, tm):
    t = idx.shape[1]
    n_pair = TOP_K * t
    n_tiles = (n_pair + N_EXPERTS * (tm - 1)) // tm
    e_flat = idx.reshape(-1)
    onehot = (e_flat[:, None] == jnp.arange(N_EXPERTS)[None, :]).astype(jnp.int32)
    running = jnp.cumsum(onehot, axis=0)
    counts = running[-1]
    rank = jnp.sum(onehot * running, 1) - 1
    tiles_per = (counts + tm - 1) // tm
    tile_end = jnp.cumsum(tiles_per)
    n_used = tile_end[-1]
    pstart = (tile_end - tiles_per) * tm
    pos = jnp.sum(onehot * pstart[None, :], 1) + rank
    tile_expert = jnp.minimum(jnp.sum((tile_end[None, :] <= jnp.arange(n_tiles)[:, None]).astype(jnp.int32), 1),
                              N_EXPERTS - 1).astype(jnp.int32)
    gidx = (jnp.arange(n_tiles * tm, dtype=jnp.int32) % t).at[pos].set(
        jnp.arange(n_pair, dtype=jnp.int32) % t, mode="promise_in_bounds", unique_indices=True)
    return gidx, tile_expert, n_used.reshape(1).astype(jnp.int32), pos.astype(jnp.int32)


def _expert_changed(te_ref, tile0):
    i = pl.program_id(0) + tile0
    return jnp.logical_or(pl.program_id(0) == 0, te_ref[i] != te_ref[jnp.maximum(i - 1, 0)])


def _ffn_up_kernel(te_ref, nu_ref, xs_ref, wg_ref, wu_ref, hid_ref, wg_bf, wu_bf, *, tile0):
    used = pl.program_id(0) + tile0 < nu_ref[0]

    @pl.when(jnp.logical_and(used, _expert_changed(te_ref, tile0)))
    def _():
        wg_bf[...] = wg_ref[0].astype(BF16)
        wu_bf[...] = wu_ref[0].astype(BF16)

    @pl.when(used)
    def _():
        xs = xs_ref[...]
        g = _dot(xs, wg_bf[...])
        u = _dot(xs, wu_bf[...])
        hid_ref[...] = ((g * jax.nn.sigmoid(g)) * u).astype(BF16)

    @pl.when(jnp.logical_not(used))
    def _():
        hid_ref[...] = jnp.zeros_like(hid_ref)


def _ffn_down_kernel(te_ref, nu_ref, hid_ref, wd_ref, *rest, tile0):
    o_ref, wd_bf = rest[-2], rest[-1]
    used = pl.program_id(0) + tile0 < nu_ref[0]

    @pl.when(jnp.logical_and(used, _expert_changed(te_ref, tile0)))
    def _():
        wd_bf[...] = wd_ref[0].astype(BF16)

    @pl.when(used)
    def _():
        o_ref[...] = _dot(hid_ref[...], wd_bf[...]).astype(o_ref.dtype)

    @pl.when(jnp.logical_not(used))
    def _():
        o_ref[...] = jnp.zeros_like(o_ref)


def grouped_ffn(xs, tile_expert, n_used, w_gate, w_up, w_down, layer, tm, tile0=0, p_total=None, ys_prev=None):
    p, d = xs.shape
    p_total = p if p_total is None else p_total
    de = w_gate.shape[3]
    n_tiles = p // tm
    wmap = lambda i, te, nu: (layer, te[i + tile0], 0, 0)
    row_in = lambda i, te, nu: (jnp.minimum(i, jnp.maximum(nu[0] - 1 - tile0, 0)), 0)
    hid = pl.pallas_call(
        functools.partial(_ffn_up_kernel, tile0=tile0),
        grid_spec=pltpu.PrefetchScalarGridSpec(
            num_scalar_prefetch=2,
            grid=(n_tiles,),
            in_specs=[pl.BlockSpec((tm, d), row_in),
                      pl.BlockSpec((None, 1, d, de), wmap),
                      pl.BlockSpec((None, 1, d, de), wmap)],
            out_specs=pl.BlockSpec((tm, de), lambda i, te, nu: (i, 0)),
            scratch_shapes=[pltpu.VMEM((d, de), BF16), pltpu.VMEM((d, de), BF16)]),
        out_shape=jax.ShapeDtypeStruct((p, de), BF16),
        compiler_params=_cparams("arbitrary"),
        name="moe_ffn_up",
    )(tile_expert, n_used, xs, w_gate, w_up)
    prev_specs, prev_args, alias = [], [], {}
    if ys_prev is not None:
        prev_specs, prev_args, alias = [pl.BlockSpec(memory_space=pl.ANY)], [ys_prev], {4: 0}
    return pl.pallas_call(
        functools.partial(_ffn_down_kernel, tile0=tile0),
        grid_spec=pltpu.PrefetchScalarGridSpec(
            num_scalar_prefetch=2,
            grid=(n_tiles,),
            in_specs=[pl.BlockSpec((tm, de), row_in),
                      pl.BlockSpec((None, 1, de, d), wmap)] + prev_specs,
            out_specs=pl.BlockSpec((tm, d), lambda i, te, nu: (i + tile0, 0)),
            scratch_shapes=[pltpu.VMEM((de, d), BF16)]),
        out_shape=jax.ShapeDtypeStruct((p_total, d), BF16),
        input_output_aliases=alias,
        compiler_params=_cparams("arbitrary"),
        name="moe_ffn_down",
    )(tile_expert, n_used, hid, w_down, *prev_args)


def _final_kernel(x_ref, y0_ref, y1_ref, w_ref, m5_ref, lg_ref, lb_ref, *rest, alpha):
    o_ref = rest[-1]
    w = w_ref[...]
    y = w[:, 0:1] * y0_ref[...].astype(F32) + w[:, 1:2] * y1_ref[...].astype(F32)
    z = alpha * x_ref[...] + m5_ref[...] * y
    mu = jnp.mean(z, -1, keepdims=True)
    zc = z - mu
    var = jnp.mean(zc * zc, -1, keepdims=True)
    o_ref[...] = zc * lax.rsqrt(var + LN_EPS) * lg_ref[...] + lb_ref[...]


def final_norm(x1, yg, wts, mods4, layer, ln_g, ln_b, alpha, seg_rows, n_batch, tile0=0, out_prev=None, tm=256):
    t, d = x1.shape
    n_tiles = yg.shape[0] // (TOP_K * tm)
    seg = _seg_fn(tm, seg_rows, n_batch)
    row = pl.BlockSpec((tm, d), lambda i: (i + tile0, 0))
    vec = pl.BlockSpec((1, d), lambda i: (0, 0))
    prev_specs, prev_args, alias = [], [], {}
    if out_prev is not None:
        prev_specs, prev_args, alias = [pl.BlockSpec(memory_space=pl.ANY)], [out_prev], {7: 0}
    return pl.pallas_call(
        functools.partial(_final_kernel, alpha=alpha),
        grid=(n_tiles,),
        in_specs=[row, pl.BlockSpec((tm, d), lambda i: (i, 0)), pl.BlockSpec((tm, d), lambda i: (i + n_tiles, 0)),
                  pl.BlockSpec((tm, TOP_K), lambda i: (i + tile0, 0)),
                  _mod_spec(d, layer, 5, lambda i: seg(i + tile0)), vec, vec] + prev_specs,
        out_specs=row,
        out_shape=jax.ShapeDtypeStruct((t, d), F32),
        input_output_aliases=alias,
        compiler_params=_cparams("arbitrary"),
        name="final_norm",
    )(x1, yg, yg, wts, mods4, ln_g.reshape(1, -1), ln_b.reshape(1, -1), *prev_args)


def moe_block(x1, h2, logits_t, router_b, w_gate, w_up, w_down, mods4, layer, ln_g, ln_b,
              alpha, seg_rows, n_batch, tm=512, tm_out=512):
    t = x1.shape[0]
    idx, wts = route(logits_t, router_b)
    gidx, tile_expert, n_used, pos = moe_plan(idx, tm)
    p_total = gidx.shape[0]

    def cuts(n_tiles, tile):
        return [(k * n_tiles // MOE_PARTS) * tile for k in range(MOE_PARTS + 1)]

    ys = None
    rows = cuts(p_total // tm, tm)
    for lo, hi in zip(rows[:-1], rows[1:]):
        xs = h2.at[gidx[lo:hi]].get(mode="promise_in_bounds")
        ys = grouped_ffn(xs, tile_expert, n_used, w_gate, w_up, w_down, layer, tm,
                         tile0=lo // tm, p_total=p_total, ys_prev=ys)
    pos2 = pos.reshape(TOP_K, t)
    out = None
    toks = cuts(t // tm_out, tm_out)
    for lo, hi in zip(toks[:-1], toks[1:]):
        yg = ys.at[pos2[:, lo:hi].reshape(-1)].get(mode="promise_in_bounds")
        out = final_norm(x1, yg, wts.T, mods4, layer, ln_g, ln_b, alpha, seg_rows, n_batch,
                         tile0=lo // tm_out, out_prev=out, tm=tm_out)
    return out


def s5_matrices(lam_re, lam_im, log_dt, b_re, b_im, c_re, c_im, d_skip):
    f32 = F32
    tc = S5_TC
    n_g, n_p = lam_re.shape[1], lam_re.shape[2]
    n_c = b_re.shape[-1]
    nb = 128 // n_c
    n_q = n_g // nb
    lr, li = lam_re.astype(f32), lam_im.astype(f32)
    dt = jnp.exp(log_dt.astype(f32))[..., None]

    def powers(jvals):
        j = jnp.asarray(np.asarray(jvals, np.float32))[:, None, None, None]
        mag = jnp.exp(lr * dt * j)
        return mag * jnp.cos(li * dt * j), mag * jnp.sin(li * dt * j)

    up = np.arange(tc)
    pw_re, pw_im = powers(np.arange(tc + 1))
    lb_re, lb_im = pw_re[1], pw_im[1]
    den = lr * lr + li * li
    fr = ((lb_re - 1.0) * lr + lb_im * li) / den
    fi = (lb_im * lr - (lb_re - 1.0) * li) / den
    br, bi = b_re.astype(f32), b_im.astype(f32)
    bb_re = fr[..., None] * br - fi[..., None] * bi
    bb_im = fr[..., None] * bi + fi[..., None] * br
    cr, ci = c_re.astype(f32), c_im.astype(f32)

    def times_b(p_re, p_im):
        return (p_re[..., None] * bb_re[None] - p_im[..., None] * bb_im[None],
                p_re[..., None] * bb_im[None] + p_im[..., None] * bb_re[None])

    e_re, e_im = times_b(pw_re, pw_im)
    kmat = jnp.sum(cr[None, :, :, :, :, None] * e_re[:tc, :, :, None, :, :]
                   - ci[None, :, :, :, :, None] * e_im[:tc, :, :, None, :, :], axis=4)
    def lag_slab(k_dir):
        return k_dir.reshape(tc, n_q, nb, n_c, n_c).transpose(1, 0, 4, 2, 3).reshape(n_q, tc, n_c, nb * n_c)
    skip = (jnp.eye(n_c, dtype=f32)[None, None, :, None, :]
            * d_skip.astype(f32).reshape(n_q, nb, n_c)[:, None, None, :, :]).reshape(n_q, 1, n_c, nb * n_c)
    k_c = jnp.concatenate([lag_slab(kmat[:tc, 0]), lag_slab(kmat[:tc, 1]), skip], 1)

    def w_slab(e):
        return e.reshape(tc, n_q, nb, n_p, n_c).transpose(1, 0, 4, 2, 3).reshape(n_q, tc, n_c, nb * n_p)
    ef_re, ef_im = times_b(*powers(tc - 1 - up))
    w_c = jnp.stack([w_slab(ef_re[:, 0]), w_slab(ef_im[:, 0]),
                     w_slab(e_re[:tc, 1]), w_slab(e_im[:tc, 1])], 2)

    def v_slabs(d, p_re, p_im):
        f_re = cr[d][None] * p_re[:, :, None, :] - ci[d][None] * p_im[:, :, None, :]
        f_im = cr[d][None] * p_im[:, :, None, :] + ci[d][None] * p_re[:, :, None, :]
        slab = lambda m: m.reshape(tc, n_q, nb, n_c, n_p).transpose(1, 0, 3, 2, 4).reshape(n_q, tc, n_c, nb * n_p)
        return slab(f_re), slab(-f_im)
    pb_re, pb_im = powers(tc - up)
    vt_c = jnp.stack(v_slabs(0, pw_re[1:, 0], pw_im[1:, 0]) + v_slabs(1, pb_re[:, 1], pb_im[:, 1]), 2)
    dec = lambda m: m.reshape(1, n_g * n_p // 128, 1, 128)
    decay = jnp.concatenate([dec(pw_re[tc, 0]), dec(pw_im[tc, 0]), dec(pw_re[tc, 1]), dec(pw_im[tc, 1])], 0)
    return k_c, w_c, vt_c, decay


def _s5_chunk_rows(ref, n):
    return jnp.concatenate([ref[pl.ds(s, n, stride=S5_TC), :] for s in range(S5_TC)], axis=1).astype(BF16)


def _s5_expand(slab, group_lanes):
    rows = 128
    tiled = jnp.concatenate([slab] * (rows // slab.shape[0]), axis=0)
    r = lax.broadcasted_iota(jnp.int32, tiled.shape, 0) // S5_CH
    l = lax.broadcasted_iota(jnp.int32, tiled.shape, 1) // group_lanes
    return jnp.where(r == l, tiled, 0.0).astype(BF16)


def _s5_in_kernel(ul_ref, uc_ref, wc_ref, fr_ref, fi_ref, br_ref, bi_ref, w_ref, *, n_lat, n_ctx, n_batch):
    b = pl.program_id(1)
    n_plane = wc_ref.shape[1]
    st = wc_ref.shape[3]

    @pl.when(b == 0)
    def _():
        for s in range(S5_TC):
            for i in range(n_plane):
                w_ref[s * 128:(s + 1) * 128, i * st:(i + 1) * st] = _s5_expand(wc_ref[s, i], S5_P)

    w_lat = _dot(_s5_chunk_rows(ul_ref, n_lat), w_ref[...])
    w_ctx = _dot(_s5_chunk_rows(uc_ref, n_ctx), w_ref[...])
    nv = fr_ref.shape[0]
    for i, ref in enumerate((fr_ref, fi_ref, br_ref, bi_ref)):
        for c in range(nv):
            lanes = slice((i * nv + c) * 128, (i * nv + c + 1) * 128)
            ref[c, pl.ds(b, n_ctx, stride=n_batch), :] = w_ctx[:, lanes]
            ref[c, pl.ds(n_ctx * n_batch + b, n_lat, stride=n_batch), :] = w_lat[:, lanes]
            ref[c, pl.ds((n_ctx + n_lat) * n_batch + b, n_ctx, stride=n_batch), :] = w_ctx[:, lanes]


def _s5_scan_kernel(wfr, wfi, wbr, wbi, dec_ref, xfr, xfi, xbr, xbi, *, n_tiles):
    nv = wfr.shape[0]
    low = lax.broadcasted_iota(jnp.int32, (nv, 8, 128), 1) < 4
    a_fr, a_fi, a_br, a_bi = dec_ref[0], dec_ref[1], dec_ref[2], dec_ref[3]

    def half_step(s_re, s_im, a_re, a_im, w_re, w_im):
        return a_re * s_re - a_im * s_im + w_re, a_re * s_im + a_im * s_re + w_im

    def one_dir(w_re_ref, w_im_ref, x_re_ref, x_im_ref, row0, s_re, s_im, a_re, a_im, first_low):
        first = low if first_low else jnp.logical_not(low)
        wt_re, wt_im = w_re_ref[:, pl.ds(row0, 8), :], w_im_ref[:, pl.ds(row0, 8), :]
        wr_re, wr_im = pltpu.roll(wt_re, 4, 1), pltpu.roll(wt_im, 4, 1)
        mid_re, mid_im = half_step(s_re, s_im, a_re, a_im, wr_re, wr_im)
        x_re_ref[:, pl.ds(row0, 8), :] = jnp.where(first, s_re, mid_re)
        x_im_ref[:, pl.ds(row0, 8), :] = jnp.where(first, s_im, mid_im)
        m_re = jnp.where(first, pltpu.roll(mid_re, 4, 1), mid_re)
        m_im = jnp.where(first, pltpu.roll(mid_im, 4, 1), mid_im)
        w2_re = jnp.where(first, wr_re, wt_re)
        w2_im = jnp.where(first, wr_im, wt_im)
        return half_step(m_re, m_im, a_re, a_im, w2_re, w2_im)

    def body(i, carry):
        f_re, f_im, b_re, b_im = carry
        rf = pl.multiple_of(i * 8, 8)
        rb = pl.multiple_of((n_tiles - 1 - i) * 8, 8)
        f_re, f_im = one_dir(wfr, wfi, xfr, xfi, rf, f_re, f_im, a_fr, a_fi, True)
        b_re, b_im = one_dir(wbr, wbi, xbr, xbi, rb, b_re, b_im, a_br, a_bi, False)
        return f_re, f_im, b_re, b_im

    z = jnp.zeros((nv, 8, 128), F32)
    lax.fori_loop(0, n_tiles, body, (z, z, z, z))


def _s5_out_kernel(ul_ref, fr_ref, fi_ref, br_ref, bi_ref, kc_ref, vc_ref, y_ref, mt_ref, vt_ref,
                   *, n_lat, n_ctx, n_batch):
    b = pl.program_id(1)
    tc = S5_TC

    @pl.when(b == 0)
    def _():
        lag = [_s5_expand(kc_ref[j], S5_CH) for j in range(2 * tc)]
        diag = _s5_expand(kc_ref[0] + kc_ref[tc] + kc_ref[2 * tc], S5_CH)
        for s in range(tc):
            for t in range(tc):
                blk = diag if s == t else (lag[t - s] if t > s else lag[tc + s - t])
                mt_ref[s * 128:(s + 1) * 128, t * 128:(t + 1) * 128] = blk
        for t in range(tc):
            for i in range(vt_ref.shape[0]):
                vt_ref[i, t * 128:(t + 1) * 128, :] = _s5_expand(vc_ref[t, i], S5_P)

    y = _dot(_s5_chunk_rows(ul_ref, n_lat), mt_ref[...])
    row0 = n_ctx * n_batch + b
    for i, ref in enumerate((fr_ref, fi_ref, br_ref, bi_ref)):
        xs = jnp.concatenate([ref[c, pl.ds(row0, n_lat, stride=n_batch), :] for c in range(ref.shape[0])], 1)
        y = y + _dot_nt(xs.astype(BF16), vt_ref[i])
    for s in range(S5_TC):
        y_ref[pl.ds(s, n_lat, stride=S5_TC), :] = y[:, s * 128:(s + 1) * 128]


def s5_bidir(u, mats, n_batch, l_lat, l_ctx):
    assert n_batch == 4, "the chunk scan packs two chunks of 4 batch rows per 8-sublane tile"
    k_c, w_c, vt_c, decay = mats
    tc = S5_TC
    wd = u.shape[1]
    n_q = wd // 128
    lane_q = tc * 128
    st_q = (128 // S5_CH) * S5_P
    n_lat, n_ctx = l_lat // tc, l_ctx // tc
    nk = n_lat + 2 * n_ctx
    assert nk % 2 == 0
    rows = nk * n_batch
    ctx0 = (n_batch * l_lat) // l_ctx
    dims = dict(n_lat=n_lat, n_ctx=n_ctx, n_batch=n_batch)

    nv = st_q // 128
    plane = jax.ShapeDtypeStruct((n_q * nv, rows, 128), F32)
    plane_spec = pl.BlockSpec((nv, rows, 128), lambda q, b: (q, 0, 0))
    ul_spec = pl.BlockSpec((l_lat, 128), lambda q, b: (b, q))
    uc_spec = pl.BlockSpec((l_ctx, 128), lambda q, b: (ctx0 + b, q))
    w_planes = pl.pallas_call(
        functools.partial(_s5_in_kernel, **dims),
        grid=(n_q, n_batch),
        in_specs=[ul_spec, uc_spec, pl.BlockSpec((None,) + w_c.shape[1:], lambda q, b: (q, 0, 0, 0, 0))],
        out_specs=[plane_spec] * 4,
        out_shape=[plane] * 4,
        scratch_shapes=[pltpu.VMEM((lane_q, 4 * st_q), BF16)],
        compiler_params=_cparams("arbitrary", "arbitrary"),
        name="s5_chunk_in",
    )(u, u, w_c)

    blk = pl.BlockSpec((nv, rows, 128), lambda j: (j, 0, 0))
    x_planes = pl.pallas_call(
        functools.partial(_s5_scan_kernel, n_tiles=rows // 8),
        grid=(n_q,),
        in_specs=[blk] * 4 + [pl.BlockSpec((4, nv, 1, 128), lambda j: (0, j, 0, 0))],
        out_specs=[blk] * 4,
        out_shape=[plane] * 4,
        compiler_params=_cparams("arbitrary"),
        name="s5_chunk_scan",
    )(*w_planes, decay)

    return pl.pallas_call(
        functools.partial(_s5_out_kernel, **dims),
        grid=(n_q, n_batch),
        in_specs=[ul_spec] + [plane_spec] * 4
                 + [pl.BlockSpec((None,) + k_c.shape[1:], lambda q, b: (q, 0, 0, 0)),
                    pl.BlockSpec((None,) + vt_c.shape[1:], lambda q, b: (q, 0, 0, 0, 0))],
        out_specs=pl.BlockSpec((l_lat, 128), lambda q, b: (b, q)),
        out_shape=jax.ShapeDtypeStruct((n_batch * l_lat, wd), F32),
        scratch_shapes=[pltpu.VMEM((lane_q, lane_q), BF16), pltpu.VMEM((4, lane_q, st_q), BF16)],
        compiler_params=_cparams("arbitrary", "arbitrary"),
        name="s5_chunk_out",
    )(u, *x_planes, k_c, vt_c)


def kernel(x, c, ctx, c_ctx, ada_w, ada_b, ln_mix_g, ln_mix_b, ln_ffn_g, ln_ffn_b, ev_w_in, ev_gate_w2,
           ev_gate_b, ev_rpb, ev_norm_g, ev_w_out, od_w_in, od_lam_re, od_lam_im, od_log_dt, od_b_re,
           od_b_im, od_c_re, od_c_im, od_d, od_w_glu, od_b_glu, od_w_out, router_w, router_b,
           moe_w_gate, moe_w_up, moe_w_down):
    n_batch, l_lat, d = x.shape
    l_ctx = ctx.shape[1]
    depth = ada_w.shape[0]
    assert depth == 2, "one even (NA + GLA) layer followed by one odd (S5) layer"
    alpha = (2.0 * depth) ** 0.25
    n_lat = n_batch * l_lat

    cvec = jnp.concatenate([c, c_ctx[None], jnp.zeros((8 - n_batch - 1, d), F32)], 0)
    mods = compute_mods(cvec, ada_w, ada_b)
    mods4 = mods.reshape(depth, 8, 1, N_MOD * d)
    x_lat, x_ctx = x.reshape(n_lat, d), ctx.reshape(n_batch * l_ctx, d)
    router_wt = router_w.T.astype(F32)

    na_w = NA_HEADS * NA_DH
    wk = GLA_HEADS * GLA_DK
    wv = GLA_HEADS * GLA_DV
    ev_in = ev_w_in.shape[2]
    pad = (-ev_in) % 256
    w_in = jnp.pad(ev_w_in[0], ((0, 0), (0, pad))).astype(BF16)
    proj = mod_matmul(x_lat, x_ctx, mods4, 0, w_in, l_lat, n_batch, tm=512, tn=(ev_in + pad) // 2)
    a_lat, a_ctx = na_attention(proj, na_bias_table(ev_rpb[0], l_lat // GRID_W), n_batch, l_lat, l_ctx)
    col_lr = (3 * na_w + 2 * wk + 2 * wv) // 128
    g2 = jnp.zeros((2, 128, wk), F32)
    g2 = g2.at[0, 0:GLA_RANK].set(ev_gate_w2[0, 0]).at[1, GLA_RANK:2 * GLA_RANK].set(ev_gate_w2[0, 1])
    o_f, o_b = gla_bidir(proj, g2.astype(BF16), ev_gate_b[0].reshape(2, 1, wk), rope_tables(l_lat),
                         n_batch, l_lat, l_ctx,
                         col_q=3 * na_w // wk, col_k=(3 * na_w + wk) // wk,
                         col_v=(3 * na_w + 2 * wk) // wv, col_lr=col_lr)
    x1, h2, logits_t = even_out(a_lat, a_ctx, o_f, o_b, proj, (3 * na_w + 2 * wk + wv) // wv, x_lat, x_ctx,
                                mods4, 0, ev_norm_g[0], ev_w_out[0].astype(BF16), ln_mix_g[0], ln_mix_b[0],
                                router_wt, alpha, l_lat, n_batch)
    w_gate, w_up, w_down = moe_w_gate, moe_w_up, moe_w_down
    rows = moe_block(x1, h2, logits_t, router_b, w_gate, w_up, w_down, mods4, 0,
                     ln_ffn_g[0], ln_ffn_b[0], alpha, l_lat, n_batch)

    u = mod_matmul(rows, None, mods4, 1, od_w_in[0].astype(BF16), l_lat, n_batch, tm=512)
    mats = s5_matrices(od_lam_re[0], od_lam_im[0], od_log_dt[0], od_b_re[0], od_b_im[0],
                       od_c_re[0], od_c_im[0], od_d[0])
    y5 = s5_bidir(u, mats, n_batch, l_lat, l_ctx)
    x1, h2, logits_t = odd_out(y5, rows, mods4, 1, od_w_glu[0].astype(BF16), od_b_glu[0],
                               od_w_out[0].astype(BF16), ln_mix_g[1], ln_mix_b[1], router_wt,
                               alpha, l_lat, n_batch)
    out = moe_block(x1, h2, logits_t, router_b, w_gate, w_up, w_down, mods4, 1,
                    ln_ffn_g[1], ln_ffn_b[1], alpha, l_lat, n_batch)
    return out.reshape(n_batch, l_lat, d)
```

```python
import functools
import math

import numpy as np
import jax
import jax.numpy as jnp
from jax import lax
from jax.experimental import pallas as pl
from jax.experimental.pallas import tpu as pltpu

F32 = jnp.float32
BF16 = jnp.bfloat16
HIGHEST = lax.Precision.HIGHEST

N_MOD = 6
LN_EPS = 1e-5
NORM_EPS = 1e-6

GRID_W = 64
NA_HEADS = 8
NA_DH = 128
NA_KR = 8
NA_KC = 16

GLA_HEADS = 4
GLA_DK = 128
GLA_DV = 256
GLA_RANK = 16
GLA_TAU = 16.0
GLA_CHUNK = 64
ROPE_BASE = 10000.0

S5_CH = 16
S5_P = 64
S5_TC = 16

N_EXPERTS = 16
N_GROUPS = 4
TOP_K = 2
MOE_PARTS = 2

VMEM_LIMIT = 56 * 1024 * 1024
NEG_BIG = -1e30


def _cparams(*sem):
    return pltpu.CompilerParams(dimension_semantics=sem, vmem_limit_bytes=VMEM_LIMIT)


def _dot(a, b):
    return jnp.dot(a, b, preferred_element_type=F32)


def _dot_nt(a, b):
    return lax.dot_general(a, b, (((1,), (1,)), ((), ())), preferred_element_type=F32)


def _dot_tn(a, b):
    return lax.dot_general(a, b, (((0,), (0,)), ((), ())), preferred_element_type=F32)


def _mods_kernel(s_ref, w_ref, b_ref, o_ref):
    s = s_ref[...]
    s = s * jax.nn.sigmoid(s)
    w = w_ref[0]
    s_hi, w_hi = s.astype(BF16), w.astype(BF16)
    s_lo = (s - s_hi.astype(F32)).astype(BF16)
    w_lo = (w - w_hi.astype(F32)).astype(BF16)
    o_ref[0] = _dot(s_hi, w_hi) + _dot(s_lo, w_hi) + _dot(s_hi, w_lo) + b_ref[0]


def compute_mods(cvec, ada_w, ada_b, tn=1024):
    n_layer, d, n = ada_w.shape
    tn = math.gcd(tn, n)
    return pl.pallas_call(
        _mods_kernel,
        grid=(n_layer, n // tn),
        in_specs=[pl.BlockSpec((8, d), lambda l, j: (0, 0)),
                  pl.BlockSpec((1, d, tn), lambda l, j: (l, 0, j)),
                  pl.BlockSpec((1, 1, tn), lambda l, j: (l, 0, j))],
        out_specs=pl.BlockSpec((1, 8, tn), lambda l, j: (l, 0, j)),
        out_shape=jax.ShapeDtypeStruct((n_layer, 8, n), F32),
        compiler_params=_cparams("arbitrary", "arbitrary"),
        name="ada_mods",
    )(cvec, ada_w, ada_b.reshape(n_layer, 1, n))


def _mod_spec(d, layer, which, seg_of_tile):
    return pl.BlockSpec((None, None, 1, d), lambda i, *_: (layer, seg_of_tile(i), 0, which))


def _seg_fn(tm, seg_rows, n_batch):
    return lambda i: jnp.minimum((i * tm) // seg_rows, n_batch)


def _two_source_specs(lat, ctx, tm, tile_of):
    n_lat_tiles = lat.shape[0] // tm
    d = lat.shape[1]
    return [pl.BlockSpec((tm, d), lambda *g: (jnp.minimum(tile_of(*g), n_lat_tiles - 1), 0)),
            pl.BlockSpec((tm, d), lambda *g: (jnp.maximum(tile_of(*g) - n_lat_tiles, 0), 0))]


def _pick_rows(lat_ref, ctx_ref, tile, n_lat_tiles):
    return jnp.where(tile < n_lat_tiles, lat_ref[...], ctx_ref[...])


def _modmm_kernel(xl_ref, xc_ref, s1_ref, s0_ref, w_ref, o_ref, *, n_lat_tiles):
    x = _pick_rows(xl_ref, xc_ref, pl.program_id(1), n_lat_tiles)
    h = x * (1.0 + s1_ref[...]) + s0_ref[...]
    o_ref[...] = _dot(h.astype(BF16), w_ref[...])


def mod_matmul(x_lat, x_ctx, mods4, layer, w_bf16, seg_rows, n_batch, tm=256, tn=None):
    if x_ctx is None:
        t, x_ctx = x_lat.shape[0], x_lat
    else:
        t = x_lat.shape[0] + x_ctx.shape[0]
    d = x_lat.shape[1]
    n = w_bf16.shape[1]
    tn = n if tn is None else tn
    seg = _seg_fn(tm, seg_rows, n_batch)
    return pl.pallas_call(
        functools.partial(_modmm_kernel, n_lat_tiles=x_lat.shape[0] // tm),
        grid=(n // tn, t // tm),
        in_specs=_two_source_specs(x_lat, x_ctx, tm, lambda j, i: i) + [
            pl.BlockSpec((None, None, 1, d), lambda j, i: (layer, seg(i), 0, 1)),
            pl.BlockSpec((None, None, 1, d), lambda j, i: (layer, seg(i), 0, 0)),
            pl.BlockSpec((d, tn), lambda j, i: (0, j))],
        out_specs=pl.BlockSpec((tm, tn), lambda j, i: (i, j)),
        out_shape=jax.ShapeDtypeStruct((t, n), F32),
        compiler_params=_cparams("arbitrary", "arbitrary"),
        name="mod_matmul",
    )(x_lat, x_ctx, mods4, mods4, w_bf16)


NA_RB = 4
NA_BAND = NA_RB + NA_KR - 1


def _na_row_start(r, rows):
    return min(max(r - NA_KR // 2, 0), rows - NA_KR)


def na_bias_table(rpb, rows):
    w = GRID_W
    q = np.arange(w)
    kc = np.arange(w)
    win0 = np.clip(q - NA_KC // 2, 0, w - NA_KC)
    ok = (kc[None, :] >= win0[:, None]) & (kc[None, :] < win0[:, None] + NA_KC)
    dc = np.clip(kc[None, :] - q[:, None] + NA_KC - 1, 0, 2 * NA_KC - 2)
    pick = ((dc[None] == np.arange(2 * NA_KC - 1)[:, None, None]) & ok[None]).astype(np.float32)
    colb = jnp.einsum("hrd,dqk->hrqk", rpb.astype(F32), jnp.asarray(pick), precision=HIGHEST)
    colb = jnp.where(ok[None, None], colb, NEG_BIG)
    neg = jnp.full((rpb.shape[0], w, w), NEG_BIG, F32)

    def block(r0):
        band0 = min(max(r0 - NA_KR // 2, 0), rows - NA_BAND)
        out = []
        for r in range(r0, r0 + NA_RB):
            rs = _na_row_start(r, rows)
            first = rs - r + NA_KR - 1
            cols = [neg] * (rs - band0) + [colb[:, first + j] for j in range(NA_KR)]
            cols += [neg] * (NA_BAND - len(cols))
            out.append(jnp.concatenate(cols, -1))
        return jnp.concatenate(out, 1)

    return jnp.stack([block(0), block(NA_RB), block(rows - NA_RB)], 1)


def _na_kernel(q_ref, k_ref, v_ref, qc_ref, kc_ref, vc_ref, bias_ref, o_ref, oc_ref, kbf, vbf, *, rows):
    w = GRID_W
    n_blk = rows // NA_RB
    scale = NA_DH ** -0.5
    kbf[...] = k_ref[...].astype(BF16)
    vbf[...] = v_ref[...].astype(BF16)
    kc = kc_ref[...].astype(BF16)
    vc = vc_ref[...].astype(BF16)

    def body(pair, carry):
        blocks = tuple(4 * pair + k for k in range(4))
        q0, scores = [], []
        for i in blocks:
            r0 = i * NA_RB
            band0 = jnp.clip(r0 - NA_KR // 2, 0, rows - NA_BAND)
            variant = jnp.where(i == 0, 0, jnp.where(i == n_blk - 1, 2, 1))
            q0.append(pl.multiple_of(r0 * w, NA_RB * w))
            k0 = pl.multiple_of(band0 * w, w)
            q = (q_ref[pl.ds(q0[-1], NA_RB * w), :] * scale).astype(BF16)
            s_loc = _dot_nt(q, kbf[pl.ds(k0, NA_BAND * w), :]) + bias_ref[variant]
            scores.append((s_loc, _dot_nt(q, kc), k0))
        probs = []
        for s_loc, s_ctx, k0 in scores:
            m = jnp.maximum(jnp.max(s_loc, -1, keepdims=True), jnp.max(s_ctx, -1, keepdims=True))
            p_loc = jnp.exp(s_loc - m)
            p_ctx = jnp.exp(s_ctx - m)
            den = jnp.sum(p_loc, -1, keepdims=True) + jnp.sum(p_ctx, -1, keepdims=True)
            probs.append((p_loc.astype(BF16), p_ctx.astype(BF16), den, k0))
        for q_start, (p_loc, p_ctx, den, k0) in zip(q0, probs):
            o = _dot(p_loc, vbf[pl.ds(k0, NA_BAND * w), :]) + _dot(p_ctx, vc)
            o_ref[pl.ds(q_start, NA_RB * w), :] = o / den
        return carry

    lax.fori_loop(0, n_blk // 4, body, 0)

    qc = (qc_ref[...] * scale).astype(BF16)
    s = _dot_nt(qc, kc)
    p = jnp.exp(s - jnp.max(s, -1, keepdims=True))
    oc_ref[...] = _dot(p.astype(BF16), vc) / jnp.sum(p, -1, keepdims=True)


def na_attention(proj, bias_tab, n_batch, l_lat, l_ctx):
    h = NA_HEADS
    dh = NA_DH
    rows = l_lat // GRID_W
    ctx0 = (n_batch * l_lat) // l_ctx
    return pl.pallas_call(
        functools.partial(_na_kernel, rows=rows),
        grid=(n_batch, h),
        in_specs=[pl.BlockSpec((l_lat, dh), lambda b, hh: (b, hh)),
                  pl.BlockSpec((l_lat, dh), lambda b, hh: (b, h + hh)),
                  pl.BlockSpec((l_lat, dh), lambda b, hh: (b, 2 * h + hh)),
                  pl.BlockSpec((l_ctx, dh), lambda b, hh: (ctx0 + b, hh)),
                  pl.BlockSpec((l_ctx, dh), lambda b, hh: (ctx0 + b, h + hh)),
                  pl.BlockSpec((l_ctx, dh), lambda b, hh: (ctx0 + b, 2 * h + hh)),
                  pl.BlockSpec((None,) + bias_tab.shape[1:], lambda b, hh: (hh, 0, 0, 0))],
        out_specs=[pl.BlockSpec((l_lat, dh), lambda b, hh: (b, hh)),
                   pl.BlockSpec((l_ctx, dh), lambda b, hh: (b, hh))],
        out_shape=[jax.ShapeDtypeStruct((n_batch * l_lat, h * dh), F32),
                   jax.ShapeDtypeStruct((n_batch * l_ctx, h * dh), F32)],
        scratch_shapes=[pltpu.VMEM((l_lat, dh), BF16), pltpu.VMEM((l_lat, dh), BF16)],
        compiler_params=_cparams("arbitrary", "arbitrary"),
        name="na_attention",
    )(proj, proj, proj, proj, proj, proj, bias_tab)


def rope_tables(l_lat):
    half = GLA_DK // 2
    nf = half // 2
    inv = ROPE_BASE ** (-np.arange(nf, dtype=np.float64) / nf)
    t = np.arange(l_lat)
    lane = np.arange(GLA_DK)
    pos = np.where(lane[None, :] < half, (t // GRID_W)[:, None], (t % GRID_W)[:, None]).astype(np.float64)
    ang = pos * inv[lane % nf][None, :]
    first = (lane % half) < nf
    cos = np.cos(ang)
    sin_a = np.where(first[None, :], -np.sin(ang), 0.0)
    sin_b = np.where(first[None, :], 0.0, np.sin(ang))
    return jnp.asarray(cos, F32), jnp.asarray(sin_a, F32), jnp.asarray(sin_b, F32)


GLA_PREP_CHUNKS = 8


def _gla_prep_kernel(q_ref, k_ref, lr_ref, cos_ref, sa_ref, sb_ref, g2_ref, gb_ref,
                     qdf, kdf, krf, elf, qdb, kdb, krb, elb, *, n_lat_tiles):
    c = GLA_CHUNK
    nch = GLA_PREP_CHUNKS
    nf = GLA_DK // 4
    gscale = GLA_DK ** -0.5
    wk = GLA_HEADS * GLA_DK
    is_lat = pl.program_id(0) < n_lat_tiles
    cos = jnp.where(is_lat, cos_ref[...], 1.0)
    sa = jnp.where(is_lat, sa_ref[...], 0.0)
    sb = jnp.where(is_lat, sb_ref[...], 0.0)

    def rope(x):
        return x * cos + pltpu.roll(x, GLA_DK - nf, 1) * sa + pltpu.roll(x, nf, 1) * sb

    qs, ks_ = [], []
    for h in range(GLA_HEADS):
        hs = slice(h * GLA_DK, (h + 1) * GLA_DK)
        qs.append(rope(q_ref[:, hs]) * gscale)
        ks_.append(rope(k_ref[:, hs]))

    lr = lr_ref[...].astype(BF16)
    row = lax.broadcasted_iota(jnp.int32, (c, c), 0)
    col = lax.broadcasted_iota(jnp.int32, (c, c), 1)
    for d, (qd, kd, kr, el) in enumerate(((qdf, kdf, krf, elf), (qdb, kdb, krb, elb))):
        reverse = d == 1
        tri = ((col >= row) if reverse else (col <= row)).astype(BF16)
        z = _dot(lr, g2_ref[d]) + gb_ref[d]
        g = (jnp.minimum(z, 0.0) - jnp.log1p(jnp.exp(-jnp.abs(z)))) * (1.0 / GLA_TAU)
        g_hi = g.astype(BF16)
        r1 = g - g_hi.astype(F32)
        g_mid = r1.astype(BF16)
        g_lo = (r1 - g_mid.astype(F32)).astype(BF16)
        parts = []
        for ci in range(nch):
            rs = slice(ci * c, (ci + 1) * c)
            parts.append(_dot(tri, g_hi[rs]) + _dot(tri, g_mid[rs]) + _dot(tri, g_lo[rs]))
        b3 = jnp.concatenate(parts, 0).reshape(nch, c, wk)
        bl3 = b3[:, 0:1, :] if reverse else b3[:, c - 1:c, :]
        el[...] = jnp.exp(bl3)
        e_b = jnp.exp(b3).reshape(nch * c, wk)
        e_nb = jnp.exp(-b3).reshape(nch * c, wk)
        e_rem = jnp.exp(bl3 - b3).reshape(nch * c, wk)
        for h in range(GLA_HEADS):
            hs = slice(h * GLA_DK, (h + 1) * GLA_DK)
            qd[:, hs] = (qs[h] * e_b[:, hs]).astype(BF16)
            kd[:, hs] = (ks_[h] * e_nb[:, hs]).astype(BF16)
            kr[:, hs] = (ks_[h] * e_rem[:, hs]).astype(BF16)


def _gla_scan_kernel(qdf, kdf, krf, elf, vf, qdb, kdb, krb, elb, vb, of_ref, ob_ref, st_ref):
    @pl.when(pl.program_id(1) == 0)
    def _():
        st_ref[...] = jnp.zeros_like(st_ref)

    c = GLA_CHUNK
    row = lax.broadcasted_iota(jnp.int32, (c, c), 0)
    col = lax.broadcasted_iota(jnp.int32, (c, c), 1)
    dirs = ((qdf, kdf, krf, elf, vf, of_ref, col <= row, (0, 1)), (qdb, kdb, krb, elb, vb, ob_ref, col >= row, (1, 0)))
    chains = [(d, h) for d in range(2) for h in range(GLA_HEADS)]
    hs = lambda h: slice(h * GLA_DK, (h + 1) * GLA_DK)
    vs = lambda h: slice(h * GLA_DV, (h + 1) * GLA_DV)
    rows = lambda k: slice(k * c, (k + 1) * c)
    order = lambda d, step: rows(dirs[d][7][step])
    q_dec = [[dirs[d][0][order(d, s), hs(h)] for d, h in chains] for s in range(2)]
    v_bf = [[dirs[d][4][order(d, s), vs(h)].astype(BF16) for d, h in chains] for s in range(2)]
    att = [[jnp.where(dirs[d][6], _dot_nt(q_dec[s][n], dirs[d][1][order(d, s), hs(h)]), 0.0).astype(BF16)
            for n, (d, h) in enumerate(chains)] for s in range(2)]
    state = [st_ref[d, h] for d, h in chains]
    for s in range(2):
        for n, (d, h) in enumerate(chains):
            dirs[d][5][order(d, s), vs(h)] = (_dot(att[s][n], v_bf[s][n])
                                              + _dot_nt(q_dec[s][n], state[n].astype(BF16)))
        state = [state[n] * dirs[d][3][dirs[d][7][s], :, hs(h)]
                 + _dot_tn(v_bf[s][n], dirs[d][2][order(d, s), hs(h)]) for n, (d, h) in enumerate(chains)]
    for n, (d, h) in enumerate(chains):
        st_ref[d, h] = state[n]


def gla_bidir(proj, g2, gb, tables, n_batch, l_lat, l_ctx, col_q, col_k, col_v, col_lr):
    c = GLA_CHUNK
    nc = l_ctx // c
    nl = l_lat // c
    nz = nl + 2 * nc
    steps = nl + nc
    wk = GLA_HEADS * GLA_DK
    wv = GLA_HEADS * GLA_DV
    t_rows = n_batch * (l_lat + l_ctx)

    tp = GLA_PREP_CHUNKS * c
    lat_tiles = l_lat // tp
    n_lat_tiles = n_batch * lat_tiles
    cos, sa, sb = tables
    tab = pl.BlockSpec((tp, GLA_DK), lambda i: (jnp.where(i < n_lat_tiles, i % lat_tiles, 0), 0))
    row_bf = jax.ShapeDtypeStruct((t_rows, wk), BF16)
    last = jax.ShapeDtypeStruct((t_rows // c, 1, wk), F32)
    row_spec = pl.BlockSpec((tp, wk), lambda i: (i, 0))
    last_spec = pl.BlockSpec((GLA_PREP_CHUNKS, 1, wk), lambda i: (i, 0, 0))
    prep = pl.pallas_call(
        functools.partial(_gla_prep_kernel, n_lat_tiles=n_lat_tiles),
        grid=(t_rows // tp,),
        in_specs=[pl.BlockSpec((tp, wk), lambda i: (i, col_q)),
                  pl.BlockSpec((tp, wk), lambda i: (i, col_k)),
                  pl.BlockSpec((tp, 128), lambda i: (i, col_lr)),
                  tab, tab, tab,
                  pl.BlockSpec((2, 128, wk), lambda i: (0, 0, 0)),
                  pl.BlockSpec((2, 1, wk), lambda i: (0, 0, 0))],
        out_specs=[row_spec, row_spec, row_spec, last_spec] * 2,
        out_shape=[row_bf, row_bf, row_bf, last] * 2,
        compiler_params=_cparams("arbitrary"),
        name="gla_prep",
    )(proj, proj, proj, cos, sa, sb, g2, gb)

    def zblk(b, j):
        lat = b * nl + (j - nc)
        ctx = n_batch * nl + b * nc + jnp.where(j < nc, j, j - nc - nl)
        return jnp.where((j >= nc) & (j < nc + nl), lat, ctx)

    assert nc % 2 == 0 and nl % 2 == 0
    fwd = lambda b, i: zblk(b, 2 * i) // 2
    bwd = lambda b, i: zblk(b, nz - 2 - 2 * i) // 2
    c2 = 2 * c

    def dir_specs(blk):
        return [pl.BlockSpec((c2, wk), lambda b, i: (blk(b, i), 0)),
                pl.BlockSpec((c2, wk), lambda b, i: (blk(b, i), 0)),
                pl.BlockSpec((c2, wk), lambda b, i: (blk(b, i), 0)),
                pl.BlockSpec((2, 1, wk), lambda b, i: (blk(b, i), 0, 0)),
                pl.BlockSpec((c2, wv), lambda b, i: (blk(b, i), col_v))]

    return pl.pallas_call(
        _gla_scan_kernel,
        grid=(n_batch, steps // 2),
        in_specs=dir_specs(fwd) + dir_specs(bwd),
        out_specs=[pl.BlockSpec((c2, wv), lambda b, i: (fwd(b, i), 0)),
                   pl.BlockSpec((c2, wv), lambda b, i: (bwd(b, i), 0))],
        out_shape=[jax.ShapeDtypeStruct((t_rows, wv), F32), jax.ShapeDtypeStruct((t_rows, wv), F32)],
        scratch_shapes=[pltpu.VMEM((2, GLA_HEADS, GLA_DV, GLA_DK), F32)],
        compiler_params=_cparams("arbitrary", "arbitrary"),
        name="gla_scan",
    )(*prep[0:4], proj, *prep[4:8], proj)


def _post_mix(out, x, m2_ref, m3_ref, m4_ref, lg_ref, lb_ref, wr_ref, alpha, x1_ref, h2_ref, lt_ref, rows):
    y = alpha * x + m2_ref[...] * out
    mu = jnp.mean(y, -1, keepdims=True)
    yc = y - mu
    var = jnp.mean(yc * yc, -1, keepdims=True)
    x1 = yc * lax.rsqrt(var + LN_EPS) * lg_ref[...] + lb_ref[...]
    h2 = x1 * (1.0 + m4_ref[...]) + m3_ref[...]
    x1_ref[rows, :] = x1
    n_exp = lt_ref.shape[0]
    h2_hi = h2.astype(BF16)
    h2_lo = (h2 - h2_hi.astype(F32)).astype(BF16)
    h2_ref[rows, :] = h2_hi
    wr = wr_ref[...]
    wr_hi = wr.astype(BF16)
    wr_lo = (wr - wr_hi.astype(F32)).astype(BF16)
    both = _dot_nt(jnp.concatenate([wr_hi, wr_lo], 0), h2_hi)
    lt_ref[:, rows] = both[:n_exp] + both[n_exp:] + _dot_nt(wr_hi, h2_lo)


def _row_halves(n):
    return (slice(0, n // 2), slice(n // 2, n))


def _even_out_kernel(al_ref, ac_ref, of_ref, ob_ref, r_ref, xl_ref, xc_ref, m2_ref, m3_ref, m4_ref, ng_ref,
                     wo_ref, lg_ref, lb_ref, wr_ref, x1_ref, h2_ref, lt_ref, *, alpha, n_lat_tiles):
    is_lat = pl.program_id(0) < n_lat_tiles
    lhs = []
    for rows in _row_halves(of_ref.shape[0]):
        o = of_ref[rows, :] + ob_ref[rows, :]
        r = r_ref[rows, :]
        gate = r * jax.nn.sigmoid(r)
        mixed = [jnp.where(is_lat, al_ref[rows, :], ac_ref[rows, :]).astype(BF16)]
        for h in range(GLA_HEADS):
            vs = slice(h * GLA_DV, (h + 1) * GLA_DV)
            oh = o[:, vs]
            nrm = oh * lax.rsqrt(jnp.mean(oh * oh, -1, keepdims=True) + NORM_EPS) * ng_ref[...]
            mixed.append((nrm * gate[:, vs]).astype(BF16))
        lhs.append(jnp.concatenate(mixed, axis=1))
    outs = [_dot(m, wo_ref[...]) for m in lhs]
    x = jnp.where(is_lat, xl_ref[...], xc_ref[...])
    _post_mix(jnp.concatenate(outs, 0), x, m2_ref, m3_ref, m4_ref, lg_ref, lb_ref, wr_ref, alpha,
              x1_ref, h2_ref, lt_ref, slice(None))


def _post_specs(d, layer, seg, tm, n_exp):
    ins = [_mod_spec(d, layer, 2, seg), _mod_spec(d, layer, 3, seg), _mod_spec(d, layer, 4, seg)]
    tail = [pl.BlockSpec((1, d), lambda i: (0, 0)), pl.BlockSpec((1, d), lambda i: (0, 0)),
            pl.BlockSpec((n_exp, d), lambda i: (0, 0))]
    outs =[pl.BlockSpec((tm, d), lambda i: (i, 0)), pl.BlockSpec((tm, d), lambda i: (i, 0)),
            pl.BlockSpec((n_exp, tm), lambda i: (0, i))]
    return ins, tail, outs


def _post_shapes(t, d, n_exp):
    return [jax.ShapeDtypeStruct((t, d), F32), jax.ShapeDtypeStruct((t, d), BF16),
            jax.ShapeDtypeStruct((n_exp, t), F32)]


def even_out(a_lat, a_ctx, o_f, o_b, proj, col_r, x_lat, x_ctx, mods4, layer, norm_g, w_out_bf16, ln_g, ln_b,
             router_wt, alpha, seg_rows, n_batch, tm=256):
    t = x_lat.shape[0] + x_ctx.shape[0]
    d = x_lat.shape[1]
    na = a_lat.shape[1]
    wv = o_f.shape[1]
    n_exp = router_wt.shape[0]
    seg = _seg_fn(tm, seg_rows, n_batch)
    ins, tail, outs = _post_specs(d, layer, seg, tm, n_exp)
    tile_of = lambda i: i
    return pl.pallas_call(
        functools.partial(_even_out_kernel, alpha=alpha, n_lat_tiles=x_lat.shape[0] // tm),
        grid=(t // tm,),
        in_specs=(_two_source_specs(a_lat, a_ctx, tm, tile_of)
                  + [pl.BlockSpec((tm, wv), lambda i: (i, 0)),
                     pl.BlockSpec((tm, wv), lambda i: (i, 0)),
                     pl.BlockSpec((tm, wv), lambda i: (i, col_r))]
                  + _two_source_specs(x_lat, x_ctx, tm, tile_of) + ins
                  + [pl.BlockSpec((1, GLA_DV), lambda i: (0, 0)),
                     pl.BlockSpec((na + wv, d), lambda i: (0, 0))] + tail),
        out_specs=outs,
        out_shape=_post_shapes(t, d, n_exp),
        compiler_params=_cparams("arbitrary"),
        name="even_out",
    )(a_lat, a_ctx, o_f, o_b, proj, x_lat, x_ctx, mods4, mods4, mods4, norm_g.reshape(1, -1), w_out_bf16,
      ln_g.reshape(1, -1), ln_b.reshape(1, -1), router_wt)


def _odd_out_kernel(y_ref, x_ref, m2_ref, m3_ref, m4_ref, wg_ref, bg_ref, wo_ref,
                    lg_ref, lb_ref, wr_ref, x1_ref, h2_ref, lt_ref, *, alpha):
    gs = [jax.nn.gelu(y_ref[r, :], approximate=True) for r in _row_halves(y_ref.shape[0])]
    zs = [_dot(g.astype(BF16), wg_ref[...]) + bg_ref[...] for g in gs]
    outs = [_dot((g * jax.nn.sigmoid(z)).astype(BF16), wo_ref[...]) for g, z in zip(gs, zs)]
    _post_mix(jnp.concatenate(outs, 0), x_ref[...], m2_ref, m3_ref, m4_ref, lg_ref, lb_ref, wr_ref, alpha,
              x1_ref, h2_ref, lt_ref, slice(None))


def odd_out(y, x, mods4, layer, w_glu_bf16, b_glu, w_out_bf16, ln_g, ln_b, router_wt,
            alpha, seg_rows, n_batch, tm=256):
    t, w5 = y.shape
    d = x.shape[1]
    n_exp = router_wt.shape[0]
    seg = _seg_fn(tm, seg_rows, n_batch)
    ins, tail, outs = _post_specs(d, layer, seg, tm, n_exp)
    return pl.pallas_call(
        functools.partial(_odd_out_kernel, alpha=alpha),
        grid=(t // tm,),
        in_specs=([pl.BlockSpec((tm, w5), lambda i: (i, 0)), pl.BlockSpec((tm, d), lambda i: (i, 0))] + ins
                  + [pl.BlockSpec((w5, w5), lambda i: (0, 0)),
                     pl.BlockSpec((1, w5), lambda i: (0, 0)),
                     pl.BlockSpec((w5, d), lambda i: (0, 0))] + tail),
        out_specs=outs,
        out_shape=_post_shapes(t, d, n_exp),
        compiler_params=_cparams("arbitrary"),
        name="odd_out",
    )(y, x, mods4, mods4, mods4, w_glu_bf16, b_glu.reshape(1, -1), w_out_bf16,
      ln_g.reshape(1, -1), ln_b.reshape(1, -1), router_wt)


def _route_kernel(lt_ref, rb_ref, idx_ref, w_ref):
    eg = N_EXPERTS // N_GROUPS
    logits = lt_ref[...]
    aff = jax.nn.sigmoid(logits)
    sel = aff + rb_ref[...]
    s = [sel[e:e + 1, :] for e in range(N_EXPERTS)]
    a = [aff[e:e + 1, :] for e in range(N_EXPERTS)]

    def top2_sum(v):
        hi1, lo1 = jnp.maximum(v[0], v[1]), jnp.minimum(v[0], v[1])
        hi2, lo2 = jnp.maximum(v[2], v[3]), jnp.minimum(v[2], v[3])
        return jnp.maximum(hi1, hi2) + jnp.maximum(jnp.minimum(hi1, hi2), jnp.maximum(lo1, lo2))

    best = top2_sum(s[0:eg])
    grp = jnp.zeros_like(best, dtype=jnp.int32)
    for g in range(1, N_GROUPS):
        sc = top2_sum(s[g * eg:(g + 1) * eg])
        better = sc > best
        best = jnp.where(better, sc, best)
        grp = jnp.where(better, g, grp)

    def pick(vals, j):
        out = vals[j]
        for g in range(1, N_GROUPS):
            out = jnp.where(grp == g, vals[g * eg + j], out)
        return out

    sv = [pick(s, j) for j in range(eg)]
    av = [pick(a, j) for j in range(eg)]

    def argmax_first(vals, exclude):
        bi = jnp.zeros_like(grp)
        bv = jnp.where(exclude == 0, -jnp.inf, vals[0]) if exclude is not None else vals[0]
        for j in range(1, eg):
            vj = jnp.where(exclude == j, -jnp.inf, vals[j]) if exclude is not None else vals[j]
            better = vj > bv
            bv = jnp.where(better, vj, bv)
            bi = jnp.where(better, j, bi)
        return bi

    i1 = argmax_first(sv, None)
    i2 = argmax_first(sv, i1)

    def take(vals, i):
        out = vals[0]
        for j in range(1, eg):
            out = jnp.where(i == j, vals[j], out)
        return out

    w1 = take(av, i1)
    w2 = take(av, i2)
    tot = w1 + w2
    idx_ref[0:1, :] = grp * eg + i1
    idx_ref[1:2, :] = grp * eg + i2
    w_ref[0:1, :] = w1 / tot
    w_ref[1:2, :] = w2 / tot


def route(logits_t, router_b, tile=1024):
    n_exp, t = logits_t.shape
    tile = math.gcd(tile, t)
    return pl.pallas_call(
        _route_kernel,
        grid=(t // tile,),
        in_specs=[pl.BlockSpec((n_exp, tile), lambda i: (0, i)),
                  pl.BlockSpec((n_exp, 1), lambda i: (0, 0))],
        out_specs=[pl.BlockSpec((TOP_K, tile), lambda i: (0, i)),
                   pl.BlockSpec((TOP_K, tile), lambda i: (0, i))],
        out_shape=[jax.ShapeDtypeStruct((TOP_K, t), jnp.int32), jax.ShapeDtypeStruct((TOP_K, t), F32)],
        compiler_params=_cparams("arbitrary"),
        name="moe_route",
    )(logits_t, router_b.reshape(n_exp, 1).astype(F32))


def moe_plan(idx, tm):
    t = idx.shape[1]
    n_pair = TOP_K * t
    n_tiles = (n_pair + N_EXPERTS * (tm - 1)) // tm
    e_flat = idx.reshape(-1)
    onehot = (e_flat[:, None] == jnp.arange(N_EXPERTS)[None, :]).astype(jnp.int32)
    running = jnp.cumsum(onehot, axis=0)
    counts = running[-1]
    rank = jnp.sum(onehot * running, 1) - 1
    tiles_per = (counts + tm - 1) // tm
    tile_end = jnp.cumsum(tiles_per)
    n_used = tile_end[-1]
    pstart = (tile_end - tiles_per) * tm
    pos = jnp.sum(onehot * pstart[None, :], 1) + rank
    tile_expert = jnp.minimum(jnp.sum((tile_end[None, :] <= jnp.arange(n_tiles)[:, None]).astype(jnp.int32), 1),
                              N_EXPERTS - 1).astype(jnp.int32)
    gidx = (jnp.arange(n_tiles * tm, dtype=jnp.int32) % t).at[pos].set(
        jnp.arange(n_pair, dtype=jnp.int32) % t, mode="promise_in_bounds", unique_indices=True)
    return gidx, tile_expert, n_used.reshape(1).astype(jnp.int32), pos.astype(jnp.int32)


def _expert_changed(te_ref, tile0):
    i = pl.program_id(0) + tile0
    return jnp.logical_or(pl.program_id(0) == 0, te_ref[i] != te_ref[jnp.maximum(i - 1, 0)])


def _ffn_up_kernel(te_ref, nu_ref, xs_ref, wg_ref, wu_ref, hid_ref, wg_bf, wu_bf, *, tile0):
    used = pl.program_id(0) + tile0 < nu_ref[0]

    @pl.when(jnp.logical_and(used, _expert_changed(te_ref, tile0)))
    def _():
        wg_bf[...] = wg_ref[0].astype(BF16)
        wu_bf[...] = wu_ref[0].astype(BF16)

    @pl.when(used)
    def _():
        xs = xs_ref[...]
        g = _dot(xs, wg_bf[...])
        u = _dot(xs, wu_bf[...])
        hid_ref[...] = ((g * jax.nn.sigmoid(g)) * u).astype(BF16)

    @pl.when(jnp.logical_not(used))
    def _():
        hid_ref[...] = jnp.zeros_like(hid_ref)


def _ffn_down_kernel(te_ref, nu_ref, hid_ref, wd_ref, *rest, tile0):
    o_ref, wd_bf = rest[-2], rest[-1]
    used = pl.program_id(0) + tile0 < nu_ref[0]

    @pl.when(jnp.logical_and(used, _expert_changed(te_ref, tile0)))
    def _():
        wd_bf[...] = wd_ref[0].astype(BF16)

    @pl.when(used)
    def _():
        o_ref[...] = _dot(hid_ref[...], wd_bf[...]).astype(o_ref.dtype)

    @pl.when(jnp.logical_not(used))
    def _():
        o_ref[...] = jnp.zeros_like(o_ref)


def grouped_ffn(xs, tile_expert, n_used, w_gate, w_up, w_down, layer, tm, tile0=0, p_total=None, ys_prev=None):
    p, d = xs.shape
    p_total = p if p_total is None else p_total
    de = w_gate.shape[3]
    n_tiles = p // tm
    wmap = lambda i, te, nu: (layer, te[i + tile0], 0, 0)
    row_in = lambda i, te, nu: (jnp.minimum(i, jnp.maximum(nu[0] - 1 - tile0, 0)), 0)
    hid = pl.pallas_call(
        functools.partial(_ffn_up_kernel, tile0=tile0),
        grid_spec=pltpu.PrefetchScalarGridSpec(
            num_scalar_prefetch=2,
            grid=(n_tiles,),
            in_specs=[pl.BlockSpec((tm, d), row_in),
                      pl.BlockSpec((None, 1, d, de), wmap),
                      pl.BlockSpec((None, 1, d, de), wmap)],
            out_specs=pl.BlockSpec((tm, de), lambda i, te, nu: (i, 0)),
            scratch_shapes=[pltpu.VMEM((d, de), BF16), pltpu.VMEM((d, de), BF16)]),
        out_shape=jax.ShapeDtypeStruct((p, de), BF16),
        compiler_params=_cparams("arbitrary"),
        name="moe_ffn_up",
    )(tile_expert, n_used, xs, w_gate, w_up)
    prev_specs, prev_args, alias = [], [], {}
    if ys_prev is not None:
        prev_specs, prev_args, alias = [pl.BlockSpec(memory_space=pl.ANY)], [ys_prev], {4: 0}
    return pl.pallas_call(
        functools.partial(_ffn_down_kernel, tile0=tile0),
        grid_spec=pltpu.PrefetchScalarGridSpec(
            num_scalar_prefetch=2,
            grid=(n_tiles,),
            in_specs=[pl.BlockSpec((tm, de), row_in),
                      pl.BlockSpec((None, 1, de, d), wmap)] + prev_specs,
            out_specs=pl.BlockSpec((tm, d), lambda i, te, nu: (i + tile0, 0)),
            scratch_shapes=[pltpu.VMEM((de, d), BF16)]),
        out_shape=jax.ShapeDtypeStruct((p_total, d), BF16),
        input_output_aliases=alias,
        compiler_params=_cparams("arbitrary"),
        name="moe_ffn_down",
    )(tile_expert, n_used, hid, w_down, *prev_args)


def _final_kernel(x_ref, y0_ref, y1_ref, w_ref, m5_ref, lg_ref, lb_ref, *rest, alpha):
    o_ref = rest[-1]
    w = w_ref[...]
    y = w[:, 0:1] * y0_ref[...].astype(F32) + w[:, 1:2] * y1_ref[...].astype(F32)
    z = alpha * x_ref[...] + m5_ref[...] * y
    mu = jnp.mean(z, -1, keepdims=True)
    zc = z - mu
    var = jnp.mean(zc * zc, -1, keepdims=True)
    o_ref[...] = zc * lax.rsqrt(var + LN_EPS) * lg_ref[...] + lb_ref[...]


def final_norm(x1, yg, wts, mods4, layer, ln_g, ln_b, alpha, seg_rows, n_batch, tile0=0, out_prev=None, tm=256):
    t, d = x1.shape
    n_tiles = yg.shape[0] // (TOP_K * tm)
    seg = _seg_fn(tm, seg_rows, n_batch)
    row = pl.BlockSpec((tm, d), lambda i: (i + tile0, 0))
    vec = pl.BlockSpec((1, d), lambda i: (0, 0))
    prev_specs, prev_args, alias = [], [], {}
    if out_prev is not None:
        prev_specs, prev_args, alias = [pl.BlockSpec(memory_space=pl.ANY)], [out_prev], {7: 0}
    return pl.pallas_call(
        functools.partial(_final_kernel, alpha=alpha),
        grid=(n_tiles,),
        in_specs=[row, pl.BlockSpec((tm, d), lambda i: (i, 0)), pl.BlockSpec((tm, d), lambda i: (i + n_tiles, 0)),
                  pl.BlockSpec((tm, TOP_K), lambda i: (i + tile0, 0)),
                  _mod_spec(d, layer, 5, lambda i: seg(i + tile0)), vec, vec] + prev_specs,
        out_specs=row,
        out_shape=jax.ShapeDtypeStruct((t, d), F32),
        input_output_aliases=alias,
        compiler_params=_cparams("arbitrary"),
        name="final_norm",
    )(x1, yg, yg, wts, mods4, ln_g.reshape(1, -1), ln_b.reshape(1, -1), *prev_args)


def moe_block(x1, h2, logits_t, router_b, w_gate, w_up, w_down, mods4, layer, ln_g, ln_b,
              alpha, seg_rows, n_batch, tm=512, tm_out=512):
    t = x1.shape[0]
    idx, wts = route(logits_t, router_b)
    gidx, tile_expert, n_used, pos = moe_plan(idx, tm)
    p_total = gidx.shape[0]

    def cuts(n_tiles, tile):
        return [(k * n_tiles // MOE_PARTS) * tile for k in range(MOE_PARTS + 1)]

    ys = None
    rows = cuts(p_total // tm, tm)
    for lo, hi in zip(rows[:-1], rows[1:]):
        xs = h2.at[gidx[lo:hi]].get(mode="promise_in_bounds")
        ys = grouped_ffn(xs, tile_expert, n_used, w_gate, w_up, w_down, layer, tm,
                         tile0=lo // tm, p_total=p_total, ys_prev=ys)
    pos2 = pos.reshape(TOP_K, t)
    out = None
    toks = cuts(t // tm_out, tm_out)
    for lo, hi in zip(toks[:-1], toks[1:]):
        yg = ys.at[pos2[:, lo:hi].reshape(-1)].get(mode="promise_in_bounds")
        out = final_norm(x1, yg, wts.T, mods4, layer, ln_g, ln_b, alpha, seg_rows, n_batch,
                         tile0=lo // tm_out, out_prev=out, tm=tm_out)
    return out


def s5_matrices(lam_re, lam_im, log_dt, b_re, b_im, c_re, c_im, d_skip):
    f32 = F32
    tc = S5_TC
    n_g, n_p = lam_re.shape[1], lam_re.shape[2]
    n_c = b_re.shape[-1]
    nb = 128 // n_c
    n_q = n_g // nb
    lr, li = lam_re.astype(f32), lam_im.astype(f32)
    dt = jnp.exp(log_dt.astype(f32))[..., None]

    def powers(jvals):
        j = jnp.asarray(np.asarray(jvals, np.float32))[:, None, None, None]
        mag = jnp.exp(lr * dt * j)
        return mag * jnp.cos(li * dt * j), mag * jnp.sin(li * dt * j)

    up = np.arange(tc)
    pw_re, pw_im = powers(np.arange(tc + 1))
    lb_re, lb_im = pw_re[1], pw_im[1]
    den = lr * lr + li * li
    fr = ((lb_re - 1.0) * lr + lb_im * li) / den
    fi = (lb_im * lr - (lb_re - 1.0) * li) / den
    br, bi = b_re.astype(f32), b_im.astype(f32)
    bb_re = fr[..., None] * br - fi[..., None] * bi
    bb_im = fr[..., None] * bi + fi[..., None] * br
    cr, ci = c_re.astype(f32), c_im.astype(f32)

    def times_b(p_re, p_im):
        return (p_re[..., None] * bb_re[None] - p_im[..., None] * bb_im[None],
                p_re[..., None] * bb_im[None] + p_im[..., None] * bb_re[None])

    e_re, e_im = times_b(pw_re, pw_im)
    kmat = jnp.sum(cr[None, :, :, :, :, None] * e_re[:tc, :, :, None, :, :]
                   - ci[None, :, :, :, :, None] * e_im[:tc, :, :, None, :, :], axis=4)
    def lag_slab(k_dir):
        return k_dir.reshape(tc, n_q, nb, n_c, n_c).transpose(1, 0, 4, 2, 3).reshape(n_q, tc, n_c, nb * n_c)
    skip = (jnp.eye(n_c, dtype=f32)[None, None, :, None, :]
            * d_skip.astype(f32).reshape(n_q, nb, n_c)[:, None, None, :, :]).reshape(n_q, 1, n_c, nb * n_c)
    k_c = jnp.concatenate([lag_slab(kmat[:tc, 0]), lag_slab(kmat[:tc, 1]), skip], 1)

    def w_slab(e):
        return e.reshape(tc, n_q, nb, n_p, n_c).transpose(1, 0, 4, 2, 3).reshape(n_q, tc, n_c, nb * n_p)
    ef_re, ef_im = times_b(*powers(tc - 1 - up))
    w_c = jnp.stack([w_slab(ef_re[:, 0]), w_slab(ef_im[:, 0]),
                     w_slab(e_re[:tc, 1]), w_slab(e_im[:tc, 1])], 2)

    def v_slabs(d, p_re, p_im):
        f_re = cr[d][None] * p_re[:, :, None, :] - ci[d][None] * p_im[:, :, None, :]
        f_im = cr[d][None] * p_im[:, :, None, :] + ci[d][None] * p_re[:, :, None, :]
        slab = lambda m: m.reshape(tc, n_q, nb, n_c, n_p).transpose(1, 0, 3, 2, 4).reshape(n_q, tc, n_c, nb * n_p)
        return slab(f_re), slab(-f_im)
    pb_re, pb_im = powers(tc - up)
    vt_c = jnp.stack(v_slabs(0, pw_re[1:, 0], pw_im[1:, 0]) + v_slabs(1, pb_re[:, 1], pb_im[:, 1]), 2)
    dec = lambda m: m.reshape(1, n_g * n_p // 128, 1, 128)
    decay = jnp.concatenate([dec(pw_re[tc, 0]), dec(pw_im[tc, 0]), dec(pw_re[tc, 1]), dec(pw_im[tc, 1])], 0)
    return k_c, w_c, vt_c, decay


def _s5_chunk_rows(ref, n):
    return jnp.concatenate([ref[pl.ds(s, n, stride=S5_TC), :] for s in range(S5_TC)], axis=1).astype(BF16)


def _s5_expand(slab, group_lanes):
    rows = 128
    tiled = jnp.concatenate([slab] * (rows // slab.shape[0]), axis=0)
    r = lax.broadcasted_iota(jnp.int32, tiled.shape, 0) // S5_CH
    l = lax.broadcasted_iota(jnp.int32, tiled.shape, 1) // group_lanes
    return jnp.where(r == l, tiled, 0.0).astype(BF16)


def _s5_in_kernel(ul_ref, uc_ref, wc_ref, fr_ref, fi_ref, br_ref, bi_ref, w_ref, *, n_lat, n_ctx, n_batch):
    b = pl.program_id(1)
    n_plane = wc_ref.shape[1]
    st = wc_ref.shape[3]

    @pl.when(b == 0)
    def _():
        for s in range(S5_TC):
            for i in range(n_plane):
                w_ref[s * 128:(s + 1) * 128, i * st:(i + 1) * st] = _s5_expand(wc_ref[s, i], S5_P)

    w_lat = _dot(_s5_chunk_rows(ul_ref, n_lat), w_ref[...])
    w_ctx = _dot(_s5_chunk_rows(uc_ref, n_ctx), w_ref[...])
    nv = fr_ref.shape[0]
    for i, ref in enumerate((fr_ref, fi_ref, br_ref, bi_ref)):
        for c in range(nv):
            lanes = slice((i * nv + c) * 128, (i * nv + c + 1) * 128)
            ref[c, pl.ds(b, n_ctx, stride=n_batch), :] = w_ctx[:, lanes]
            ref[c, pl.ds(n_ctx * n_batch + b, n_lat, stride=n_batch), :] = w_lat[:, lanes]
            ref[c, pl.ds((n_ctx + n_lat) * n_batch + b, n_ctx, stride=n_batch), :] = w_ctx[:, lanes]


def _s5_scan_kernel(wfr, wfi, wbr, wbi, dec_ref, xfr, xfi, xbr, xbi, *, n_tiles):
    nv = wfr.shape[0]
    low = lax.broadcasted_iota(jnp.int32, (nv, 8, 128), 1) < 4
    a_fr, a_fi, a_br, a_bi = dec_ref[0], dec_ref[1], dec_ref[2], dec_ref[3]

    def half_step(s_re, s_im, a_re, a_im, w_re, w_im):
        return a_re * s_re - a_im * s_im + w_re, a_re * s_im + a_im * s_re + w_im

    def one_dir(w_re_ref, w_im_ref, x_re_ref, x_im_ref, row0, s_re, s_im, a_re, a_im, first_low):
        first = low if first_low else jnp.logical_not(low)
        wt_re, wt_im = w_re_ref[:, pl.ds(row0, 8), :], w_im_ref[:, pl.ds(row0, 8), :]
        wr_re, wr_im = pltpu.roll(wt_re, 4, 1), pltpu.roll(wt_im, 4, 1)
        mid_re, mid_im = half_step(s_re, s_im, a_re, a_im, wr_re, wr_im)
        x_re_ref[:, pl.ds(row0, 8), :] = jnp.where(first, s_re, mid_re)
        x_im_ref[:, pl.ds(row0, 8), :] = jnp.where(first, s_im, mid_im)
        m_re = jnp.where(first, pltpu.roll(mid_re, 4, 1), mid_re)
        m_im = jnp.where(first, pltpu.roll(mid_im, 4, 1), mid_im)
        w2_re = jnp.where(first, wr_re, wt_re)
        w2_im = jnp.where(first, wr_im, wt_im)
        return half_step(m_re, m_im, a_re, a_im, w2_re, w2_im)

    def body(i, carry):
        f_re, f_im, b_re, b_im = carry
        rf = pl.multiple_of(i * 8, 8)
        rb = pl.multiple_of((n_tiles - 1 - i) * 8, 8)
        f_re, f_im = one_dir(wfr, wfi, xfr, xfi, rf, f_re, f_im, a_fr, a_fi, True)
        b_re, b_im = one_dir(wbr, wbi, xbr, xbi, rb, b_re, b_im, a_br, a_bi, False)
        return f_re, f_im, b_re, b_im

    z = jnp.zeros((nv, 8, 128), F32)
    lax.fori_loop(0, n_tiles, body, (z, z, z, z))


def _s5_out_kernel(ul_ref, fr_ref, fi_ref, br_ref, bi_ref, kc_ref, vc_ref, y_ref, mt_ref, vt_ref,
                   *, n_lat, n_ctx, n_batch):
    b = pl.program_id(1)
    tc = S5_TC

    @pl.when(b == 0)
    def _():
        lag = [_s5_expand(kc_ref[j], S5_CH) for j in range(2 * tc)]
        diag = _s5_expand(kc_ref[0] + kc_ref[tc] + kc_ref[2 * tc], S5_CH)
        for s in range(tc):
            for t in range(tc):
                blk = diag if s == t else (lag[t - s] if t > s else lag[tc + s - t])
                mt_ref[s * 128:(s + 1) * 128, t * 128:(t + 1) * 128] = blk
        for t in range(tc):
            for i in range(vt_ref.shape[0]):
                vt_ref[i, t * 128:(t + 1) * 128, :] = _s5_expand(vc_ref[t, i], S5_P)

    y = _dot(_s5_chunk_rows(ul_ref, n_lat), mt_ref[...])
    row0 = n_ctx * n_batch + b
    for i, ref in enumerate((fr_ref, fi_ref, br_ref, bi_ref)):
        xs = jnp.concatenate([ref[c, pl.ds(row0, n_lat, stride=n_batch), :] for c in range(ref.shape[0])], 1)
        y = y + _dot_nt(xs.astype(BF16), vt_ref[i])
    for s in range(S5_TC):
        y_ref[pl.ds(s, n_lat, stride=S5_TC), :] = y[:, s * 128:(s + 1) * 128]


def s5_bidir(u, mats, n_batch, l_lat, l_ctx):
    assert n_batch == 4, "the chunk scan packs two chunks of 4 batch rows per 8-sublane tile"
    k_c, w_c, vt_c, decay = mats
    tc = S5_TC
    wd = u.shape[1]
    n_q = wd // 128
    lane_q = tc * 128
    st_q = (128 // S5_CH) * S5_P
    n_lat, n_ctx = l_lat // tc, l_ctx // tc
    nk = n_lat + 2 * n_ctx
    assert nk % 2 == 0
    rows = nk * n_batch
    ctx0 = (n_batch * l_lat) // l_ctx
    dims = dict(n_lat=n_lat, n_ctx=n_ctx, n_batch=n_batch)

    nv = st_q // 128
    plane = jax.ShapeDtypeStruct((n_q * nv, rows, 128), F32)
    plane_spec = pl.BlockSpec((nv, rows, 128), lambda q, b: (q, 0, 0))
    ul_spec = pl.BlockSpec((l_lat, 128), lambda q, b: (b, q))
    uc_spec = pl.BlockSpec((l_ctx, 128), lambda q, b: (ctx0 + b, q))
    w_planes = pl.pallas_call(
        functools.partial(_s5_in_kernel, **dims),
        grid=(n_q, n_batch),
        in_specs=[ul_spec, uc_spec, pl.BlockSpec((None,) + w_c.shape[1:], lambda q, b: (q, 0, 0, 0, 0))],
        out_specs=[plane_spec] * 4,
        out_shape=[plane] * 4,
        scratch_shapes=[pltpu.VMEM((lane_q, 4 * st_q), BF16)],
        compiler_params=_cparams("arbitrary", "arbitrary"),
        name="s5_chunk_in",
    )(u, u, w_c)

    blk = pl.BlockSpec((nv, rows, 128), lambda j: (j, 0, 0))
    x_planes = pl.pallas_call(
        functools.partial(_s5_scan_kernel, n_tiles=rows // 8),
        grid=(n_q,),
        in_specs=[blk] * 4 + [pl.BlockSpec((4, nv, 1, 128), lambda j: (0, j, 0, 0))],
        out_specs=[blk] * 4,
        out_shape=[plane] * 4,
        compiler_params=_cparams("arbitrary"),
        name="s5_chunk_scan",
    )(*w_planes, decay)

    return pl.pallas_call(
        functools.partial(_s5_out_kernel, **dims),
        grid=(n_q, n_batch),
        in_specs=[ul_spec] + [plane_spec] * 4
                 + [pl.BlockSpec((None,) + k_c.shape[1:], lambda q, b: (q, 0, 0, 0)),
                    pl.BlockSpec((None,) + vt_c.shape[1:], lambda q, b: (q, 0, 0, 0, 0))],
        out_specs=pl.BlockSpec((l_lat, 128), lambda q, b: (b, q)),
        out_shape=jax.ShapeDtypeStruct((n_batch * l_lat, wd), F32),
        scratch_shapes=[pltpu.VMEM((lane_q, lane_q), BF16), pltpu.VMEM((4, lane_q, st_q), BF16)],
        compiler_params=_cparams("arbitrary", "arbitrary"),
        name="s5_chunk_out",
    )(u, *x_planes, k_c, vt_c)


def kernel(x, c, ctx, c_ctx, ada_w, ada_b, ln_mix_g, ln_mix_b, ln_ffn_g, ln_ffn_b, ev_w_in, ev_gate_w2,
           ev_gate_b, ev_rpb, ev_norm_g, ev_w_out, od_w_in, od_lam_re, od_lam_im, od_log_dt, od_b_re,
           od_b_im, od_c_re, od_c_im, od_d, od_w_glu, od_b_glu, od_w_out, router_w, router_b,
           moe_w_gate, moe_w_up, moe_w_down):
    n_batch, l_lat, d = x.shape
    l_ctx = ctx.shape[1]
    depth = ada_w.shape[0]
    assert depth == 2, "one even (NA + GLA) layer followed by one odd (S5) layer"
    alpha = (2.0 * depth) ** 0.25
    n_lat = n_batch * l_lat

    cvec = jnp.concatenate([c, c_ctx[None], jnp.zeros((8 - n_batch - 1, d), F32)], 0)
    mods = compute_mods(cvec, ada_w, ada_b)
    mods4 = mods.reshape(depth, 8, 1, N_MOD * d)
    x_lat, x_ctx = x.reshape(n_lat, d), ctx.reshape(n_batch * l_ctx, d)
    router_wt = router_w.T.astype(F32)

    na_w = NA_HEADS * NA_DH
    wk = GLA_HEADS * GLA_DK
    wv = GLA_HEADS * GLA_DV
    ev_in = ev_w_in.shape[2]
    pad = (-ev_in) % 256
    w_in = jnp.pad(ev_w_in[0], ((0, 0), (0, pad))).astype(BF16)
    proj = mod_matmul(x_lat, x_ctx, mods4, 0, w_in, l_lat, n_batch, tm=512, tn=(ev_in + pad) // 2)
    a_lat, a_ctx = na_attention(proj, na_bias_table(ev_rpb[0], l_lat // GRID_W), n_batch, l_lat, l_ctx)
    col_lr = (3 * na_w + 2 * wk + 2 * wv) // 128
    g2 = jnp.zeros((2, 128, wk), F32)
    g2 = g2.at[0, 0:GLA_RANK].set(ev_gate_w2[0, 0]).at[1, GLA_RANK:2 * GLA_RANK].set(ev_gate_w2[0, 1])
    o_f, o_b = gla_bidir(proj, g2.astype(BF16), ev_gate_b[0].reshape(2, 1, wk), rope_tables(l_lat),
                         n_batch, l_lat, l_ctx,
                         col_q=3 * na_w // wk, col_k=(3 * na_w + wk) // wk,
                         col_v=(3 * na_w + 2 * wk) // wv, col_lr=col_lr)
    x1, h2, logits_t = even_out(a_lat, a_ctx, o_f, o_b, proj, (3 * na_w + 2 * wk + wv) // wv, x_lat, x_ctx,
                                mods4, 0, ev_norm_g[0], ev_w_out[0].astype(BF16), ln_mix_g[0], ln_mix_b[0],
                                router_wt, alpha, l_lat, n_batch)
    w_gate, w_up, w_down = moe_w_gate, moe_w_up, moe_w_down
    rows = moe_block(x1, h2, logits_t, router_b, w_gate, w_up, w_down, mods4, 0,
                     ln_ffn_g[0], ln_ffn_b[0], alpha, l_lat, n_batch)

    u = mod_matmul(rows, None, mods4, 1, od_w_in[0].astype(BF16), l_lat, n_batch, tm=512)
    mats = s5_matrices(od_lam_re[0], od_lam_im[0], od_log_dt[0], od_b_re[0], od_b_im[0],
                       od_c_re[0], od_c_im[0], od_d[0])
    y5 = s5_bidir(u, mats, n_batch, l_lat, l_ctx)
    x1, h2, logits_t = odd_out(y5, rows, mods4, 1, od_w_glu[0].astype(BF16), od_b_glu[0],
                               od_w_out[0].astype(BF16), ln_mix_g[1], ln_mix_b[1], router_wt,
                               alpha, l_lat, n_batch)
    out = moe_block(x1, h2, logits_t, router_b, w_gate, w_up, w_down, mods4, 1,
                    ln_ffn_g[1], ln_ffn_b[1], alpha, l_lat, n_batch)
    return out.reshape(n_batch, l_lat, d)
```

```python
import functools
import math

import numpy as np
import jax
import jax.numpy as jnp
from jax import lax
from jax.experimental import pallas as pl
from jax.experimental.pallas import tpu as pltpu

F32 = jnp.float32
BF16 = jnp.bfloat16
HIGHEST = lax.Precision.HIGHEST

N_MOD = 6
LN_EPS = 1e-5
NORM_EPS = 1e-6

GRID_W = 64
NA_HEADS = 8
NA_DH = 128
NA_KR = 8
NA_KC = 16

GLA_HEADS = 4
GLA_DK = 128
GLA_DV = 256
GLA_RANK = 16
GLA_TAU = 16.0
GLA_CHUNK = 64
ROPE_BASE = 10000.0

S5_CH = 16
S5_P = 64
S5_TC = 16

N_EXPERTS = 16
N_GROUPS = 4
TOP_K = 2
MOE_PARTS = 2

VMEM_LIMIT = 56 * 1024 * 1024
NEG_BIG = -1e30


def _cparams(*sem):
    return pltpu.CompilerParams(dimension_semantics=sem, vmem_limit_bytes=VMEM_LIMIT)


def _dot(a, b):
    return jnp.dot(a, b, preferred_element_type=F32)


def _dot_nt(a, b):
    return lax.dot_general(a, b, (((1,), (1,)), ((), ())), preferred_element_type=F32)


def _dot_tn(a, b):
    return lax.dot_general(a, b, (((0,), (0,)), ((), ())), preferred_element_type=F32)


def _mods_kernel(s_ref, w_ref, b_ref, o_ref):
    s = s_ref[...]
    s = s * jax.nn.sigmoid(s)
    w = w_ref[0]
    s_hi, w_hi = s.astype(BF16), w.astype(BF16)
    s_lo = (s - s_hi.astype(F32)).astype(BF16)
    w_lo = (w - w_hi.astype(F32)).astype(BF16)
    o_ref[0] = _dot(s_hi, w_hi) + _dot(s_lo, w_hi) + _dot(s_hi, w_lo) + b_ref[0]


def compute_mods(cvec, ada_w, ada_b, tn=1024):
    n_layer, d, n = ada_w.shape
    tn = math.gcd(tn, n)
    return pl.pallas_call(
        _mods_kernel,
        grid=(n_layer, n // tn),
        in_specs=[pl.BlockSpec((8, d), lambda l, j: (0, 0)),
                  pl.BlockSpec((1, d, tn), lambda l, j: (l, 0, j)),
                  pl.BlockSpec((1, 1, tn), lambda l, j: (l, 0, j))],
        out_specs=pl.BlockSpec((1, 8, tn), lambda l, j: (l, 0, j)),
        out_shape=jax.ShapeDtypeStruct((n_layer, 8, n), F32),
        compiler_params=_cparams("arbitrary", "arbitrary"),
        name="ada_mods",
    )(cvec, ada_w, ada_b.reshape(n_layer, 1, n))


def _mod_spec(d, layer, which, seg_of_tile):
    return pl.BlockSpec((None, None, 1, d), lambda i, *_: (layer, seg_of_tile(i), 0, which))


def _seg_fn(tm, seg_rows, n_batch):
    return lambda i: jnp.minimum((i * tm) // seg_rows, n_batch)


def _two_source_specs(lat, ctx, tm, tile_of):
    n_lat_tiles = lat.shape[0] // tm
    d = lat.shape[1]
    return [pl.BlockSpec((tm, d), lambda *g: (jnp.minimum(tile_of(*g), n_lat_tiles - 1), 0)),
            pl.BlockSpec((tm, d), lambda *g: (jnp.maximum(tile_of(*g) - n_lat_tiles, 0), 0))]


def _pick_rows(lat_ref, ctx_ref, tile, n_lat_tiles):
    return jnp.where(tile < n_lat_tiles, lat_ref[...], ctx_ref[...])


def _modmm_kernel(xl_ref, xc_ref, s1_ref, s0_ref, w_ref, o_ref, *, n_lat_tiles):
    x = _pick_rows(xl_ref, xc_ref, pl.program_id(1), n_lat_tiles)
    h = x * (1.0 + s1_ref[...]) + s0_ref[...]
    o_ref[...] = _dot(h.astype(BF16), w_ref[...])


def mod_matmul(x_lat, x_ctx, mods4, layer, w_bf16, seg_rows, n_batch, tm=256, tn=None):
    if x_ctx is None:
        t, x_ctx = x_lat.shape[0], x_lat
    else:
        t = x_lat.shape[0] + x_ctx.shape[0]
    d = x_lat.shape[1]
    n = w_bf16.shape[1]
    tn = n if tn is None else tn
    seg = _seg_fn(tm, seg_rows, n_batch)
    return pl.pallas_call(
        functools.partial(_modmm_kernel, n_lat_tiles=x_lat.shape[0] // tm),
        grid=(n // tn, t // tm),
        in_specs=_two_source_specs(x_lat, x_ctx, tm, lambda j, i: i) + [
            pl.BlockSpec((None, None, 1, d), lambda j, i: (layer, seg(i), 0, 1)),
            pl.BlockSpec((None, None, 1, d), lambda j, i: (layer, seg(i), 0, 0)),
            pl.BlockSpec((d, tn), lambda j, i: (0, j))],
        out_specs=pl.BlockSpec((tm, tn), lambda j, i: (i, j)),
        out_shape=jax.ShapeDtypeStruct((t, n), F32),
        compiler_params=_cparams("arbitrary", "arbitrary"),
        name="mod_matmul",
    )(x_lat, x_ctx, mods4, mods4, w_bf16)


NA_RB = 4
NA_BAND = NA_RB + NA_KR - 1


def _na_row_start(r, rows):
    return min(max(r - NA_KR // 2, 0), rows - NA_KR)


def na_bias_table(rpb, rows):
    w = GRID_W
    q = np.arange(w)
    kc = np.arange(w)
    win0 = np.clip(q - NA_KC // 2, 0, w - NA_KC)
    ok = (kc[None, :] >= win0[:, None]) & (kc[None, :] < win0[:, None] + NA_KC)
    dc = np.clip(kc[None, :] - q[:, None] + NA_KC - 1, 0, 2 * NA_KC - 2)
    pick = ((dc[None] == np.arange(2 * NA_KC - 1)[:, None, None]) & ok[None]).astype(np.float32)
    colb = jnp.einsum("hrd,dqk->hrqk", rpb.astype(F32), jnp.asarray(pick), precision=HIGHEST)
    colb = jnp.where(ok[None, None], colb, NEG_BIG)
    neg = jnp.full((rpb.shape[0], w, w), NEG_BIG, F32)

    def block(r0):
        band0 = min(max(r0 - NA_KR // 2, 0), rows - NA_BAND)
        out = []
        for r in range(r0, r0 + NA_RB):
            rs = _na_row_start(r, rows)
            first = rs - r + NA_KR - 1
            cols = [neg] * (rs - band0) + [colb[:, first + j] for j in range(NA_KR)]
            cols += [neg] * (NA_BAND - len(cols))
            out.append(jnp.concatenate(cols, -1))
        return jnp.concatenate(out, 1)

    return jnp.stack([block(0), block(NA_RB), block(rows - NA_RB)], 1)


def _na_kernel(q_ref, k_ref, v_ref, qc_ref, kc_ref, vc_ref, bias_ref, o_ref, oc_ref, kbf, vbf, *, rows):
    w = GRID_W
    n_blk = rows // NA_RB
    scale = NA_DH ** -0.5
    kbf[...] = k_ref[...].astype(BF16)
    vbf[...] = v_ref[...].astype(BF16)
    kc = kc_ref[...].astype(BF16)
    vc = vc_ref[...].astype(BF16)

    def body(pair, carry):
        blocks = tuple(4 * pair + k for k in range(4))
        q0, scores = [], []
        for i in blocks:
            r0 = i * NA_RB
            band0 = jnp.clip(r0 - NA_KR // 2, 0, rows - NA_BAND)
            variant = jnp.where(i == 0, 0, jnp.where(i == n_blk - 1, 2, 1))
            q0.append(pl.multiple_of(r0 * w, NA_RB * w))
            k0 = pl.multiple_of(band0 * w, w)
            q = (q_ref[pl.ds(q0[-1], NA_RB * w), :] * scale).astype(BF16)
            s_loc = _dot_nt(q, kbf[pl.ds(k0, NA_BAND * w), :]) + bias_ref[variant]
            scores.append((s_loc, _dot_nt(q, kc), k0))
        probs = []
        for s_loc, s_ctx, k0 in scores:
            m = jnp.maximum(jnp.max(s_loc, -1, keepdims=True), jnp.max(s_ctx, -1, keepdims=True))
            p_loc = jnp.exp(s_loc - m)
            p_ctx = jnp.exp(s_ctx - m)
            den = jnp.sum(p_loc, -1, keepdims=True) + jnp.sum(p_ctx, -1, keepdims=True)
            probs.append((p_loc.astype(BF16), p_ctx.astype(BF16), den, k0))
        for q_start, (p_loc, p_ctx, den, k0) in zip(q0, probs):
            o = _dot(p_loc, vbf[pl.ds(k0, NA_BAND * w), :]) + _dot(p_ctx, vc)
            o_ref[pl.ds(q_start, NA_RB * w), :] = o / den
        return carry

    lax.fori_loop(0, n_blk // 4, body, 0)

    qc = (qc_ref[...] * scale).astype(BF16)
    s = _dot_nt(qc, kc)
    p = jnp.exp(s - jnp.max(s, -1, keepdims=True))
    oc_ref[...] = _dot(p.astype(BF16), vc) / jnp.sum(p, -1, keepdims=True)


def na_attention(proj, bias_tab, n_batch, l_lat, l_ctx):
    h = NA_HEADS
    dh = NA_DH
    rows = l_lat // GRID_W
    ctx0 = (n_batch * l_lat) // l_ctx
    return pl.pallas_call(
        functools.partial(_na_kernel, rows=rows),
        grid=(n_batch, h),
        in_specs=[pl.BlockSpec((l_lat, dh), lambda b, hh: (b, hh)),
                  pl.BlockSpec((l_lat, dh), lambda b, hh: (b, h + hh)),
                  pl.BlockSpec((l_lat, dh), lambda b, hh: (b, 2 * h + hh)),
                  pl.BlockSpec((l_ctx, dh), lambda b, hh: (ctx0 + b, hh)),
                  pl.BlockSpec((l_ctx, dh), lambda b, hh: (ctx0 + b, h + hh)),
                  pl.BlockSpec((l_ctx, dh), lambda b, hh: (ctx0 + b, 2 * h + hh)),
                  pl.BlockSpec((None,) + bias_tab.shape[1:], lambda b, hh: (hh, 0, 0, 0))],
        out_specs=[pl.BlockSpec((l_lat, dh), lambda b, hh: (b, hh)),
                   pl.BlockSpec((l_ctx, dh), lambda b, hh: (b, hh))],
        out_shape=[jax.ShapeDtypeStruct((n_batch * l_lat, h * dh), F32),
                   jax.ShapeDtypeStruct((n_batch * l_ctx, h * dh), F32)],
        scratch_shapes=[pltpu.VMEM((l_lat, dh), BF16), pltpu.VMEM((l_lat, dh), BF16)],
        compiler_params=_cparams("arbitrary", "arbitrary"),
        name="na_attention",
    )(proj, proj, proj, proj, proj, proj, bias_tab)


def rope_tables(l_lat):
    half = GLA_DK // 2
    nf = half // 2
    inv = ROPE_BASE ** (-np.arange(nf, dtype=np.float64) / nf)
    t = np.arange(l_lat)
    lane = np.arange(GLA_DK)
    pos = np.where(lane[None, :] < half, (t // GRID_W)[:, None], (t % GRID_W)[:, None]).astype(np.float64)
    ang = pos * inv[lane % nf][None, :]
    first = (lane % half) < nf
    cos = np.cos(ang)
    sin_a = np.where(first[None, :], -np.sin(ang), 0.0)
    sin_b = np.where(first[None, :], 0.0, np.sin(ang))
    return jnp.asarray(cos, F32), jnp.asarray(sin_a, F32), jnp.asarray(sin_b, F32)


GLA_PREP_CHUNKS = 8
GLA_SCAN_CHUNKS = 4


def _gla_prep_kernel(q_ref, k_ref, lr_ref, cos_ref, sa_ref, sb_ref, g2_ref, gb_ref,
                     qdf, kdf, krf, elf, qdb, kdb, krb, elb, *, n_lat_tiles):
    c = GLA_CHUNK
    nch = GLA_PREP_CHUNKS
    nf = GLA_DK // 4
    gscale = GLA_DK ** -0.5
    wk = GLA_HEADS * GLA_DK
    is_lat = pl.program_id(0) < n_lat_tiles
    cos = jnp.where(is_lat, cos_ref[...], 1.0)
    sa = jnp.where(is_lat, sa_ref[...], 0.0)
    sb = jnp.where(is_lat, sb_ref[...], 0.0)

    def rope(x):
        return x * cos + pltpu.roll(x, GLA_DK - nf, 1) * sa + pltpu.roll(x, nf, 1) * sb

    qs, ks_ = [], []
    for h in range(GLA_HEADS):
        hs = slice(h * GLA_DK, (h + 1) * GLA_DK)
        qs.append(rope(q_ref[:, hs]) * gscale)
        ks_.append(rope(k_ref[:, hs]))

    lr = lr_ref[...].astype(BF16)
    row = lax.broadcasted_iota(jnp.int32, (c, c), 0)
    col = lax.broadcasted_iota(jnp.int32, (c, c), 1)
    for d, (qd, kd, kr, el) in enumerate(((qdf, kdf, krf, elf), (qdb, kdb, krb, elb))):
        reverse = d == 1
        tri = ((col >= row) if reverse else (col <= row)).astype(BF16)
        z = _dot(lr, g2_ref[d]) + gb_ref[d]
        g = (jnp.minimum(z, 0.0) - jnp.log1p(jnp.exp(-jnp.abs(z)))) * (1.0 / GLA_TAU)
        g_hi = g.astype(BF16)
        r1 = g - g_hi.astype(F32)
        g_mid = r1.astype(BF16)
        g_lo = (r1 - g_mid.astype(F32)).astype(BF16)
        parts = []
        for ci in range(nch):
            rs = slice(ci * c, (ci + 1) * c)
            parts.append(_dot(tri, g_hi[rs]) + _dot(tri, g_mid[rs]) + _dot(tri, g_lo[rs]))
        b3 = jnp.concatenate(parts, 0).reshape(nch, c, wk)
        bl3 = b3[:, 0:1, :] if reverse else b3[:, c - 1:c, :]
        el[...] = jnp.exp(bl3)
        e_b = jnp.exp(b3).reshape(nch * c, wk)
        e_nb = jnp.exp(-b3).reshape(nch * c, wk)
        e_rem = jnp.exp(bl3 - b3).reshape(nch * c, wk)
        for h in range(GLA_HEADS):
            hs = slice(h * GLA_DK, (h + 1) * GLA_DK)
            qd[:, hs] = (qs[h] * e_b[:, hs]).astype(BF16)
            kd[:, hs] = (ks_[h] * e_nb[:, hs]).astype(BF16)
            kr[:, hs] = (ks_[h] * e_rem[:, hs]).astype(BF16)


def _gla_scan_kernel(qdf, kdf, krf, elf, vf, qdb, kdb, krb, elb, vb, of_ref, ob_ref, st_ref):
    @pl.when(pl.program_id(1) == 0)
    def _():
        st_ref[...] = jnp.zeros_like(st_ref)

    c = GLA_CHUNK
    row = lax.broadcasted_iota(jnp.int32, (c, c), 0)
    col = lax.broadcasted_iota(jnp.int32, (c, c), 1)
    nsc = GLA_SCAN_CHUNKS
    up, down = tuple(range(nsc)), tuple(reversed(range(nsc)))
    dirs = ((qdf, kdf, krf, elf, vf, of_ref, col <= row, up), (qdb, kdb, krb, elb, vb, ob_ref, col >= row, down))
    chains = [(d, h) for d in range(2) for h in range(GLA_HEADS)]
    hs = lambda h: slice(h * GLA_DK, (h + 1) * GLA_DK)
    vs = lambda h: slice(h * GLA_DV, (h + 1) * GLA_DV)
    rows = lambda k: slice(k * c, (k + 1) * c)
    order = lambda d, step: rows(dirs[d][7][step])
    q_dec = [[dirs[d][0][order(d, s), hs(h)] for d, h in chains] for s in range(nsc)]
    v_bf = [[dirs[d][4][order(d, s), vs(h)].astype(BF16) for d, h in chains] for s in range(nsc)]
    att = [[jnp.where(dirs[d][6], _dot_nt(q_dec[s][n], dirs[d][1][order(d, s), hs(h)]), 0.0).astype(BF16)
            for n, (d, h) in enumerate(chains)] for s in range(nsc)]
    state = [st_ref[d, h] for d, h in chains]
    for s in range(nsc):
        for n, (d, h) in enumerate(chains):
            dirs[d][5][order(d, s), vs(h)] = (_dot(att[s][n], v_bf[s][n])
                                              + _dot_nt(q_dec[s][n], state[n].astype(BF16)))
        state = [state[n] * dirs[d][3][dirs[d][7][s], :, hs(h)]
                 + _dot_tn(v_bf[s][n], dirs[d][2][order(d, s), hs(h)]) for n, (d, h) in enumerate(chains)]
    for n, (d, h) in enumerate(chains):
        st_ref[d, h] = state[n]


def gla_bidir(proj, g2, gb, tables, n_batch, l_lat, l_ctx, col_q, col_k, col_v, col_lr):
    c = GLA_CHUNK
    nc = l_ctx // c
    nl = l_lat // c
    nz = nl + 2 * nc
    steps = nl + nc
    wk = GLA_HEADS * GLA_DK
    wv = GLA_HEADS * GLA_DV
    t_rows = n_batch * (l_lat + l_ctx)

    tp = GLA_PREP_CHUNKS * c
    lat_tiles = l_lat // tp
    n_lat_tiles = n_batch * lat_tiles
    cos, sa, sb = tables
    tab = pl.BlockSpec((tp, GLA_DK), lambda i: (jnp.where(i < n_lat_tiles, i % lat_tiles, 0), 0))
    row_bf = jax.ShapeDtypeStruct((t_rows, wk), BF16)
    last = jax.ShapeDtypeStruct((t_rows // c, 1, wk), F32)
    row_spec = pl.BlockSpec((tp, wk), lambda i: (i, 0))
    last_spec = pl.BlockSpec((GLA_PREP_CHUNKS, 1, wk), lambda i: (i, 0, 0))
    prep = pl.pallas_call(
        functools.partial(_gla_prep_kernel, n_lat_tiles=n_lat_tiles),
        grid=(t_rows // tp,),
        in_specs=[pl.BlockSpec((tp, wk), lambda i: (i, col_q)),
                  pl.BlockSpec((tp, wk), lambda i: (i, col_k)),
                  pl.BlockSpec((tp, 128), lambda i: (i, col_lr)),
                  tab, tab, tab,
                  pl.BlockSpec((2, 128, wk), lambda i: (0, 0, 0)),
                  pl.BlockSpec((2, 1, wk), lambda i: (0, 0, 0))],
        out_specs=[row_spec, row_spec, row_spec, last_spec] * 2,
        out_shape=[row_bf, row_bf, row_bf, last] * 2,
        compiler_params=_cparams("arbitrary"),
        name="gla_prep",
    )(proj, proj, proj, cos, sa, sb, g2, gb)

    def zblk(b, j):
        lat = b * nl + (j - nc)
        ctx = n_batch * nl + b * nc + jnp.where(j < nc, j, j - nc - nl)
        return jnp.where((j >= nc) & (j < nc + nl), lat, ctx)

    nsc = GLA_SCAN_CHUNKS
    assert nc % nsc == 0 and nl % nsc == 0
    fwd = lambda b, i: zblk(b, nsc * i) // nsc
    bwd = lambda b, i: zblk(b, nz - nsc - nsc * i) // nsc
    c2 = nsc * c

    def dir_specs(blk):
        return [pl.BlockSpec((c2, wk), lambda b, i: (blk(b, i), 0)),
                pl.BlockSpec((c2, wk), lambda b, i: (blk(b, i), 0)),
                pl.BlockSpec((c2, wk), lambda b, i: (blk(b, i), 0)),
                pl.BlockSpec((nsc, 1, wk), lambda b, i: (blk(b, i), 0, 0)),
                pl.BlockSpec((c2, wv), lambda b, i: (blk(b, i), col_v))]

    return pl.pallas_call(
        _gla_scan_kernel,
        grid=(n_batch, steps // nsc),
        in_specs=dir_specs(fwd) + dir_specs(bwd),
        out_specs=[pl.BlockSpec((c2, wv), lambda b, i: (fwd(b, i), 0)),
                   pl.BlockSpec((c2, wv), lambda b, i: (bwd(b, i), 0))],
        out_shape=[jax.ShapeDtypeStruct((t_rows, wv), F32), jax.ShapeDtypeStruct((t_rows, wv), F32)],
        scratch_shapes=[pltpu.VMEM((2, GLA_HEADS, GLA_DV, GLA_DK), F32)],
        compiler_params=_cparams("arbitrary", "arbitrary"),
        name="gla_scan",
    )(*prep[0:4], proj, *prep[4:8], proj)


def _post_mix(out, x, m2_ref, m3_ref, m4_ref, lg_ref, lb_ref, wr_ref, alpha, x1_ref, h2_ref, lt_ref, rows):
    y = alpha * x + m2_ref[...] * out
    mu = jnp.mean(y, -1, keepdims=True)
    yc = y - mu
    var = jnp.mean(yc * yc, -1, keepdims=True)
    x1 = yc * lax.rsqrt(var + LN_EPS) * lg_ref[...] + lb_ref[...]
    h2 = x1 * (1.0 + m4_ref[...]) + m3_ref[...]
    x1_ref[rows, :] = x1
    n_exp = lt_ref.shape[0]
    h2_hi = h2.astype(BF16)
    h2_lo = (h2 - h2_hi.astype(F32)).astype(BF16)
    h2_ref[rows, :] = h2_hi
    wr = wr_ref[...]
    wr_hi = wr.astype(BF16)
    wr_lo = (wr - wr_hi.astype(F32)).astype(BF16)
    both = _dot_nt(jnp.concatenate([wr_hi, wr_lo], 0), h2_hi)
    lt_ref[:, rows] = both[:n_exp] + both[n_exp:] + _dot_nt(wr_hi, h2_lo)


def _row_halves(n):
    return (slice(0, n // 2), slice(n // 2, n))


def _even_out_kernel(al_ref, ac_ref, of_ref, ob_ref, r_ref, xl_ref, xc_ref, m2_ref, m3_ref, m4_ref, ng_ref,
                     wo_ref, lg_ref, lb_ref, wr_ref, x1_ref, h2_ref, lt_ref, *, alpha, n_lat_tiles):
    is_lat = pl.program_id(0) < n_lat_tiles
    lhs = []
    for rows in _row_halves(of_ref.shape[0]):
        o = of_ref[rows, :] + ob_ref[rows, :]
        r = r_ref[rows, :]
        gate = r * jax.nn.sigmoid(r)
        mixed = [jnp.where(is_lat, al_ref[rows, :], ac_ref[rows, :]).astype(BF16)]
        for h in range(GLA_HEADS):
            vs = slice(h * GLA_DV, (h + 1) * GLA_DV)
            oh = o[:, vs]
            nrm = oh * lax.rsqrt(jnp.mean(oh * oh, -1, keepdims=True) + NORM_EPS) * ng_ref[...]
            mixed.append((nrm * gate[:, vs]).astype(BF16))
        lhs.append(jnp.concatenate(mixed, axis=1))
    outs = [_dot(m, wo_ref[...]) for m in lhs]
    x = jnp.where(is_lat, xl_ref[...], xc_ref[...])
    _post_mix(jnp.concatenate(outs, 0), x, m2_ref, m3_ref, m4_ref, lg_ref, lb_ref, wr_ref, alpha,
              x1_ref, h2_ref, lt_ref, slice(None))


def _post_specs(d, layer, seg, tm, n_exp):
    ins = [_mod_spec(d, layer, 2, seg), _mod_spec(d, layer, 3, seg), _mod_spec(d, layer, 4, seg)]
    tail = [pl.BlockSpec((1, d), lambda i: (0, 0)), pl.BlockSpec((1, d), lambda i: (0, 0)),
            pl.BlockSpec((n_exp, d), lambda i: (0, 0))]
    outs =[pl.BlockSpec((tm, d), lambda i: (i, 0)), pl.BlockSpec((tm, d), lambda i: (i, 0)),
            pl.BlockSpec((n_exp, tm), lambda i: (0, i))]
    return ins, tail, outs


def _post_shapes(t, d, n_exp):
    return [jax.ShapeDtypeStruct((t, d), F32), jax.ShapeDtypeStruct((t, d), BF16),
            jax.ShapeDtypeStruct((n_exp, t), F32)]


def even_out(a_lat, a_ctx, o_f, o_b, proj, col_r, x_lat, x_ctx, mods4, layer, norm_g, w_out_bf16, ln_g, ln_b,
             router_wt, alpha, seg_rows, n_batch, tm=256):
    t = x_lat.shape[0] + x_ctx.shape[0]
    d = x_lat.shape[1]
    na = a_lat.shape[1]
    wv = o_f.shape[1]
    n_exp = router_wt.shape[0]
    seg = _seg_fn(tm, seg_rows, n_batch)
    ins, tail, outs = _post_specs(d, layer, seg, tm, n_exp)
    tile_of = lambda i: i
    return pl.pallas_call(
        functools.partial(_even_out_kernel, alpha=alpha, n_lat_tiles=x_lat.shape[0] // tm),
        grid=(t // tm,),
        in_specs=(_two_source_specs(a_lat, a_ctx, tm, tile_of)
                  + [pl.BlockSpec((tm, wv), lambda i: (i, 0)),
                     pl.BlockSpec((tm, wv), lambda i: (i, 0)),
                     pl.BlockSpec((tm, wv), lambda i: (i, col_r))]
                  + _two_source_specs(x_lat, x_ctx, tm, tile_of) + ins
                  + [pl.BlockSpec((1, GLA_DV), lambda i: (0, 0)),
                     pl.BlockSpec((na + wv, d), lambda i: (0, 0))] + tail),
        out_specs=outs,
        out_shape=_post_shapes(t, d, n_exp),
        compiler_params=_cparams("arbitrary"),
        name="even_out",
    )(a_lat, a_ctx, o_f, o_b, proj, x_lat, x_ctx, mods4, mods4, mods4, norm_g.reshape(1, -1), w_out_bf16,
      ln_g.reshape(1, -1), ln_b.reshape(1, -1), router_wt)


def _odd_out_kernel(y_ref, x_ref, m2_ref, m3_ref, m4_ref, wg_ref, bg_ref, wo_ref,
                    lg_ref, lb_ref, wr_ref, x1_ref, h2_ref, lt_ref, *, alpha):
    gs = [jax.nn.gelu(y_ref[r, :], approximate=True) for r in _row_halves(y_ref.shape[0])]
    zs = [_dot(g.astype(BF16), wg_ref[...]) + bg_ref[...] for g in gs]
    outs = [_dot((g * jax.nn.sigmoid(z)).astype(BF16), wo_ref[...]) for g, z in zip(gs, zs)]
    _post_mix(jnp.concatenate(outs, 0), x_ref[...], m2_ref, m3_ref, m4_ref, lg_ref, lb_ref, wr_ref, alpha,
              x1_ref, h2_ref, lt_ref, slice(None))


def odd_out(y, x, mods4, layer, w_glu_bf16, b_glu, w_out_bf16, ln_g, ln_b, router_wt,
            alpha, seg_rows, n_batch, tm=256):
    t, w5 = y.shape
    d = x.shape[1]
    n_exp = router_wt.shape[0]
    seg = _seg_fn(tm, seg_rows, n_batch)
    ins, tail, outs = _post_specs(d, layer, seg, tm, n_exp)
    return pl.pallas_call(
        functools.partial(_odd_out_kernel, alpha=alpha),
        grid=(t // tm,),
        in_specs=([pl.BlockSpec((tm, w5), lambda i: (i, 0)), pl.BlockSpec((tm, d), lambda i: (i, 0))] + ins
                  + [pl.BlockSpec((w5, w5), lambda i: (0, 0)),
                     pl.BlockSpec((1, w5), lambda i: (0, 0)),
                     pl.BlockSpec((w5, d), lambda i: (0, 0))] + tail),
        out_specs=outs,
        out_shape=_post_shapes(t, d, n_exp),
        compiler_params=_cparams("arbitrary"),
        name="odd_out",
    )(y, x, mods4, mods4, mods4, w_glu_bf16, b_glu.reshape(1, -1), w_out_bf16,
      ln_g.reshape(1, -1), ln_b.reshape(1, -1), router_wt)


def _route_kernel(lt_ref, rb_ref, idx_ref, w_ref):
    eg = N_EXPERTS // N_GROUPS
    logits = lt_ref[...]
    aff = jax.nn.sigmoid(logits)
    sel = aff + rb_ref[...]
    s = [sel[e:e + 1, :] for e in range(N_EXPERTS)]
    a = [aff[e:e + 1, :] for e in range(N_EXPERTS)]

    def top2_sum(v):
        hi1, lo1 = jnp.maximum(v[0], v[1]), jnp.minimum(v[0], v[1])
        hi2, lo2 = jnp.maximum(v[2], v[3]), jnp.minimum(v[2], v[3])
        return jnp.maximum(hi1, hi2) + jnp.maximum(jnp.minimum(hi1, hi2), jnp.maximum(lo1, lo2))

    best = top2_sum(s[0:eg])
    grp = jnp.zeros_like(best, dtype=jnp.int32)
    for g in range(1, N_GROUPS):
        sc = top2_sum(s[g * eg:(g + 1) * eg])
        better = sc > best
        best = jnp.where(better, sc, best)
        grp = jnp.where(better, g, grp)

    def pick(vals, j):
        out = vals[j]
        for g in range(1, N_GROUPS):
            out = jnp.where(grp == g, vals[g * eg + j], out)
        return out

    sv = [pick(s, j) for j in range(eg)]
    av = [pick(a, j) for j in range(eg)]

    def argmax_first(vals, exclude):
        bi = jnp.zeros_like(grp)
        bv = jnp.where(exclude == 0, -jnp.inf, vals[0]) if exclude is not None else vals[0]
        for j in range(1, eg):
            vj = jnp.where(exclude == j, -jnp.inf, vals[j]) if exclude is not None else vals[j]
            better = vj > bv
            bv = jnp.where(better, vj, bv)
            bi = jnp.where(better, j, bi)
        return bi

    i1 = argmax_first(sv, None)
    i2 = argmax_first(sv, i1)

    def take(vals, i):
        out = vals[0]
        for j in range(1, eg):
            out = jnp.where(i == j, vals[j], out)
        return out

    w1 = take(av, i1)
    w2 = take(av, i2)
    tot = w1 + w2
    idx_ref[0:1, :] = grp * eg + i1
    idx_ref[1:2, :] = grp * eg + i2
    w_ref[0:1, :] = w1 / tot
    w_ref[1:2, :] = w2 / tot


def route(logits_t, router_b, tile=1024):
    n_exp, t = logits_t.shape
    tile = math.gcd(tile, t)
    return pl.pallas_call(
        _route_kernel,
        grid=(t // tile,),
        in_specs=[pl.BlockSpec((n_exp, tile), lambda i: (0, i)),
                  pl.BlockSpec((n_exp, 1), lambda i: (0, 0))],
        out_specs=[pl.BlockSpec((TOP_K, tile), lambda i: (0, i)),
                   pl.BlockSpec((TOP_K, tile), lambda i: (0, i))],
        out_shape=[jax.ShapeDtypeStruct((TOP_K, t), jnp.int32), jax.ShapeDtypeStruct((TOP_K, t), F32)],
        compiler_params=_cparams("arbitrary"),
        name="moe_route",
    )(logits_t, router_b.reshape(n_exp, 1).astype(F32))


def moe_plan(idx, tm):
    t = idx.shape[1]
    n_pair = TOP_K * t
    n_tiles = (n_pair + N_EXPERTS * (tm - 1)) // tm
    e_flat = idx.reshape(-1)
    onehot = (e_flat[:, None] == jnp.arange(N_EXPERTS)[None, :]).astype(jnp.int32)
    running = jnp.cumsum(onehot, axis=0)
    counts = running[-1]
    rank = jnp.sum(onehot * running, 1) - 1
    tiles_per = (counts + tm - 1) // tm
    tile_end = jnp.cumsum(tiles_per)
    n_used = tile_end[-1]
    pstart = (tile_end - tiles_per) * tm
    pos = jnp.sum(onehot * pstart[None, :], 1) + rank
    tile_expert = jnp.minimum(jnp.sum((tile_end[None, :] <= jnp.arange(n_tiles)[:, None]).astype(jnp.int32), 1),
                              N_EXPERTS - 1).astype(jnp.int32)
    gidx = (jnp.arange(n_tiles * tm, dtype=jnp.int32) % t).at[pos].set(
        jnp.arange(n_pair, dtype=jnp.int32) % t, mode="promise_in_bounds", unique_indices=True)
    return gidx, tile_expert, n_used.reshape(1).astype(jnp.int32), pos.astype(jnp.int32)


def _expert_changed(te_ref, tile0):
    i = pl.program_id(0) + tile0
    return jnp.logical_or(pl.program_id(0) == 0, te_ref[i] != te_ref[jnp.maximum(i - 1, 0)])


def _ffn_up_kernel(te_ref, nu_ref, xs_ref, wg_ref, wu_ref, hid_ref, wg_bf, wu_bf, *, tile0):
    used = pl.program_id(0) + tile0 < nu_ref[0]

    @pl.when(jnp.logical_and(used, _expert_changed(te_ref, tile0)))
    def _():
        wg_bf[...] = wg_ref[0].astype(BF16)
        wu_bf[...] = wu_ref[0].astype(BF16)

    @pl.when(used)
    def _():
        xs = xs_ref[...]
        g = _dot(xs, wg_bf[...])
        u = _dot(xs, wu_bf[...])
        hid_ref[...] = ((g * jax.nn.sigmoid(g)) * u).astype(BF16)

    @pl.when(jnp.logical_not(used))
    def _():
        hid_ref[...] = jnp.zeros_like(hid_ref)


def _ffn_down_kernel(te_ref, nu_ref, hid_ref, wd_ref, *rest, tile0):
    o_ref, wd_bf = rest[-2], rest[-1]
    used = pl.program_id(0) + tile0 < nu_ref[0]

    @pl.when(jnp.logical_and(used, _expert_changed(te_ref, tile0)))
    def _():
        wd_bf[...] = wd_ref[0].astype(BF16)

    @pl.when(used)
    def _():
        o_ref[...] = _dot(hid_ref[...], wd_bf[...]).astype(o_ref.dtype)

    @pl.when(jnp.logical_not(used))
    def _():
        o_ref[...] = jnp.zeros_like(o_ref)


def grouped_ffn(xs, tile_expert, n_used, w_gate, w_up, w_down, layer, tm, tile0=0, p_total=None, ys_prev=None):
    p, d = xs.shape
    p_total = p if p_total is None else p_total
    de = w_gate.shape[3]
    n_tiles = p // tm
    wmap = lambda i, te, nu: (layer, te[i + tile0], 0, 0)
    row_in = lambda i, te, nu: (jnp.minimum(i, jnp.maximum(nu[0] - 1 - tile0, 0)), 0)
    hid = pl.pallas_call(
        functools.partial(_ffn_up_kernel, tile0=tile0),
        grid_spec=pltpu.PrefetchScalarGridSpec(
            num_scalar_prefetch=2,
            grid=(n_tiles,),
            in_specs=[pl.BlockSpec((tm, d), row_in),
                      pl.BlockSpec((None, 1, d, de), wmap),
                      pl.BlockSpec((None, 1, d, de), wmap)],
            out_specs=pl.BlockSpec((tm, de), lambda i, te, nu: (i, 0)),
            scratch_shapes=[pltpu.VMEM((d, de), BF16), pltpu.VMEM((d, de), BF16)]),
        out_shape=jax.ShapeDtypeStruct((p, de), BF16),
        compiler_params=_cparams("arbitrary"),
        name="moe_ffn_up",
    )(tile_expert, n_used, xs, w_gate, w_up)
    prev_specs, prev_args, alias = [], [], {}
    if ys_prev is not None:
        prev_specs, prev_args, alias = [pl.BlockSpec(memory_space=pl.ANY)], [ys_prev], {4: 0}
    return pl.pallas_call(
        functools.partial(_ffn_down_kernel, tile0=tile0),
        grid_spec=pltpu.PrefetchScalarGridSpec(
            num_scalar_prefetch=2,
            grid=(n_tiles,),
            in_specs=[pl.BlockSpec((tm, de), row_in),
                      pl.BlockSpec((None, 1, de, d), wmap)] + prev_specs,
            out_specs=pl.BlockSpec((tm, d), lambda i, te, nu: (i + tile0, 0)),
            scratch_shapes=[pltpu.VMEM((de, d), BF16)]),
        out_shape=jax.ShapeDtypeStruct((p_total, d), BF16),
        input_output_aliases=alias,
        compiler_params=_cparams("arbitrary"),
        name="moe_ffn_down",
    )(tile_expert, n_used, hid, w_down, *prev_args)


def _final_kernel(x_ref, y0_ref, y1_ref, w_ref, m5_ref, lg_ref, lb_ref, *rest, alpha):
    o_ref = rest[-1]
    w = w_ref[...]
    y = w[:, 0:1] * y0_ref[...].astype(F32) + w[:, 1:2] * y1_ref[...].astype(F32)
    z = alpha * x_ref[...] + m5_ref[...] * y
    mu = jnp.mean(z, -1, keepdims=True)
    zc = z - mu
    var = jnp.mean(zc * zc, -1, keepdims=True)
    o_ref[...] = zc * lax.rsqrt(var + LN_EPS) * lg_ref[...] + lb_ref[...]


def final_norm(x1, yg, wts, mods4, layer, ln_g, ln_b, alpha, seg_rows, n_batch, tile0=0, out_prev=None, tm=256):
    t, d = x1.shape
    n_tiles = yg.shape[0] // (TOP_K * tm)
    seg = _seg_fn(tm, seg_rows, n_batch)
    row = pl.BlockSpec((tm, d), lambda i: (i + tile0, 0))
    vec = pl.BlockSpec((1, d), lambda i: (0, 0))
    prev_specs, prev_args, alias = [], [], {}
    if out_prev is not None:
        prev_specs, prev_args, alias = [pl.BlockSpec(memory_space=pl.ANY)], [out_prev], {7: 0}
    return pl.pallas_call(
        functools.partial(_final_kernel, alpha=alpha),
        grid=(n_tiles,),
        in_specs=[row, pl.BlockSpec((tm, d), lambda i: (i, 0)), pl.BlockSpec((tm, d), lambda i: (i + n_tiles, 0)),
                  pl.BlockSpec((tm, TOP_K), lambda i: (i + tile0, 0)),
                  _mod_spec(d, layer, 5, lambda i: seg(i + tile0)), vec, vec] + prev_specs,
        out_specs=row,
        out_shape=jax.ShapeDtypeStruct((t, d), F32),
        input_output_aliases=alias,
        compiler_params=_cparams("arbitrary"),
        name="final_norm",
    )(x1, yg, yg, wts, mods4, ln_g.reshape(1, -1), ln_b.reshape(1, -1), *prev_args)


def moe_block(x1, h2, logits_t, router_b, w_gate, w_up, w_down, mods4, layer, ln_g, ln_b,
              alpha, seg_rows, n_batch, tm=512, tm_out=512):
    t = x1.shape[0]
    idx, wts = route(logits_t, router_b)
    gidx, tile_expert, n_used, pos = moe_plan(idx, tm)
    p_total = gidx.shape[0]

    def cuts(n_tiles, tile):
        return [(k * n_tiles // MOE_PARTS) * tile for k in range(MOE_PARTS + 1)]

    ys = None
    rows = cuts(p_total // tm, tm)
    for lo, hi in zip(rows[:-1], rows[1:]):
        xs = h2.at[gidx[lo:hi]].get(mode="promise_in_bounds")
        ys = grouped_ffn(xs, tile_expert, n_used, w_gate, w_up, w_down, layer, tm,
                         tile0=lo // tm, p_total=p_total, ys_prev=ys)
    pos2 = pos.reshape(TOP_K, t)
    out = None
    toks = cuts(t // tm_out, tm_out)
    for lo, hi in zip(toks[:-1], toks[1:]):
        yg = ys.at[pos2[:, lo:hi].reshape(-1)].get(mode="promise_in_bounds")
        out = final_norm(x1, yg, wts.T, mods4, layer, ln_g, ln_b, alpha, seg_rows, n_batch,
                         tile0=lo // tm_out, out_prev=out, tm=tm_out)
    return out


def s5_matrices(lam_re, lam_im, log_dt, b_re, b_im, c_re, c_im, d_skip):
    f32 = F32
    tc = S5_TC
    n_g, n_p = lam_re.shape[1], lam_re.shape[2]
    n_c = b_re.shape[-1]
    nb = 128 // n_c
    n_q = n_g // nb
    lr, li = lam_re.astype(f32), lam_im.astype(f32)
    dt = jnp.exp(log_dt.astype(f32))[..., None]

    def powers(jvals):
        j = jnp.asarray(np.asarray(jvals, np.float32))[:, None, None, None]
        mag = jnp.exp(lr * dt * j)
        return mag * jnp.cos(li * dt * j), mag * jnp.sin(li * dt * j)

    up = np.arange(tc)
    pw_re, pw_im = powers(np.arange(tc + 1))
    lb_re, lb_im = pw_re[1], pw_im[1]
    den = lr * lr + li * li
    fr = ((lb_re - 1.0) * lr + lb_im * li) / den
    fi = (lb_im * lr - (lb_re - 1.0) * li) / den
    br, bi = b_re.astype(f32), b_im.astype(f32)
    bb_re = fr[..., None] * br - fi[..., None] * bi
    bb_im = fr[..., None] * bi + fi[..., None] * br
    cr, ci = c_re.astype(f32), c_im.astype(f32)

    def times_b(p_re, p_im):
        return (p_re[..., None] * bb_re[None] - p_im[..., None] * bb_im[None],
                p_re[..., None] * bb_im[None] + p_im[..., None] * bb_re[None])

    e_re, e_im = times_b(pw_re, pw_im)
    kmat = jnp.sum(cr[None, :, :, :, :, None] * e_re[:tc, :, :, None, :, :]
                   - ci[None, :, :, :, :, None] * e_im[:tc, :, :, None, :, :], axis=4)
    def lag_slab(k_dir):
        return k_dir.reshape(tc, n_q, nb, n_c, n_c).transpose(1, 0, 4, 2, 3).reshape(n_q, tc, n_c, nb * n_c)
    skip = (jnp.eye(n_c, dtype=f32)[None, None, :, None, :]
            * d_skip.astype(f32).reshape(n_q, nb, n_c)[:, None, None, :, :]).reshape(n_q, 1, n_c, nb * n_c)
    k_c = jnp.concatenate([lag_slab(kmat[:tc, 0]), lag_slab(kmat[:tc, 1]), skip], 1)

    def w_slab(e):
        return e.reshape(tc, n_q, nb, n_p, n_c).transpose(1, 0, 4, 2, 3).reshape(n_q, tc, n_c, nb * n_p)
    ef_re, ef_im = times_b(*powers(tc - 1 - up))
    w_c = jnp.stack([w_slab(ef_re[:, 0]), w_slab(ef_im[:, 0]),
                     w_slab(e_re[:tc, 1]), w_slab(e_im[:tc, 1])], 2)

    def v_slabs(d, p_re, p_im):
        f_re = cr[d][None] * p_re[:, :, None, :] - ci[d][None] * p_im[:, :, None, :]
        f_im = cr[d][None] * p_im[:, :, None, :] + ci[d][None] * p_re[:, :, None, :]
        slab = lambda m: m.reshape(tc, n_q, nb, n_c, n_p).transpose(1, 0, 3, 2, 4).reshape(n_q, tc, n_c, nb * n_p)
        return slab(f_re), slab(-f_im)
    pb_re, pb_im = powers(tc - up)
    vt_c = jnp.stack(v_slabs(0, pw_re[1:, 0], pw_im[1:, 0]) + v_slabs(1, pb_re[:, 1], pb_im[:, 1]), 2)
    dec = lambda m: m.reshape(1, n_g * n_p // 128, 1, 128)
    decay = jnp.concatenate([dec(pw_re[tc, 0]), dec(pw_im[tc, 0]), dec(pw_re[tc, 1]), dec(pw_im[tc, 1])], 0)
    return k_c, w_c, vt_c, decay


def _s5_chunk_rows(ref, n):
    return jnp.concatenate([ref[pl.ds(s, n, stride=S5_TC), :] for s in range(S5_TC)], axis=1).astype(BF16)


def _s5_expand(slab, group_lanes):
    rows = 128
    tiled = jnp.concatenate([slab] * (rows // slab.shape[0]), axis=0)
    r = lax.broadcasted_iota(jnp.int32, tiled.shape, 0) // S5_CH
    l = lax.broadcasted_iota(jnp.int32, tiled.shape, 1) // group_lanes
    return jnp.where(r == l, tiled, 0.0).astype(BF16)


def _s5_in_kernel(ul_ref, uc_ref, wc_ref, fr_ref, fi_ref, br_ref, bi_ref, w_ref, *, n_lat, n_ctx, n_batch):
    b = pl.program_id(1)
    n_plane = wc_ref.shape[1]
    st = wc_ref.shape[3]

    @pl.when(b == 0)
    def _():
        for s in range(S5_TC):
            for i in range(n_plane):
                w_ref[s * 128:(s + 1) * 128, i * st:(i + 1) * st] = _s5_expand(wc_ref[s, i], S5_P)

    w_lat = _dot(_s5_chunk_rows(ul_ref, n_lat), w_ref[...])
    w_ctx = _dot(_s5_chunk_rows(uc_ref, n_ctx), w_ref[...])
    nv = fr_ref.shape[0]
    for i, ref in enumerate((fr_ref, fi_ref, br_ref, bi_ref)):
        for c in range(nv):
            lanes = slice((i * nv + c) * 128, (i * nv + c + 1) * 128)
            ref[c, pl.ds(b, n_ctx, stride=n_batch), :] = w_ctx[:, lanes]
            ref[c, pl.ds(n_ctx * n_batch + b, n_lat, stride=n_batch), :] = w_lat[:, lanes]
            ref[c, pl.ds((n_ctx + n_lat) * n_batch + b, n_ctx, stride=n_batch), :] = w_ctx[:, lanes]


def _s5_scan_kernel(wfr, wfi, wbr, wbi, dec_ref, xfr, xfi, xbr, xbi, *, n_tiles):
    nv = wfr.shape[0]
    low = lax.broadcasted_iota(jnp.int32, (nv, 8, 128), 1) < 4
    a_fr, a_fi, a_br, a_bi = dec_ref[0], dec_ref[1], dec_ref[2], dec_ref[3]

    def half_step(s_re, s_im, a_re, a_im, w_re, w_im):
        return a_re * s_re - a_im * s_im + w_re, a_re * s_im + a_im * s_re + w_im

    def one_dir(w_re_ref, w_im_ref, x_re_ref, x_im_ref, row0, s_re, s_im, a_re, a_im, first_low):
        first = low if first_low else jnp.logical_not(low)
        wt_re, wt_im = w_re_ref[:, pl.ds(row0, 8), :], w_im_ref[:, pl.ds(row0, 8), :]
        wr_re, wr_im = pltpu.roll(wt_re, 4, 1), pltpu.roll(wt_im, 4, 1)
        mid_re, mid_im = half_step(s_re, s_im, a_re, a_im, wr_re, wr_im)
        x_re_ref[:, pl.ds(row0, 8), :] = jnp.where(first, s_re, mid_re)
        x_im_ref[:, pl.ds(row0, 8), :] = jnp.where(first, s_im, mid_im)
        m_re = jnp.where(first, pltpu.roll(mid_re, 4, 1), mid_re)
        m_im = jnp.where(first, pltpu.roll(mid_im, 4, 1), mid_im)
        w2_re = jnp.where(first, wr_re, wt_re)
        w2_im = jnp.where(first, wr_im, wt_im)
        return half_step(m_re, m_im, a_re, a_im, w2_re, w2_im)

    def body(i, carry):
        f_re, f_im, b_re, b_im = carry
        rf = pl.multiple_of(i * 8, 8)
        rb = pl.multiple_of((n_tiles - 1 - i) * 8, 8)
        f_re, f_im = one_dir(wfr, wfi, xfr, xfi, rf, f_re, f_im, a_fr, a_fi, True)
        b_re, b_im = one_dir(wbr, wbi, xbr, xbi, rb, b_re, b_im, a_br, a_bi, False)
        return f_re, f_im, b_re, b_im

    z = jnp.zeros((nv, 8, 128), F32)
    lax.fori_loop(0, n_tiles, body, (z, z, z, z))


def _s5_out_kernel(ul_ref, fr_ref, fi_ref, br_ref, bi_ref, kc_ref, vc_ref, y_ref, mt_ref, vt_ref,
                   *, n_lat, n_ctx, n_batch):
    b = pl.program_id(1)
    tc = S5_TC

    @pl.when(b == 0)
    def _():
        lag = [_s5_expand(kc_ref[j], S5_CH) for j in range(2 * tc)]
        diag = _s5_expand(kc_ref[0] + kc_ref[tc] + kc_ref[2 * tc], S5_CH)
        for s in range(tc):
            for t in range(tc):
                blk = diag if s == t else (lag[t - s] if t > s else lag[tc + s - t])
                mt_ref[s * 128:(s + 1) * 128, t * 128:(t + 1) * 128] = blk
        for t in range(tc):
            for i in range(vt_ref.shape[0]):
                vt_ref[i, t * 128:(t + 1) * 128, :] = _s5_expand(vc_ref[t, i], S5_P)

    y = _dot(_s5_chunk_rows(ul_ref, n_lat), mt_ref[...])
    row0 = n_ctx * n_batch + b
    for i, ref in enumerate((fr_ref, fi_ref, br_ref, bi_ref)):
        xs = jnp.concatenate([ref[c, pl.ds(row0, n_lat, stride=n_batch), :] for c in range(ref.shape[0])], 1)
        y = y + _dot_nt(xs.astype(BF16), vt_ref[i])
    for s in range(S5_TC):
        y_ref[pl.ds(s, n_lat, stride=S5_TC), :] = y[:, s * 128:(s + 1) * 128]


def s5_bidir(u, mats, n_batch, l_lat, l_ctx):
    assert n_batch == 4, "the chunk scan packs two chunks of 4 batch rows per 8-sublane tile"
    k_c, w_c, vt_c, decay = mats
    tc = S5_TC
    wd = u.shape[1]
    n_q = wd // 128
    lane_q = tc * 128
    st_q = (128 // S5_CH) * S5_P
    n_lat, n_ctx = l_lat // tc, l_ctx // tc
    nk = n_lat + 2 * n_ctx
    assert nk % 2 == 0
    rows = nk * n_batch
    ctx0 = (n_batch * l_lat) // l_ctx
    dims = dict(n_lat=n_lat, n_ctx=n_ctx, n_batch=n_batch)

    nv = st_q // 128
    plane = jax.ShapeDtypeStruct((n_q * nv, rows, 128), F32)
    plane_spec = pl.BlockSpec((nv, rows, 128), lambda q, b: (q, 0, 0))
    ul_spec = pl.BlockSpec((l_lat, 128), lambda q, b: (b, q))
    uc_spec = pl.BlockSpec((l_ctx, 128), lambda q, b: (ctx0 + b, q))
    w_planes = pl.pallas_call(
        functools.partial(_s5_in_kernel, **dims),
        grid=(n_q, n_batch),
        in_specs=[ul_spec, uc_spec, pl.BlockSpec((None,) + w_c.shape[1:], lambda q, b: (q, 0, 0, 0, 0))],
        out_specs=[plane_spec] * 4,
        out_shape=[plane] * 4,
        scratch_shapes=[pltpu.VMEM((lane_q, 4 * st_q), BF16)],
        compiler_params=_cparams("arbitrary", "arbitrary"),
        name="s5_chunk_in",
    )(u, u, w_c)

    blk = pl.BlockSpec((nv, rows, 128), lambda j: (j, 0, 0))
    x_planes = pl.pallas_call(
        functools.partial(_s5_scan_kernel, n_tiles=rows // 8),
        grid=(n_q,),
        in_specs=[blk] * 4 + [pl.BlockSpec((4, nv, 1, 128), lambda j: (0, j, 0, 0))],
        out_specs=[blk] * 4,
        out_shape=[plane] * 4,
        compiler_params=_cparams("arbitrary"),
        name="s5_chunk_scan",
    )(*w_planes, decay)

    return pl.pallas_call(
        functools.partial(_s5_out_kernel, **dims),
        grid=(n_q, n_batch),
        in_specs=[ul_spec] + [plane_spec] * 4
                 + [pl.BlockSpec((None,) + k_c.shape[1:], lambda q, b: (q, 0, 0, 0)),
                    pl.BlockSpec((None,) + vt_c.shape[1:], lambda q, b: (q, 0, 0, 0, 0))],
        out_specs=pl.BlockSpec((l_lat, 128), lambda q, b: (b, q)),
        out_shape=jax.ShapeDtypeStruct((n_batch * l_lat, wd), F32),
        scratch_shapes=[pltpu.VMEM((lane_q, lane_q), BF16), pltpu.VMEM((4, lane_q, st_q), BF16)],
        compiler_params=_cparams("arbitrary", "arbitrary"),
        name="s5_chunk_out",
    )(u, *x_planes, k_c, vt_c)


def kernel(x, c, ctx, c_ctx, ada_w, ada_b, ln_mix_g, ln_mix_b, ln_ffn_g, ln_ffn_b, ev_w_in, ev_gate_w2,
           ev_gate_b, ev_rpb, ev_norm_g, ev_w_out, od_w_in, od_lam_re, od_lam_im, od_log_dt, od_b_re,
           od_b_im, od_c_re, od_c_im, od_d, od_w_glu, od_b_glu, od_w_out, router_w, router_b,
           moe_w_gate, moe_w_up, moe_w_down):
    n_batch, l_lat, d = x.shape
    l_ctx = ctx.shape[1]
    depth = ada_w.shape[0]
    assert depth == 2, "one even (NA + GLA) layer followed by one odd (S5) layer"
    alpha = (2.0 * depth) ** 0.25
    n_lat = n_batch * l_lat

    cvec = jnp.concatenate([c, c_ctx[None], jnp.zeros((8 - n_batch - 1, d), F32)], 0)
    mods = compute_mods(cvec, ada_w, ada_b)
    mods4 = mods.reshape(depth, 8, 1, N_MOD * d)
    x_lat, x_ctx = x.reshape(n_lat, d), ctx.reshape(n_batch * l_ctx, d)
    router_wt = router_w.T.astype(F32)

    na_w = NA_HEADS * NA_DH
    wk = GLA_HEADS * GLA_DK
    wv = GLA_HEADS * GLA_DV
    ev_in = ev_w_in.shape[2]
    pad = (-ev_in) % 256
    w_in = jnp.pad(ev_w_in[0], ((0, 0), (0, pad))).astype(BF16)
    proj = mod_matmul(x_lat, x_ctx, mods4, 0, w_in, l_lat, n_batch, tm=512, tn=(ev_in + pad) // 2)
    a_lat, a_ctx = na_attention(proj, na_bias_table(ev_rpb[0], l_lat // GRID_W), n_batch, l_lat, l_ctx)
    col_lr = (3 * na_w + 2 * wk + 2 * wv) // 128
    g2 = jnp.zeros((2, 128, wk), F32)
    g2 = g2.at[0, 0:GLA_RANK].set(ev_gate_w2[0, 0]).at[1, GLA_RANK:2 * GLA_RANK].set(ev_gate_w2[0, 1])
    o_f, o_b = gla_bidir(proj, g2.astype(BF16), ev_gate_b[0].reshape(2, 1, wk), rope_tables(l_lat),
                         n_batch, l_lat, l_ctx,
                         col_q=3 * na_w // wk, col_k=(3 * na_w + wk) // wk,
                         col_v=(3 * na_w + 2 * wk) // wv, col_lr=col_lr)
    x1, h2, logits_t = even_out(a_lat, a_ctx, o_f, o_b, proj, (3 * na_w + 2 * wk + wv) // wv, x_lat, x_ctx,
                                mods4, 0, ev_norm_g[0], ev_w_out[0].astype(BF16), ln_mix_g[0], ln_mix_b[0],
                                router_wt, alpha, l_lat, n_batch)
    w_gate, w_up, w_down = moe_w_gate, moe_w_up, moe_w_down
    rows = moe_block(x1, h2, logits_t, router_b, w_gate, w_up, w_down, mods4, 0,
                     ln_ffn_g[0], ln_ffn_b[0], alpha, l_lat, n_batch)

    u = mod_matmul(rows, None, mods4, 1, od_w_in[0].astype(BF16), l_lat, n_batch, tm=512)
    mats = s5_matrices(od_lam_re[0], od_lam_im[0], od_log_dt[0], od_b_re[0], od_b_im[0],
                       od_c_re[0], od_c_im[0], od_d[0])
    y5 = s5_bidir(u, mats, n_batch, l_lat, l_ctx)
    x1, h2, logits_t = odd_out(y5, rows, mods4, 1, od_w_glu[0].astype(BF16), od_b_glu[0],
                               od_w_out[0].astype(BF16), ln_mix_g[1], ln_mix_b[1], router_wt,
                               alpha, l_lat, n_batch)
    out = moe_block(x1, h2, logits_t, router_b, w_gate, w_up, w_down, mods4, 1,
                    ln_ffn_g[1], ln_ffn_b[1], alpha, l_lat, n_batch)
    return out.reshape(n_batch, l_lat, d)
```
